```python
import math
import jax
import jax.numpy as jnp
from jax import lax
import numpy as np

D_MODEL = 1024
BATCH = 8
SEQ = 2048
DEPTH = 2

CTX_LEN = 256
GRID_W = 64
D_MIX = D_MODEL
HY_WIDTH = D_MIX // 4
HY_GROUPS = 4
HY_SHORT = 3
HY_POS_DIM = 33
HY_FFN = 64
HY_FAST_DECAY = 0.3
HY_SLOW_DECAY = 1.5
HY_TARGET = 1e-2
ML_WIDTH = D_MIX // 2
ML_HEADS = 4
ML_HEAD_DIM = ML_WIDTH // ML_HEADS
ML_CONV = 4
ML_CHUNK = 128
RG_WIDTH = D_MIX - HY_WIDTH - ML_WIDTH
RG_HEADS = 4
RG_BLOCK = RG_WIDTH // RG_HEADS
RG_CONV = 4
RG_C = 8.0
D_FF = 4 * D_MODEL
N_MOD = 6
EPS = 1e-6
HY_COLS = 3 * HY_WIDTH
ML_GATE_COLS = 4 * ML_HEADS
ML_COLS = 4 * ML_WIDTH + ML_GATE_COLS
RG_COLS = 2 * RG_WIDTH
P_IN = HY_COLS + ML_COLS + RG_COLS
OFF_ML = HY_COLS
OFF_RG = HY_COLS + ML_COLS

kernel_name = 'hybrid_hyena_mlstm_rglru_dit'


def rmsnorm(x, g):
    xf = x.astype(jnp.float32)
    y = xf * lax.rsqrt(jnp.mean(xf * xf, axis=-1, keepdims=True) + EPS)
    return (y * g.astype(jnp.float32)).astype(x.dtype)


def head_rmsnorm(y, n_heads, g):
    shp = y.shape
    yf = y.astype(jnp.float32).reshape(shp[:-1] + (n_heads, shp[-1] // n_heads))
    yf = yf * lax.rsqrt(jnp.mean(yf * yf, axis=-1, keepdims=True) + EPS)
    return yf.reshape(shp) * g.astype(jnp.float32)


def dwconv(x, w):
    K = w.shape[0]
    pl = (K - 1) // 2
    return lax.conv_general_dilated(
        x, w[:, None, :].astype(x.dtype), window_strides=(1,), padding=[(pl, K - 1 - pl)],
        dimension_numbers=('NWC', 'WIO', 'NWC'), feature_group_count=x.shape[-1])


def to_col_major(a, rows):
    B, T, C = a.shape
    return a.reshape(B, rows, GRID_W, C).transpose(0, 2, 1, 3).reshape(B, T, C)


def from_col_major(a, rows):
    B, T, C = a.shape
    return a.reshape(B, GRID_W, rows, C).transpose(0, 2, 1, 3).reshape(B, T, C)


def hyena_filters(L, w1, b1, w2, b2, w3, b3, freq):
    f32 = jnp.float32
    pos = jnp.arange(L, dtype=f32)[:, None]
    t = pos / max(L - 1, 1)
    bands = (HY_POS_DIM - 1) // 2
    fr = jnp.linspace(1e-4, bands - 1, bands, dtype=f32)[None]
    ang = (2.0 * math.pi / L) * fr * pos
    feats = jnp.concatenate([t, jnp.cos(ang), -jnp.sin(ang)], axis=-1)
    fq = freq.astype(f32)
    h = jnp.sin(fq[0] * (feats @ w1.astype(f32) + b1.astype(f32)))
    h = jnp.sin(fq[1] * (h @ w2.astype(f32) + b2.astype(f32)))
    h = (h @ w3.astype(f32) + b3.astype(f32)).reshape(L, 2, HY_WIDTH)
    max_decay = math.log(HY_TARGET) / HY_FAST_DECAY
    min_decay = math.log(HY_TARGET) / HY_SLOW_DECAY
    deltas = jnp.linspace(min_decay, max_decay, HY_WIDTH, dtype=f32)
    h = h * jnp.exp(-t * jnp.abs(deltas))[:, None, :]
    h_fwd, h_bwd = h[:, 0], h[:, 1]
    return jnp.concatenate([h_fwd, jnp.zeros((1, HY_WIDTH), f32), h_bwd[:0:-1]], axis=0)


def hyena_longconv(u, filt, bias):
    L = u.shape[1]
    uf = u.astype(jnp.float32)
    U = jnp.fft.rfft(uf, n=2 * L, axis=1)
    F = jnp.fft.rfft(filt, n=2 * L, axis=0)
    y = jnp.fft.irfft(U * F[None], n=2 * L, axis=1)[:, :L]
    return y + uf * bias.astype(jnp.float32)


def hyena_group(z, p):
    L = z.shape[1]
    z = dwconv(z, p['hy_conv'])
    x0, x1, v = jnp.split(z, 3, axis=-1)
    filt = hyena_filters(L, p['hy_w1'], p['hy_b1'], p['hy_w2'], p['hy_b2'],
                         p['hy_w3'], p['hy_b3'], p['hy_freq'])
    return x0.astype(jnp.float32) * hyena_longconv(x1 * v, filt, p['hy_bias'])


def mlstm_prep(z, p):
    B, T, _ = z.shape
    f32 = jnp.float32
    qk = jax.nn.silu(dwconv(z[..., :2 * ML_WIDTH], p['ml_conv']))
    heads = lambda a: a.astype(f32).reshape(B, T, ML_HEADS, ML_HEAD_DIM).transpose(0, 2, 1, 3)
    q = heads(qk[..., :ML_WIDTH])
    k = heads(qk[..., ML_WIDTH:]) * (ML_HEAD_DIM ** -0.5)
    v = heads(z[..., 2 * ML_WIDTH:3 * ML_WIDTH])
    gates = z[..., 4 * ML_WIDTH:].astype(f32) + p['ml_gate_b'].astype(f32)
    gates = gates.reshape(B, T, 4, ML_HEADS).transpose(2, 0, 3, 1)
    return q, k, v, gates[0::2], jax.nn.log_sigmoid(gates[1::2])


def mlstm_scan(q, k, v, log_i, log_f, state, need_out):
    B, H, T, DH = q.shape
    N = T // ML_CHUNK
    ch = lambda a: a.reshape(a.shape[:2] + (N, ML_CHUNK) + a.shape[3:])
    q, k, v, li, lf = ch(q), ch(k), ch(v), ch(log_i), ch(log_f)
    b = jnp.cumsum(lf, axis=-1)
    g = b[..., -1:] - b + li
    m_loc = jnp.max(g, axis=-1)
    w = jnp.exp(g - m_loc[..., None])
    C_loc = jnp.einsum('bhnl,bhnld,bhnle->bhnde', w, v, k)
    n_loc = jnp.einsum('bhnl,bhnle->bhne', w, k)

    def step(carry, inp):
        C, n, m = carry
        b_last, ml, Cl, nl = inp
        m_new = jnp.maximum(b_last + m, ml)
        a = jnp.exp(b_last + m - m_new)
        s = jnp.exp(ml - m_new)
        C_new = a[..., None, None] * C + s[..., None, None] * Cl
        n_new = a[..., None] * n + s[..., None] * nl
        return (C_new, n_new, m_new), (C, n, m)

    mv = lambda a: jnp.moveaxis(a, 2, 0)
    final, starts = lax.scan(step, state, (mv(b[..., -1]), mv(m_loc), mv(C_loc), mv(n_loc)))
    if not need_out:
        return None, final
    C0, n0, m0 = [jnp.moveaxis(a, 0, 2) for a in starts]
    causal = jnp.tril(jnp.ones((ML_CHUNK, ML_CHUNK), dtype=bool))
    D = jnp.where(causal, b[..., :, None] - b[..., None, :] + li[..., None, :], -jnp.inf)
    inter = b + m0[..., None]
    m = jnp.maximum(inter, jnp.max(D, axis=-1))
    S = jnp.einsum('bhntd,bhnsd->bhnts', q, k) * jnp.exp(D - m[..., None])
    wi = jnp.exp(inter - m)
    num = jnp.einsum('bhnts,bhnsd->bhntd', S, v) + wi[..., None] * jnp.einsum('bhnde,bhnte->bhntd', C0, q)
    den = jnp.sum(S, axis=-1) + wi * jnp.einsum('bhne,bhnte->bhnt', n0, q)
    h = num / jnp.maximum(jnp.abs(den), jnp.exp(-m))[..., None]
    return h.reshape(B, H, T, DH), final


def mlstm_dirs(q, k, v, log_i, log_f, states, need_out):
    rev = lambda a: jnp.flip(a, axis=2)
    h_f, s_f = mlstm_scan(q, k, v, log_i[0], log_f[0], states[0], need_out)
    h_b, s_b = mlstm_scan(rev(q), rev(k), rev(v), rev(log_i[1]), rev(log_f[1]), states[1], need_out)
    h = h_f + rev(h_b) if need_out else None
    return h, (s_f, s_b)


def heads_to_tokens(h):
    B, H, T, DH = h.shape
    return h.transpose(0, 2, 1, 3).reshape(B, T, H * DH)


def rglru_prep(zx, p):
    B, T, _ = zx.shape
    f32 = jnp.float32
    xr = dwconv(zx, p['rg_conv']).astype(f32)
    pre = jnp.einsum('bthi,dghij->dgbthj', xr.reshape(B, T, RG_HEADS, RG_BLOCK),
                     p['rg_gate_w'].astype(f32)).reshape(2, 2, B, T, RG_WIDTH)
    pre = pre + p['rg_gate_b'].astype(f32)[:, :, None, None, :]
    r = jax.nn.sigmoid(pre[:, 0])
    i = jax.nn.sigmoid(pre[:, 1])
    log_a = -RG_C * r * jax.nn.softplus(-p['rg_lambda'].astype(f32))[:, None, None, :]
    a = jnp.exp(log_a)
    b = jnp.sqrt(-jnp.expm1(2.0 * log_a)) * (i * xr[None])
    return a, b


def linear_scan(a, b, h0):
    def comb(l, r):
        return l[0] * r[0], r[0] * l[1] + r[1]
    A, Hs = lax.associative_scan(comb, (a, b), axis=1)
    return A * h0[:, None] + Hs


def rglru_dirs(a, b, h0s):
    h_f = linear_scan(a[0], b[0], h0s[0])
    h_b = jnp.flip(linear_scan(jnp.flip(a[1], 1), jnp.flip(b[1], 1), h0s[1]), 1)
    return h_f + h_b, (h_f[:, -1], h_b[:, 0])


def merge_groups(y_hy, h_ml, o_pre, h_rg, gate_pre, p, dt):
    g = p['mix_norm_g']
    y_hy = head_rmsnorm(y_hy, HY_GROUPS, g[:HY_WIDTH])
    y_ml = head_rmsnorm(h_ml, ML_HEADS, g[HY_WIDTH:HY_WIDTH + ML_WIDTH]) * jax.nn.sigmoid(o_pre.astype(jnp.float32))
    y_rg = head_rmsnorm(h_rg * jax.nn.gelu(gate_pre.astype(jnp.float32)), RG_HEADS, g[HY_WIDTH + ML_WIDTH:])
    y = jnp.concatenate([y_hy, y_ml, y_rg], axis=-1).astype(dt)
    return y @ p['w_out']


def token_mixer(h_ctx, h_lat, p, rows, need_ctx_out):
    B = h_lat.shape[0]
    f32 = jnp.float32
    z_ctx = h_ctx @ p['w_in']
    z_lat = h_lat @ p['w_in']
    ml_zero = (jnp.zeros((B, ML_HEADS, ML_HEAD_DIM, ML_HEAD_DIM), f32),
               jnp.zeros((B, ML_HEADS, ML_HEAD_DIM), f32), jnp.zeros((B, ML_HEADS), f32))
    rg_zero = jnp.zeros((B, RG_WIDTH), f32)
    cq, ck, cv, cli, clf = mlstm_prep(z_ctx[..., OFF_ML:OFF_RG], p)
    hm_c, ml_states = mlstm_dirs(cq, ck, cv, cli, clf, (ml_zero, ml_zero), need_ctx_out)
    ca, cb = rglru_prep(z_ctx[..., OFF_RG:OFF_RG + RG_WIDTH], p)
    hr_c, rg_states = rglru_dirs(ca, cb, (rg_zero, rg_zero))
    y_ctx = None
    if need_ctx_out:
        y_hy_c = hyena_group(z_ctx[..., :HY_COLS], p)
        y_ctx = merge_groups(y_hy_c, heads_to_tokens(hm_c),
                             z_ctx[..., OFF_ML + 3 * ML_WIDTH:OFF_ML + 4 * ML_WIDTH],
                             hr_c, z_ctx[..., OFF_RG + RG_WIDTH:], p, h_ctx.dtype)
    y_hy = hyena_group(z_lat[..., :HY_COLS], p)
    lq, lk, lv, lli, llf = mlstm_prep(to_col_major(z_lat[..., OFF_ML:OFF_RG], rows), p)
    hm, _ = mlstm_dirs(lq, lk, lv, lli, llf, ml_states, True)
    hm = from_col_major(heads_to_tokens(hm), rows)
    la, lb = rglru_prep(z_lat[..., OFF_RG:OFF_RG + RG_WIDTH], p)
    hr, _ = rglru_dirs(la, lb, rg_states)
    y_lat = merge_groups(y_hy, hm, z_lat[..., OFF_ML + 3 * ML_WIDTH:OFF_ML + 4 * ML_WIDTH],
                         hr, z_lat[..., OFF_RG + RG_WIDTH:], p, h_lat.dtype)
    return y_ctx, y_lat


def sq_relu_mlp(h, w1, w2):
    return jnp.square(jax.nn.relu(h @ w1)) @ w2


def modulate(t, g, m, i):
    return rmsnorm(t, g) * (1.0 + m[:, :, i + 1]) + m[:, :, i]


def setup_inputs(seed: int = 0) -> dict:
    key = jax.random.key(seed)
    ks = jax.random.split(key, 32)

    def nrm(k, shape, scale):
        return jax.random.normal(k, shape, jnp.float32) * scale

    x = nrm(ks[0], (BATCH, SEQ, D_MODEL), 1.0)
    c = nrm(ks[1], (BATCH, D_MODEL), 1.0)
    ctx = nrm(ks[2], (BATCH, CTX_LEN, D_MODEL), 1.0)
    c_ctx = nrm(ks[3], (D_MODEL,), 1.0)
    ada_w = nrm(ks[4], (DEPTH, D_MODEL, N_MOD * D_MODEL), D_MODEL ** -0.5)
    ada_b = nrm(ks[5], (DEPTH, N_MOD * D_MODEL), 0.02)
    norm1_g = 1.0 + nrm(ks[6], (DEPTH, D_MODEL), 0.02)
    norm2_g = 1.0 + nrm(ks[7], (DEPTH, D_MODEL), 0.02)
    w_in = nrm(ks[8], (DEPTH, D_MODEL, P_IN), D_MODEL ** -0.5)
    hy_conv = nrm(ks[9], (DEPTH, HY_SHORT, HY_COLS), HY_SHORT ** -0.5)
    hy_w1 = nrm(ks[10], (DEPTH, HY_POS_DIM, HY_FFN), HY_POS_DIM ** -0.5)
    hy_b1 = nrm(ks[11], (DEPTH, HY_FFN), 0.02)
    hy_w2 = nrm(ks[12], (DEPTH, HY_FFN, HY_FFN), HY_FFN ** -0.5)
    hy_b2 = nrm(ks[13], (DEPTH, HY_FFN), 0.02)
    hy_w3 = nrm(ks[14], (DEPTH, HY_FFN, 2 * HY_WIDTH), HY_FFN ** -0.5)
    hy_b3 = nrm(ks[15], (DEPTH, 2 * HY_WIDTH), 0.02)
    hy_freq = 1.0 + nrm(ks[16], (DEPTH, 2, HY_FFN), 0.02)
    hy_bias = nrm(ks[17], (DEPTH, HY_WIDTH), 1.0)
    ml_conv = nrm(ks[18], (DEPTH, ML_CONV, 2 * ML_WIDTH), ML_CONV ** -0.5)
    i_bias = nrm(ks[19], (DEPTH, 2, 1, ML_HEADS), 0.1)
    f_bias = jnp.linspace(3.0, 6.0, ML_HEADS, dtype=jnp.float32) + nrm(ks[20], (DEPTH, 2, 1, ML_HEADS), 0.1)
    ml_gate_b = jnp.concatenate([i_bias, f_bias], axis=2).reshape(DEPTH, ML_GATE_COLS)
    rg_conv = nrm(ks[21], (DEPTH, RG_CONV, RG_WIDTH), RG_CONV ** -0.5)
    rg_gate_w = nrm(ks[22], (DEPTH, 2, 2, RG_HEADS, RG_BLOCK, RG_BLOCK), RG_BLOCK ** -0.5)
    rg_gate_b = nrm(ks[23], (DEPTH, 2, 2, RG_WIDTH), 0.02)
    a_c = jax.random.uniform(ks[24], (DEPTH, 2, RG_WIDTH), jnp.float32, 0.9, 0.999)
    s = a_c ** (1.0 / RG_C)
    rg_lambda = jnp.log(s) - jnp.log1p(-s)
    mix_norm_g = 1.0 + nrm(ks[25], (DEPTH, D_MIX), 0.02)
    w_out = nrm(ks[26], (DEPTH, D_MIX, D_MODEL), D_MIX ** -0.5)
    mlp_w1 = nrm(ks[27], (DEPTH, D_MODEL, D_FF), D_MODEL ** -0.5)
    mlp_w2 = nrm(ks[28], (DEPTH, D_FF, D_MODEL), D_FF ** -0.5)
    final_g = 1.0 + nrm(ks[29], (D_MODEL,), 0.02)
    return {'x': x, 'c': c, 'ctx': ctx, 'c_ctx': c_ctx, 'ada_w': ada_w, 'ada_b': ada_b,
            'norm1_g': norm1_g, 'norm2_g': norm2_g, 'w_in': w_in, 'hy_conv': hy_conv,
            'hy_w1': hy_w1, 'hy_b1': hy_b1, 'hy_w2': hy_w2, 'hy_b2': hy_b2, 'hy_w3': hy_w3,
            'hy_b3': hy_b3, 'hy_freq': hy_freq, 'hy_bias': hy_bias, 'ml_conv': ml_conv,
            'ml_gate_b': ml_gate_b, 'rg_conv': rg_conv, 'rg_gate_w': rg_gate_w,
            'rg_gate_b': rg_gate_b, 'rg_lambda': rg_lambda, 'mix_norm_g': mix_norm_g,
            'w_out': w_out, 'mlp_w1': mlp_w1, 'mlp_w2': mlp_w2, 'final_g': final_g}


def reference(x, c, ctx, c_ctx, ada_w, ada_b, norm1_g, norm2_g, w_in, hy_conv,
              hy_w1, hy_b1, hy_w2, hy_b2, hy_w3, hy_b3, hy_freq, hy_bias, ml_conv,
              ml_gate_b, rg_conv, rg_gate_w, rg_gate_b, rg_lambda, mix_norm_g,
              w_out, mlp_w1, mlp_w2, final_g):
    B, T, D = x.shape
    rows = T // GRID_W
    ctx_s = ctx
    for l in range(DEPTH):
        need_ctx = l < DEPTH - 1
        p = {'w_in': w_in[l], 'w_out': w_out[l], 'hy_conv': hy_conv[l], 'hy_w1': hy_w1[l],
             'hy_b1': hy_b1[l], 'hy_w2': hy_w2[l], 'hy_b2': hy_b2[l], 'hy_w3': hy_w3[l],
             'hy_b3': hy_b3[l], 'hy_freq': hy_freq[l], 'hy_bias': hy_bias[l],
             'ml_conv': ml_conv[l], 'ml_gate_b': ml_gate_b[l], 'rg_conv': rg_conv[l],
             'rg_gate_w': rg_gate_w[l], 'rg_gate_b': rg_gate_b[l], 'rg_lambda': rg_lambda[l],
             'mix_norm_g': mix_norm_g[l]}
        mod = (jax.nn.silu(c) @ ada_w[l] + ada_b[l]).reshape(B, 1, N_MOD, D)
        mod_c = (jax.nn.silu(c_ctx) @ ada_w[l] + ada_b[l]).reshape(1, 1, N_MOD, D)
        hx = modulate(x, norm1_g[l], mod, 0)
        hc = modulate(ctx_s, norm1_g[l], mod_c, 0)
        y_ctx, y_lat = token_mixer(hc, hx, p, rows, need_ctx)
        x = x + mod[:, :, 2] * y_lat
        x = x + mod[:, :, 5] * sq_relu_mlp(modulate(x, norm2_g[l], mod, 3), mlp_w1[l], mlp_w2[l])
        if need_ctx:
            ctx_s = ctx_s + mod_c[:, :, 2] * y_ctx
            ctx_s = ctx_s + mod_c[:, :, 5] * sq_relu_mlp(modulate(ctx_s, norm2_g[l], mod_c, 3), mlp_w1[l], mlp_w2[l])
    return rmsnorm(x, final_g)
```

```python
import functools
import math

import numpy as np
import jax
import jax.numpy as jnp
from jax import lax
from jax.experimental import pallas as pl
from jax.experimental.pallas import tpu as pltpu

F32 = jnp.float32
BF16 = jnp.bfloat16

D_MODEL = 1024
DEPTH = 2
GRID_W = 64
HY_WIDTH = 256
HY_GROUPS = 4
HY_SHORT = 3
HY_POS_DIM = 33
HY_FFN = 64
HY_FAST_DECAY = 0.3
HY_SLOW_DECAY = 1.5
HY_TARGET = 1e-2
ML_WIDTH = 512
ML_HEADS = 4
ML_HEAD_DIM = 128
ML_CONV = 4
ML_CHUNK = 128
RG_WIDTH = 256
RG_HEADS = 4
RG_BLOCK = 64
RG_CONV = 4
RG_C = 8.0
D_FF = 4096
N_MOD = 6
EPS = 1e-6
HY_COLS = 3 * HY_WIDTH
ML_GATE_COLS = 4 * ML_HEADS
ML_COLS = 4 * ML_WIDTH + ML_GATE_COLS
OFF_ML = HY_COLS
OFF_RG = HY_COLS + ML_COLS

LANES = 128
SUBLANES = 8
VMEM_LIMIT = 56 * 1024 * 1024

ZA_COLS = 1792
ZM_COLS = 1664
PAD = SUBLANES


def _cparams(sem):
    return pltpu.CompilerParams(dimension_semantics=sem, vmem_limit_bytes=VMEM_LIMIT)


def _const_spec(shape):
    nd = len(shape)
    return pl.BlockSpec(shape, lambda *_: (0,) * nd, pipeline_mode=pl.Buffered(1))


def _split3(v):
    hi = v.astype(BF16)
    r = v - hi.astype(F32)
    mid = r.astype(BF16)
    lo = (r - mid.astype(F32)).astype(BF16)
    return hi, mid, lo


def _dot(a, b):
    return jnp.dot(a, b, preferred_element_type=F32)


def _dot_f32(a, b):
    ah, am, al = _split3(a)
    bh, bm, bl = _split3(b)
    return (_dot(ah, bh) + (_dot(ah, bm) + _dot(am, bh))
            + (_dot(ah, bl) + _dot(am, bm) + _dot(al, bh)))


def _dot_nt(a, b):
    return lax.dot_general(a, b, (((1,), (1,)), ((), ())), preferred_element_type=F32)


def _dot_tn(a, b):
    return lax.dot_general(a, b, (((0,), (0,)), ((), ())), preferred_element_type=F32)


def _sigmoid(v):
    return 1.0 / (1.0 + jnp.exp(-v))


def _modulated_norm(x, g, shift, scale):
    ms = jnp.mean(x * x, axis=-1, keepdims=True)
    return (x * lax.rsqrt(ms + EPS) * g) * (1.0 + scale) + shift


def _dwconv(pad_ref, w_ref, seq, ksize):
    left = (ksize - 1) // 2
    acc = None
    for j in range(ksize):
        term = w_ref[j:j + 1, :] * pad_ref[PAD + j - left:PAD + j - left + seq, :]
        acc = term if acc is None else acc + term
    return acc


def _fill_padded(pad_ref, rows, seq):
    width = pad_ref.shape[1]
    pad_ref[0:PAD, :] = jnp.zeros((PAD, width), F32)
    pad_ref[PAD + seq:2 * PAD + seq, :] = jnp.zeros((PAD, width), F32)
    pad_ref[PAD:PAD + seq, :] = rows


def _ada_kernel(c_ref, w_ref, b_ref, o_ref):
    c = c_ref[...]
    s = c * _sigmoid(c)
    o_ref[0] = _dot_f32(s, w_ref[0]) + b_ref[0]


def _ada(cc, ada_w, ada_b):
    depth, d, n = ada_w.shape
    tn = 1536
    return pl.pallas_call(
        _ada_kernel,
        grid=(depth, n // tn),
        in_specs=[pl.BlockSpec(cc.shape, lambda l, j: (0, 0)),
                  pl.BlockSpec((1, d, tn), lambda l, j: (l, 0, j)),
                  pl.BlockSpec((1, 1, tn), lambda l, j: (l, 0, j))],
        out_specs=pl.BlockSpec((1, cc.shape[0], tn), lambda l, j: (l, 0, j)),
        out_shape=jax.ShapeDtypeStruct((depth, cc.shape[0], n), F32),
        compiler_params=_cparams(("parallel", "parallel")),
        name="ada",
    )(cc, ada_w, ada_b.reshape(depth, 1, n))


def _inproj_kernel(x_ref, sh_ref, sc_ref, g_ref, w_ref, o_ref):
    h = _modulated_norm(x_ref[0], g_ref[...], sh_ref[0], sc_ref[0])
    o_ref[0] = _dot(h.astype(BF16), w_ref[...])


def _mod_spec(arr):
    if arr.shape[0] == 1:
        return pl.BlockSpec((1, 1, arr.shape[2]), lambda b, i: (0, 0, 0))
    return pl.BlockSpec((1, 1, arr.shape[2]), lambda b, i: (b, 0, 0))


def _inproj(x, shift, scale, g, w, tm):
    bsz, seq, d = x.shape
    p = w.shape[1]
    return pl.pallas_call(
        _inproj_kernel,
        grid=(bsz, seq // tm),
        in_specs=[pl.BlockSpec((1, tm, d), lambda b, i: (b, i, 0)),
                  _mod_spec(shift), _mod_spec(scale),
                  pl.BlockSpec((1, d), lambda b, i: (0, 0)),
                  _const_spec((d, p))],
        out_specs=pl.BlockSpec((1, tm, p), lambda b, i: (b, i, 0)),
        out_shape=jax.ShapeDtypeStruct((bsz, seq, p), F32),
        compiler_params=_cparams(("parallel", "parallel")),
        name="inproj",
    )(x, shift, scale, g, w)


def _dft_tables(seq):
    n = 2 * seq
    k = np.arange(seq, dtype=np.int64)[:, None]
    s = np.arange(seq, dtype=np.int64)[None, :]
    ang = (2.0 * np.pi / n) * ((k * s) % n).astype(np.float64)
    top = np.cos(ang)
    bot = -np.sin(ang)
    bot[0, :] = np.where(np.arange(seq) % 2 == 0, 1.0, -1.0)
    fwd = np.concatenate([top, bot], axis=0)
    scale = np.full((n, 1), 2.0 / n)
    scale[0, 0] = 1.0 / n
    scale[seq, 0] = 1.0 / n
    inv = (fwd * scale).T
    return (jnp.asarray(fwd.astype(np.float32)).astype(BF16),
            jnp.asarray(inv.astype(np.float32)).astype(BF16))


def _hy_feature_tables(seq):
    pos = np.arange(seq, dtype=np.float64)[:, None]
    t = pos / max(seq - 1, 1)
    bands = (HY_POS_DIM - 1) // 2
    fr = np.linspace(1e-4, bands - 1, bands).astype(np.float32).astype(np.float64)[None]
    ang = (2.0 * math.pi / seq) * fr * pos
    feats = np.concatenate([t, np.cos(ang), -np.sin(ang)], axis=-1)
    feats = np.pad(feats, ((0, 0), (0, LANES - HY_POS_DIM)))
    max_decay = math.log(HY_TARGET) / HY_FAST_DECAY
    min_decay = math.log(HY_TARGET) / HY_SLOW_DECAY
    deltas = np.linspace(min_decay, max_decay, HY_WIDTH).astype(np.float32).astype(np.float64)
    decay = np.exp(-t * np.abs(deltas)[None, :])
    return jnp.asarray(feats, dtype=F32), jnp.asarray(decay, dtype=F32)


def _hy_filter_kernel(feats_ref, w1_ref, b1_ref, w2_ref, b2_ref, w3_ref, b3_ref, fq_ref,
                      decay_ref, wf_ref, f_ref, h_scr, *, seq, tr):
    i = pl.program_id(0)

    @pl.when(i == 0)
    def _():
        h = jnp.sin(fq_ref[0:1, :] * (_dot_f32(feats_ref[...], w1_ref[...]) + b1_ref[...]))
        h = jnp.sin(fq_ref[1:2, :] * (_dot_f32(h, w2_ref[...]) + b2_ref[...]))
        h = _dot_f32(h, w3_ref[...]) + b3_ref[...]
        dec = decay_ref[...]
        hf = h[:, :HY_WIDTH] * dec
        hb = h[:, HY_WIDTH:] * dec
        row = lax.broadcasted_iota(jnp.int32, hb.shape, 0)
        hb = jnp.where(row == 0, 0.0, hb)
        h_scr[:, :HY_WIDTH] = hf.astype(BF16)
        h_scr[:, HY_WIDTH:] = hb.astype(BF16)

    g = _dot(wf_ref[...], h_scr[...])
    gf = g[:, :HY_WIDTH]
    gb = g[:, HY_WIDTH:]
    row = i * tr + lax.broadcasted_iota(jnp.int32, gf.shape, 0)
    f_ref[...] = jnp.where(row <= seq, gf + gb, gf - gb)


def _hy_filter(seq, p, fwd):
    feats, decay = _hy_feature_tables(seq)
    tr = min(seq, 512)
    w1 = jnp.pad(p['hy_w1'], ((0, LANES - HY_POS_DIM), (0, 0)))
    full = lambda shp: pl.BlockSpec(shp, lambda i: (0,) * len(shp))
    return pl.pallas_call(
        functools.partial(_hy_filter_kernel, seq=seq, tr=tr),
        grid=(2 * seq // tr,),
        in_specs=[full((seq, LANES)), full((LANES, HY_FFN)), full((1, HY_FFN)),
                  full((HY_FFN, HY_FFN)), full((1, HY_FFN)),
                  full((HY_FFN, 2 * HY_WIDTH)), full((1, 2 * HY_WIDTH)),
                  full((2, HY_FFN)), full((seq, HY_WIDTH)),
                  pl.BlockSpec((tr, seq), lambda i: (i, 0))],
        out_specs=pl.BlockSpec((tr, HY_WIDTH), lambda i: (i, 0)),
        out_shape=jax.ShapeDtypeStruct((2 * seq, HY_WIDTH), F32),
        scratch_shapes=[pltpu.VMEM((seq, 2 * HY_WIDTH), BF16)],
        compiler_params=_cparams(("arbitrary",)),
        name="hy_filter",
    )(feats, w1, p['hy_b1'][None], p['hy_w2'], p['hy_b2'][None], p['hy_w3'], p['hy_b3'][None],
      p['hy_freq'], decay, fwd)


def _hy_pre_kernel(z_ref, cw_ref, bias_ref, x0_ref, e_ref, u_ref, pad_ref, *, seq):
    _fill_padded(pad_ref, z_ref[0], seq)
    zc = _dwconv(pad_ref, cw_ref, seq, HY_SHORT)
    x0 = zc[:, :HY_WIDTH]
    u = zc[:, HY_WIDTH:2 * HY_WIDTH] * zc[:, 2 * HY_WIDTH:]
    x0_ref[...] = x0
    e_ref[...] = x0 * (u * bias_ref[...])
    u_ref[...] = u.astype(BF16)


def _hy_fwd_kernel(u_ref, wt_ref, wb_ref, ft_ref, fb_ref, yt_ref, yb_ref, *, nb):
    ft, fb = ft_ref[...], fb_ref[...]
    first = pl.program_id(0) == 0
    row0 = jnp.logical_and(lax.broadcasted_iota(jnp.int32, ft.shape, 0) == 0, first)
    f_im = jnp.where(row0, 0.0, fb)
    f_re2 = jnp.where(row0, fb, ft)
    for b in range(nb):
        cols = slice(b * HY_WIDTH, (b + 1) * HY_WIDTH)
        ub = u_ref[:, cols]
        pr = _dot(wt_ref[...], ub)
        pi = _dot(wb_ref[...], ub)
        yt_ref[:, cols] = (pr * ft - pi * f_im).astype(BF16)
        yb_ref[:, cols] = (pr * f_im + pi * f_re2).astype(BF16)


def _hy_inv_kernel(yt_ref, yb_ref, wt_ref, wb_ref, x0_ref, e_ref, o_ref, *, nb):
    for b in range(nb):
        cols = slice(b * HY_WIDTH, (b + 1) * HY_WIDTH)
        y = _dot(wt_ref[...], yt_ref[:, cols]) + _dot(wb_ref[...], yb_ref[:, cols])
        o_ref[:, cols] = x0_ref[:, cols] * y + e_ref[:, cols]


def _hy_conv(z_a, p, filt, fwd, inv):
    bsz, seq, _ = z_a.shape
    wide = bsz * HY_WIDTH
    bcol = pl.BlockSpec((seq, HY_WIDTH), lambda b: (0, b))
    x0, e, u = pl.pallas_call(
        functools.partial(_hy_pre_kernel, seq=seq),
        grid=(bsz,),
        in_specs=[pl.BlockSpec((1, seq, HY_COLS), lambda b: (b, 0, 0)),
                  pl.BlockSpec((HY_SHORT, HY_COLS), lambda b: (0, 0)),
                  pl.BlockSpec((1, HY_WIDTH), lambda b: (0, 0))],
        out_specs=[bcol, bcol, bcol],
        out_shape=[jax.ShapeDtypeStruct((seq, wide), F32), jax.ShapeDtypeStruct((seq, wide), F32),
                   jax.ShapeDtypeStruct((seq, wide), BF16)],
        scratch_shapes=[pltpu.VMEM((seq + 2 * PAD, HY_COLS), F32)],
        compiler_params=_cparams(("parallel",)),
        name="hy_pre",
    )(z_a, p['hy_conv'], p['hy_bias'][None])

    tk = min(seq, 512)
    kt = seq // tk
    yt, yb = pl.pallas_call(
        functools.partial(_hy_fwd_kernel, nb=bsz),
        grid=(kt,),
        in_specs=[_const_spec((seq, wide)),
                  pl.BlockSpec((tk, seq), lambda i: (i, 0)),
                  pl.BlockSpec((tk, seq), lambda i: (kt + i, 0)),
                  pl.BlockSpec((tk, HY_WIDTH), lambda i: (i, 0)),
                  pl.BlockSpec((tk, HY_WIDTH), lambda i: (kt + i, 0))],
        out_specs=[pl.BlockSpec((tk, wide), lambda i: (i, 0)), pl.BlockSpec((tk, wide), lambda i: (i, 0))],
        out_shape=[jax.ShapeDtypeStruct((seq, wide), BF16), jax.ShapeDtypeStruct((seq, wide), BF16)],
        compiler_params=_cparams(("parallel",)),
        name="hy_fwd",
    )(u, fwd, fwd, filt, filt)

    tt = min(seq, 256)
    return pl.pallas_call(
        functools.partial(_hy_inv_kernel, nb=bsz),
        grid=(seq // tt,),
        in_specs=[_const_spec((seq, wide)), _const_spec((seq, wide)),
                  pl.BlockSpec((tt, seq), lambda i: (i, 0)),
                  pl.BlockSpec((tt, seq), lambda i: (i, 1)),
                  pl.BlockSpec((tt, wide), lambda i: (i, 0)),
                  pl.BlockSpec((tt, wide), lambda i: (i, 0))],
        out_specs=pl.BlockSpec((tt, wide), lambda i: (i, 0)),
        out_shape=jax.ShapeDtypeStruct((seq, wide), F32),
        compiler_params=_cparams(("parallel",)),
        name="hy_inv",
    )(yt, yb, inv, inv, x0, e)


def _ml_gate_kernel(x_ref, b_ref, low_ref, up_ref, o_ref):
    o_ref[0] = x_ref[0] + b_ref[0]
    o_ref[2] = x_ref[2] + b_ref[2]
    for plane, tri_ref in ((1, low_ref), (3, up_ref)):
        pre = x_ref[plane] + b_ref[plane]
        lf = jnp.minimum(pre, 0.0) - jnp.log(1.0 + jnp.exp(-jnp.abs(pre)))
        hi, mid, lo = _split3(lf)
        tri = tri_ref[...]
        o_ref[plane] = _dot(tri, hi) + _dot(tri, mid) + _dot(tri, lo)


def _ml_gates(gates, gate_b):
    bsz, seq, _ = gates.shape
    n = seq // ML_CHUNK
    m = bsz * ML_HEADS * n
    x = gates.reshape(bsz, n, ML_CHUNK, 4, ML_HEADS).transpose(3, 2, 0, 4, 1).reshape(4, ML_CHUNK, m)
    bias = jnp.broadcast_to(gate_b.reshape(4, 1, 1, ML_HEADS, 1), (4, 1, bsz, ML_HEADS, n)).reshape(4, 1, m)
    r = np.arange(ML_CHUNK)
    low = jnp.asarray(r[:, None] >= r[None, :], dtype=BF16)
    up = jnp.asarray(r[:, None] <= r[None, :], dtype=BF16)
    full = lambda shp: pl.BlockSpec(shp, lambda: (0,) * len(shp))
    out = pl.pallas_call(
        _ml_gate_kernel,
        in_specs=[full((4, ML_CHUNK, m)), full((4, 1, m)), full((ML_CHUNK, ML_CHUNK)),
                  full((ML_CHUNK, ML_CHUNK))],
        out_specs=full((4, ML_CHUNK, m)),
        out_shape=jax.ShapeDtypeStruct((4, ML_CHUNK, m), F32),
        compiler_params=pltpu.CompilerParams(vmem_limit_bytes=VMEM_LIMIT),
        name="ml_gates",
    )(x, bias, low, up)
    out = out.reshape(4, ML_CHUNK, bsz, ML_HEADS, n)
    return out.transpose(2, 3, 4, 1, 0), out.transpose(2, 3, 4, 0, 1)


def _ml_unit(d, n, first, q_scr, k_scr, v_scr, col_ref, row_ref, h_ref, c_ref, n_ref, m_ref):
    sl = pl.ds(pl.multiple_of(n * ML_CHUNK, ML_CHUNK), ML_CHUNK)
    q = q_scr[sl, :]
    k = k_scr[sl, :]
    v = v_scr[sl, :]
    colv = col_ref[0, 0, n]
    rowv = row_ref[0, 0, n]
    li_c, b_c = colv[:, 2 * d:2 * d + 1], colv[:, 2 * d + 1:2 * d + 2]
    li_r, b_r = rowv[2 * d:2 * d + 1, :], rowv[2 * d + 1:2 * d + 2, :]
    b_all = b_r[:, ML_CHUNK - 1:ML_CHUNK] if d == 0 else b_r[:, 0:1]
    c0 = c_ref[0, d, 0]
    n0 = n_ref[0, d, 0]
    m0 = m_ref[0, d, 0][:, 0:1]

    tt = lax.broadcasted_iota(jnp.int32, (ML_CHUNK, ML_CHUNK), 0)
    ss = lax.broadcasted_iota(jnp.int32, (ML_CHUNK, ML_CHUNK), 1)
    seen = (ss <= tt) if d == 0 else (ss >= tt)
    dmat = jnp.where(seen, b_c - b_r + li_r, -jnp.inf)
    inter = b_c + m0
    mt = jnp.maximum(inter, jnp.max(dmat, axis=1, keepdims=True))
    s = _dot_nt(q, k) * jnp.exp(dmat - mt)
    wi = jnp.exp(inter - mt)
    num = _dot(s.astype(BF16), v) + wi * _dot_nt(q, c0.astype(BF16))
    qf = q.astype(F32)
    den = jnp.sum(s, axis=1, keepdims=True) + wi * jnp.sum(qf * n0, axis=1, keepdims=True)
    h = num / jnp.maximum(jnp.abs(den), jnp.exp(-mt))
    if first:
        h_ref[0, sl, :] = h
    else:
        h_ref[0, sl, :] = h_ref[0, sl, :] + h

    g_r = b_all - b_r + li_r
    g_c = b_all - b_c + li_c
    m_loc = jnp.max(g_r, axis=1, keepdims=True)
    w_c = jnp.exp(g_c - m_loc)
    kf = k.astype(F32)
    c_loc = _dot_tn((v.astype(F32) * w_c).astype(BF16), k)
    n_loc = jnp.sum(kf * w_c, axis=0, keepdims=True)
    m_new = jnp.maximum(b_all + m0, m_loc)
    a = jnp.exp(b_all + m0 - m_new)
    sc = jnp.exp(m_loc - m_new)
    c_ref[0, d, 0] = a * c0 + sc * c_loc
    n_ref[0, d, 0] = a * n0 + sc * n_loc
    m_ref[0, d, 0] = jnp.broadcast_to(m_new, (1, ML_HEAD_DIM))


def _ml_kernel(zq_ref, zk_ref, zv_ref, cwq_ref, cwk_ref, col_ref, row_ref, c0_ref, n0_ref, m0_ref,
               h_ref, c_ref, n_ref, m_ref, pad_ref, q_scr, k_scr, v_scr, *, seq):
    nchunk = seq // ML_CHUNK
    _fill_padded(pad_ref, zq_ref[0], seq)
    cq = _dwconv(pad_ref, cwq_ref, seq, ML_CONV)
    q_scr[...] = (cq * _sigmoid(cq)).astype(BF16)
    pad_ref[PAD:PAD + seq, :] = zk_ref[0]
    ck = _dwconv(pad_ref, cwk_ref, seq, ML_CONV)
    k_scr[...] = (ck * _sigmoid(ck) * (ML_HEAD_DIM ** -0.5)).astype(BF16)
    v_scr[...] = zv_ref[0].astype(BF16)
    c_ref[...] = c0_ref[...]
    n_ref[...] = n0_ref[...]
    m_ref[...] = m0_ref[...]

    def step(first):
        def body(i, carry):
            args = (q_scr, k_scr, v_scr, col_ref, row_ref, h_ref, c_ref, n_ref, m_ref)
            _ml_unit(0, i, first, *args)
            _ml_unit(1, nchunk - 1 - i, first, *args)
            return carry
        return body

    lax.fori_loop(0, nchunk // 2, step(True), 0)
    lax.fori_loop(nchunk // 2, nchunk, step(False), 0)


def _mlstm(z_ml, p, state):
    bsz, seq, _ = z_ml.shape
    nchunk = seq // ML_CHUNK
    col, row = _ml_gates(z_ml[..., 3 * ML_WIDTH:3 * ML_WIDTH + ML_GATE_COLS], p['ml_gate_b'])
    dh, nh = ML_HEAD_DIM, ML_HEADS
    zspec = lambda off: pl.BlockSpec((1, seq, dh), lambda b, h: (b, 0, off + h))
    cspec = pl.BlockSpec((1, 2, 1, dh, dh), lambda b, h: (b, 0, h, 0, 0))
    vspec = pl.BlockSpec((1, 2, 1, 1, dh), lambda b, h: (b, 0, h, 0, 0))
    return pl.pallas_call(
        functools.partial(_ml_kernel, seq=seq),
        grid=(bsz, nh),
        in_specs=[zspec(0), zspec(nh), zspec(2 * nh),
                  pl.BlockSpec((ML_CONV, dh), lambda b, h: (0, h)),
                  pl.BlockSpec((ML_CONV, dh), lambda b, h: (0, nh + h)),
                  pl.BlockSpec((1, 1, nchunk, ML_CHUNK, 4), lambda b, h: (b, h, 0, 0, 0)),
                  pl.BlockSpec((1, 1, nchunk, 4, ML_CHUNK), lambda b, h: (b, h, 0, 0, 0)),
                  cspec, vspec, vspec],
        out_specs=[pl.BlockSpec((1, seq, dh), lambda b, h: (b, 0, h)), cspec, vspec, vspec],
        out_shape=[jax.ShapeDtypeStruct((bsz, seq, ML_WIDTH), F32),
                   jax.ShapeDtypeStruct((bsz, 2, nh, dh, dh), F32),
                   jax.ShapeDtypeStruct((bsz, 2, nh, 1, dh), F32),
                   jax.ShapeDtypeStruct((bsz, 2, nh, 1, dh), F32)],
        scratch_shapes=[pltpu.VMEM((seq + 2 * PAD, dh), F32),
                        pltpu.VMEM((seq, dh), BF16), pltpu.VMEM((seq, dh), BF16),
                        pltpu.VMEM((seq, dh), BF16)],
        compiler_params=_cparams(("parallel", "parallel")),
        name="mlstm",
    )(z_ml, z_ml, z_ml, p['ml_conv'], p['ml_conv'], col, row, *state)


def _rg_scan_block(a, b, carry, reverse):
    row = lax.broadcasted_iota(jnp.int32, a.shape, 0)
    for sh in (1, 2, 4):
        if reverse:
            a_sh = pltpu.roll(a, SUBLANES - sh, 0)
            b_sh = pltpu.roll(b, SUBLANES - sh, 0)
            ok = row < SUBLANES - sh
        else:
            a_sh = pltpu.roll(a, sh, 0)
            b_sh = pltpu.roll(b, sh, 0)
            ok = row >= sh
        b = jnp.where(ok, a * b_sh + b, b)
        a = jnp.where(ok, a * a_sh, a)
    h = a * carry + b
    new_carry = h[0:1, :] if reverse else h[SUBLANES - 1:SUBLANES, :]
    return h, new_carry


def _rg_kernel(z_ref, cw_ref, wg_ref, bg_ref, lam_ref, h0_ref, o_ref, st_ref,
               pad_ref, a_scr, b_scr, *, seq):
    _fill_padded(pad_ref, z_ref[0], seq)
    xr = _dwconv(pad_ref, cw_ref, seq, RG_CONV)
    pre = _dot(xr.astype(BF16), wg_ref[...]) + bg_ref[...]
    lam = lam_ref[...]
    sp = jnp.maximum(-lam, 0.0) + jnp.log(1.0 + jnp.exp(-jnp.abs(lam)))
    for d in range(2):
        r = _sigmoid(pre[:, (2 * d) * RG_WIDTH:(2 * d + 1) * RG_WIDTH])
        gi = _sigmoid(pre[:, (2 * d + 1) * RG_WIDTH:(2 * d + 2) * RG_WIDTH])
        log_a = (-RG_C) * r * sp[d:d + 1, :]
        a = jnp.exp(log_a)
        a_scr[d] = a
        b_scr[d] = jnp.sqrt(-jnp.tanh(log_a) * (a * a + 1.0)) * (gi * xr)

    nblk = seq // SUBLANES

    def body(i, carry):
        cf, cb = carry
        sf = pl.ds(pl.multiple_of(i * SUBLANES, SUBLANES), SUBLANES)
        sb = pl.ds(pl.multiple_of((nblk - 1 - i) * SUBLANES, SUBLANES), SUBLANES)
        hf, cf = _rg_scan_block(a_scr[0, sf, :], b_scr[0, sf, :], cf, False)
        hb, cb = _rg_scan_block(a_scr[1, sb, :], b_scr[1, sb, :], cb, True)
        b_scr[0, sf, :] = hf
        b_scr[1, sb, :] = hb
        return cf, cb

    cf, cb = lax.fori_loop(0, nblk, body, (h0_ref[0, 0:1, :], h0_ref[0, 1:2, :]))
    st_ref[0, 0:1, :] = cf
    st_ref[0, 1:2, :] = cb
    o_ref[0] = b_scr[0] + b_scr[1]


def _rg_gate_weights(p):
    w = p['rg_gate_w']
    eye = jnp.eye(RG_HEADS, dtype=w.dtype)
    full = jnp.einsum('dghij,hk->dghikj', w, eye)
    full = full.reshape(2, 2, RG_WIDTH, RG_WIDTH)
    wg = full.transpose(2, 0, 1, 3).reshape(RG_WIDTH, 4 * RG_WIDTH).astype(BF16)
    bg = p['rg_gate_b'].reshape(1, 4 * RG_WIDTH)
    return wg, bg


def _rglru(z_a, p, h0):
    bsz, seq, _ = z_a.shape
    c = RG_WIDTH
    wg, bg = _rg_gate_weights(p)
    return pl.pallas_call(
        functools.partial(_rg_kernel, seq=seq),
        grid=(bsz,),
        in_specs=[pl.BlockSpec((1, seq, c), lambda b: (b, 0, HY_COLS // c)),
                  pl.BlockSpec((RG_CONV, c), lambda b: (0, 0)),
                  pl.BlockSpec((c, 4 * c), lambda b: (0, 0)),
                  pl.BlockSpec((1, 4 * c), lambda b: (0, 0)),
                  pl.BlockSpec((2, c), lambda b: (0, 0)),
                  pl.BlockSpec((1, 2, c), lambda b: (b, 0, 0))],
        out_specs=[pl.BlockSpec((1, seq, c), lambda b: (b, 0, 0)),
                   pl.BlockSpec((1, 2, c), lambda b: (b, 0, 0))],
        out_shape=[jax.ShapeDtypeStruct((bsz, seq, c), F32),
                   jax.ShapeDtypeStruct((bsz, 2, c), F32)],
        scratch_shapes=[pltpu.VMEM((seq + 2 * PAD, c), F32),
                        pltpu.VMEM((2, seq, c), F32), pltpu.VMEM((2, seq, c), F32)],
        compiler_params=_cparams(("parallel",)),
        name="rglru",
    )(z_a, p['rg_conv'], wg, bg, p['rg_lambda'], h0)


def _group_mean_sq(y, m_ref):
    sq = y * y
    hi = sq.astype(BF16)
    lo = (sq - hi.astype(F32)).astype(BF16)
    return _dot(hi, m_ref[...]) + _dot(lo, m_ref[...])


def _merge_kernel(x_ref, yhy_ref, hm_ref, o_ref, hr_ref, gate_ref, g_ref, m64_ref, wo_ref, gt_ref,
                  out_ref):
    g = g_ref[...]
    yhy = yhy_ref[...]
    yhy = yhy * lax.rsqrt(_group_mean_sq(yhy, m64_ref) + EPS) * g[:, :HY_WIDTH]
    parts = [yhy.astype(BF16)]
    hm = hm_ref[0]
    o_gate = _sigmoid(o_ref[0])
    for h in range(ML_HEADS):
        sl = slice(h * ML_HEAD_DIM, (h + 1) * ML_HEAD_DIM)
        hh = hm[:, sl]
        ms = jnp.mean(hh * hh, axis=-1, keepdims=True)
        yh = hh * lax.rsqrt(ms + EPS) * g[:, HY_WIDTH + h * ML_HEAD_DIM:HY_WIDTH + (h + 1) * ML_HEAD_DIM]
        parts.append((yh * o_gate[:, sl]).astype(BF16))
    gp = gate_ref[0]
    gelu = 0.5 * gp * (1.0 + jnp.tanh(math.sqrt(2.0 / math.pi) * (gp + 0.044715 * (gp * gp * gp))))
    yrg = hr_ref[0] * gelu
    yrg = yrg * lax.rsqrt(_group_mean_sq(yrg, m64_ref) + EPS) * g[:, HY_WIDTH + ML_WIDTH:]
    parts.append(yrg.astype(BF16))
    y = jnp.concatenate(parts, axis=-1)
    out_ref[0] = x_ref[0] + gt_ref[0] * _dot(y, wo_ref[...])


def _merge_out(x, y_hy, hm, z_a, hr, mix_g, w_out, gate, tm):
    bsz, seq, d = x.shape
    r = np.arange(HY_WIDTH) // (HY_WIDTH // HY_GROUPS)
    m64 = jnp.asarray((r[:, None] == r[None, :]) / float(HY_WIDTH // HY_GROUPS), dtype=BF16)
    tok = lambda w, j: pl.BlockSpec((1, tm, w), lambda b, i: (b, i, j))
    return pl.pallas_call(
        _merge_kernel,
        grid=(bsz, seq // tm),
        in_specs=[tok(d, 0),
                  pl.BlockSpec((tm, HY_WIDTH), lambda b, i: (i, b)),
                  tok(ML_WIDTH, 0),
                  tok(ML_WIDTH, 2),
                  tok(RG_WIDTH, 0),
                  tok(RG_WIDTH, 6),
                  pl.BlockSpec((1, d), lambda b, i: (0, 0)),
                  pl.BlockSpec((HY_WIDTH, HY_WIDTH), lambda b, i: (0, 0)),
                  _const_spec((d, d)), _mod_spec(gate)],
        out_specs=tok(d, 0),
        out_shape=jax.ShapeDtypeStruct((bsz, seq, d), F32),
        compiler_params=_cparams(("parallel", "parallel")),
        name="merge_out",
    )(x, y_hy, hm, z_a, hr, z_a, mix_g, m64, w_out, gate)


def _mlp_kernel(x_ref, sh_ref, sc_ref, gt_ref, g_ref, w1_ref, w2_ref, fg_ref, o_ref, *, final, fc):
    x = x_ref[0]
    h = _modulated_norm(x, g_ref[...], sh_ref[0], sc_ref[0]).astype(BF16)
    acc = None
    for j in range(D_FF // fc):
        a = jnp.maximum(_dot(h, w1_ref[:, j * fc:(j + 1) * fc]), 0.0)
        t = _dot((a * a).astype(BF16), w2_ref[j * fc:(j + 1) * fc, :])
        acc = t if acc is None else acc + t
    y = x + gt_ref[0] * acc
    if final:
        ms = jnp.mean(y * y, axis=-1, keepdims=True)
        y = y * lax.rsqrt(ms + EPS) * fg_ref[...]
    o_ref[0] = y


def _mlp(x, shift, scale, gate, g, w1, w2, final_g, final, tm):
    bsz, seq, d = x.shape
    return pl.pallas_call(
        functools.partial(_mlp_kernel, final=final, fc=1024),
        grid=(bsz, seq // tm),
        in_specs=[pl.BlockSpec((1, tm, d), lambda b, i: (b, i, 0)),
                  _mod_spec(shift), _mod_spec(scale), _mod_spec(gate),
                  pl.BlockSpec((1, d), lambda b, i: (0, 0)),
                  _const_spec((d, D_FF)), _const_spec((D_FF, d)),
                  pl.BlockSpec((1, d), lambda b, i: (0, 0))],
        out_specs=pl.BlockSpec((1, tm, d), lambda b, i: (b, i, 0)),
        out_shape=jax.ShapeDtypeStruct((bsz, seq, d), F32),
        compiler_params=_cparams(("parallel", "parallel")),
        name="mlp",
    )(x, shift, scale, gate, g, w1, w2, final_g)


def _to_col_major(a, rows):
    bsz, seq, c = a.shape
    return a.reshape(bsz, rows, GRID_W, c).transpose(0, 2, 1, 3).reshape(bsz, seq, c)


def _from_col_major(a, rows):
    bsz, seq, c = a.shape
    return a.reshape(bsz, GRID_W, rows, c).transpose(0, 2, 1, 3).reshape(bsz, seq, c)


def _split_w_in(w_in):
    rg = OFF_RG
    o0 = OFF_ML + 3 * ML_WIDTH
    w_a = jnp.concatenate([w_in[:, :HY_COLS], w_in[:, rg:rg + RG_WIDTH],
                           w_in[:, o0:o0 + ML_WIDTH], w_in[:, rg + RG_WIDTH:rg + 2 * RG_WIDTH]], axis=1)
    g0 = OFF_ML + 4 * ML_WIDTH
    w_m = jnp.concatenate([w_in[:, OFF_ML:o0], w_in[:, g0:g0 + ML_GATE_COLS],
                           jnp.zeros((w_in.shape[0], ZM_COLS - 3 * ML_WIDTH - ML_GATE_COLS), w_in.dtype)],
                          axis=1)
    return w_a.astype(BF16), w_m.astype(BF16)


def kernel(x, c, ctx, c_ctx, ada_w, ada_b, norm1_g, norm2_g, w_in, hy_conv, hy_w1, hy_b1, hy_w2, hy_b2, hy_w3, hy_b3, hy_freq, hy_bias, ml_conv, ml_gate_b, rg_conv, rg_gate_w, rg_gate_b, rg_lambda, mix_norm_g, w_out, mlp_w1, mlp_w2, final_g):
    bsz, seq, d = x.shape
    clen = ctx.shape[1]
    rows = seq // GRID_W
    stacked = dict(hy_conv=hy_conv, hy_w1=hy_w1, hy_b1=hy_b1, hy_w2=hy_w2, hy_b2=hy_b2, hy_w3=hy_w3,
                   hy_b3=hy_b3, hy_freq=hy_freq, hy_bias=hy_bias, ml_conv=ml_conv, ml_gate_b=ml_gate_b,
                   rg_conv=rg_conv, rg_gate_w=rg_gate_w, rg_gate_b=rg_gate_b, rg_lambda=rg_lambda)

    nrow = 2 * SUBLANES
    cc = jnp.concatenate([c, c_ctx[None], jnp.zeros((nrow - bsz - 1, d), F32)], axis=0)
    mods = _ada(cc, ada_w, ada_b)

    fwd_l, inv_l = _dft_tables(seq)
    fwd_c, inv_c = _dft_tables(clen)
    dh, nh = ML_HEAD_DIM, ML_HEADS
    ml_zero = (jnp.zeros((bsz, 2, nh, dh, dh), F32), jnp.zeros((bsz, 2, nh, 1, dh), F32),
               jnp.zeros((bsz, 2, nh, 1, dh), F32))
    rg_zero = jnp.zeros((bsz, 2, RG_WIDTH), F32)

    ctx_s = ctx
    for l in range(DEPTH):
        need_ctx = l < DEPTH - 1
        p = {k_: v_[l] for k_, v_ in stacked.items()}
        mod = mods[l, :bsz].reshape(bsz, N_MOD, 1, d)
        modc = mods[l, bsz].reshape(N_MOD, 1, 1, d)
        m_lat = [mod[:, i] for i in range(N_MOD)]
        m_ctx = [modc[i] for i in range(N_MOD)]
        g1 = norm1_g[l][None]
        g2 = norm2_g[l][None]
        w_a, w_m = _split_w_in(w_in[l])
        w_o = w_out[l].astype(BF16)
        w1 = mlp_w1[l].astype(BF16)
        w2 = mlp_w2[l].astype(BF16)
        mix_g = mix_norm_g[l][None]

        zc_a = _inproj(ctx_s, m_ctx[0], m_ctx[1], g1, w_a, clen)
        zc_m = _inproj(ctx_s, m_ctx[0], m_ctx[1], g1, w_m, clen)
        hm_c, ml_c, ml_n, ml_m = _mlstm(zc_m, p, ml_zero)
        hr_c, rg_state = _rglru(zc_a, p, rg_zero)

        z_a = _inproj(x, m_lat[0], m_lat[1], g1, w_a, 512)
        z_m = _inproj(_to_col_major(x, rows), m_lat[0], m_lat[1], g1, w_m, 512)
        y_hy = _hy_conv(z_a, p, _hy_filter(seq, p, fwd_l), fwd_l, inv_l)
        hm, _, _, _ = _mlstm(z_m, p, (ml_c, ml_n, ml_m))
        hm = _from_col_major(hm, rows)
        hr, _ = _rglru(z_a, p, rg_state)
        x = _merge_out(x, y_hy, hm, z_a, hr, mix_g, w_o, m_lat[2], 512)
        x = _mlp(x, m_lat[3], m_lat[4], m_lat[5], g2, w1, w2, final_g[None],
                 final=(l == DEPTH - 1), tm=512)

        if need_ctx:
            y_hy_c = _hy_conv(zc_a, p, _hy_filter(clen, p, fwd_c), fwd_c, inv_c)
            ctx_s = _merge_out(ctx_s, y_hy_c, hm_c, zc_a, hr_c, mix_g, w_o, m_ctx[2], clen)
            ctx_s = _mlp(ctx_s, m_ctx[3], m_ctx[4], m_ctx[5], g2, w1, w2, final_g[None],
                         final=False, tm=clen)
    return x
```

```python
import functools
import math

import numpy as np
import jax
import jax.numpy as jnp
from jax import lax
from jax.experimental import pallas as pl
from jax.experimental.pallas import tpu as pltpu

F32 = jnp.float32
BF16 = jnp.bfloat16

D_MODEL = 1024
DEPTH = 2
GRID_W = 64
HY_WIDTH = 256
HY_GROUPS = 4
HY_SHORT = 3
HY_POS_DIM = 33
HY_FFN = 64
HY_FAST_DECAY = 0.3
HY_SLOW_DECAY = 1.5
HY_TARGET = 1e-2
ML_WIDTH = 512
ML_HEADS = 4
ML_HEAD_DIM = 128
ML_CONV = 4
ML_CHUNK = 128
RG_WIDTH = 256
RG_HEADS = 4
RG_BLOCK = 64
RG_CONV = 4
RG_C = 8.0
D_FF = 4096
N_MOD = 6
EPS = 1e-6
HY_COLS = 3 * HY_WIDTH
ML_GATE_COLS = 4 * ML_HEADS
ML_COLS = 4 * ML_WIDTH + ML_GATE_COLS
OFF_ML = HY_COLS
OFF_RG = HY_COLS + ML_COLS

LANES = 128
SUBLANES = 8
VMEM_LIMIT = 56 * 1024 * 1024

ZA_COLS = 1792
ZM_COLS = 1664
PAD = SUBLANES


def _cparams(sem):
    return pltpu.CompilerParams(dimension_semantics=sem, vmem_limit_bytes=VMEM_LIMIT)


def _const_spec(shape):
    nd = len(shape)
    return pl.BlockSpec(shape, lambda *_: (0,) * nd, pipeline_mode=pl.Buffered(1))


def _split3(v):
    hi = v.astype(BF16)
    r = v - hi.astype(F32)
    mid = r.astype(BF16)
    lo = (r - mid.astype(F32)).astype(BF16)
    return hi, mid, lo


def _dot(a, b):
    return jnp.dot(a, b, preferred_element_type=F32)


def _dot_f32(a, b):
    ah, am, al = _split3(a)
    bh, bm, bl = _split3(b)
    return (_dot(ah, bh) + (_dot(ah, bm) + _dot(am, bh))
            + (_dot(ah, bl) + _dot(am, bm) + _dot(al, bh)))


def _dot_2x(a, b):
    ah = a.astype(BF16)
    al = (a - ah.astype(F32)).astype(BF16)
    bh = b.astype(BF16)
    bl = (b - bh.astype(F32)).astype(BF16)
    return _dot(ah, bh) + (_dot(ah, bl) + _dot(al, bh))


def _sigmoid(v):
    return 1.0 / (1.0 + jnp.exp(-v))


def _modulated_norm(x, g, shift, scale):
    ms = jnp.mean(x * x, axis=-1, keepdims=True)
    return (x * lax.rsqrt(ms + EPS) * g) * (1.0 + scale) + shift


def _dwconv(pad_ref, w_ref, seq, ksize):
    left = (ksize - 1) // 2
    acc = None
    for j in range(ksize):
        term = w_ref[j:j + 1, :] * pad_ref[PAD + j - left:PAD + j - left + seq, :]
        acc = term if acc is None else acc + term
    return acc


def _fill_padded(pad_ref, rows, seq):
    width = pad_ref.shape[1]
    pad_ref[0:PAD, :] = jnp.zeros((PAD, width), F32)
    pad_ref[PAD + seq:2 * PAD + seq, :] = jnp.zeros((PAD, width), F32)
    pad_ref[PAD:PAD + seq, :] = rows


def _ada_kernel(c_ref, w_ref, b_ref, o_ref):
    c = c_ref[...]
    s = c * _sigmoid(c)
    o_ref[0] = _dot_2x(s, w_ref[0]) + b_ref[0]


def _ada(cc, ada_w, ada_b):
    depth, d, n = ada_w.shape
    tn = 1536
    return pl.pallas_call(
        _ada_kernel,
        grid=(depth, n // tn),
        in_specs=[pl.BlockSpec(cc.shape, lambda l, j: (0, 0)),
                  pl.BlockSpec((1, d, tn), lambda l, j: (l, 0, j)),
                  pl.BlockSpec((1, 1, tn), lambda l, j: (l, 0, j))],
        out_specs=pl.BlockSpec((1, cc.shape[0], tn), lambda l, j: (l, 0, j)),
        out_shape=jax.ShapeDtypeStruct((depth, cc.shape[0], n), F32),
        compiler_params=_cparams(("parallel", "parallel")),
        name="ada",
    )(cc, ada_w, ada_b.reshape(depth, 1, n))


def _inproj_kernel(x_ref, sh_ref, sc_ref, g_ref, w_ref, o_ref):
    h = _modulated_norm(x_ref[0], g_ref[...], sh_ref[0], sc_ref[0])
    o_ref[0] = _dot(h.astype(BF16), w_ref[...])


def _mod_spec(arr):
    if arr.shape[0] == 1:
        return pl.BlockSpec((1, 1, arr.shape[2]), lambda b, i: (0, 0, 0))
    return pl.BlockSpec((1, 1, arr.shape[2]), lambda b, i: (b, 0, 0))


def _inproj(x, shift, scale, g, w, tm):
    bsz, seq, d = x.shape
    p = w.shape[1]
    return pl.pallas_call(
        _inproj_kernel,
        grid=(bsz, seq // tm),
        in_specs=[pl.BlockSpec((1, tm, d), lambda b, i: (b, i, 0)),
                  _mod_spec(shift), _mod_spec(scale),
                  pl.BlockSpec((1, d), lambda b, i: (0, 0)),
                  _const_spec((d, p))],
        out_specs=pl.BlockSpec((1, tm, p), lambda b, i: (b, i, 0)),
        out_shape=jax.ShapeDtypeStruct((bsz, seq, p), F32),
        compiler_params=_cparams(("parallel", "parallel")),
        name="inproj",
    )(x, shift, scale, g, w)


def _inproj_cm_kernel(x_ref, sh_ref, sc_ref, g_ref, w_ref, o_ref, h_scr, *, rows, ncol):
    for c in range(ncol):
        h = _modulated_norm(x_ref[0, :, c, :], g_ref[...], sh_ref[0], sc_ref[0])
        h_scr[c * rows:(c + 1) * rows, :] = h.astype(BF16)
    o_ref[0] = _dot(h_scr[...], w_ref[...])


def _inproj_col_major(x, shift, scale, g, w, ncol):
    bsz, seq, d = x.shape
    rows = seq // GRID_W
    p = w.shape[1]
    tm = rows * ncol
    return pl.pallas_call(
        functools.partial(_inproj_cm_kernel, rows=rows, ncol=ncol),
        grid=(bsz, GRID_W // ncol),
        in_specs=[pl.BlockSpec((1, rows, ncol, d), lambda b, i: (b, 0, i, 0)),
                  _mod_spec(shift), _mod_spec(scale),
                  pl.BlockSpec((1, d), lambda b, i: (0, 0)),
                  _const_spec((d, p))],
        out_specs=pl.BlockSpec((1, tm, p), lambda b, i: (b, i, 0)),
        out_shape=jax.ShapeDtypeStruct((bsz, seq, p), F32),
        scratch_shapes=[pltpu.VMEM((tm, d), BF16)],
        compiler_params=_cparams(("parallel", "parallel")),
        name="inproj_cm",
    )(x.reshape(bsz, rows, GRID_W, d), shift, scale, g, w)


def _dft_tables(seq):
    n = 2 * seq
    k = np.arange(seq, dtype=np.int64)[:, None]
    s = np.arange(seq, dtype=np.int64)[None, :]
    ang = (2.0 * np.pi / n) * ((k * s) % n).astype(np.float64)
    top = np.cos(ang)
    bot = -np.sin(ang)
    bot[0, :] = np.where(np.arange(seq) % 2 == 0, 1.0, -1.0)
    fwd = np.concatenate([top, bot], axis=0)
    scale = np.full((n, 1), 2.0 / n)
    scale[0, 0] = 1.0 / n
    scale[seq, 0] = 1.0 / n
    inv = (fwd * scale).T
    return (jnp.asarray(fwd.astype(np.float32)).astype(BF16),
            jnp.asarray(inv.astype(np.float32)).astype(BF16))


def _hy_feature_tables(seq):
    pos = np.arange(seq, dtype=np.float64)[:, None]
    t = pos / max(seq - 1, 1)
    bands = (HY_POS_DIM - 1) // 2
    fr = np.linspace(1e-4, bands - 1, bands).astype(np.float32).astype(np.float64)[None]
    ang = (2.0 * math.pi / seq) * fr * pos
    feats = np.concatenate([t, np.cos(ang), -np.sin(ang)], axis=-1)
    feats = np.pad(feats, ((0, 0), (0, LANES - HY_POS_DIM)))
    max_decay = math.log(HY_TARGET) / HY_FAST_DECAY
    min_decay = math.log(HY_TARGET) / HY_SLOW_DECAY
    deltas = np.linspace(min_decay, max_decay, HY_WIDTH).astype(np.float32).astype(np.float64)
    decay = np.exp(-t * np.abs(deltas)[None, :])
    return jnp.asarray(feats, dtype=F32), jnp.asarray(decay, dtype=F32)


def _hy_filter_kernel(feats_ref, w1_ref, b1_ref, w2_ref, b2_ref, w3_ref, b3_ref, fq_ref,
                      decay_ref, wf_ref, f_ref, h_scr, *, seq, tr):
    i = pl.program_id(0)

    @pl.when(i == 0)
    def _():
        h = jnp.sin(fq_ref[0:1, :] * (_dot_f32(feats_ref[...], w1_ref[...]) + b1_ref[...]))
        h = jnp.sin(fq_ref[1:2, :] * (_dot_f32(h, w2_ref[...]) + b2_ref[...]))
        h = _dot_f32(h, w3_ref[...]) + b3_ref[...]
        dec = decay_ref[...]
        hf = h[:, :HY_WIDTH] * dec
        hb = h[:, HY_WIDTH:] * dec
        row = lax.broadcasted_iota(jnp.int32, hb.shape, 0)
        hb = jnp.where(row == 0, 0.0, hb)
        h_scr[:, :HY_WIDTH] = hf.astype(BF16)
        h_scr[:, HY_WIDTH:] = hb.astype(BF16)

    g = _dot(wf_ref[...], h_scr[...])
    gf = g[:, :HY_WIDTH]
    gb = g[:, HY_WIDTH:]
    row = i * tr + lax.broadcasted_iota(jnp.int32, gf.shape, 0)
    f_ref[...] = jnp.where(row <= seq, gf + gb, gf - gb)


def _hy_filter(seq, p, fwd):
    feats, decay = _hy_feature_tables(seq)
    tr = min(seq, 512)
    w1 = jnp.pad(p['hy_w1'], ((0, LANES - HY_POS_DIM), (0, 0)))
    full = lambda shp: pl.BlockSpec(shp, lambda i: (0,) * len(shp))
    return pl.pallas_call(
        functools.partial(_hy_filter_kernel, seq=seq, tr=tr),
        grid=(2 * seq // tr,),
        in_specs=[full((seq, LANES)), full((LANES, HY_FFN)), full((1, HY_FFN)),
                  full((HY_FFN, HY_FFN)), full((1, HY_FFN)),
                  full((HY_FFN, 2 * HY_WIDTH)), full((1, 2 * HY_WIDTH)),
                  full((2, HY_FFN)), full((seq, HY_WIDTH)),
                  pl.BlockSpec((tr, seq), lambda i: (i, 0))],
        out_specs=pl.BlockSpec((tr, HY_WIDTH), lambda i: (i, 0)),
        out_shape=jax.ShapeDtypeStruct((2 * seq, HY_WIDTH), F32),
        scratch_shapes=[pltpu.VMEM((seq, 2 * HY_WIDTH), BF16)],
        compiler_params=_cparams(("arbitrary",)),
        name="hy_filter",
    )(feats, w1, p['hy_b1'][None], p['hy_w2'], p['hy_b2'][None], p['hy_w3'], p['hy_b3'][None],
      p['hy_freq'], decay, fwd)


def _hy_pre_kernel(z_ref, cw_ref, bias_ref, x0_ref, e_ref, u_ref, pad_ref, *, seq):
    _fill_padded(pad_ref, z_ref[0], seq)
    zc = _dwconv(pad_ref, cw_ref, seq, HY_SHORT)
    x0 = zc[:, :HY_WIDTH]
    u = zc[:, HY_WIDTH:2 * HY_WIDTH] * zc[:, 2 * HY_WIDTH:]
    x0_ref[...] = x0
    e_ref[...] = x0 * (u * bias_ref[...])
    u_ref[...] = u.astype(BF16)


def _hy_fwd_kernel(u_ref, wt_ref, wb_ref, ft_ref, fb_ref, yt_ref, yb_ref, *, nb):
    ft, fb = ft_ref[...], fb_ref[...]
    first = pl.program_id(0) == 0
    row0 = jnp.logical_and(lax.broadcasted_iota(jnp.int32, ft.shape, 0) == 0, first)
    f_im = jnp.where(row0, 0.0, fb)
    f_re2 = jnp.where(row0, fb, ft)
    for b in range(nb):
        cols = slice(b * HY_WIDTH, (b + 1) * HY_WIDTH)
        ub = u_ref[:, cols]
        pr = _dot(wt_ref[...], ub)
        pi = _dot(wb_ref[...], ub)
        yt_ref[:, cols] = (pr * ft - pi * f_im).astype(BF16)
        yb_ref[:, cols] = (pr * f_im + pi * f_re2).astype(BF16)


def _hy_inv_kernel(yt_ref, yb_ref, wt_ref, wb_ref, x0_ref, e_ref, o_ref, *, nb):
    for b in range(nb):
        cols = slice(b * HY_WIDTH, (b + 1) * HY_WIDTH)
        y = _dot(wt_ref[...], yt_ref[:, cols]) + _dot(wb_ref[...], yb_ref[:, cols])
        o_ref[:, cols] = x0_ref[:, cols] * y + e_ref[:, cols]


def _hy_conv(z_a, p, filt, fwd, inv):
    bsz, seq, _ = z_a.shape
    wide = bsz * HY_WIDTH
    bcol = pl.BlockSpec((seq, HY_WIDTH), lambda b: (0, b))
    x0, e, u = pl.pallas_call(
        functools.partial(_hy_pre_kernel, seq=seq),
        grid=(bsz,),
        in_specs=[pl.BlockSpec((1, seq, HY_COLS), lambda b: (b, 0, 0)),
                  pl.BlockSpec((HY_SHORT, HY_COLS), lambda b: (0, 0)),
                  pl.BlockSpec((1, HY_WIDTH), lambda b: (0, 0))],
        out_specs=[bcol, bcol, bcol],
        out_shape=[jax.ShapeDtypeStruct((seq, wide), F32), jax.ShapeDtypeStruct((seq, wide), F32),
                   jax.ShapeDtypeStruct((seq, wide), BF16)],
        scratch_shapes=[pltpu.VMEM((seq + 2 * PAD, HY_COLS), F32)],
        compiler_params=_cparams(("parallel",)),
        name="hy_pre",
    )(z_a, p['hy_conv'], p['hy_bias'][None])

    tk = min(seq, 512)
    kt = seq // tk
    yt, yb = pl.pallas_call(
        functools.partial(_hy_fwd_kernel, nb=bsz),
        grid=(kt,),
        in_specs=[_const_spec((seq, wide)),
                  pl.BlockSpec((tk, seq), lambda i: (i, 0)),
                  pl.BlockSpec((tk, seq), lambda i: (kt + i, 0)),
                  pl.BlockSpec((tk, HY_WIDTH), lambda i: (i, 0)),
                  pl.BlockSpec((tk, HY_WIDTH), lambda i: (kt + i, 0))],
        out_specs=[pl.BlockSpec((tk, wide), lambda i: (i, 0)), pl.BlockSpec((tk, wide), lambda i: (i, 0))],
        out_shape=[jax.ShapeDtypeStruct((seq, wide), BF16), jax.ShapeDtypeStruct((seq, wide), BF16)],
        compiler_params=_cparams(("parallel",)),
        name="hy_fwd",
    )(u, fwd, fwd, filt, filt)

    tt = min(seq, 256)
    return pl.pallas_call(
        functools.partial(_hy_inv_kernel, nb=bsz),
        grid=(seq // tt,),
        in_specs=[_const_spec((seq, wide)), _const_spec((seq, wide)),
                  pl.BlockSpec((tt, seq), lambda i: (i, 0)),
                  pl.BlockSpec((tt, seq), lambda i: (i, 1)),
                  pl.BlockSpec((tt, wide), lambda i: (i, 0)),
                  pl.BlockSpec((tt, wide), lambda i: (i, 0))],
        out_specs=pl.BlockSpec((tt, wide), lambda i: (i, 0)),
        out_shape=jax.ShapeDtypeStruct((seq, wide), F32),
        compiler_params=_cparams(("parallel",)),
        name="hy_inv",
    )(yt, yb, inv, inv, x0, e)


def _running_max(v, reverse):
    row = lax.broadcasted_iota(jnp.int32, v.shape, 0)
    k = 1
    while k < ML_CHUNK:
        if reverse:
            sh = jnp.where(row < ML_CHUNK - k, pltpu.roll(v, ML_CHUNK - k, 0), -jnp.inf)
        else:
            sh = jnp.where(row >= k, pltpu.roll(v, k, 0), -jnp.inf)
        v = jnp.maximum(v, sh)
        k *= 2
    return v


def _ml_gate_kernel(x_ref, b_ref, low_ref, up_ref, o_ref):
    for d, tri_ref in ((0, low_ref), (1, up_ref)):
        li = x_ref[2 * d] + b_ref[2 * d]
        pre = x_ref[2 * d + 1] + b_ref[2 * d + 1]
        lf = jnp.minimum(pre, 0.0) - jnp.log(1.0 + jnp.exp(-jnp.abs(pre)))
        hi, mid, lo = _split3(lf)
        tri = tri_ref[...]
        cum = _dot(tri, hi) + _dot(tri, mid) + _dot(tri, lo)
        r = li - cum
        o_ref[3 * d] = r
        o_ref[3 * d + 1] = _running_max(r, reverse=(d == 1))
        o_ref[3 * d + 2] = cum


def _ml_gates(gates, gate_b):
    bsz, seq, _ = gates.shape
    n = seq // ML_CHUNK
    m = bsz * ML_HEADS * n
    x = gates.reshape(bsz, n, ML_CHUNK, 4, ML_HEADS).transpose(3, 2, 0, 4, 1).reshape(4, ML_CHUNK, m)
    bias = jnp.broadcast_to(gate_b.reshape(4, 1, 1, ML_HEADS, 1), (4, 1, bsz, ML_HEADS, n)).reshape(4, 1, m)
    r = np.arange(ML_CHUNK)
    low = jnp.asarray(r[:, None] >= r[None, :], dtype=BF16)
    up = jnp.asarray(r[:, None] <= r[None, :], dtype=BF16)
    full = lambda shp: pl.BlockSpec(shp, lambda: (0,) * len(shp))
    out = pl.pallas_call(
        _ml_gate_kernel,
        in_specs=[full((4, ML_CHUNK, m)), full((4, 1, m)), full((ML_CHUNK, ML_CHUNK)),
                  full((ML_CHUNK, ML_CHUNK))],
        out_specs=full((6, ML_CHUNK, m)),
        out_shape=jax.ShapeDtypeStruct((6, ML_CHUNK, m), F32),
        compiler_params=pltpu.CompilerParams(vmem_limit_bytes=VMEM_LIMIT),
        name="ml_gates",
    )(x, bias, low, up)
    out = out.reshape(6, ML_CHUNK, bsz, ML_HEADS, n)
    row = jnp.stack([out[0], out[3]]).transpose(2, 3, 4, 0, 1)
    col = jnp.stack([out[1], out[2], out[4], out[5]]).transpose(2, 3, 4, 1, 0)
    return col, row


ML_HEADS_PER_STEP = 2


def _ml_unit(d, hh, n, first, q_scr, kt_scr, va_scr, col_ref, row_ref, h_ref, c_ref, m_ref):
    dh = ML_HEAD_DIM
    sl = pl.ds(pl.multiple_of(n * ML_CHUNK, ML_CHUNK), ML_CHUNK)
    q = q_scr[sl, hh * dh:(hh + 1) * dh]
    kt = kt_scr[hh, n]
    va = va_scr[sl, hh * 2 * dh:(hh + 1) * 2 * dh]
    colv = col_ref[0, hh, n]
    r_r = row_ref[0, hh, n][d:d + 1, :]
    mx_c, b_c = colv[:, 2 * d:2 * d + 1], colv[:, 2 * d + 1:2 * d + 2]
    last = ML_CHUNK - 1 if d == 0 else 0
    b_all = b_c[last:last + 1, :]
    mx_all = mx_c[last:last + 1, :]
    c0 = c_ref[0, d, hh]
    m0 = m_ref[0, d, hh][:, 0:1]

    tt = lax.broadcasted_iota(jnp.int32, (ML_CHUNK, ML_CHUNK), 0)
    ss = lax.broadcasted_iota(jnp.int32, (ML_CHUNK, ML_CHUNK), 1)
    seen = (ss <= tt) if d == 0 else (ss >= tt)
    mu_c = jnp.maximum(m0, mx_c)
    mu = jnp.broadcast_to(mu_c, (ML_CHUNK, ML_CHUNK))
    floor = jnp.broadcast_to(jnp.exp(-(b_c + mu_c)), (ML_CHUNK, ML_CHUNK))
    e = jnp.where(seen, jnp.exp(r_r - mu), 0.0)
    wi = jnp.exp(m0 - mu)
    s = (_dot(q, kt) * e).astype(BF16)
    inter = _dot(q, c0.astype(BF16))
    res = _dot(s, va)
    num = res[:, :dh] + wi * inter[:, :dh]
    den = res[:, dh:] + wi * inter[:, dh:]
    h = num / jnp.maximum(jnp.abs(den), floor)
    osl = slice(hh * dh, (hh + 1) * dh)
    if first:
        h_ref[0, sl, osl] = h
    else:
        h_ref[0, sl, osl] = h_ref[0, sl, osl] + h

    w_r = jnp.exp(r_r - mx_all)
    upd = _dot((kt.astype(F32) * w_r).astype(BF16), va)
    m_loc = b_all + mx_all
    m_new = jnp.maximum(b_all + m0, m_loc)
    a = jnp.exp(b_all + m0 - m_new)
    sc = jnp.exp(m_loc - m_new)
    c_ref[0, d, hh] = a * c0 + sc * upd
    m_ref[0, d, hh] = jnp.broadcast_to(m_new, (1, dh))


def _ml_kernel(zq_ref, zk_ref, zv_ref, cwq_ref, cwk_ref, col_ref, row_ref, c0_ref, m0_ref,
               h_ref, c_ref, m_ref, pad_ref, q_scr, kt_scr, va_scr, *, seq):
    nchunk = seq // ML_CHUNK
    dh = ML_HEAD_DIM
    hp = ML_HEADS_PER_STEP
    _fill_padded(pad_ref, zq_ref[0], seq)
    cq = _dwconv(pad_ref, cwq_ref, seq, ML_CONV)
    q_scr[...] = (cq * _sigmoid(cq)).astype(BF16)
    pad_ref[PAD:PAD + seq, :] = zk_ref[0]
    for n in range(nchunk):
        rows = slice(n * ML_CHUNK, (n + 1) * ML_CHUNK)
        left = (ML_CONV - 1) // 2
        ck = None
        for j in range(ML_CONV):
            lo = PAD + n * ML_CHUNK + j - left
            term = cwk_ref[j:j + 1, :] * pad_ref[lo:lo + ML_CHUNK, :]
            ck = term if ck is None else ck + term
        kk = ck * _sigmoid(ck) * (dh ** -0.5)
        for hh in range(hp):
            kt_scr[hh, n] = kk[:, hh * dh:(hh + 1) * dh].T.astype(BF16)
            va_scr[rows, hh * 2 * dh:hh * 2 * dh + dh] = zv_ref[0, rows, hh * dh:(hh + 1) * dh].astype(BF16)
            va_scr[rows, hh * 2 * dh + dh:(hh + 1) * 2 * dh] = jnp.ones((ML_CHUNK, dh), BF16)
    c_ref[...] = c0_ref[...]
    m_ref[...] = m0_ref[...]

    def step(first):
        def body(i, carry):
            args = (q_scr, kt_scr, va_scr, col_ref, row_ref, h_ref, c_ref, m_ref)
            for hh in range(hp):
                _ml_unit(0, hh, i, first, *args)
                _ml_unit(1, hh, nchunk - 1 - i, first, *args)
            return carry
        return body

    lax.fori_loop(0, nchunk // 2, step(True), 0)
    lax.fori_loop(nchunk // 2, nchunk, step(False), 0)


def _mlstm(z_ml, p, state):
    bsz, seq, _ = z_ml.shape
    nchunk = seq // ML_CHUNK
    col, row = _ml_gates(z_ml[..., 3 * ML_WIDTH:3 * ML_WIDTH + ML_GATE_COLS], p['ml_gate_b'])
    dh, nh, hp = ML_HEAD_DIM, ML_HEADS, ML_HEADS_PER_STEP
    ng = nh // hp
    w = hp * dh
    zspec = lambda off: pl.BlockSpec((1, seq, w), lambda b, g: (b, 0, off + g))
    cspec = pl.BlockSpec((1, 2, hp, dh, 2 * dh), lambda b, g: (b, 0, g, 0, 0))
    mspec = pl.BlockSpec((1, 2, hp, 1, dh), lambda b, g: (b, 0, g, 0, 0))
    return pl.pallas_call(
        functools.partial(_ml_kernel, seq=seq),
        grid=(bsz, ng),
        in_specs=[zspec(0), zspec(ng), zspec(2 * ng),
                  pl.BlockSpec((ML_CONV, w), lambda b, g: (0, g)),
                  pl.BlockSpec((ML_CONV, w), lambda b, g: (0, ng + g)),
                  pl.BlockSpec((1, hp, nchunk, ML_CHUNK, 4), lambda b, g: (b, g, 0, 0, 0)),
                  pl.BlockSpec((1, hp, nchunk, 2, ML_CHUNK), lambda b, g: (b, g, 0, 0, 0)),
                  cspec, mspec],
        out_specs=[pl.BlockSpec((1, seq, w), lambda b, g: (b, 0, g)), cspec, mspec],
        out_shape=[jax.ShapeDtypeStruct((bsz, seq, ML_WIDTH), F32),
                   jax.ShapeDtypeStruct((bsz, 2, nh, dh, 2 * dh), F32),
                   jax.ShapeDtypeStruct((bsz, 2, nh, 1, dh), F32)],
        scratch_shapes=[pltpu.VMEM((seq + 2 * PAD, w), F32),
                        pltpu.VMEM((seq, w), BF16),
                        pltpu.VMEM((hp, nchunk, dh, ML_CHUNK), BF16),
                        pltpu.VMEM((seq, 2 * w), BF16)],
        compiler_params=_cparams(("parallel", "parallel")),
        name="mlstm",
    )(z_ml, z_ml, z_ml, p['ml_conv'], p['ml_conv'], col, row, *state)


def _rg_scan_block(a, b, carry, reverse):
    row = lax.broadcasted_iota(jnp.int32, a.shape, 0)
    for sh in (1, 2, 4):
        if reverse:
            a_sh = pltpu.roll(a, SUBLANES - sh, 0)
            b_sh = pltpu.roll(b, SUBLANES - sh, 0)
            ok = row < SUBLANES - sh
        else:
            a_sh = pltpu.roll(a, sh, 0)
            b_sh = pltpu.roll(b, sh, 0)
            ok = row >= sh
        b = jnp.where(ok, a * b_sh + b, b)
        a = jnp.where(ok, a * a_sh, a)
    h = a * carry + b
    new_carry = h[0:1, :] if reverse else h[SUBLANES - 1:SUBLANES, :]
    return h, new_carry


def _rg_kernel(z_ref, cw_ref, wg_ref, bg_ref, lam_ref, h0_ref, o_ref, st_ref,
               pad_ref, a_scr, b_scr, *, seq):
    _fill_padded(pad_ref, z_ref[0], seq)
    xr = _dwconv(pad_ref, cw_ref, seq, RG_CONV)
    pre = _dot(xr.astype(BF16), wg_ref[...]) + bg_ref[...]
    lam = lam_ref[...]
    sp = jnp.maximum(-lam, 0.0) + jnp.log(1.0 + jnp.exp(-jnp.abs(lam)))
    for d in range(2):
        r = _sigmoid(pre[:, (2 * d) * RG_WIDTH:(2 * d + 1) * RG_WIDTH])
        gi = _sigmoid(pre[:, (2 * d + 1) * RG_WIDTH:(2 * d + 2) * RG_WIDTH])
        log_a = (-RG_C) * r * sp[d:d + 1, :]
        a = jnp.exp(log_a)
        a_scr[d] = a
        b_scr[d] = jnp.sqrt(-jnp.tanh(log_a) * (a * a + 1.0)) * (gi * xr)

    nblk = seq // SUBLANES

    def body(i, carry):
        cf, cb = carry
        sf = pl.ds(pl.multiple_of(i * SUBLANES, SUBLANES), SUBLANES)
        sb = pl.ds(pl.multiple_of((nblk - 1 - i) * SUBLANES, SUBLANES), SUBLANES)
        hf, cf = _rg_scan_block(a_scr[0, sf, :], b_scr[0, sf, :], cf, False)
        hb, cb = _rg_scan_block(a_scr[1, sb, :], b_scr[1, sb, :], cb, True)
        b_scr[0, sf, :] = hf
        b_scr[1, sb, :] = hb
        return cf, cb

    cf, cb = lax.fori_loop(0, nblk, body, (h0_ref[0, 0:1, :], h0_ref[0, 1:2, :]))
    st_ref[0, 0:1, :] = cf
    st_ref[0, 1:2, :] = cb
    o_ref[0] = b_scr[0] + b_scr[1]


def _rg_gate_weights(p):
    w = p['rg_gate_w']
    eye = jnp.eye(RG_HEADS, dtype=w.dtype)
    full = jnp.einsum('dghij,hk->dghikj', w, eye)
    full = full.reshape(2, 2, RG_WIDTH, RG_WIDTH)
    wg = full.transpose(2, 0, 1, 3).reshape(RG_WIDTH, 4 * RG_WIDTH).astype(BF16)
    bg = p['rg_gate_b'].reshape(1, 4 * RG_WIDTH)
    return wg, bg


def _rglru(z_a, p, h0):
    bsz, seq, _ = z_a.shape
    c = RG_WIDTH
    wg, bg = _rg_gate_weights(p)
    return pl.pallas_call(
        functools.partial(_rg_kernel, seq=seq),
        grid=(bsz,),
        in_specs=[pl.BlockSpec((1, seq, c), lambda b: (b, 0, HY_COLS // c)),
                  pl.BlockSpec((RG_CONV, c), lambda b: (0, 0)),
                  pl.BlockSpec((c, 4 * c), lambda b: (0, 0)),
                  pl.BlockSpec((1, 4 * c), lambda b: (0, 0)),
                  pl.BlockSpec((2, c), lambda b: (0, 0)),
                  pl.BlockSpec((1, 2, c), lambda b: (b, 0, 0))],
        out_specs=[pl.BlockSpec((1, seq, c), lambda b: (b, 0, 0)),
                   pl.BlockSpec((1, 2, c), lambda b: (b, 0, 0))],
        out_shape=[jax.ShapeDtypeStruct((bsz, seq, c), F32),
                   jax.ShapeDtypeStruct((bsz, 2, c), F32)],
        scratch_shapes=[pltpu.VMEM((seq + 2 * PAD, c), F32),
                        pltpu.VMEM((2, seq, c), F32), pltpu.VMEM((2, seq, c), F32)],
        compiler_params=_cparams(("parallel",)),
        name="rglru",
    )(z_a, p['rg_conv'], wg, bg, p['rg_lambda'], h0)


def _group_mean_sq(y, m_ref):
    sq = y * y
    hi = sq.astype(BF16)
    lo = (sq - hi.astype(F32)).astype(BF16)
    return _dot(hi, m_ref[...]) + _dot(lo, m_ref[...])


def _mix_mlp_kernel(x_ref, yhy_ref, hm_ref, o_ref, hr_ref, gate_ref, g_ref, m64_ref, wo_ref, gt_ref,
                    sh_ref, sc_ref, gt2_ref, g2_ref, w1_ref, w2_ref, fg_ref, out_ref, *, final, fc, hm_rows):
    g = g_ref[...]
    yhy = yhy_ref[...]
    yhy = yhy * lax.rsqrt(_group_mean_sq(yhy, m64_ref) + EPS) * g[:, :HY_WIDTH]
    parts = [yhy.astype(BF16)]
    if hm_rows:
        hm = jnp.concatenate([hm_ref[0, :, r, :] for r in range(hm_rows)], axis=0)
    else:
        hm = hm_ref[0]
    o_gate = _sigmoid(o_ref[0])
    for h in range(ML_HEADS):
        sl = slice(h * ML_HEAD_DIM, (h + 1) * ML_HEAD_DIM)
        hh = hm[:, sl]
        ms = jnp.mean(hh * hh, axis=-1, keepdims=True)
        yh = hh * lax.rsqrt(ms + EPS) * g[:, HY_WIDTH + h * ML_HEAD_DIM:HY_WIDTH + (h + 1) * ML_HEAD_DIM]
        parts.append((yh * o_gate[:, sl]).astype(BF16))
    gp = gate_ref[0]
    gelu = 0.5 * gp * (1.0 + jnp.tanh(math.sqrt(2.0 / math.pi) * (gp + 0.044715 * (gp * gp * gp))))
    yrg = hr_ref[0] * gelu
    yrg = yrg * lax.rsqrt(_group_mean_sq(yrg, m64_ref) + EPS) * g[:, HY_WIDTH + ML_WIDTH:]
    parts.append(yrg.astype(BF16))
    y = jnp.concatenate(parts, axis=-1)
    x = x_ref[0] + gt_ref[0] * _dot(y, wo_ref[...])

    h = _modulated_norm(x, g2_ref[...], sh_ref[0], sc_ref[0]).astype(BF16)
    acc = None
    for j in range(D_FF // fc):
        a = jnp.maximum(_dot(h, w1_ref[:, j * fc:(j + 1) * fc]), 0.0)
        t = _dot((a * a).astype(BF16), w2_ref[j * fc:(j + 1) * fc, :])
        acc = t if acc is None else acc + t
    out = x + gt2_ref[0] * acc
    if final:
        ms = jnp.mean(out * out, axis=-1, keepdims=True)
        out = out * lax.rsqrt(ms + EPS) * fg_ref[...]
    out_ref[0] = out


def _mix_mlp(x, y_hy, hm, z_a, hr, mix_g, w_out, gate, shift, scale, gate2, g2, w1, w2, final_g, final, tm,
             hm_col_major):
    bsz, seq, d = x.shape
    r = np.arange(HY_WIDTH) // (HY_WIDTH // HY_GROUPS)
    m64 = jnp.asarray((r[:, None] == r[None, :]) / float(HY_WIDTH // HY_GROUPS), dtype=BF16)
    tok = lambda w, j: pl.BlockSpec((1, tm, w), lambda b, i: (b, i, j))
    row = pl.BlockSpec((1, d), lambda b, i: (0, 0))
    if hm_col_major:
        hm_rows = tm // GRID_W
        hm = hm.reshape(bsz, GRID_W, seq // GRID_W, ML_WIDTH)
        hm_spec = pl.BlockSpec((1, GRID_W, hm_rows, ML_WIDTH), lambda b, i: (b, 0, i, 0))
    else:
        hm_rows = 0
        hm_spec = tok(ML_WIDTH, 0)
    return pl.pallas_call(
        functools.partial(_mix_mlp_kernel, final=final, fc=1024, hm_rows=hm_rows),
        grid=(bsz, seq // tm),
        in_specs=[tok(d, 0),
                  pl.BlockSpec((tm, HY_WIDTH), lambda b, i: (i, b)),
                  hm_spec,
                  tok(ML_WIDTH, 2),
                  tok(RG_WIDTH, 0),
                  tok(RG_WIDTH, 6),
                  row,
                  pl.BlockSpec((HY_WIDTH, HY_WIDTH), lambda b, i: (0, 0)),
                  _const_spec((d, d)), _mod_spec(gate),
                  _mod_spec(shift), _mod_spec(scale), _mod_spec(gate2), row,
                  _const_spec((d, D_FF)), _const_spec((D_FF, d)), row],
        out_specs=tok(d, 0),
        out_shape=jax.ShapeDtypeStruct((bsz, seq, d), F32),
        compiler_params=_cparams(("parallel", "parallel")),
        name="mix_mlp",
    )(x, y_hy, hm, z_a, hr, z_a, mix_g, m64, w_out, gate, shift, scale, gate2, g2, w1, w2, final_g)


def _split_w_in(w_in):
    rg = OFF_RG
    o0 = OFF_ML + 3 * ML_WIDTH
    w_a = jnp.concatenate([w_in[:, :HY_COLS], w_in[:, rg:rg + RG_WIDTH],
                           w_in[:, o0:o0 + ML_WIDTH], w_in[:, rg + RG_WIDTH:rg + 2 * RG_WIDTH]], axis=1)
    g0 = OFF_ML + 4 * ML_WIDTH
    w_m = jnp.concatenate([w_in[:, OFF_ML:o0], w_in[:, g0:g0 + ML_GATE_COLS],
                           jnp.zeros((w_in.shape[0], ZM_COLS - 3 * ML_WIDTH - ML_GATE_COLS), w_in.dtype)],
                          axis=1)
    return w_a.astype(BF16), w_m.astype(BF16)


def kernel(x, c, ctx, c_ctx, ada_w, ada_b, norm1_g, norm2_g, w_in, hy_conv, hy_w1, hy_b1, hy_w2, hy_b2, hy_w3, hy_b3, hy_freq, hy_bias, ml_conv, ml_gate_b, rg_conv, rg_gate_w, rg_gate_b, rg_lambda, mix_norm_g, w_out, mlp_w1, mlp_w2, final_g):
    bsz, seq, d = x.shape
    clen = ctx.shape[1]
    stacked = dict(hy_conv=hy_conv, hy_w1=hy_w1, hy_b1=hy_b1, hy_w2=hy_w2, hy_b2=hy_b2, hy_w3=hy_w3,
                   hy_b3=hy_b3, hy_freq=hy_freq, hy_bias=hy_bias, ml_conv=ml_conv, ml_gate_b=ml_gate_b,
                   rg_conv=rg_conv, rg_gate_w=rg_gate_w, rg_gate_b=rg_gate_b, rg_lambda=rg_lambda)

    nrow = 2 * SUBLANES
    cc = jnp.concatenate([c, c_ctx[None], jnp.zeros((nrow - bsz - 1, d), F32)], axis=0)
    mods = _ada(cc, ada_w, ada_b)

    fwd_l, inv_l = _dft_tables(seq)
    fwd_c, inv_c = _dft_tables(clen)
    dh, nh = ML_HEAD_DIM, ML_HEADS
    ml_zero = (jnp.zeros((bsz, 2, nh, dh, 2 * dh), F32), jnp.zeros((bsz, 2, nh, 1, dh), F32))
    rg_zero = jnp.zeros((bsz, 2, RG_WIDTH), F32)

    ctx_s = ctx
    for l in range(DEPTH):
        need_ctx = l < DEPTH - 1
        p = {k_: v_[l] for k_, v_ in stacked.items()}
        mod = mods[l, :bsz].reshape(bsz, N_MOD, 1, d)
        modc = mods[l, bsz].reshape(N_MOD, 1, 1, d)
        m_lat = [mod[:, i] for i in range(N_MOD)]
        m_ctx = [modc[i] for i in range(N_MOD)]
        g1 = norm1_g[l][None]
        g2 = norm2_g[l][None]
        w_a, w_m = _split_w_in(w_in[l])
        w_o = w_out[l].astype(BF16)
        w1 = mlp_w1[l].astype(BF16)
        w2 = mlp_w2[l].astype(BF16)
        mix_g = mix_norm_g[l][None]

        zc_a = _inproj(ctx_s, m_ctx[0], m_ctx[1], g1, w_a, clen)
        zc_m = _inproj(ctx_s, m_ctx[0], m_ctx[1], g1, w_m, clen)
        hm_c, ml_c, ml_m = _mlstm(zc_m, p, ml_zero)
        hr_c, rg_state = _rglru(zc_a, p, rg_zero)

        z_a = _inproj(x, m_lat[0], m_lat[1], g1, w_a, 512)
        z_m = _inproj_col_major(x, m_lat[0], m_lat[1], g1, w_m, 16)
        y_hy = _hy_conv(z_a, p, _hy_filter(seq, p, fwd_l), fwd_l, inv_l)
        hm, _, _ = _mlstm(z_m, p, (ml_c, ml_m))
        hr, _ = _rglru(z_a, p, rg_state)
        x = _mix_mlp(x, y_hy, hm, z_a, hr, mix_g, w_o, m_lat[2], m_lat[3], m_lat[4], m_lat[5], g2, w1, w2,
                     final_g[None], final=(l == DEPTH - 1), tm=512, hm_col_major=True)

        if need_ctx:
            y_hy_c = _hy_conv(zc_a, p, _hy_filter(clen, p, fwd_c), fwd_c, inv_c)
            ctx_s = _mix_mlp(ctx_s, y_hy_c, hm_c, zc_a, hr_c, mix_g, w_o, m_ctx[2], m_ctx[3], m_ctx[4],
                             m_ctx[5], g2, w1, w2, final_g[None], final=False, tm=clen, hm_col_major=False)
    return x
```

```python
import functools
import math

import numpy as np
import jax
import jax.numpy as jnp
from jax import lax
from jax.experimental import pallas as pl
from jax.experimental.pallas import tpu as pltpu

F32 = jnp.float32
BF16 = jnp.bfloat16

D_MODEL = 1024
DEPTH = 2
GRID_W = 64
HY_WIDTH = 256
HY_GROUPS = 4
HY_SHORT = 3
HY_POS_DIM = 33
HY_FFN = 64
HY_FAST_DECAY = 0.3
HY_SLOW_DECAY = 1.5
HY_TARGET = 1e-2
ML_WIDTH = 512
ML_HEADS = 4
ML_HEAD_DIM = 128
ML_CONV = 4
ML_CHUNK = 128
RG_WIDTH = 256
RG_HEADS = 4
RG_BLOCK = 64
RG_CONV = 4
RG_C = 8.0
D_FF = 4096
N_MOD = 6
EPS = 1e-6
HY_COLS = 3 * HY_WIDTH
ML_GATE_COLS = 4 * ML_HEADS
ML_COLS = 4 * ML_WIDTH + ML_GATE_COLS
OFF_ML = HY_COLS
OFF_RG = HY_COLS + ML_COLS

LANES = 128
SUBLANES = 8
VMEM_LIMIT = 56 * 1024 * 1024

ZA_COLS = 1792
ZM_COLS = 1664
PAD = SUBLANES
MOD_ROWS = 2 * SUBLANES
ML_HEADS_PER_STEP = 2


def _cparams(sem):
    return pltpu.CompilerParams(dimension_semantics=sem, vmem_limit_bytes=VMEM_LIMIT)


def _const_spec(shape):
    nd = len(shape)
    return pl.BlockSpec(shape, lambda *_: (0,) * nd, pipeline_mode=pl.Buffered(1))


def _layer_spec(l, shape, single=False):
    nd = len(shape)
    mode = dict(pipeline_mode=pl.Buffered(1)) if single else {}
    return pl.BlockSpec((None,) + tuple(shape), lambda *_: (l,) + (0,) * nd, **mode)


def _mod_spec(l, idx, row):
    blk = (None, None, None, 1, D_MODEL)
    if row is None:
        return pl.BlockSpec(blk, lambda b, *_: (l, b, idx, 0, 0))
    return pl.BlockSpec(blk, lambda *_: (l, row, idx, 0, 0))


def _split3(v):
    hi = v.astype(BF16)
    r = v - hi.astype(F32)
    mid = r.astype(BF16)
    lo = (r - mid.astype(F32)).astype(BF16)
    return hi, mid, lo


def _dot(a, b):
    return jnp.dot(a, b, preferred_element_type=F32)


def _dot_f32(a, b):
    ah, am, al = _split3(a)
    bh, bm, bl = _split3(b)
    return (_dot(ah, bh) + (_dot(ah, bm) + _dot(am, bh))
            + (_dot(ah, bl) + _dot(am, bm) + _dot(al, bh)))


def _dot_2x(a, b):
    ah = a.astype(BF16)
    al = (a - ah.astype(F32)).astype(BF16)
    bh = b.astype(BF16)
    bl = (b - bh.astype(F32)).astype(BF16)
    return _dot(ah, bh) + (_dot(ah, bl) + _dot(al, bh))


def _sigmoid(v):
    return 0.5 + 0.5 * jnp.tanh(0.5 * v)


def _silu(v):
    h = 0.5 * v
    return h + h * jnp.tanh(h)


def _modulated_norm(x, g, shift, scale):
    ms = jnp.mean(x * x, axis=-1, keepdims=True)
    return (x * lax.rsqrt(ms + EPS) * g) * (1.0 + scale) + shift


def _dwconv(pad_ref, w_ref, start, nrows, ksize):
    left = (ksize - 1) // 2
    acc = None
    for j in range(ksize):
        lo = PAD + start + j - left
        term = w_ref[j:j + 1, :] * pad_ref[lo:lo + nrows, :]
        acc = term if acc is None else acc + term
    return acc


def _fill_padded(pad_ref, rows, seq):
    width = pad_ref.shape[1]
    pad_ref[0:PAD, :] = jnp.zeros((PAD, width), F32)
    pad_ref[PAD + seq:2 * PAD + seq, :] = jnp.zeros((PAD, width), F32)
    pad_ref[PAD:PAD + seq, :] = rows


def _ada_kernel(c_ref, w_ref, b_ref, o_ref):
    o_ref[0] = _dot_2x(_silu(c_ref[...]), w_ref[0]) + b_ref[0]


def _ada(cc, ada_w, ada_b):
    depth, d, n = ada_w.shape
    tn = 1536
    return pl.pallas_call(
        _ada_kernel,
        grid=(depth, n // tn),
        in_specs=[pl.BlockSpec(cc.shape, lambda l, j: (0, 0)),
                  pl.BlockSpec((1, d, tn), lambda l, j: (l, 0, j)),
                  pl.BlockSpec((1, 1, tn), lambda l, j: (l, 0, j))],
        out_specs=pl.BlockSpec((1, cc.shape[0], tn), lambda l, j: (l, 0, j)),
        out_shape=jax.ShapeDtypeStruct((depth, cc.shape[0], n), F32),
        compiler_params=_cparams(("parallel", "parallel")),
        name="ada",
    )(cc, ada_w, ada_b.reshape(depth, 1, n))


def _inproj_seq_kernel(x_ref, sh_ref, sc_ref, g_ref, wa_ref, wm_ref, za_ref, zm_ref):
    h = _modulated_norm(x_ref[0], g_ref[...], sh_ref[...], sc_ref[...]).astype(BF16)
    za_ref[0] = _dot(h, wa_ref[...])
    zm_ref[0] = _dot(h, wm_ref[...])


def _inproj_seq(x, mods, prm, l, row, tm):
    bsz, seq, d = x.shape
    return pl.pallas_call(
        _inproj_seq_kernel,
        grid=(bsz, seq // tm),
        in_specs=[pl.BlockSpec((1, tm, d), lambda b, i: (b, i, 0)),
                  _mod_spec(l, 0, row), _mod_spec(l, 1, row),
                  _layer_spec(l, (1, d)),
                  _layer_spec(l, (d, ZA_COLS), single=True), _layer_spec(l, (d, ZM_COLS), single=True)],
        out_specs=[pl.BlockSpec((1, tm, ZA_COLS), lambda b, i: (b, i, 0)),
                   pl.BlockSpec((1, tm, ZM_COLS), lambda b, i: (b, i, 0))],
        out_shape=[jax.ShapeDtypeStruct((bsz, seq, ZA_COLS), F32),
                   jax.ShapeDtypeStruct((bsz, seq, ZM_COLS), F32)],
        compiler_params=_cparams(("parallel", "parallel")),
        name="inproj_seq",
    )(x, mods, mods, prm['g1'], prm['w_a'], prm['w_m'])


def _inproj_grid_kernel(x_ref, sh_ref, sc_ref, g_ref, wa_ref, wm_ref, perm_ref, za_ref, zm_ref,
                        *, rows, ncol):
    d = g_ref.shape[1]
    x = x_ref[0].reshape(rows * ncol, d)
    h = _modulated_norm(x, g_ref[...], sh_ref[...], sc_ref[...]).astype(BF16)
    za_ref[0] = _dot(h, wa_ref[...]).reshape(rows, ncol, ZA_COLS)
    hp = _dot(perm_ref[...], h).astype(BF16)
    zm_ref[0] = _dot(hp, wm_ref[...])


def _inproj_grid(x, mods, prm, l, ncol):
    bsz, seq, d = x.shape
    rows = seq // GRID_W
    tm = rows * ncol
    src = np.arange(tm)
    r, c = src // ncol, src % ncol
    perm = np.zeros((tm, tm), np.float32)
    perm[c * rows + r, src] = 1.0
    za, zm = pl.pallas_call(
        functools.partial(_inproj_grid_kernel, rows=rows, ncol=ncol),
        grid=(bsz, GRID_W // ncol),
        in_specs=[pl.BlockSpec((1, rows, ncol, d), lambda b, i: (b, 0, i, 0)),
                  _mod_spec(l, 0, None), _mod_spec(l, 1, None),
                  _layer_spec(l, (1, d)),
                  _layer_spec(l, (d, ZA_COLS), single=True), _layer_spec(l, (d, ZM_COLS), single=True),
                  _const_spec((tm, tm))],
        out_specs=[pl.BlockSpec((1, rows, ncol, ZA_COLS), lambda b, i: (b, 0, i, 0)),
                   pl.BlockSpec((1, tm, ZM_COLS), lambda b, i: (b, i, 0))],
        out_shape=[jax.ShapeDtypeStruct((bsz, rows, GRID_W, ZA_COLS), F32),
                   jax.ShapeDtypeStruct((bsz, seq, ZM_COLS), F32)],
        compiler_params=_cparams(("parallel", "parallel")),
        name="inproj_grid",
    )(x.reshape(bsz, rows, GRID_W, d), mods, mods, prm['g1'], prm['w_a'], prm['w_m'],
      jnp.asarray(perm, dtype=BF16))
    return za.reshape(bsz, seq, ZA_COLS), zm


def _dft_tables(seq):
    n = 2 * seq
    k = np.arange(seq, dtype=np.int64)[:, None]
    s = np.arange(seq, dtype=np.int64)[None, :]
    ang = (2.0 * np.pi / n) * ((k * s) % n).astype(np.float64)
    top = np.cos(ang)
    bot = -np.sin(ang)
    bot[0, :] = np.where(np.arange(seq) % 2 == 0, 1.0, -1.0)
    fwd = np.concatenate([top, bot], axis=0)
    scale = np.full((n, 1), 2.0 / n)
    scale[0, 0] = 1.0 / n
    scale[seq, 0] = 1.0 / n
    inv = (fwd * scale).T
    return (jnp.asarray(fwd.astype(np.float32)).astype(BF16),
            jnp.asarray(inv.astype(np.float32)).astype(BF16))


def _hy_feature_tables(seq):
    pos = np.arange(seq, dtype=np.float64)[:, None]
    t = pos / max(seq - 1, 1)
    bands = (HY_POS_DIM - 1) // 2
    fr = np.linspace(1e-4, bands - 1, bands).astype(np.float32).astype(np.float64)[None]
    ang = (2.0 * math.pi / seq) * fr * pos
    feats = np.concatenate([t, np.cos(ang), -np.sin(ang)], axis=-1)
    feats = np.pad(feats, ((0, 0), (0, LANES - HY_POS_DIM)))
    max_decay = math.log(HY_TARGET) / HY_FAST_DECAY
    min_decay = math.log(HY_TARGET) / HY_SLOW_DECAY
    deltas = np.linspace(min_decay, max_decay, HY_WIDTH).astype(np.float32).astype(np.float64)
    decay = np.exp(-t * np.abs(deltas)[None, :])
    return jnp.asarray(feats, dtype=F32), jnp.asarray(decay, dtype=F32)


def _hy_filter_kernel(feats_ref, w1_ref, b1_ref, w2_ref, b2_ref, w3_ref, b3_ref, fq_ref,
                      decay_ref, wf_ref, f_ref, h_scr, *, seq, tr):
    i = pl.program_id(0)

    @pl.when(i == 0)
    def _():
        h = jnp.sin(fq_ref[0:1, :] * (_dot_f32(feats_ref[...], w1_ref[...]) + b1_ref[...]))
        h = jnp.sin(fq_ref[1:2, :] * (_dot_f32(h, w2_ref[...]) + b2_ref[...]))
        h = _dot_f32(h, w3_ref[...]) + b3_ref[...]
        dec = decay_ref[...]
        hf = h[:, :HY_WIDTH] * dec
        hb = h[:, HY_WIDTH:] * dec
        row = lax.broadcasted_iota(jnp.int32, hb.shape, 0)
        hb = jnp.where(row == 0, 0.0, hb)
        h_scr[:, :HY_WIDTH] = hf.astype(BF16)
        h_scr[:, HY_WIDTH:] = hb.astype(BF16)

    g = _dot(wf_ref[...], h_scr[...])
    gf = g[:, :HY_WIDTH]
    gb = g[:, HY_WIDTH:]
    row = i * tr + lax.broadcasted_iota(jnp.int32, gf.shape, 0)
    f_ref[...] = jnp.where(row <= seq, gf + gb, gf - gb)


def _hy_filter(seq, prm, l, fwd):
    feats, decay = _hy_feature_tables(seq)
    tr = min(seq, 512)
    full = lambda shp: pl.BlockSpec(shp, lambda i: (0,) * len(shp))
    return pl.pallas_call(
        functools.partial(_hy_filter_kernel, seq=seq, tr=tr),
        grid=(2 * seq // tr,),
        in_specs=[full((seq, LANES)), _layer_spec(l, (LANES, HY_FFN)), _layer_spec(l, (1, HY_FFN)),
                  _layer_spec(l, (HY_FFN, HY_FFN)), _layer_spec(l, (1, HY_FFN)),
                  _layer_spec(l, (HY_FFN, 2 * HY_WIDTH)), _layer_spec(l, (1, 2 * HY_WIDTH)),
                  _layer_spec(l, (2, HY_FFN)), full((seq, HY_WIDTH)),
                  pl.BlockSpec((tr, seq), lambda i: (i, 0))],
        out_specs=pl.BlockSpec((tr, HY_WIDTH), lambda i: (i, 0)),
        out_shape=jax.ShapeDtypeStruct((2 * seq, HY_WIDTH), F32),
        scratch_shapes=[pltpu.VMEM((seq, 2 * HY_WIDTH), BF16)],
        compiler_params=_cparams(("arbitrary",)),
        name="hy_filter",
    )(feats, prm['hy_w1'], prm['hy_b1'], prm['hy_w2'], prm['hy_b2'], prm['hy_w3'], prm['hy_b3'],
      prm['hy_freq'], decay, fwd)


def _hy_pre_kernel(z_ref, cw_ref, bias_ref, x0_ref, e_ref, u_ref, pad_ref, *, seq):
    _fill_padded(pad_ref, z_ref[0], seq)
    zc = _dwconv(pad_ref, cw_ref, 0, seq, HY_SHORT)
    x0 = zc[:, :HY_WIDTH]
    u = zc[:, HY_WIDTH:2 * HY_WIDTH] * zc[:, 2 * HY_WIDTH:]
    x0_ref[...] = x0
    e_ref[...] = x0 * (u * bias_ref[...])
    u_ref[...] = u.astype(BF16)


def _hy_fwd_kernel(u_ref, wt_ref, wb_ref, ft_ref, fb_ref, yt_ref, yb_ref, *, nb):
    ft, fb = ft_ref[...], fb_ref[...]
    first = pl.program_id(0) == 0
    row0 = jnp.logical_and(lax.broadcasted_iota(jnp.int32, ft.shape, 0) == 0, first)
    f_im = jnp.where(row0, 0.0, fb)
    f_re2 = jnp.where(row0, fb, ft)
    for b in range(nb):
        cols = slice(b * HY_WIDTH, (b + 1) * HY_WIDTH)
        ub = u_ref[:, cols]
        pr = _dot(wt_ref[...], ub)
        pi = _dot(wb_ref[...], ub)
        yt_ref[:, cols] = (pr * ft - pi * f_im).astype(BF16)
        yb_ref[:, cols] = (pr * f_im + pi * f_re2).astype(BF16)


def _hy_inv_kernel(yt_ref, yb_ref, wt_ref, wb_ref, x0_ref, e_ref, o_ref, *, nb):
    for b in range(nb):
        cols = slice(b * HY_WIDTH, (b + 1) * HY_WIDTH)
        y = _dot(wt_ref[...], yt_ref[:, cols]) + _dot(wb_ref[...], yb_ref[:, cols])
        o_ref[:, cols] = x0_ref[:, cols] * y + e_ref[:, cols]


def _hy_conv(z_a, prm, l, filt, fwd, inv):
    bsz, seq, _ = z_a.shape
    wide = bsz * HY_WIDTH
    bcol = pl.BlockSpec((seq, HY_WIDTH), lambda b: (0, b))
    x0, e, u = pl.pallas_call(
        functools.partial(_hy_pre_kernel, seq=seq),
        grid=(bsz,),
        in_specs=[pl.BlockSpec((1, seq, HY_COLS), lambda b: (b, 0, 0)),
                  _layer_spec(l, (HY_SHORT, HY_COLS)), _layer_spec(l, (1, HY_WIDTH))],
        out_specs=[bcol, bcol, bcol],
        out_shape=[jax.ShapeDtypeStruct((seq, wide), F32), jax.ShapeDtypeStruct((seq, wide), F32),
                   jax.ShapeDtypeStruct((seq, wide), BF16)],
        scratch_shapes=[pltpu.VMEM((seq + 2 * PAD, HY_COLS), F32)],
        compiler_params=_cparams(("parallel",)),
        name="hy_pre",
    )(z_a, prm['hy_conv'], prm['hy_bias'])

    tk = min(seq, 512)
    kt = seq // tk
    yt, yb = pl.pallas_call(
        functools.partial(_hy_fwd_kernel, nb=bsz),
        grid=(kt,),
        in_specs=[_const_spec((seq, wide)),
                  pl.BlockSpec((tk, seq), lambda i: (i, 0)),
                  pl.BlockSpec((tk, seq), lambda i: (kt + i, 0)),
                  pl.BlockSpec((tk, HY_WIDTH), lambda i: (i, 0)),
                  pl.BlockSpec((tk, HY_WIDTH), lambda i: (kt + i, 0))],
        out_specs=[pl.BlockSpec((tk, wide), lambda i: (i, 0)), pl.BlockSpec((tk, wide), lambda i: (i, 0))],
        out_shape=[jax.ShapeDtypeStruct((seq, wide), BF16), jax.ShapeDtypeStruct((seq, wide), BF16)],
        compiler_params=_cparams(("parallel",)),
        name="hy_fwd",
    )(u, fwd, fwd, filt, filt)

    tt = min(seq, 256)
    return pl.pallas_call(
        functools.partial(_hy_inv_kernel, nb=bsz),
        grid=(seq // tt,),
        in_specs=[_const_spec((seq, wide)), _const_spec((seq, wide)),
                  pl.BlockSpec((tt, seq), lambda i: (i, 0)),
                  pl.BlockSpec((tt, seq), lambda i: (i, 1)),
                  pl.BlockSpec((tt, wide), lambda i: (i, 0)),
                  pl.BlockSpec((tt, wide), lambda i: (i, 0))],
        out_specs=pl.BlockSpec((tt, wide), lambda i: (i, 0)),
        out_shape=jax.ShapeDtypeStruct((seq, wide), F32),
        compiler_params=_cparams(("parallel",)),
        name="hy_inv",
    )(yt, yb, inv, inv, x0, e)


def _running_max(v, reverse):
    row = lax.broadcasted_iota(jnp.int32, v.shape, 0)
    k = 1
    while k < ML_CHUNK:
        if reverse:
            sh = jnp.where(row < ML_CHUNK - k, pltpu.roll(v, ML_CHUNK - k, 0), -jnp.inf)
        else:
            sh = jnp.where(row >= k, pltpu.roll(v, k, 0), -jnp.inf)
        v = jnp.maximum(v, sh)
        k *= 2
    return v


def _ml_gate_kernel(x_ref, b_ref, low_ref, up_ref, o_ref):
    for d, tri_ref in ((0, low_ref), (1, up_ref)):
        li = x_ref[2 * d] + b_ref[2 * d]
        pre = x_ref[2 * d + 1] + b_ref[2 * d + 1]
        lf = jnp.minimum(pre, 0.0) - jnp.log(1.0 + jnp.exp(-jnp.abs(pre)))
        hi, mid, lo = _split3(lf)
        tri = tri_ref[...]
        cum = _dot(tri, hi) + _dot(tri, mid) + _dot(tri, lo)
        r = li - cum
        o_ref[3 * d] = r
        o_ref[3 * d + 1] = _running_max(r, reverse=(d == 1))
        o_ref[3 * d + 2] = cum


def _ml_gates(gates, gate_b):
    bsz, seq, _ = gates.shape
    n = seq // ML_CHUNK
    m = bsz * ML_HEADS * n
    x = gates.reshape(bsz, n, ML_CHUNK, 4, ML_HEADS).transpose(3, 2, 0, 4, 1).reshape(4, ML_CHUNK, m)
    bias = jnp.broadcast_to(gate_b.reshape(4, 1, 1, ML_HEADS, 1), (4, 1, bsz, ML_HEADS, n)).reshape(4, 1, m)
    r = np.arange(ML_CHUNK)
    low = jnp.asarray(r[:, None] >= r[None, :], dtype=BF16)
    up = jnp.asarray(r[:, None] <= r[None, :], dtype=BF16)
    full = lambda shp: pl.BlockSpec(shp, lambda: (0,) * len(shp))
    out = pl.pallas_call(
        _ml_gate_kernel,
        in_specs=[full((4, ML_CHUNK, m)), full((4, 1, m)), full((ML_CHUNK, ML_CHUNK)),
                  full((ML_CHUNK, ML_CHUNK))],
        out_specs=full((6, ML_CHUNK, m)),
        out_shape=jax.ShapeDtypeStruct((6, ML_CHUNK, m), F32),
        compiler_params=pltpu.CompilerParams(vmem_limit_bytes=VMEM_LIMIT),
        name="ml_gates",
    )(x, bias, low, up)
    out = out.reshape(6, ML_CHUNK, bsz, ML_HEADS, n)
    row = jnp.stack([out[0], out[3]]).transpose(2, 3, 4, 0, 1)
    col = jnp.stack([out[1], out[2], out[4], out[5]]).transpose(2, 3, 4, 1, 0)
    return col, row


def _ml_unit(d, hh, n, first, q_scr, kt_scr, va_scr, col_ref, row_ref, h_ref, c_ref, m_ref):
    dh = ML_HEAD_DIM
    sl = pl.ds(pl.multiple_of(n * ML_CHUNK, ML_CHUNK), ML_CHUNK)
    q = q_scr[sl, hh * dh:(hh + 1) * dh]
    kt = kt_scr[hh, n]
    va = va_scr[sl, hh * 2 * dh:(hh + 1) * 2 * dh]
    colv = col_ref[0, hh, n]
    r_r = row_ref[0, hh, n][d:d + 1, :]
    mx_c, b_c = colv[:, 2 * d:2 * d + 1], colv[:, 2 * d + 1:2 * d + 2]
    last = ML_CHUNK - 1 if d == 0 else 0
    b_all = b_c[last:last + 1, :]
    mx_all = mx_c[last:last + 1, :]
    c0 = c_ref[0, d, hh]
    m0 = m_ref[0, d, hh][:, 0:1]

    tt = lax.broadcasted_iota(jnp.int32, (ML_CHUNK, ML_CHUNK), 0)
    ss = lax.broadcasted_iota(jnp.int32, (ML_CHUNK, ML_CHUNK), 1)
    seen = (ss <= tt) if d == 0 else (ss >= tt)
    mu_c = jnp.maximum(m0, mx_c)
    mu = jnp.broadcast_to(mu_c, (ML_CHUNK, ML_CHUNK))
    floor = jnp.broadcast_to(jnp.exp(-(b_c + mu_c)), (ML_CHUNK, ML_CHUNK))
    e = jnp.where(seen, jnp.exp(r_r - mu), 0.0)
    wi = jnp.exp(m0 - mu)
    s = (_dot(q, kt) * e).astype(BF16)
    inter = _dot(q, c0.astype(BF16))
    res = _dot(s, va)
    num = res[:, :dh] + wi * inter[:, :dh]
    den = res[:, dh:] + wi * inter[:, dh:]
    h = num / jnp.maximum(jnp.abs(den), floor)
    osl = slice(hh * dh, (hh + 1) * dh)
    if first:
        h_ref[0, sl, osl] = h
    else:
        h_ref[0, sl, osl] = h_ref[0, sl, osl] + h

    w_r = jnp.exp(r_r - mx_all)
    upd = _dot((kt.astype(F32) * w_r).astype(BF16), va)
    m_loc = b_all + mx_all
    m_new = jnp.maximum(b_all + m0, m_loc)
    a = jnp.exp(b_all + m0 - m_new)
    sc = jnp.exp(m_loc - m_new)
    c_ref[0, d, hh] = a * c0 + sc * upd
    m_ref[0, d, hh] = jnp.broadcast_to(m_new, (1, dh))


def _ml_kernel(zq_ref, zk_ref, zv_ref, cwq_ref, cwk_ref, col_ref, row_ref, c0_ref, m0_ref,
               h_ref, c_ref, m_ref, pad_ref, q_scr, kt_scr, va_scr, *, seq):
    nchunk = seq // ML_CHUNK
    dh = ML_HEAD_DIM
    hp = ML_HEADS_PER_STEP
    _fill_padded(pad_ref, zq_ref[0], seq)
    q_scr[...] = _silu(_dwconv(pad_ref, cwq_ref, 0, seq, ML_CONV)).astype(BF16)
    pad_ref[PAD:PAD + seq, :] = zk_ref[0]
    for n in range(nchunk):
        rows = slice(n * ML_CHUNK, (n + 1) * ML_CHUNK)
        kk = _silu(_dwconv(pad_ref, cwk_ref, n * ML_CHUNK, ML_CHUNK, ML_CONV)) * (dh ** -0.5)
        for hh in range(hp):
            kt_scr[hh, n] = kk[:, hh * dh:(hh + 1) * dh].T.astype(BF16)
            va_scr[rows, hh * 2 * dh:hh * 2 * dh + dh] = zv_ref[0, rows, hh * dh:(hh + 1) * dh].astype(BF16)
            va_scr[rows, hh * 2 * dh + dh:(hh + 1) * 2 * dh] = jnp.ones((ML_CHUNK, dh), BF16)
    c_ref[...] = c0_ref[...]
    m_ref[...] = m0_ref[...]

    def step(first):
        def body(i, carry):
            args = (q_scr, kt_scr, va_scr, col_ref, row_ref, h_ref, c_ref, m_ref)
            for hh in range(hp):
                _ml_unit(0, hh, i, first, *args)
                _ml_unit(1, hh, nchunk - 1 - i, first, *args)
            return carry
        return body

    lax.fori_loop(0, nchunk // 2, step(True), 0)
    lax.fori_loop(nchunk // 2, nchunk, step(False), 0)


def _mlstm(z_ml, prm, l, state):
    bsz, seq, _ = z_ml.shape
    nchunk = seq // ML_CHUNK
    col, row = _ml_gates(z_ml[..., 3 * ML_WIDTH:3 * ML_WIDTH + ML_GATE_COLS], prm['ml_gate_b'][l])
    dh, nh, hp = ML_HEAD_DIM, ML_HEADS, ML_HEADS_PER_STEP
    ng = nh // hp
    w = hp * dh
    zspec = lambda off: pl.BlockSpec((1, seq, w), lambda b, g: (b, 0, off + g))
    cspec = pl.BlockSpec((1, 2, hp, dh, 2 * dh), lambda b, g: (b, 0, g, 0, 0))
    mspec = pl.BlockSpec((1, 2, hp, 1, dh), lambda b, g: (b, 0, g, 0, 0))
    return pl.pallas_call(
        functools.partial(_ml_kernel, seq=seq),
        grid=(bsz, ng),
        in_specs=[zspec(0), zspec(ng), zspec(2 * ng),
                  pl.BlockSpec((None, ML_CONV, w), lambda b, g: (l, 0, g)),
                  pl.BlockSpec((None, ML_CONV, w), lambda b, g: (l, 0, ng + g)),
                  pl.BlockSpec((1, hp, nchunk, ML_CHUNK, 4), lambda b, g: (b, g, 0, 0, 0)),
                  pl.BlockSpec((1, hp, nchunk, 2, ML_CHUNK), lambda b, g: (b, g, 0, 0, 0)),
                  cspec, mspec],
        out_specs=[pl.BlockSpec((1, seq, w), lambda b, g: (b, 0, g)), cspec, mspec],
        out_shape=[jax.ShapeDtypeStruct((bsz, seq, ML_WIDTH), F32),
                   jax.ShapeDtypeStruct((bsz, 2, nh, dh, 2 * dh), F32),
                   jax.ShapeDtypeStruct((bsz, 2, nh, 1, dh), F32)],
        scratch_shapes=[pltpu.VMEM((seq + 2 * PAD, w), F32),
                        pltpu.VMEM((seq, w), BF16),
                        pltpu.VMEM((hp, nchunk, dh, ML_CHUNK), BF16),
                        pltpu.VMEM((seq, 2 * w), BF16)],
        compiler_params=_cparams(("parallel", "parallel")),
        name="mlstm",
    )(z_ml, z_ml, z_ml, prm['ml_conv'], prm['ml_conv'], col, row, *state)


def _rg_scan_block(a, b, carry, reverse):
    row = lax.broadcasted_iota(jnp.int32, a.shape, 0)
    for sh in (1, 2, 4):
        if reverse:
            a_sh = pltpu.roll(a, SUBLANES - sh, 0)
            b_sh = pltpu.roll(b, SUBLANES - sh, 0)
            ok = row < SUBLANES - sh
        else:
            a_sh = pltpu.roll(a, sh, 0)
            b_sh = pltpu.roll(b, sh, 0)
            ok = row >= sh
        b = jnp.where(ok, a * b_sh + b, b)
        a = jnp.where(ok, a * a_sh, a)
    h = a * carry + b
    new_carry = h[0:1, :] if reverse else h[SUBLANES - 1:SUBLANES, :]
    return h, new_carry


def _rg_kernel(z_ref, cw_ref, wg_ref, bg_ref, lam_ref, h0_ref, o_ref, st_ref,
               pad_ref, a_scr, b_scr, *, seq):
    _fill_padded(pad_ref, z_ref[0], seq)
    xr = _dwconv(pad_ref, cw_ref, 0, seq, RG_CONV)
    half_pre = _dot(xr.astype(BF16), wg_ref[...]) + bg_ref[...]
    lam = lam_ref[...]
    sp = jnp.maximum(-lam, 0.0) + jnp.log(1.0 + jnp.exp(-jnp.abs(lam)))
    coef = (-0.5 * RG_C * math.log2(math.e)) * sp
    half_x = 0.5 * xr
    for d in range(2):
        t_r = jnp.tanh(half_pre[:, (2 * d) * RG_WIDTH:(2 * d + 1) * RG_WIDTH])
        t_i = jnp.tanh(half_pre[:, (2 * d + 1) * RG_WIDTH:(2 * d + 2) * RG_WIDTH])
        cd = coef[d:d + 1, :]
        a = jnp.exp2(cd + cd * t_r)
        a_scr[d] = a
        y = 1.0 - a * a
        root = jnp.where(y > 0.0, y * lax.rsqrt(y), 0.0)
        b_scr[d] = root * (half_x + half_x * t_i)

    nblk = seq // SUBLANES

    def body(i, carry):
        cf, cb = carry
        sf = pl.ds(pl.multiple_of(i * SUBLANES, SUBLANES), SUBLANES)
        sb = pl.ds(pl.multiple_of((nblk - 1 - i) * SUBLANES, SUBLANES), SUBLANES)
        hf, cf = _rg_scan_block(a_scr[0, sf, :], b_scr[0, sf, :], cf, False)
        hb, cb = _rg_scan_block(a_scr[1, sb, :], b_scr[1, sb, :], cb, True)
        b_scr[0, sf, :] = hf
        b_scr[1, sb, :] = hb
        return cf, cb

    cf, cb = lax.fori_loop(0, nblk, body, (h0_ref[0, 0:1, :], h0_ref[0, 1:2, :]))
    st_ref[0, 0:1, :] = cf
    st_ref[0, 1:2, :] = cb
    o_ref[0] = b_scr[0] + b_scr[1]


def _rg_gate_weights(rg_gate_w, rg_gate_b):
    depth = rg_gate_w.shape[0]
    eye = jnp.eye(RG_HEADS, dtype=rg_gate_w.dtype)
    full = jnp.einsum('ldghij,hk->ldghikj', 0.5 * rg_gate_w, eye)
    full = full.reshape(depth, 2, 2, RG_WIDTH, RG_WIDTH)
    wg = full.transpose(0, 3, 1, 2, 4).reshape(depth, RG_WIDTH, 4 * RG_WIDTH).astype(BF16)
    return wg, 0.5 * rg_gate_b.reshape(depth, 1, 4 * RG_WIDTH)


def _rglru(z_a, prm, l, h0):
    bsz, seq, _ = z_a.shape
    c = RG_WIDTH
    return pl.pallas_call(
        functools.partial(_rg_kernel, seq=seq),
        grid=(bsz,),
        in_specs=[pl.BlockSpec((1, seq, c), lambda b: (b, 0, HY_COLS // c)),
                  _layer_spec(l, (RG_CONV, c)), _layer_spec(l, (c, 4 * c)), _layer_spec(l, (1, 4 * c)),
                  _layer_spec(l, (2, c)),
                  pl.BlockSpec((1, 2, c), lambda b: (b, 0, 0))],
        out_specs=[pl.BlockSpec((1, seq, c), lambda b: (b, 0, 0)),
                   pl.BlockSpec((1, 2, c), lambda b: (b, 0, 0))],
        out_shape=[jax.ShapeDtypeStruct((bsz, seq, c), F32),
                   jax.ShapeDtypeStruct((bsz, 2, c), F32)],
        scratch_shapes=[pltpu.VMEM((seq + 2 * PAD, c), F32),
                        pltpu.VMEM((2, seq, c), F32), pltpu.VMEM((2, seq, c), F32)],
        compiler_params=_cparams(("parallel",)),
        name="rglru",
    )(z_a, prm['rg_conv'], prm['rg_wg'], prm['rg_bg'], prm['rg_lambda'], h0)


def _group_mean_sq(y, m_ref):
    sq = y * y
    hi = sq.astype(BF16)
    lo = (sq - hi.astype(F32)).astype(BF16)
    return _dot(hi, m_ref[...]) + _dot(lo, m_ref[...])


def _mix_mlp_kernel(x_ref, yhy_ref, hm_ref, o_ref, hr_ref, gate_ref, g_ref, m64_ref, wo_ref, gt_ref,
                    sh_ref, sc_ref, gt2_ref, g2_ref, w1_ref, w2_ref, fg_ref, out_ref, *, final, fc, hm_rows):
    g = g_ref[...]
    yhy = yhy_ref[...]
    yhy = yhy * lax.rsqrt(_group_mean_sq(yhy, m64_ref) + EPS) * g[:, :HY_WIDTH]
    parts = [yhy.astype(BF16)]
    if hm_rows:
        hm = jnp.concatenate([hm_ref[0, :, r, :] for r in range(hm_rows)], axis=0)
    else:
        hm = hm_ref[0]
    o_gate = _sigmoid(o_ref[0])
    for h in range(ML_HEADS):
        sl = slice(h * ML_HEAD_DIM, (h + 1) * ML_HEAD_DIM)
        hh = hm[:, sl]
        ms = jnp.mean(hh * hh, axis=-1, keepdims=True)
        yh = hh * lax.rsqrt(ms + EPS) * g[:, HY_WIDTH + h * ML_HEAD_DIM:HY_WIDTH + (h + 1) * ML_HEAD_DIM]
        parts.append((yh * o_gate[:, sl]).astype(BF16))
    gp = gate_ref[0]
    gelu = 0.5 * gp * (1.0 + jnp.tanh(math.sqrt(2.0 / math.pi) * (gp + 0.044715 * (gp * gp * gp))))
    yrg = hr_ref[0] * gelu
    yrg = yrg * lax.rsqrt(_group_mean_sq(yrg, m64_ref) + EPS) * g[:, HY_WIDTH + ML_WIDTH:]
    parts.append(yrg.astype(BF16))
    y = jnp.concatenate(parts, axis=-1)
    x = x_ref[0] + gt_ref[...] * _dot(y, wo_ref[...])

    h = _modulated_norm(x, g2_ref[...], sh_ref[...], sc_ref[...]).astype(BF16)
    acc = None
    for j in range(D_FF // fc):
        a = jnp.maximum(_dot(h, w1_ref[:, j * fc:(j + 1) * fc]), 0.0)
        t = _dot((a * a).astype(BF16), w2_ref[j * fc:(j + 1) * fc, :])
        acc = t if acc is None else acc + t
    out = x + gt2_ref[...] * acc
    if final:
        ms = jnp.mean(out * out, axis=-1, keepdims=True)
        out = out * lax.rsqrt(ms + EPS) * fg_ref[...]
    out_ref[0] = out


def _mix_mlp(x, y_hy, hm, z_a, hr, mods, prm, l, row, final_g, final, tm, hm_col_major):
    bsz, seq, d = x.shape
    r = np.arange(HY_WIDTH) // (HY_WIDTH // HY_GROUPS)
    m64 = jnp.asarray((r[:, None] == r[None, :]) / float(HY_WIDTH // HY_GROUPS), dtype=BF16)
    tok = lambda w, j: pl.BlockSpec((1, tm, w), lambda b, i: (b, i, j))
    if hm_col_major:
        hm_rows = tm // GRID_W
        hm = hm.reshape(bsz, GRID_W, seq // GRID_W, ML_WIDTH)
        hm_spec = pl.BlockSpec((1, GRID_W, hm_rows, ML_WIDTH), lambda b, i: (b, 0, i, 0))
    else:
        hm_rows = 0
        hm_spec = tok(ML_WIDTH, 0)
    return pl.pallas_call(
        functools.partial(_mix_mlp_kernel, final=final, fc=1024, hm_rows=hm_rows),
        grid=(bsz, seq // tm),
        in_specs=[tok(d, 0),
                  pl.BlockSpec((tm, HY_WIDTH), lambda b, i: (i, b)),
                  hm_spec,
                  tok(ML_WIDTH, 2),
                  tok(RG_WIDTH, 0),
                  tok(RG_WIDTH, 6),
                  _layer_spec(l, (1, d)),
                  pl.BlockSpec((HY_WIDTH, HY_WIDTH), lambda b, i: (0, 0)),
                  _layer_spec(l, (d, d), single=True), _mod_spec(l, 2, row),
                  _mod_spec(l, 3, row), _mod_spec(l, 4, row), _mod_spec(l, 5, row),
                  _layer_spec(l, (1, d)),
                  _layer_spec(l, (d, D_FF), single=True), _layer_spec(l, (D_FF, d), single=True),
                  pl.BlockSpec((1, d), lambda b, i: (0, 0))],
        out_specs=tok(d, 0),
        out_shape=jax.ShapeDtypeStruct((bsz, seq, d), F32),
        compiler_params=_cparams(("parallel", "parallel")),
        name="mix_mlp",
    )(x, y_hy, hm, z_a, hr, z_a, prm['mix_g'], m64, prm['w_o'], mods, mods, mods, mods, prm['g2'],
      prm['w1'], prm['w2'], final_g)


def _split_w_in(w_in):
    wb = w_in.astype(BF16)
    rg = OFF_RG
    o0 = OFF_ML + 3 * ML_WIDTH
    w_a = jnp.concatenate([wb[..., :HY_COLS], wb[..., rg:rg + RG_WIDTH],
                           wb[..., o0:o0 + ML_WIDTH], wb[..., rg + RG_WIDTH:rg + 2 * RG_WIDTH]], axis=-1)
    g0 = OFF_ML + 4 * ML_WIDTH
    pad = jnp.zeros(wb.shape[:-1] + (ZM_COLS - 3 * ML_WIDTH - ML_GATE_COLS,), BF16)
    w_m = jnp.concatenate([wb[..., OFF_ML:o0], wb[..., g0:g0 + ML_GATE_COLS], pad], axis=-1)
    return w_a, w_m


def _prepare(norm1_g, norm2_g, w_in, hy_conv, hy_w1, hy_b1, hy_w2, hy_b2, hy_w3, hy_b3, hy_freq, hy_bias,
             ml_conv, ml_gate_b, rg_conv, rg_gate_w, rg_gate_b, rg_lambda, mix_norm_g, w_out, mlp_w1, mlp_w2):
    w_a, w_m = _split_w_in(w_in)
    rg_wg, rg_bg = _rg_gate_weights(rg_gate_w, rg_gate_b)
    return dict(
        g1=norm1_g[:, None, :], g2=norm2_g[:, None, :], mix_g=mix_norm_g[:, None, :],
        w_a=w_a, w_m=w_m, w_o=w_out.astype(BF16), w1=mlp_w1.astype(BF16), w2=mlp_w2.astype(BF16),
        hy_conv=hy_conv, hy_bias=hy_bias[:, None, :],
        hy_w1=jnp.pad(hy_w1, ((0, 0), (0, LANES - HY_POS_DIM), (0, 0))), hy_b1=hy_b1[:, None, :],
        hy_w2=hy_w2, hy_b2=hy_b2[:, None, :], hy_w3=hy_w3, hy_b3=hy_b3[:, None, :], hy_freq=hy_freq,
        ml_conv=ml_conv, ml_gate_b=ml_gate_b,
        rg_conv=rg_conv, rg_wg=rg_wg, rg_bg=rg_bg, rg_lambda=rg_lambda)


def kernel(x, c, ctx, c_ctx, ada_w, ada_b, norm1_g, norm2_g, w_in, hy_conv, hy_w1, hy_b1, hy_w2, hy_b2, hy_w3, hy_b3, hy_freq, hy_bias, ml_conv, ml_gate_b, rg_conv, rg_gate_w, rg_gate_b, rg_lambda, mix_norm_g, w_out, mlp_w1, mlp_w2, final_g):
    bsz, seq, d = x.shape
    clen = ctx.shape[1]
    prm = _prepare(norm1_g, norm2_g, w_in, hy_conv, hy_w1, hy_b1, hy_w2, hy_b2, hy_w3, hy_b3, hy_freq,
                   hy_bias, ml_conv, ml_gate_b, rg_conv, rg_gate_w, rg_gate_b, rg_lambda, mix_norm_g,
                   w_out, mlp_w1, mlp_w2)

    cc = jnp.concatenate([c, c_ctx[None], jnp.zeros((MOD_ROWS - bsz - 1, d), F32)], axis=0)
    mods = _ada(cc, ada_w, ada_b).reshape(DEPTH, MOD_ROWS, N_MOD, 1, d)
    ctx_row = bsz

    fwd_l, inv_l = _dft_tables(seq)
    fwd_c, inv_c = _dft_tables(clen)
    dh, nh = ML_HEAD_DIM, ML_HEADS
    ml_zero = (jnp.zeros((bsz, 2, nh, dh, 2 * dh), F32), jnp.zeros((bsz, 2, nh, 1, dh), F32))
    rg_zero = jnp.zeros((bsz, 2, RG_WIDTH), F32)
    fg = final_g[None]

    ctx_s = ctx
    for l in range(DEPTH):
        need_ctx = l < DEPTH - 1
        zc_a, zc_m = _inproj_seq(ctx_s, mods, prm, l, ctx_row, clen)
        hm_c, ml_c, ml_m = _mlstm(zc_m, prm, l, ml_zero)
        hr_c, rg_state = _rglru(zc_a, prm, l, rg_zero)

        z_a, z_m = _inproj_grid(x, mods, prm, l, 16)
        y_hy = _hy_conv(z_a, prm, l, _hy_filter(seq, prm, l, fwd_l), fwd_l, inv_l)
        hm, _, _ = _mlstm(z_m, prm, l, (ml_c, ml_m))
        hr, _ = _rglru(z_a, prm, l, rg_state)
        x = _mix_mlp(x, y_hy, hm, z_a, hr, mods, prm, l, None, fg, final=(l == DEPTH - 1), tm=512,
                     hm_col_major=True)

        if need_ctx:
            y_hy_c = _hy_conv(zc_a, prm, l, _hy_filter(clen, prm, l, fwd_c), fwd_c, inv_c)
            ctx_s = _mix_mlp(ctx_s, y_hy_c, hm_c, zc_a, hr_c, mods, prm, l, ctx_row, fg, final=False,
                             tm=clen, hm_col_major=False)
    return x
```

```python
import functools
import math

import numpy as np
import jax
import jax.numpy as jnp
from jax import lax
from jax.experimental import pallas as pl
from jax.experimental.pallas import tpu as pltpu

F32 = jnp.float32
BF16 = jnp.bfloat16

D_MODEL = 1024
DEPTH = 2
GRID_W = 64
HY_WIDTH = 256
HY_GROUPS = 4
HY_SHORT = 3
HY_POS_DIM = 33
HY_FFN = 64
HY_FAST_DECAY = 0.3
HY_SLOW_DECAY = 1.5
HY_TARGET = 1e-2
ML_WIDTH = 512
ML_HEADS = 4
ML_HEAD_DIM = 128
ML_CONV = 4
ML_CHUNK = 128
RG_WIDTH = 256
RG_HEADS = 4
RG_BLOCK = 64
RG_CONV = 4
RG_C = 8.0
D_FF = 4096
N_MOD = 6
EPS = 1e-6
HY_COLS = 3 * HY_WIDTH
ML_GATE_COLS = 4 * ML_HEADS
ML_COLS = 4 * ML_WIDTH + ML_GATE_COLS
OFF_ML = HY_COLS
OFF_RG = HY_COLS + ML_COLS

LANES = 128
SUBLANES = 8
VMEM_LIMIT = 56 * 1024 * 1024

ZA_COLS = 1792
ZM_COLS = 1664
PAD = SUBLANES
MOD_ROWS = 2 * SUBLANES
ML_HEADS_PER_STEP = 2


def _cparams(sem):
    return pltpu.CompilerParams(dimension_semantics=sem, vmem_limit_bytes=VMEM_LIMIT)


def _const_spec(shape):
    nd = len(shape)
    return pl.BlockSpec(shape, lambda *_: (0,) * nd, pipeline_mode=pl.Buffered(1))


def _layer_spec(l, shape, single=False):
    nd = len(shape)
    mode = dict(pipeline_mode=pl.Buffered(1)) if single else {}
    return pl.BlockSpec((None,) + tuple(shape), lambda *_: (l,) + (0,) * nd, **mode)


def _mod_spec(l, idx, row):
    blk = (None, None, None, 1, D_MODEL)
    if row is None:
        return pl.BlockSpec(blk, lambda b, *_: (l, b, idx, 0, 0))
    return pl.BlockSpec(blk, lambda *_: (l, row, idx, 0, 0))


def _split3(v):
    hi = v.astype(BF16)
    r = v - hi.astype(F32)
    mid = r.astype(BF16)
    lo = (r - mid.astype(F32)).astype(BF16)
    return hi, mid, lo


def _dot(a, b):
    return jnp.dot(a, b, preferred_element_type=F32)


def _dot_f32(a, b):
    ah, am, al = _split3(a)
    bh, bm, bl = _split3(b)
    return (_dot(ah, bh) + (_dot(ah, bm) + _dot(am, bh))
            + (_dot(ah, bl) + _dot(am, bm) + _dot(al, bh)))


def _dot_2x(a, b):
    ah = a.astype(BF16)
    al = (a - ah.astype(F32)).astype(BF16)
    bh = b.astype(BF16)
    bl = (b - bh.astype(F32)).astype(BF16)
    return _dot(ah, bh) + (_dot(ah, bl) + _dot(al, bh))


def _sigmoid(v):
    return 0.5 + 0.5 * jnp.tanh(0.5 * v)


def _silu(v):
    h = 0.5 * v
    return h + h * jnp.tanh(h)


def _modulated_norm(x, g, shift, scale):
    ms = jnp.mean(x * x, axis=-1, keepdims=True)
    return (x * lax.rsqrt(ms + EPS) * g) * (1.0 + scale) + shift


def _dwconv(pad_ref, w_ref, start, nrows, ksize):
    left = (ksize - 1) // 2
    acc = None
    for j in range(ksize):
        lo = PAD + start + j - left
        term = w_ref[j:j + 1, :] * pad_ref[lo:lo + nrows, :]
        acc = term if acc is None else acc + term
    return acc


def _fill_padded(pad_ref, rows, seq):
    width = pad_ref.shape[1]
    pad_ref[0:PAD, :] = jnp.zeros((PAD, width), F32)
    pad_ref[PAD + seq:2 * PAD + seq, :] = jnp.zeros((PAD, width), F32)
    pad_ref[PAD:PAD + seq, :] = rows


def _ada_kernel(c_ref, w_ref, b_ref, o_ref):
    o_ref[0] = _dot_2x(_silu(c_ref[...]), w_ref[0]) + b_ref[0]


def _ada(cc, ada_w, ada_b):
    depth, d, n = ada_w.shape
    tn = 1536
    return pl.pallas_call(
        _ada_kernel,
        grid=(depth, n // tn),
        in_specs=[pl.BlockSpec(cc.shape, lambda l, j: (0, 0)),
                  pl.BlockSpec((1, d, tn), lambda l, j: (l, 0, j)),
                  pl.BlockSpec((1, 1, tn), lambda l, j: (l, 0, j))],
        out_specs=pl.BlockSpec((1, cc.shape[0], tn), lambda l, j: (l, 0, j)),
        out_shape=jax.ShapeDtypeStruct((depth, cc.shape[0], n), F32),
        compiler_params=_cparams(("parallel", "parallel")),
        name="ada",
    )(cc, ada_w, ada_b.reshape(depth, 1, n))


def _inproj_seq_kernel(x_ref, sh_ref, sc_ref, g_ref, wa_ref, wm_ref, za_ref, zm_ref):
    h = _modulated_norm(x_ref[0], g_ref[...], sh_ref[...], sc_ref[...]).astype(BF16)
    za_ref[0] = _dot(h, wa_ref[...])
    zm_ref[0] = _dot(h, wm_ref[...])


def _inproj_seq(x, mods, prm, l, row, tm):
    bsz, seq, d = x.shape
    return pl.pallas_call(
        _inproj_seq_kernel,
        grid=(bsz, seq // tm),
        in_specs=[pl.BlockSpec((1, tm, d), lambda b, i: (b, i, 0)),
                  _mod_spec(l, 0, row), _mod_spec(l, 1, row),
                  _layer_spec(l, (1, d)),
                  _layer_spec(l, (d, ZA_COLS), single=True), _layer_spec(l, (d, ZM_COLS), single=True)],
        out_specs=[pl.BlockSpec((1, tm, ZA_COLS), lambda b, i: (b, i, 0)),
                   pl.BlockSpec((1, tm, ZM_COLS), lambda b, i: (b, i, 0))],
        out_shape=[jax.ShapeDtypeStruct((bsz, seq, ZA_COLS), F32),
                   jax.ShapeDtypeStruct((bsz, seq, ZM_COLS), F32)],
        compiler_params=_cparams(("parallel", "parallel")),
        name="inproj_seq",
    )(x, mods, mods, prm['g1'], prm['w_a'], prm['w_m'])


def _inproj_grid_kernel(x_ref, sh_ref, sc_ref, g_ref, wa_ref, wm_ref, perm_ref, za_ref, zm_ref,
                        *, rows, ncol):
    d = g_ref.shape[1]
    x = x_ref[0].reshape(rows * ncol, d)
    h = _modulated_norm(x, g_ref[...], sh_ref[...], sc_ref[...]).astype(BF16)
    za_ref[0] = _dot(h, wa_ref[...]).reshape(rows, ncol, ZA_COLS)
    hp = _dot(perm_ref[...], h).astype(BF16)
    zm_ref[0] = _dot(hp, wm_ref[...])


def _inproj_grid(x, mods, prm, l, ncol):
    bsz, seq, d = x.shape
    rows = seq // GRID_W
    tm = rows * ncol
    src = np.arange(tm)
    r, c = src // ncol, src % ncol
    perm = np.zeros((tm, tm), np.float32)
    perm[c * rows + r, src] = 1.0
    za, zm = pl.pallas_call(
        functools.partial(_inproj_grid_kernel, rows=rows, ncol=ncol),
        grid=(bsz, GRID_W // ncol),
        in_specs=[pl.BlockSpec((1, rows, ncol, d), lambda b, i: (b, 0, i, 0)),
                  _mod_spec(l, 0, None), _mod_spec(l, 1, None),
                  _layer_spec(l, (1, d)),
                  _layer_spec(l, (d, ZA_COLS), single=True), _layer_spec(l, (d, ZM_COLS), single=True),
                  _const_spec((tm, tm))],
        out_specs=[pl.BlockSpec((1, rows, ncol, ZA_COLS), lambda b, i: (b, 0, i, 0)),
                   pl.BlockSpec((1, tm, ZM_COLS), lambda b, i: (b, i, 0))],
        out_shape=[jax.ShapeDtypeStruct((bsz, rows, GRID_W, ZA_COLS), F32),
                   jax.ShapeDtypeStruct((bsz, seq, ZM_COLS), F32)],
        compiler_params=_cparams(("parallel", "parallel")),
        name="inproj_grid",
    )(x.reshape(bsz, rows, GRID_W, d), mods, mods, prm['g1'], prm['w_a'], prm['w_m'],
      jnp.asarray(perm, dtype=BF16))
    return za.reshape(bsz, seq, ZA_COLS), zm


def _dft_tables(seq):
    n = 2 * seq
    k = np.arange(seq, dtype=np.int64)[:, None]
    s = np.arange(seq, dtype=np.int64)[None, :]
    ang = (2.0 * np.pi / n) * ((k * s) % n).astype(np.float64)
    top = np.cos(ang)
    bot = -np.sin(ang)
    bot[0, :] = np.where(np.arange(seq) % 2 == 0, 1.0, -1.0)
    fwd = np.concatenate([top, bot], axis=0)
    scale = np.full((n, 1), 2.0 / n)
    scale[0, 0] = 1.0 / n
    scale[seq, 0] = 1.0 / n
    inv = (fwd * scale).T
    return (jnp.asarray(fwd.astype(np.float32)).astype(BF16),
            jnp.asarray(inv.astype(np.float32)).astype(BF16))


def _hy_feature_tables(seq):
    pos = np.arange(seq, dtype=np.float64)[:, None]
    t = pos / max(seq - 1, 1)
    bands = (HY_POS_DIM - 1) // 2
    fr = np.linspace(1e-4, bands - 1, bands).astype(np.float32).astype(np.float64)[None]
    ang = (2.0 * math.pi / seq) * fr * pos
    feats = np.concatenate([t, np.cos(ang), -np.sin(ang)], axis=-1)
    feats = np.pad(feats, ((0, 0), (0, LANES - HY_POS_DIM)))
    max_decay = math.log(HY_TARGET) / HY_FAST_DECAY
    min_decay = math.log(HY_TARGET) / HY_SLOW_DECAY
    deltas = np.linspace(min_decay, max_decay, HY_WIDTH).astype(np.float32).astype(np.float64)
    decay = np.exp(-t * np.abs(deltas)[None, :])
    return jnp.asarray(feats, dtype=F32), jnp.asarray(decay, dtype=F32)


def _hy_filter_kernel(feats_ref, w1_ref, b1_ref, w2_ref, b2_ref, w3_ref, b3_ref, fq_ref,
                      decay_ref, wf_ref, f_ref, h_scr, *, seq, tr):
    i = pl.program_id(0)

    @pl.when(i == 0)
    def _():
        h = jnp.sin(fq_ref[0:1, :] * (_dot_f32(feats_ref[...], w1_ref[...]) + b1_ref[...]))
        h = jnp.sin(fq_ref[1:2, :] * (_dot_f32(h, w2_ref[...]) + b2_ref[...]))
        h = _dot_f32(h, w3_ref[...]) + b3_ref[...]
        dec = decay_ref[...]
        hf = h[:, :HY_WIDTH] * dec
        hb = h[:, HY_WIDTH:] * dec
        row = lax.broadcasted_iota(jnp.int32, hb.shape, 0)
        hb = jnp.where(row == 0, 0.0, hb)
        h_scr[:, :HY_WIDTH] = hf.astype(BF16)
        h_scr[:, HY_WIDTH:] = hb.astype(BF16)

    g = _dot(wf_ref[...], h_scr[...])
    gf = g[:, :HY_WIDTH]
    gb = g[:, HY_WIDTH:]
    row = i * tr + lax.broadcasted_iota(jnp.int32, gf.shape, 0)
    f_ref[...] = jnp.where(row <= seq, gf + gb, gf - gb)


def _hy_filter(seq, prm, l, fwd):
    feats, decay = _hy_feature_tables(seq)
    tr = min(seq, 512)
    full = lambda shp: pl.BlockSpec(shp, lambda i: (0,) * len(shp))
    return pl.pallas_call(
        functools.partial(_hy_filter_kernel, seq=seq, tr=tr),
        grid=(2 * seq // tr,),
        in_specs=[full((seq, LANES)), _layer_spec(l, (LANES, HY_FFN)), _layer_spec(l, (1, HY_FFN)),
                  _layer_spec(l, (HY_FFN, HY_FFN)), _layer_spec(l, (1, HY_FFN)),
                  _layer_spec(l, (HY_FFN, 2 * HY_WIDTH)), _layer_spec(l, (1, 2 * HY_WIDTH)),
                  _layer_spec(l, (2, HY_FFN)), full((seq, HY_WIDTH)),
                  pl.BlockSpec((tr, seq), lambda i: (i, 0))],
        out_specs=pl.BlockSpec((tr, HY_WIDTH), lambda i: (i, 0)),
        out_shape=jax.ShapeDtypeStruct((2 * seq, HY_WIDTH), F32),
        scratch_shapes=[pltpu.VMEM((seq, 2 * HY_WIDTH), BF16)],
        compiler_params=_cparams(("arbitrary",)),
        name="hy_filter",
    )(feats, prm['hy_w1'], prm['hy_b1'], prm['hy_w2'], prm['hy_b2'], prm['hy_w3'], prm['hy_b3'],
      prm['hy_freq'], decay, fwd)


def _hy_pre_kernel(z_ref, cw_ref, bias_ref, x0_ref, e_ref, u_ref, pad_ref, *, seq):
    _fill_padded(pad_ref, z_ref[0], seq)
    zc = _dwconv(pad_ref, cw_ref, 0, seq, HY_SHORT)
    x0 = zc[:, :HY_WIDTH]
    u = zc[:, HY_WIDTH:2 * HY_WIDTH] * zc[:, 2 * HY_WIDTH:]
    x0_ref[...] = x0
    e_ref[...] = x0 * (u * bias_ref[...])
    u_ref[...] = u.astype(BF16)


def _hy_fwd_kernel(u_ref, wt_ref, wb_ref, ft_ref, fb_ref, yt_ref, yb_ref, *, nb):
    ft, fb = ft_ref[...], fb_ref[...]
    first = pl.program_id(0) == 0
    row0 = jnp.logical_and(lax.broadcasted_iota(jnp.int32, ft.shape, 0) == 0, first)
    f_im = jnp.where(row0, 0.0, fb)
    f_re2 = jnp.where(row0, fb, ft)
    for b in range(nb):
        cols = slice(b * HY_WIDTH, (b + 1) * HY_WIDTH)
        ub = u_ref[:, cols]
        pr = _dot(wt_ref[...], ub)
        pi = _dot(wb_ref[...], ub)
        yt_ref[:, cols] = (pr * ft - pi * f_im).astype(BF16)
        yb_ref[:, cols] = (pr * f_im + pi * f_re2).astype(BF16)


def _hy_inv_kernel(yt_ref, yb_ref, wt_ref, wb_ref, x0_ref, e_ref, o_ref, *, nb):
    for b in range(nb):
        cols = slice(b * HY_WIDTH, (b + 1) * HY_WIDTH)
        y = _dot(wt_ref[...], yt_ref[:, cols]) + _dot(wb_ref[...], yb_ref[:, cols])
        o_ref[:, cols] = x0_ref[:, cols] * y + e_ref[:, cols]


def _hy_conv(z_a, prm, l, filt, fwd, inv):
    bsz, seq, _ = z_a.shape
    wide = bsz * HY_WIDTH
    bcol = pl.BlockSpec((seq, HY_WIDTH), lambda b: (0, b))
    x0, e, u = pl.pallas_call(
        functools.partial(_hy_pre_kernel, seq=seq),
        grid=(bsz,),
        in_specs=[pl.BlockSpec((1, seq, HY_COLS), lambda b: (b, 0, 0)),
                  _layer_spec(l, (HY_SHORT, HY_COLS)), _layer_spec(l, (1, HY_WIDTH))],
        out_specs=[bcol, bcol, bcol],
        out_shape=[jax.ShapeDtypeStruct((seq, wide), F32), jax.ShapeDtypeStruct((seq, wide), F32),
                   jax.ShapeDtypeStruct((seq, wide), BF16)],
        scratch_shapes=[pltpu.VMEM((seq + 2 * PAD, HY_COLS), F32)],
        compiler_params=_cparams(("parallel",)),
        name="hy_pre",
    )(z_a, prm['hy_conv'], prm['hy_bias'])

    tk = min(seq, 512)
    kt = seq // tk
    yt, yb = pl.pallas_call(
        functools.partial(_hy_fwd_kernel, nb=bsz),
        grid=(kt,),
        in_specs=[_const_spec((seq, wide)),
                  pl.BlockSpec((tk, seq), lambda i: (i, 0)),
                  pl.BlockSpec((tk, seq), lambda i: (kt + i, 0)),
                  pl.BlockSpec((tk, HY_WIDTH), lambda i: (i, 0)),
                  pl.BlockSpec((tk, HY_WIDTH), lambda i: (kt + i, 0))],
        out_specs=[pl.BlockSpec((tk, wide), lambda i: (i, 0)), pl.BlockSpec((tk, wide), lambda i: (i, 0))],
        out_shape=[jax.ShapeDtypeStruct((seq, wide), BF16), jax.ShapeDtypeStruct((seq, wide), BF16)],
        compiler_params=_cparams(("parallel",)),
        name="hy_fwd",
    )(u, fwd, fwd, filt, filt)

    tt = min(seq, 256)
    return pl.pallas_call(
        functools.partial(_hy_inv_kernel, nb=bsz),
        grid=(seq // tt,),
        in_specs=[_const_spec((seq, wide)), _const_spec((seq, wide)),
                  pl.BlockSpec((tt, seq), lambda i: (i, 0)),
                  pl.BlockSpec((tt, seq), lambda i: (i, 1)),
                  pl.BlockSpec((tt, wide), lambda i: (i, 0)),
                  pl.BlockSpec((tt, wide), lambda i: (i, 0))],
        out_specs=pl.BlockSpec((tt, wide), lambda i: (i, 0)),
        out_shape=jax.ShapeDtypeStruct((seq, wide), F32),
        compiler_params=_cparams(("parallel",)),
        name="hy_inv",
    )(yt, yb, inv, inv, x0, e)


def _running_max(v, reverse):
    row = lax.broadcasted_iota(jnp.int32, v.shape, 0)
    k = 1
    while k < ML_CHUNK:
        if reverse:
            sh = jnp.where(row < ML_CHUNK - k, pltpu.roll(v, ML_CHUNK - k, 0), -jnp.inf)
        else:
            sh = jnp.where(row >= k, pltpu.roll(v, k, 0), -jnp.inf)
        v = jnp.maximum(v, sh)
        k *= 2
    return v


def _ml_gate_kernel(x_ref, b_ref, low_ref, up_ref, o_ref):
    for d, tri_ref in ((0, low_ref), (1, up_ref)):
        li = x_ref[2 * d] + b_ref[2 * d]
        pre = x_ref[2 * d + 1] + b_ref[2 * d + 1]
        lf = jnp.minimum(pre, 0.0) - jnp.log(1.0 + jnp.exp(-jnp.abs(pre)))
        hi, mid, lo = _split3(lf)
        tri = tri_ref[...]
        cum = _dot(tri, hi) + _dot(tri, mid) + _dot(tri, lo)
        r = li - cum
        o_ref[3 * d] = r
        o_ref[3 * d + 1] = _running_max(r, reverse=(d == 1))
        o_ref[3 * d + 2] = cum
    o_ref[6] = jnp.zeros(o_ref.shape[1:], F32)
    o_ref[7] = jnp.zeros(o_ref.shape[1:], F32)


def _ml_gates(gates, gate_b):
    bsz, seq, _ = gates.shape
    n = seq // ML_CHUNK
    m = bsz * ML_HEADS * n
    x = gates.reshape(bsz, n, ML_CHUNK, 4, ML_HEADS).transpose(3, 2, 0, 4, 1).reshape(4, ML_CHUNK, m)
    bias = jnp.broadcast_to(gate_b.reshape(4, 1, 1, ML_HEADS, 1), (4, 1, bsz, ML_HEADS, n)).reshape(4, 1, m)
    r = np.arange(ML_CHUNK)
    low = jnp.asarray(r[:, None] >= r[None, :], dtype=BF16)
    up = jnp.asarray(r[:, None] <= r[None, :], dtype=BF16)
    full = lambda shp: pl.BlockSpec(shp, lambda: (0,) * len(shp))
    out = pl.pallas_call(
        _ml_gate_kernel,
        in_specs=[full((4, ML_CHUNK, m)), full((4, 1, m)), full((ML_CHUNK, ML_CHUNK)),
                  full((ML_CHUNK, ML_CHUNK))],
        out_specs=full((SUBLANES, ML_CHUNK, m)),
        out_shape=jax.ShapeDtypeStruct((SUBLANES, ML_CHUNK, m), F32),
        compiler_params=pltpu.CompilerParams(vmem_limit_bytes=VMEM_LIMIT),
        name="ml_gates",
    )(x, bias, low, up)
    return out.reshape(SUBLANES, ML_CHUNK, bsz, ML_HEADS, n).transpose(2, 3, 4, 0, 1)


def _ml_column(rows, j):
    if j:
        rows = pltpu.roll(rows, SUBLANES - j, 0)
    return rows.T[:, 0:1]


def _ml_state_step(d, hh, n, kt_scr, va_scr, row_ref, c_ref, m_ref, cseq_scr, mseq_scr):
    sl = pl.ds(pl.multiple_of(n * ML_CHUNK, ML_CHUNK), ML_CHUNK)
    dh = ML_HEAD_DIM
    kt = kt_scr[hh, n]
    va = va_scr[sl, hh * 2 * dh:(hh + 1) * 2 * dh]
    rows = row_ref[0, hh, n]
    r_r = rows[3 * d:3 * d + 1, :]
    last = ML_CHUNK - 1 if d == 0 else 0
    mx_all = rows[3 * d + 1:3 * d + 2, last:last + 1]
    b_all = rows[3 * d + 2:3 * d + 3, last:last + 1]
    c0 = c_ref[0, d, hh]
    m0v = m_ref[0, d, hh]
    m0 = m0v[:, 0:1]
    cseq_scr[d, hh, n] = c0.astype(BF16)
    mseq_scr[d, hh, n] = m0v
    w_r = jnp.exp(r_r - mx_all)
    upd = _dot(kt * w_r.astype(BF16), va)
    m_loc = b_all + mx_all
    m_new = jnp.maximum(b_all + m0, m_loc)
    a = jnp.exp(b_all + m0 - m_new)
    sc = jnp.exp(m_loc - m_new)
    c_ref[0, d, hh] = a * c0 + sc * upd
    m_ref[0, d, hh] = jnp.broadcast_to(m_new, (1, dh))


def _ml_output(hh, n, q_scr, kt_scr, va_scr, row_ref, cseq_scr, mseq_scr, h_ref):
    dh = ML_HEAD_DIM
    sl = pl.ds(pl.multiple_of(n * ML_CHUNK, ML_CHUNK), ML_CHUNK)
    q = q_scr[sl, hh * dh:(hh + 1) * dh]
    va = va_scr[sl, hh * 2 * dh:(hh + 1) * 2 * dh]
    qk = _dot(q, kt_scr[hh, n])
    tt = lax.broadcasted_iota(jnp.int32, (ML_CHUNK, ML_CHUNK), 0)
    ss = lax.broadcasted_iota(jnp.int32, (ML_CHUNK, ML_CHUNK), 1)
    rows = row_ref[0, hh, n]
    h = None
    for d in range(2):
        r_r = rows[3 * d:3 * d + 1, :]
        mx_c = _ml_column(rows, 3 * d + 1)
        b_c = _ml_column(rows, 3 * d + 2)
        m0 = mseq_scr[d, hh, n][:, 0:1]
        seen = (ss <= tt) if d == 0 else (ss >= tt)
        mu_c = jnp.maximum(m0, mx_c)
        mu = jnp.broadcast_to(mu_c, (ML_CHUNK, ML_CHUNK))
        floor = jnp.broadcast_to(jnp.exp(-(b_c + mu_c)), (ML_CHUNK, ML_CHUNK))
        e = jnp.where(seen, jnp.exp(r_r - mu), 0.0)
        wi = jnp.exp(m0 - mu).astype(BF16)
        s = (qk * e).astype(BF16)
        res = _dot(jnp.concatenate([s, wi * q], axis=1), jnp.concatenate([va, cseq_scr[d, hh, n]], axis=0))
        hd = res[:, :dh] / jnp.maximum(jnp.abs(res[:, dh:]), floor)
        h = hd if h is None else h + hd
    h_ref[0, sl, hh * dh:(hh + 1) * dh] = h


def _ml_kernel(zq_ref, zk_ref, zv_ref, cwq_ref, cwk_ref, row_ref, c0_ref, m0_ref,
               h_ref, c_ref, m_ref, pad_ref, q_scr, kt_scr, va_scr, cseq_scr, mseq_scr, *, seq):
    nchunk = seq // ML_CHUNK
    dh = ML_HEAD_DIM
    hp = ML_HEADS_PER_STEP
    _fill_padded(pad_ref, zq_ref[0], seq)
    q_scr[...] = _silu(_dwconv(pad_ref, cwq_ref, 0, seq, ML_CONV)).astype(BF16)
    pad_ref[PAD:PAD + seq, :] = zk_ref[0]
    for n in range(nchunk):
        rows = slice(n * ML_CHUNK, (n + 1) * ML_CHUNK)
        kk = _silu(_dwconv(pad_ref, cwk_ref, n * ML_CHUNK, ML_CHUNK, ML_CONV)) * (dh ** -0.5)
        for hh in range(hp):
            kt_scr[hh, n] = kk[:, hh * dh:(hh + 1) * dh].T.astype(BF16)
            va_scr[rows, hh * 2 * dh:hh * 2 * dh + dh] = zv_ref[0, rows, hh * dh:(hh + 1) * dh].astype(BF16)
            va_scr[rows, hh * 2 * dh + dh:(hh + 1) * 2 * dh] = jnp.ones((ML_CHUNK, dh), BF16)
    c_ref[...] = c0_ref[...]
    m_ref[...] = m0_ref[...]

    def state_body(i, carry):
        for hh in range(hp):
            for d, n in ((0, i), (1, nchunk - 1 - i)):
                _ml_state_step(d, hh, n, kt_scr, va_scr, row_ref, c_ref, m_ref, cseq_scr, mseq_scr)
        return carry

    lax.fori_loop(0, nchunk, state_body, 0)

    def out_body(n, carry):
        for hh in range(hp):
            _ml_output(hh, n, q_scr, kt_scr, va_scr, row_ref, cseq_scr, mseq_scr, h_ref)
        return carry

    lax.fori_loop(0, nchunk, out_body, 0, unroll=2)


def _mlstm(z_ml, prm, l, state):
    bsz, seq, _ = z_ml.shape
    nchunk = seq // ML_CHUNK
    row = _ml_gates(z_ml[..., 3 * ML_WIDTH:3 * ML_WIDTH + ML_GATE_COLS], prm['ml_gate_b'][l])
    dh, nh, hp = ML_HEAD_DIM, ML_HEADS, ML_HEADS_PER_STEP
    ng = nh // hp
    w = hp * dh
    zspec = lambda off: pl.BlockSpec((1, seq, w), lambda b, g: (b, 0, off + g))
    cspec = pl.BlockSpec((1, 2, hp, dh, 2 * dh), lambda b, g: (b, 0, g, 0, 0))
    mspec = pl.BlockSpec((1, 2, hp, 1, dh), lambda b, g: (b, 0, g, 0, 0))
    return pl.pallas_call(
        functools.partial(_ml_kernel, seq=seq),
        grid=(bsz, ng),
        in_specs=[zspec(0), zspec(ng), zspec(2 * ng),
                  pl.BlockSpec((None, ML_CONV, w), lambda b, g: (l, 0, g)),
                  pl.BlockSpec((None, ML_CONV, w), lambda b, g: (l, 0, ng + g)),
                  pl.BlockSpec((1, hp, nchunk, SUBLANES, ML_CHUNK), lambda b, g: (b, g, 0, 0, 0)),
                  cspec, mspec],
        out_specs=[pl.BlockSpec((1, seq, w), lambda b, g: (b, 0, g)), cspec, mspec],
        out_shape=[jax.ShapeDtypeStruct((bsz, seq, ML_WIDTH), F32),
                   jax.ShapeDtypeStruct((bsz, 2, nh, dh, 2 * dh), F32),
                   jax.ShapeDtypeStruct((bsz, 2, nh, 1, dh), F32)],
        scratch_shapes=[pltpu.VMEM((seq + 2 * PAD, w), F32),
                        pltpu.VMEM((seq, w), BF16),
                        pltpu.VMEM((hp, nchunk, dh, ML_CHUNK), BF16),
                        pltpu.VMEM((seq, 2 * w), BF16),
                        pltpu.VMEM((2, hp, nchunk, dh, 2 * dh), BF16),
                        pltpu.VMEM((2, hp, nchunk, 1, dh), F32)],
        compiler_params=_cparams(("parallel", "parallel")),
        name="mlstm",
    )(z_ml, z_ml, z_ml, prm['ml_conv'], prm['ml_conv'], row, *state)


def _rg_scan_block(a, b, carry, reverse):
    row = lax.broadcasted_iota(jnp.int32, a.shape, 0)
    for sh in (1, 2, 4):
        if reverse:
            a_sh = pltpu.roll(a, SUBLANES - sh, 0)
            b_sh = pltpu.roll(b, SUBLANES - sh, 0)
            ok = row < SUBLANES - sh
        else:
            a_sh = pltpu.roll(a, sh, 0)
            b_sh = pltpu.roll(b, sh, 0)
            ok = row >= sh
        b = jnp.where(ok, a * b_sh + b, b)
        a = jnp.where(ok, a * a_sh, a)
    h = a * carry + b
    new_carry = h[0:1, :] if reverse else h[SUBLANES - 1:SUBLANES, :]
    return h, new_carry


def _rg_kernel(z_ref, cw_ref, wg_ref, bg_ref, lam_ref, h0_ref, o_ref, st_ref,
               pad_ref, a_scr, b_scr, *, seq):
    _fill_padded(pad_ref, z_ref[0], seq)
    xr = _dwconv(pad_ref, cw_ref, 0, seq, RG_CONV)
    half_pre = _dot(xr.astype(BF16), wg_ref[...]) + bg_ref[...]
    lam = lam_ref[...]
    sp = jnp.maximum(-lam, 0.0) + jnp.log(1.0 + jnp.exp(-jnp.abs(lam)))
    coef = (-0.5 * RG_C * math.log2(math.e)) * sp
    half_x = 0.5 * xr
    for d in range(2):
        t_r = jnp.tanh(half_pre[:, (2 * d) * RG_WIDTH:(2 * d + 1) * RG_WIDTH])
        t_i = jnp.tanh(half_pre[:, (2 * d + 1) * RG_WIDTH:(2 * d + 2) * RG_WIDTH])
        cd = coef[d:d + 1, :]
        a = jnp.exp2(cd + cd * t_r)
        a_scr[d] = a
        y = 1.0 - a * a
        root = jnp.where(y > 0.0, y * lax.rsqrt(y), 0.0)
        b_scr[d] = root * (half_x + half_x * t_i)

    nblk = seq // SUBLANES

    def body(i, carry):
        cf, cb = carry
        sf = pl.ds(pl.multiple_of(i * SUBLANES, SUBLANES), SUBLANES)
        sb = pl.ds(pl.multiple_of((nblk - 1 - i) * SUBLANES, SUBLANES), SUBLANES)
        hf, cf = _rg_scan_block(a_scr[0, sf, :], b_scr[0, sf, :], cf, False)
        hb, cb = _rg_scan_block(a_scr[1, sb, :], b_scr[1, sb, :], cb, True)
        b_scr[0, sf, :] = hf
        b_scr[1, sb, :] = hb
        return cf, cb

    cf, cb = lax.fori_loop(0, nblk, body, (h0_ref[0, 0:1, :], h0_ref[0, 1:2, :]))
    st_ref[0, 0:1, :] = cf
    st_ref[0, 1:2, :] = cb
    o_ref[0] = b_scr[0] + b_scr[1]


def _rg_gate_weights(rg_gate_w, rg_gate_b):
    depth = rg_gate_w.shape[0]
    eye = jnp.eye(RG_HEADS, dtype=rg_gate_w.dtype)
    full = jnp.einsum('ldghij,hk->ldghikj', 0.5 * rg_gate_w, eye)
    full = full.reshape(depth, 2, 2, RG_WIDTH, RG_WIDTH)
    wg = full.transpose(0, 3, 1, 2, 4).reshape(depth, RG_WIDTH, 4 * RG_WIDTH).astype(BF16)
    return wg, 0.5 * rg_gate_b.reshape(depth, 1, 4 * RG_WIDTH)


def _rglru(z_a, prm, l, h0):
    bsz, seq, _ = z_a.shape
    c = RG_WIDTH
    return pl.pallas_call(
        functools.partial(_rg_kernel, seq=seq),
        grid=(bsz,),
        in_specs=[pl.BlockSpec((1, seq, c), lambda b: (b, 0, HY_COLS // c)),
                  _layer_spec(l, (RG_CONV, c)), _layer_spec(l, (c, 4 * c)), _layer_spec(l, (1, 4 * c)),
                  _layer_spec(l, (2, c)),
                  pl.BlockSpec((1, 2, c), lambda b: (b, 0, 0))],
        out_specs=[pl.BlockSpec((1, seq, c), lambda b: (b, 0, 0)),
                   pl.BlockSpec((1, 2, c), lambda b: (b, 0, 0))],
        out_shape=[jax.ShapeDtypeStruct((bsz, seq, c), F32),
                   jax.ShapeDtypeStruct((bsz, 2, c), F32)],
        scratch_shapes=[pltpu.VMEM((seq + 2 * PAD, c), F32),
                        pltpu.VMEM((2, seq, c), F32), pltpu.VMEM((2, seq, c), F32)],
        compiler_params=_cparams(("parallel",)),
        name="rglru",
    )(z_a, prm['rg_conv'], prm['rg_wg'], prm['rg_bg'], prm['rg_lambda'], h0)


def _group_mean_sq(y, m_ref):
    sq = y * y
    hi = sq.astype(BF16)
    lo = (sq - hi.astype(F32)).astype(BF16)
    return _dot(hi, m_ref[...]) + _dot(lo, m_ref[...])


def _mix_mlp_kernel(x_ref, yhy_ref, hm_ref, o_ref, hr_ref, gate_ref, g_ref, m64_ref, wo_ref, gt_ref,
                    sh_ref, sc_ref, gt2_ref, g2_ref, w1_ref, w2_ref, fg_ref, out_ref, *, final, fc, hm_rows):
    g = g_ref[...]
    yhy = yhy_ref[...]
    yhy = yhy * lax.rsqrt(_group_mean_sq(yhy, m64_ref) + EPS) * g[:, :HY_WIDTH]
    parts = [yhy.astype(BF16)]
    if hm_rows:
        hm = jnp.concatenate([hm_ref[0, :, r, :] for r in range(hm_rows)], axis=0)
    else:
        hm = hm_ref[0]
    o_gate = _sigmoid(o_ref[0])
    for h in range(ML_HEADS):
        sl = slice(h * ML_HEAD_DIM, (h + 1) * ML_HEAD_DIM)
        hh = hm[:, sl]
        ms = jnp.mean(hh * hh, axis=-1, keepdims=True)
        yh = hh * lax.rsqrt(ms + EPS) * g[:, HY_WIDTH + h * ML_HEAD_DIM:HY_WIDTH + (h + 1) * ML_HEAD_DIM]
        parts.append((yh * o_gate[:, sl]).astype(BF16))
    gp = gate_ref[0]
    gelu = 0.5 * gp * (1.0 + jnp.tanh(math.sqrt(2.0 / math.pi) * (gp + 0.044715 * (gp * gp * gp))))
    yrg = hr_ref[0] * gelu
    yrg = yrg * lax.rsqrt(_group_mean_sq(yrg, m64_ref) + EPS) * g[:, HY_WIDTH + ML_WIDTH:]
    parts.append(yrg.astype(BF16))
    y = jnp.concatenate(parts, axis=-1)
    x = x_ref[0] + gt_ref[...] * _dot(y, wo_ref[...])

    h = _modulated_norm(x, g2_ref[...], sh_ref[...], sc_ref[...]).astype(BF16)
    acc = None
    for j in range(D_FF // fc):
        a = jnp.maximum(_dot(h, w1_ref[:, j * fc:(j + 1) * fc]), 0.0)
        t = _dot((a * a).astype(BF16), w2_ref[j * fc:(j + 1) * fc, :])
        acc = t if acc is None else acc + t
    out = x + gt2_ref[...] * acc
    if final:
        ms = jnp.mean(out * out, axis=-1, keepdims=True)
        out = out * lax.rsqrt(ms + EPS) * fg_ref[...]
    out_ref[0] = out


def _mix_mlp(x, y_hy, hm, z_a, hr, mods, prm, l, row, final_g, final, tm, hm_col_major):
    bsz, seq, d = x.shape
    r = np.arange(HY_WIDTH) // (HY_WIDTH // HY_GROUPS)
    m64 = jnp.asarray((r[:, None] == r[None, :]) / float(HY_WIDTH // HY_GROUPS), dtype=BF16)
    tok = lambda w, j: pl.BlockSpec((1, tm, w), lambda b, i: (b, i, j))
    if hm_col_major:
        hm_rows = tm // GRID_W
        hm = hm.reshape(bsz, GRID_W, seq // GRID_W, ML_WIDTH)
        hm_spec = pl.BlockSpec((1, GRID_W, hm_rows, ML_WIDTH), lambda b, i: (b, 0, i, 0))
    else:
        hm_rows = 0
        hm_spec = tok(ML_WIDTH, 0)
    return pl.pallas_call(
        functools.partial(_mix_mlp_kernel, final=final, fc=1024, hm_rows=hm_rows),
        grid=(bsz, seq // tm),
        in_specs=[tok(d, 0),
                  pl.BlockSpec((tm, HY_WIDTH), lambda b, i: (i, b)),
                  hm_spec,
                  tok(ML_WIDTH, 2),
                  tok(RG_WIDTH, 0),
                  tok(RG_WIDTH, 6),
                  _layer_spec(l, (1, d)),
                  pl.BlockSpec((HY_WIDTH, HY_WIDTH), lambda b, i: (0, 0)),
                  _layer_spec(l, (d, d), single=True), _mod_spec(l, 2, row),
                  _mod_spec(l, 3, row), _mod_spec(l, 4, row), _mod_spec(l, 5, row),
                  _layer_spec(l, (1, d)),
                  _layer_spec(l, (d, D_FF), single=True), _layer_spec(l, (D_FF, d), single=True),
                  pl.BlockSpec((1, d), lambda b, i: (0, 0))],
        out_specs=tok(d, 0),
        out_shape=jax.ShapeDtypeStruct((bsz, seq, d), F32),
        compiler_params=_cparams(("parallel", "parallel")),
        name="mix_mlp",
    )(x, y_hy, hm, z_a, hr, z_a, prm['mix_g'], m64, prm['w_o'], mods, mods, mods, mods, prm['g2'],
      prm['w1'], prm['w2'], final_g)


def _split_w_in(w_in):
    wb = w_in.astype(BF16)
    rg = OFF_RG
    o0 = OFF_ML + 3 * ML_WIDTH
    w_a = jnp.concatenate([wb[..., :HY_COLS], wb[..., rg:rg + RG_WIDTH],
                           wb[..., o0:o0 + ML_WIDTH], wb[..., rg + RG_WIDTH:rg + 2 * RG_WIDTH]], axis=-1)
    g0 = OFF_ML + 4 * ML_WIDTH
    pad = jnp.zeros(wb.shape[:-1] + (ZM_COLS - 3 * ML_WIDTH - ML_GATE_COLS,), BF16)
    w_m = jnp.concatenate([wb[..., OFF_ML:o0], wb[..., g0:g0 + ML_GATE_COLS], pad], axis=-1)
    return w_a, w_m


def _prepare(norm1_g, norm2_g, w_in, hy_conv, hy_w1, hy_b1, hy_w2, hy_b2, hy_w3, hy_b3, hy_freq, hy_bias,
             ml_conv, ml_gate_b, rg_conv, rg_gate_w, rg_gate_b, rg_lambda, mix_norm_g, w_out, mlp_w1, mlp_w2):
    w_a, w_m = _split_w_in(w_in)
    rg_wg, rg_bg = _rg_gate_weights(rg_gate_w, rg_gate_b)
    return dict(
        g1=norm1_g[:, None, :], g2=norm2_g[:, None, :], mix_g=mix_norm_g[:, None, :],
        w_a=w_a, w_m=w_m, w_o=w_out.astype(BF16), w1=mlp_w1.astype(BF16), w2=mlp_w2.astype(BF16),
        hy_conv=hy_conv, hy_bias=hy_bias[:, None, :],
        hy_w1=jnp.pad(hy_w1, ((0, 0), (0, LANES - HY_POS_DIM), (0, 0))), hy_b1=hy_b1[:, None, :],
        hy_w2=hy_w2, hy_b2=hy_b2[:, None, :], hy_w3=hy_w3, hy_b3=hy_b3[:, None, :], hy_freq=hy_freq,
        ml_conv=ml_conv, ml_gate_b=ml_gate_b,
        rg_conv=rg_conv, rg_wg=rg_wg, rg_bg=rg_bg, rg_lambda=rg_lambda)


def kernel(x, c, ctx, c_ctx, ada_w, ada_b, norm1_g, norm2_g, w_in, hy_conv, hy_w1, hy_b1, hy_w2, hy_b2, hy_w3, hy_b3, hy_freq, hy_bias, ml_conv, ml_gate_b, rg_conv, rg_gate_w, rg_gate_b, rg_lambda, mix_norm_g, w_out, mlp_w1, mlp_w2, final_g):
    bsz, seq, d = x.shape
    clen = ctx.shape[1]
    prm = _prepare(norm1_g, norm2_g, w_in, hy_conv, hy_w1, hy_b1, hy_w2, hy_b2, hy_w3, hy_b3, hy_freq,
                   hy_bias, ml_conv, ml_gate_b, rg_conv, rg_gate_w, rg_gate_b, rg_lambda, mix_norm_g,
                   w_out, mlp_w1, mlp_w2)

    cc = jnp.concatenate([c, c_ctx[None], jnp.zeros((MOD_ROWS - bsz - 1, d), F32)], axis=0)
    mods = _ada(cc, ada_w, ada_b).reshape(DEPTH, MOD_ROWS, N_MOD, 1, d)
    ctx_row = bsz

    fwd_l, inv_l = _dft_tables(seq)
    fwd_c, inv_c = _dft_tables(clen)
    dh, nh = ML_HEAD_DIM, ML_HEADS
    ml_zero = (jnp.zeros((bsz, 2, nh, dh, 2 * dh), F32), jnp.zeros((bsz, 2, nh, 1, dh), F32))
    rg_zero = jnp.zeros((bsz, 2, RG_WIDTH), F32)
    fg = final_g[None]

    ctx_s = ctx
    for l in range(DEPTH):
        need_ctx = l < DEPTH - 1
        zc_a, zc_m = _inproj_seq(ctx_s, mods, prm, l, ctx_row, clen)
        hm_c, ml_c, ml_m = _mlstm(zc_m, prm, l, ml_zero)
        hr_c, rg_state = _rglru(zc_a, prm, l, rg_zero)

        z_a, z_m = _inproj_grid(x, mods, prm, l, 16)
        y_hy = _hy_conv(z_a, prm, l, _hy_filter(seq, prm, l, fwd_l), fwd_l, inv_l)
        hm, _, _ = _mlstm(z_m, prm, l, (ml_c, ml_m))
        hr, _ = _rglru(z_a, prm, l, rg_state)
        x = _mix_mlp(x, y_hy, hm, z_a, hr, mods, prm, l, None, fg, final=(l == DEPTH - 1), tm=512,
                     hm_col_major=True)

        if need_ctx:
            y_hy_c = _hy_conv(zc_a, prm, l, _hy_filter(clen, prm, l, fwd_c), fwd_c, inv_c)
            ctx_s = _mix_mlp(ctx_s, y_hy_c, hm_c, zc_a, hr_c, mods, prm, l, ctx_row, fg, final=False,
                             tm=clen, hm_col_major=False)
    return x
```

```python
import functools
import math

import numpy as np
import jax
import jax.numpy as jnp
from jax import lax
from jax.experimental import pallas as pl
from jax.experimental.pallas import tpu as pltpu

F32 = jnp.float32
BF16 = jnp.bfloat16

D_MODEL = 1024
DEPTH = 2
GRID_W = 64
HY_WIDTH = 256
HY_GROUPS = 4
HY_SHORT = 3
HY_POS_DIM = 33
HY_FFN = 64
HY_FAST_DECAY = 0.3
HY_SLOW_DECAY = 1.5
HY_TARGET = 1e-2
ML_WIDTH = 512
ML_HEADS = 4
ML_HEAD_DIM = 128
ML_CONV = 4
ML_CHUNK = 128
RG_WIDTH = 256
RG_HEADS = 4
RG_BLOCK = 64
RG_CONV = 4
RG_C = 8.0
D_FF = 4096
N_MOD = 6
EPS = 1e-6
HY_COLS = 3 * HY_WIDTH
ML_GATE_COLS = 4 * ML_HEADS
ML_COLS = 4 * ML_WIDTH + ML_GATE_COLS
OFF_ML = HY_COLS
OFF_RG = HY_COLS + ML_COLS

LANES = 128
SUBLANES = 8
VMEM_LIMIT = 56 * 1024 * 1024

ZA_COLS = 1792
ZM_COLS = 1664
PAD = SUBLANES
MOD_ROWS = 2 * SUBLANES
ML_HEADS_PER_STEP = 2


def _cparams(sem):
    return pltpu.CompilerParams(dimension_semantics=sem, vmem_limit_bytes=VMEM_LIMIT)


def _const_spec(shape):
    nd = len(shape)
    return pl.BlockSpec(shape, lambda *_: (0,) * nd, pipeline_mode=pl.Buffered(1))


def _layer_spec(l, shape, single=False):
    nd = len(shape)
    mode = dict(pipeline_mode=pl.Buffered(1)) if single else {}
    return pl.BlockSpec((None,) + tuple(shape), lambda *_: (l,) + (0,) * nd, **mode)


def _mod_spec(l, idx, row):
    blk = (None, None, None, 1, D_MODEL)
    if row is None:
        return pl.BlockSpec(blk, lambda i, b: (l, b, idx, 0, 0))
    return pl.BlockSpec(blk, lambda i, b: (l, row, idx, 0, 0))


def _split3(v):
    hi = v.astype(BF16)
    r = v - hi.astype(F32)
    mid = r.astype(BF16)
    lo = (r - mid.astype(F32)).astype(BF16)
    return hi, mid, lo


def _dot(a, b):
    return jnp.dot(a, b, preferred_element_type=F32)


def _dot_f32(a, b):
    ah, am, al = _split3(a)
    bh, bm, bl = _split3(b)
    return (_dot(ah, bh) + (_dot(ah, bm) + _dot(am, bh))
            + (_dot(ah, bl) + _dot(am, bm) + _dot(al, bh)))


def _dot_2x(a, b):
    ah = a.astype(BF16)
    al = (a - ah.astype(F32)).astype(BF16)
    bh = b.astype(BF16)
    bl = (b - bh.astype(F32)).astype(BF16)
    return _dot(ah, bh) + (_dot(ah, bl) + _dot(al, bh))


def _sigmoid(v):
    return 0.5 + 0.5 * jnp.tanh(0.5 * v)


def _silu(v):
    h = 0.5 * v
    return h + h * jnp.tanh(h)


def _modulated_norm(x, g, shift, scale):
    ms = jnp.mean(x * x, axis=-1, keepdims=True)
    return (x * lax.rsqrt(ms + EPS) * g) * (1.0 + scale) + shift


def _dwconv(pad_ref, w_ref, start, nrows, ksize):
    left = (ksize - 1) // 2
    acc = None
    for j in range(ksize):
        lo = PAD + start + j - left
        term = w_ref[j:j + 1, :] * pad_ref[lo:lo + nrows, :]
        acc = term if acc is None else acc + term
    return acc


def _fill_padded(pad_ref, rows, seq):
    width = pad_ref.shape[1]
    pad_ref[0:PAD, :] = jnp.zeros((PAD, width), F32)
    pad_ref[PAD + seq:2 * PAD + seq, :] = jnp.zeros((PAD, width), F32)
    pad_ref[PAD:PAD + seq, :] = rows


def _ada_kernel(c_ref, w_ref, b_ref, o_ref):
    o_ref[0] = _dot_2x(_silu(c_ref[...]), w_ref[0]) + b_ref[0]


def _ada(cc, ada_w, ada_b):
    depth, d, n = ada_w.shape
    tn = 1536
    return pl.pallas_call(
        _ada_kernel,
        grid=(depth, n // tn),
        in_specs=[pl.BlockSpec(cc.shape, lambda l, j: (0, 0)),
                  pl.BlockSpec((1, d, tn), lambda l, j: (l, 0, j)),
                  pl.BlockSpec((1, 1, tn), lambda l, j: (l, 0, j))],
        out_specs=pl.BlockSpec((1, cc.shape[0], tn), lambda l, j: (l, 0, j)),
        out_shape=jax.ShapeDtypeStruct((depth, cc.shape[0], n), F32),
        compiler_params=_cparams(("parallel", "parallel")),
        name="ada",
    )(cc, ada_w, ada_b.reshape(depth, 1, n))


def _inproj_seq_kernel(x_ref, sh_ref, sc_ref, g_ref, wa_ref, wm_ref, za_ref, zm_ref, zr_ref):
    b = pl.program_id(1)
    h = _modulated_norm(x_ref[0], g_ref[...], sh_ref[...], sc_ref[...]).astype(BF16)
    za = _dot(h, wa_ref[...])
    za_ref[0] = za
    zm_ref[0] = _dot(h, wm_ref[...])
    zr_ref[:, pl.ds(b, 1), :] = za[:, HY_COLS:HY_COLS + RG_WIDTH].reshape(za.shape[0], 1, RG_WIDTH)


def _inproj_seq(x, mods, prm, l, row, tm):
    bsz, seq, d = x.shape
    return pl.pallas_call(
        _inproj_seq_kernel,
        grid=(seq // tm, bsz),
        in_specs=[pl.BlockSpec((1, tm, d), lambda i, b: (b, i, 0)),
                  _mod_spec(l, 0, row), _mod_spec(l, 1, row),
                  _layer_spec(l, (1, d)),
                  _layer_spec(l, (d, ZA_COLS), single=True), _layer_spec(l, (d, ZM_COLS), single=True)],
        out_specs=[pl.BlockSpec((1, tm, ZA_COLS), lambda i, b: (b, i, 0)),
                   pl.BlockSpec((1, tm, ZM_COLS), lambda i, b: (b, i, 0)),
                   pl.BlockSpec((tm, bsz, RG_WIDTH), lambda i, b: (i, 0, 0))],
        out_shape=[jax.ShapeDtypeStruct((bsz, seq, ZA_COLS), F32),
                   jax.ShapeDtypeStruct((bsz, seq, ZM_COLS), F32),
                   jax.ShapeDtypeStruct((seq, bsz, RG_WIDTH), F32)],
        compiler_params=_cparams(("parallel", "arbitrary")),
        name="inproj_seq",
    )(x, mods, mods, prm['g1'], prm['w_a'], prm['w_m'])


def _inproj_grid_kernel(x_ref, sh_ref, sc_ref, g_ref, wa_ref, wm_ref, perm_ref, za_ref, zm_ref, zr_ref,
                        *, rows, ncol):
    b = pl.program_id(1)
    d = g_ref.shape[1]
    x = x_ref[0].reshape(rows * ncol, d)
    h = _modulated_norm(x, g_ref[...], sh_ref[...], sc_ref[...]).astype(BF16)
    za = _dot(h, wa_ref[...])
    za_ref[0] = za.reshape(rows, ncol, ZA_COLS)
    zr_ref[:, :, pl.ds(b, 1), :] = za[:, HY_COLS:HY_COLS + RG_WIDTH].reshape(rows, ncol, 1, RG_WIDTH)
    hp = _dot(perm_ref[...], h).astype(BF16)
    zm_ref[0] = _dot(hp, wm_ref[...])


def _inproj_grid(x, mods, prm, l, ncol):
    bsz, seq, d = x.shape
    rows = seq // GRID_W
    tm = rows * ncol
    src = np.arange(tm)
    r, c = src // ncol, src % ncol
    perm = np.zeros((tm, tm), np.float32)
    perm[c * rows + r, src] = 1.0
    za, zm, zr = pl.pallas_call(
        functools.partial(_inproj_grid_kernel, rows=rows, ncol=ncol),
        grid=(GRID_W // ncol, bsz),
        in_specs=[pl.BlockSpec((1, rows, ncol, d), lambda i, b: (b, 0, i, 0)),
                  _mod_spec(l, 0, None), _mod_spec(l, 1, None),
                  _layer_spec(l, (1, d)),
                  _layer_spec(l, (d, ZA_COLS), single=True), _layer_spec(l, (d, ZM_COLS), single=True),
                  _const_spec((tm, tm))],
        out_specs=[pl.BlockSpec((1, rows, ncol, ZA_COLS), lambda i, b: (b, 0, i, 0)),
                   pl.BlockSpec((1, tm, ZM_COLS), lambda i, b: (b, i, 0)),
                   pl.BlockSpec((rows, ncol, bsz, RG_WIDTH), lambda i, b: (0, i, 0, 0))],
        out_shape=[jax.ShapeDtypeStruct((bsz, rows, GRID_W, ZA_COLS), F32),
                   jax.ShapeDtypeStruct((bsz, seq, ZM_COLS), F32),
                   jax.ShapeDtypeStruct((rows, GRID_W, bsz, RG_WIDTH), F32)],
        compiler_params=_cparams(("parallel", "arbitrary")),
        name="inproj_grid",
    )(x.reshape(bsz, rows, GRID_W, d), mods, mods, prm['g1'], prm['w_a'], prm['w_m'],
      jnp.asarray(perm, dtype=BF16))
    return za.reshape(bsz, seq, ZA_COLS), zm, zr.reshape(seq, bsz, RG_WIDTH)


def _dft_tables(seq):
    n = 2 * seq
    k = np.arange(seq, dtype=np.int64)[:, None]
    s = np.arange(seq, dtype=np.int64)[None, :]
    ang = (2.0 * np.pi / n) * ((k * s) % n).astype(np.float64)
    top = np.cos(ang)
    bot = -np.sin(ang)
    bot[0, :] = np.where(np.arange(seq) % 2 == 0, 1.0, -1.0)
    fwd = np.concatenate([top, bot], axis=0)
    scale = np.full((n, 1), 2.0 / n)
    scale[0, 0] = 1.0 / n
    scale[seq, 0] = 1.0 / n
    inv = (fwd * scale).T
    return jnp.asarray(fwd.astype(np.float32)), jnp.asarray(inv.astype(np.float32))


def _hy_feature_tables(seq):
    pos = np.arange(seq, dtype=np.float64)[:, None]
    t = pos / max(seq - 1, 1)
    bands = (HY_POS_DIM - 1) // 2
    fr = np.linspace(1e-4, bands - 1, bands).astype(np.float32).astype(np.float64)[None]
    ang = (2.0 * math.pi / seq) * fr * pos
    feats = np.concatenate([t, np.cos(ang), -np.sin(ang)], axis=-1)
    feats = np.pad(feats, ((0, 0), (0, LANES - HY_POS_DIM)))
    max_decay = math.log(HY_TARGET) / HY_FAST_DECAY
    min_decay = math.log(HY_TARGET) / HY_SLOW_DECAY
    deltas = np.linspace(min_decay, max_decay, HY_WIDTH).astype(np.float32).astype(np.float64)
    decay = np.exp(-t * np.abs(deltas)[None, :])
    return jnp.asarray(feats, dtype=F32), jnp.asarray(decay, dtype=F32)


def _hy_filter_kernel(feats_ref, w1_ref, b1_ref, w2_ref, b2_ref, w3_ref, b3_ref, fq_ref,
                      decay_ref, wf_ref, f_ref, h_scr, *, seq, tr):
    i = pl.program_id(0)

    @pl.when(i == 0)
    def _():
        h = jnp.sin(fq_ref[0:1, :] * (_dot_f32(feats_ref[...], w1_ref[...]) + b1_ref[...]))
        h = jnp.sin(fq_ref[1:2, :] * (_dot_f32(h, w2_ref[...]) + b2_ref[...]))
        h = _dot_f32(h, w3_ref[...]) + b3_ref[...]
        dec = decay_ref[...]
        hf = h[:, :HY_WIDTH] * dec
        hb = h[:, HY_WIDTH:] * dec
        row = lax.broadcasted_iota(jnp.int32, hb.shape, 0)
        hb = jnp.where(row == 0, 0.0, hb)
        h_scr[:, :HY_WIDTH] = hf.astype(BF16)
        h_scr[:, HY_WIDTH:] = hb.astype(BF16)

    g = _dot(wf_ref[...].astype(BF16), h_scr[...])
    gf = g[:, :HY_WIDTH]
    gb = g[:, HY_WIDTH:]
    row = i * tr + lax.broadcasted_iota(jnp.int32, gf.shape, 0)
    f_ref[...] = jnp.where(row <= seq, gf + gb, gf - gb)


def _hy_filter(seq, prm, l, fwd):
    feats, decay = _hy_feature_tables(seq)
    tr = min(seq, 512)
    full = lambda shp: pl.BlockSpec(shp, lambda i: (0,) * len(shp))
    return pl.pallas_call(
        functools.partial(_hy_filter_kernel, seq=seq, tr=tr),
        grid=(2 * seq // tr,),
        in_specs=[full((seq, LANES)), _layer_spec(l, (LANES, HY_FFN)), _layer_spec(l, (1, HY_FFN)),
                  _layer_spec(l, (HY_FFN, HY_FFN)), _layer_spec(l, (1, HY_FFN)),
                  _layer_spec(l, (HY_FFN, 2 * HY_WIDTH)), _layer_spec(l, (1, 2 * HY_WIDTH)),
                  _layer_spec(l, (2, HY_FFN)), full((seq, HY_WIDTH)),
                  pl.BlockSpec((tr, seq), lambda i: (i, 0))],
        out_specs=pl.BlockSpec((tr, HY_WIDTH), lambda i: (i, 0)),
        out_shape=jax.ShapeDtypeStruct((2 * seq, HY_WIDTH), F32),
        scratch_shapes=[pltpu.VMEM((seq, 2 * HY_WIDTH), BF16)],
        compiler_params=_cparams(("arbitrary",)),
        name="hy_filter",
    )(feats, prm['hy_w1'], prm['hy_b1'], prm['hy_w2'], prm['hy_b2'], prm['hy_w3'], prm['hy_b3'],
      prm['hy_freq'], decay, fwd)


def _hy_pre_kernel(z_ref, cw_ref, bias_ref, x0_ref, e_ref, u_ref, pad_ref, *, seq):
    _fill_padded(pad_ref, z_ref[0], seq)
    zc = _dwconv(pad_ref, cw_ref, 0, seq, HY_SHORT)
    x0 = zc[:, :HY_WIDTH]
    u = zc[:, HY_WIDTH:2 * HY_WIDTH] * zc[:, 2 * HY_WIDTH:]
    x0_ref[...] = x0
    e_ref[...] = x0 * (u * bias_ref[...])
    u_ref[...] = u.astype(BF16)


def _hy_fwd_kernel(u_ref, wt_ref, wb_ref, ft_ref, fb_ref, yt_ref, yb_ref, *, nb):
    ft, fb = ft_ref[...], fb_ref[...]
    first = pl.program_id(0) == 0
    row0 = jnp.logical_and(lax.broadcasted_iota(jnp.int32, ft.shape, 0) == 0, first)
    f_im = jnp.where(row0, 0.0, fb)
    f_re2 = jnp.where(row0, fb, ft)
    wt = wt_ref[...].astype(BF16)
    wb = wb_ref[...].astype(BF16)
    for b in range(nb):
        cols = slice(b * HY_WIDTH, (b + 1) * HY_WIDTH)
        ub = u_ref[:, cols]
        pr = _dot(wt, ub)
        pi = _dot(wb, ub)
        yt_ref[:, cols] = (pr * ft - pi * f_im).astype(BF16)
        yb_ref[:, cols] = (pr * f_im + pi * f_re2).astype(BF16)


def _hy_inv_kernel(yt_ref, yb_ref, wt_ref, wb_ref, x0_ref, e_ref, o_ref, *, nb):
    wt = wt_ref[...].astype(BF16)
    wb = wb_ref[...].astype(BF16)
    for b in range(nb):
        cols = slice(b * HY_WIDTH, (b + 1) * HY_WIDTH)
        y = _dot(wt, yt_ref[:, cols]) + _dot(wb, yb_ref[:, cols])
        o_ref[:, cols] = x0_ref[:, cols] * y + e_ref[:, cols]


def _hy_conv(z_a, prm, l, filt, fwd, inv):
    bsz, seq, _ = z_a.shape
    wide = bsz * HY_WIDTH
    bcol = pl.BlockSpec((seq, HY_WIDTH), lambda b: (0, b))
    x0, e, u = pl.pallas_call(
        functools.partial(_hy_pre_kernel, seq=seq),
        grid=(bsz,),
        in_specs=[pl.BlockSpec((1, seq, HY_COLS), lambda b: (b, 0, 0)),
                  _layer_spec(l, (HY_SHORT, HY_COLS)), _layer_spec(l, (1, HY_WIDTH))],
        out_specs=[bcol, bcol, bcol],
        out_shape=[jax.ShapeDtypeStruct((seq, wide), F32), jax.ShapeDtypeStruct((seq, wide), F32),
                   jax.ShapeDtypeStruct((seq, wide), BF16)],
        scratch_shapes=[pltpu.VMEM((seq + 2 * PAD, HY_COLS), F32)],
        compiler_params=_cparams(("parallel",)),
        name="hy_pre",
    )(z_a, prm['hy_conv'], prm['hy_bias'])

    tk = min(seq, 512)
    kt = seq // tk
    yt, yb = pl.pallas_call(
        functools.partial(_hy_fwd_kernel, nb=bsz),
        grid=(kt,),
        in_specs=[_const_spec((seq, wide)),
                  pl.BlockSpec((tk, seq), lambda i: (i, 0)),
                  pl.BlockSpec((tk, seq), lambda i: (kt + i, 0)),
                  pl.BlockSpec((tk, HY_WIDTH), lambda i: (i, 0)),
                  pl.BlockSpec((tk, HY_WIDTH), lambda i: (kt + i, 0))],
        out_specs=[pl.BlockSpec((tk, wide), lambda i: (i, 0)), pl.BlockSpec((tk, wide), lambda i: (i, 0))],
        out_shape=[jax.ShapeDtypeStruct((seq, wide), BF16), jax.ShapeDtypeStruct((seq, wide), BF16)],
        compiler_params=_cparams(("parallel",)),
        name="hy_fwd",
    )(u, fwd, fwd, filt, filt)

    tt = min(seq, 256)
    return pl.pallas_call(
        functools.partial(_hy_inv_kernel, nb=bsz),
        grid=(seq // tt,),
        in_specs=[_const_spec((seq, wide)), _const_spec((seq, wide)),
                  pl.BlockSpec((tt, seq), lambda i: (i, 0)),
                  pl.BlockSpec((tt, seq), lambda i: (i, 1)),
                  pl.BlockSpec((tt, wide), lambda i: (i, 0)),
                  pl.BlockSpec((tt, wide), lambda i: (i, 0))],
        out_specs=pl.BlockSpec((tt, wide), lambda i: (i, 0)),
        out_shape=jax.ShapeDtypeStruct((seq, wide), F32),
        compiler_params=_cparams(("parallel",)),
        name="hy_inv",
    )(yt, yb, inv, inv, x0, e)


def _running_max(v, reverse):
    row = lax.broadcasted_iota(jnp.int32, v.shape, 0)
    k = 1
    while k < ML_CHUNK:
        if reverse:
            sh = jnp.where(row < ML_CHUNK - k, pltpu.roll(v, ML_CHUNK - k, 0), -jnp.inf)
        else:
            sh = jnp.where(row >= k, pltpu.roll(v, k, 0), -jnp.inf)
        v = jnp.maximum(v, sh)
        k *= 2
    return v


def _ml_gate_kernel(x_ref, b_ref, low_ref, up_ref, o_ref):
    for d, tri_ref in ((0, low_ref), (1, up_ref)):
        li = x_ref[2 * d] + b_ref[2 * d]
        pre = x_ref[2 * d + 1] + b_ref[2 * d + 1]
        lf = jnp.minimum(pre, 0.0) - jnp.log(1.0 + jnp.exp(-jnp.abs(pre)))
        hi, mid, lo = _split3(lf)
        tri = tri_ref[...]
        cum = _dot(tri, hi) + _dot(tri, mid) + _dot(tri, lo)
        r = li - cum
        o_ref[3 * d] = r
        o_ref[3 * d + 1] = _running_max(r, reverse=(d == 1))
        o_ref[3 * d + 2] = cum
    o_ref[6] = jnp.zeros(o_ref.shape[1:], F32)
    o_ref[7] = jnp.zeros(o_ref.shape[1:], F32)


def _ml_gates(gates, gate_b):
    bsz, seq, _ = gates.shape
    n = seq // ML_CHUNK
    m = bsz * ML_HEADS * n
    x = gates.reshape(bsz, n, ML_CHUNK, 4, ML_HEADS).transpose(3, 2, 0, 4, 1).reshape(4, ML_CHUNK, m)
    bias = jnp.broadcast_to(gate_b.reshape(4, 1, 1, ML_HEADS, 1), (4, 1, bsz, ML_HEADS, n)).reshape(4, 1, m)
    r = np.arange(ML_CHUNK)
    low = jnp.asarray(r[:, None] >= r[None, :], dtype=BF16)
    up = jnp.asarray(r[:, None] <= r[None, :], dtype=BF16)
    full = lambda shp: pl.BlockSpec(shp, lambda: (0,) * len(shp))
    out = pl.pallas_call(
        _ml_gate_kernel,
        in_specs=[full((4, ML_CHUNK, m)), full((4, 1, m)), full((ML_CHUNK, ML_CHUNK)),
                  full((ML_CHUNK, ML_CHUNK))],
        out_specs=full((SUBLANES, ML_CHUNK, m)),
        out_shape=jax.ShapeDtypeStruct((SUBLANES, ML_CHUNK, m), F32),
        compiler_params=pltpu.CompilerParams(vmem_limit_bytes=VMEM_LIMIT),
        name="ml_gates",
    )(x, bias, low, up)
    return out.reshape(SUBLANES, ML_CHUNK, bsz, ML_HEADS, n).transpose(2, 3, 4, 0, 1)


def _ml_column(rows, j):
    if j:
        rows = pltpu.roll(rows, SUBLANES - j, 0)
    return rows.T[:, 0:1]


def _ml_state_step(d, hh, n, kt_scr, va_scr, row_ref, c_ref, m_ref, cseq_scr, mseq_scr):
    sl = pl.ds(pl.multiple_of(n * ML_CHUNK, ML_CHUNK), ML_CHUNK)
    dh = ML_HEAD_DIM
    kt = kt_scr[hh, n]
    va = va_scr[sl, hh * 2 * dh:(hh + 1) * 2 * dh]
    rows = row_ref[0, hh, n]
    r_r = rows[3 * d:3 * d + 1, :]
    last = ML_CHUNK - 1 if d == 0 else 0
    mx_all = rows[3 * d + 1:3 * d + 2, last:last + 1]
    b_all = rows[3 * d + 2:3 * d + 3, last:last + 1]
    c0 = c_ref[0, d, hh]
    m0v = m_ref[0, d, hh]
    m0 = m0v[:, 0:1]
    cseq_scr[d, hh, n] = c0.astype(BF16)
    mseq_scr[d, hh, n] = m0v
    w_r = jnp.exp(r_r - mx_all)
    upd = _dot(kt * w_r.astype(BF16), va)
    m_loc = b_all + mx_all
    m_new = jnp.maximum(b_all + m0, m_loc)
    a = jnp.exp(b_all + m0 - m_new)
    sc = jnp.exp(m_loc - m_new)
    c_ref[0, d, hh] = a * c0 + sc * upd
    m_ref[0, d, hh] = jnp.broadcast_to(m_new, (1, dh))


def _ml_output(hh, n, q_scr, kt_scr, va_scr, row_ref, cseq_scr, mseq_scr, h_ref):
    dh = ML_HEAD_DIM
    sl = pl.ds(pl.multiple_of(n * ML_CHUNK, ML_CHUNK), ML_CHUNK)
    q = q_scr[sl, hh * dh:(hh + 1) * dh]
    va = va_scr[sl, hh * 2 * dh:(hh + 1) * 2 * dh]
    qk = _dot(q, kt_scr[hh, n])
    tt = lax.broadcasted_iota(jnp.int32, (ML_CHUNK, ML_CHUNK), 0)
    ss = lax.broadcasted_iota(jnp.int32, (ML_CHUNK, ML_CHUNK), 1)
    rows = row_ref[0, hh, n]
    h = None
    for d in range(2):
        r_r = rows[3 * d:3 * d + 1, :]
        mx_c = _ml_column(rows, 3 * d + 1)
        b_c = _ml_column(rows, 3 * d + 2)
        m0 = mseq_scr[d, hh, n][:, 0:1]
        seen = (ss <= tt) if d == 0 else (ss >= tt)
        mu_c = jnp.maximum(m0, mx_c)
        mu = jnp.broadcast_to(mu_c, (ML_CHUNK, ML_CHUNK))
        floor = jnp.broadcast_to(jnp.exp(-(b_c + mu_c)), (ML_CHUNK, ML_CHUNK))
        e = jnp.where(seen, jnp.exp(r_r - mu), 0.0)
        wi = jnp.exp(m0 - mu).astype(BF16)
        s = (qk * e).astype(BF16)
        res = _dot(jnp.concatenate([s, wi * q], axis=1), jnp.concatenate([va, cseq_scr[d, hh, n]], axis=0))
        hd = res[:, :dh] / jnp.maximum(jnp.abs(res[:, dh:]), floor)
        h = hd if h is None else h + hd
    h_ref[0, sl, hh * dh:(hh + 1) * dh] = h


def _ml_kernel(zq_ref, zk_ref, zv_ref, cwq_ref, cwk_ref, row_ref, c0_ref, m0_ref,
               h_ref, c_ref, m_ref, pad_ref, q_scr, kt_scr, va_scr, cseq_scr, mseq_scr, *, seq):
    nchunk = seq // ML_CHUNK
    dh = ML_HEAD_DIM
    hp = ML_HEADS_PER_STEP
    _fill_padded(pad_ref, zq_ref[0], seq)
    q_scr[...] = _silu(_dwconv(pad_ref, cwq_ref, 0, seq, ML_CONV)).astype(BF16)
    pad_ref[PAD:PAD + seq, :] = zk_ref[0]
    for n in range(nchunk):
        rows = slice(n * ML_CHUNK, (n + 1) * ML_CHUNK)
        kk = _silu(_dwconv(pad_ref, cwk_ref, n * ML_CHUNK, ML_CHUNK, ML_CONV)) * (dh ** -0.5)
        for hh in range(hp):
            kt_scr[hh, n] = kk[:, hh * dh:(hh + 1) * dh].T.astype(BF16)
            va_scr[rows, hh * 2 * dh:hh * 2 * dh + dh] = zv_ref[0, rows, hh * dh:(hh + 1) * dh].astype(BF16)
            va_scr[rows, hh * 2 * dh + dh:(hh + 1) * 2 * dh] = jnp.ones((ML_CHUNK, dh), BF16)
    c_ref[...] = c0_ref[...]
    m_ref[...] = m0_ref[...]

    def state_body(i, carry):
        for hh in range(hp):
            for d, n in ((0, i), (1, nchunk - 1 - i)):
                _ml_state_step(d, hh, n, kt_scr, va_scr, row_ref, c_ref, m_ref, cseq_scr, mseq_scr)
        return carry

    lax.fori_loop(0, nchunk, state_body, 0, unroll=2)

    def out_body(n, carry):
        for hh in range(hp):
            _ml_output(hh, n, q_scr, kt_scr, va_scr, row_ref, cseq_scr, mseq_scr, h_ref)
        return carry

    lax.fori_loop(0, nchunk, out_body, 0, unroll=2)


def _mlstm(z_ml, prm, l, state):
    bsz, seq, _ = z_ml.shape
    nchunk = seq // ML_CHUNK
    row = _ml_gates(z_ml[..., 3 * ML_WIDTH:3 * ML_WIDTH + ML_GATE_COLS], prm['ml_gate_b'][l])
    dh, nh, hp = ML_HEAD_DIM, ML_HEADS, ML_HEADS_PER_STEP
    ng = nh // hp
    w = hp * dh
    zspec = lambda off: pl.BlockSpec((1, seq, w), lambda b, g: (b, 0, off + g))
    cspec = pl.BlockSpec((1, 2, hp, dh, 2 * dh), lambda b, g: (b, 0, g, 0, 0))
    mspec = pl.BlockSpec((1, 2, hp, 1, dh), lambda b, g: (b, 0, g, 0, 0))
    return pl.pallas_call(
        functools.partial(_ml_kernel, seq=seq),
        grid=(bsz, ng),
        in_specs=[zspec(0), zspec(ng), zspec(2 * ng),
                  pl.BlockSpec((None, ML_CONV, w), lambda b, g: (l, 0, g)),
                  pl.BlockSpec((None, ML_CONV, w), lambda b, g: (l, 0, ng + g)),
                  pl.BlockSpec((1, hp, nchunk, SUBLANES, ML_CHUNK), lambda b, g: (b, g, 0, 0, 0)),
                  cspec, mspec],
        out_specs=[pl.BlockSpec((1, seq, w), lambda b, g: (b, 0, g)), cspec, mspec],
        out_shape=[jax.ShapeDtypeStruct((bsz, seq, ML_WIDTH), F32),
                   jax.ShapeDtypeStruct((bsz, 2, nh, dh, 2 * dh), F32),
                   jax.ShapeDtypeStruct((bsz, 2, nh, 1, dh), F32)],
        scratch_shapes=[pltpu.VMEM((seq + 2 * PAD, w), F32),
                        pltpu.VMEM((seq, w), BF16),
                        pltpu.VMEM((hp, nchunk, dh, ML_CHUNK), BF16),
                        pltpu.VMEM((seq, 2 * w), BF16),
                        pltpu.VMEM((2, hp, nchunk, dh, 2 * dh), BF16),
                        pltpu.VMEM((2, hp, nchunk, 1, dh), F32)],
        compiler_params=_cparams(("parallel", "parallel")),
        name="mlstm",
    )(z_ml, z_ml, z_ml, prm['ml_conv'], prm['ml_conv'], row, *state)


RG_TILE = 256


def _rg_kernel(z_ref, cw_ref, wg_ref, bg_ref, lam_ref, h0_ref, of_ref, ob_ref, st_ref,
               zp_scr, a_scr, b_scr, h_scr, carry_scr, *, seq, tt):
    i = pl.program_id(0)
    nt = seq // tt
    nb = z_ref.shape[1]
    c = RG_WIDTH

    @pl.when(i == 0)
    def _():
        carry_scr[...] = h0_ref[...]

    lam = lam_ref[...]
    sp = jnp.maximum(-lam, 0.0) + jnp.log(1.0 + jnp.exp(-jnp.abs(lam)))
    coef = (-0.5 * RG_C * math.log2(math.e)) * sp
    left = (RG_CONV - 1) // 2
    right = RG_CONV - 1 - left
    for d, tile in ((0, i), (1, nt - 1 - i)):
        t0 = tile * tt
        zp_scr[left:left + tt] = z_ref[pl.ds(t0, tt)]
        for k in range(left):
            src = t0 - left + k
            zp_scr[k:k + 1] = z_ref[pl.ds(jnp.maximum(src, 0), 1)] * (src >= 0).astype(F32)
        for k in range(right):
            src = t0 + tt + k
            zp_scr[left + tt + k:left + tt + k + 1] = (
                z_ref[pl.ds(jnp.minimum(src, seq - 1), 1)] * (src < seq).astype(F32))
        xr = None
        for j in range(RG_CONV):
            term = cw_ref[j:j + 1, :] * zp_scr[j:j + tt]
            xr = term if xr is None else xr + term
        half_pre = (_dot(xr.reshape(tt * nb, c).astype(BF16), wg_ref[:, 2 * d * c:2 * (d + 1) * c])
                    + bg_ref[:, 2 * d * c:2 * (d + 1) * c])
        t_r = jnp.tanh(half_pre[:, :c])
        t_i = jnp.tanh(half_pre[:, c:])
        cd = coef[d:d + 1, :]
        a = jnp.exp2(cd + cd * t_r)
        y = 1.0 - a * a
        root = jnp.where(y > 0.0, y * lax.rsqrt(y), 0.0)
        half_x = 0.5 * xr.reshape(tt * nb, c)
        a_scr[d] = a.reshape(tt, nb, c)
        b_scr[d] = (root * (half_x + half_x * t_i)).reshape(tt, nb, c)

    def body(s, carry):
        hf, hb = carry
        hf = a_scr[0, s] * hf + b_scr[0, s]
        h_scr[0, s] = hf
        sb = tt - 1 - s
        hb = a_scr[1, sb] * hb + b_scr[1, sb]
        h_scr[1, sb] = hb
        return hf, hb

    hf, hb = lax.fori_loop(0, tt, body, (carry_scr[0], carry_scr[1]), unroll=8)
    carry_scr[0] = hf
    carry_scr[1] = hb
    st_ref[0] = hf
    st_ref[1] = hb
    for d, o_ref in ((0, of_ref), (1, ob_ref)):
        o_ref[...] = jnp.swapaxes(h_scr[d].reshape(tt // SUBLANES, SUBLANES, nb, c), 1, 2)


def _rg_gate_weights(rg_gate_w, rg_gate_b):
    depth = rg_gate_w.shape[0]
    eye = jnp.eye(RG_HEADS, dtype=rg_gate_w.dtype)
    full = jnp.einsum('ldghij,hk->ldghikj', 0.5 * rg_gate_w, eye)
    full = full.reshape(depth, 2, 2, RG_WIDTH, RG_WIDTH)
    wg = full.transpose(0, 3, 1, 2, 4).reshape(depth, RG_WIDTH, 4 * RG_WIDTH).astype(BF16)
    return wg, 0.5 * rg_gate_b.reshape(depth, 1, 4 * RG_WIDTH)


def _rglru(zr, prm, l, h0):
    seq, bsz, c = zr.shape
    tt = min(seq, RG_TILE)
    nt = seq // tt
    ng = tt // SUBLANES
    state_spec = pl.BlockSpec((2, bsz, c), lambda i: (0, 0, 0))
    out_sds = jax.ShapeDtypeStruct((seq // SUBLANES, bsz, SUBLANES, c), F32)
    return pl.pallas_call(
        functools.partial(_rg_kernel, seq=seq, tt=tt),
        grid=(nt,),
        in_specs=[_const_spec((seq, bsz, c)),
                  _layer_spec(l, (RG_CONV, c)), _layer_spec(l, (c, 4 * c)), _layer_spec(l, (1, 4 * c)),
                  _layer_spec(l, (2, c)), state_spec],
        out_specs=[pl.BlockSpec((ng, bsz, SUBLANES, c), lambda i: (i, 0, 0, 0)),
                   pl.BlockSpec((ng, bsz, SUBLANES, c), lambda i: (nt - 1 - i, 0, 0, 0)),
                   state_spec],
        out_shape=[out_sds, out_sds, jax.ShapeDtypeStruct((2, bsz, c), F32)],
        scratch_shapes=[pltpu.VMEM((tt + RG_CONV - 1, bsz, c), F32),
                        pltpu.VMEM((2, tt, bsz, c), F32), pltpu.VMEM((2, tt, bsz, c), F32),
                        pltpu.VMEM((2, tt, bsz, c), F32), pltpu.VMEM((2, bsz, c), F32)],
        compiler_params=_cparams(("arbitrary",)),
        name="rglru",
    )(zr, prm['rg_conv'], prm['rg_wg'], prm['rg_bg'], prm['rg_lambda'], h0)


def _group_mean_sq(y, m_ref):
    sq = y * y
    hi = sq.astype(BF16)
    lo = (sq - hi.astype(F32)).astype(BF16)
    return _dot(hi, m_ref[...]) + _dot(lo, m_ref[...])


def _mix_mlp_kernel(x_ref, yhy_ref, hm_ref, o_ref, hrf_ref, hrb_ref, gate_ref, g_ref, m64_ref, wo_ref,
                    gt_ref, sh_ref, sc_ref, gt2_ref, g2_ref, w1_ref, w2_ref, fg_ref, out_ref,
                    *, final, fc, hm_rows):
    g = g_ref[...]
    yhy = yhy_ref[...]
    yhy = yhy * lax.rsqrt(_group_mean_sq(yhy, m64_ref) + EPS) * g[:, :HY_WIDTH]
    parts = [yhy.astype(BF16)]
    if hm_rows:
        hm = jnp.concatenate([hm_ref[0, :, r, :] for r in range(hm_rows)], axis=0)
    else:
        hm = hm_ref[0]
    o_gate = _sigmoid(o_ref[0])
    for h in range(ML_HEADS):
        sl = slice(h * ML_HEAD_DIM, (h + 1) * ML_HEAD_DIM)
        hh = hm[:, sl]
        ms = jnp.mean(hh * hh, axis=-1, keepdims=True)
        yh = hh * lax.rsqrt(ms + EPS) * g[:, HY_WIDTH + h * ML_HEAD_DIM:HY_WIDTH + (h + 1) * ML_HEAD_DIM]
        parts.append((yh * o_gate[:, sl]).astype(BF16))
    gp = gate_ref[0]
    gelu = 0.5 * gp * (1.0 + jnp.tanh(math.sqrt(2.0 / math.pi) * (gp + 0.044715 * (gp * gp * gp))))
    hr = hrf_ref[...] + hrb_ref[...]
    yrg = hr.reshape(hr.shape[0] * SUBLANES, RG_WIDTH) * gelu
    yrg = yrg * lax.rsqrt(_group_mean_sq(yrg, m64_ref) + EPS) * g[:, HY_WIDTH + ML_WIDTH:]
    parts.append(yrg.astype(BF16))
    y = jnp.concatenate(parts, axis=-1)
    x = x_ref[0] + gt_ref[...] * _dot(y, wo_ref[...])

    h = _modulated_norm(x, g2_ref[...], sh_ref[...], sc_ref[...]).astype(BF16)
    acc = None
    for j in range(D_FF // fc):
        a = jnp.maximum(_dot(h, w1_ref[:, j * fc:(j + 1) * fc]), 0.0)
        t = _dot((a * a).astype(BF16), w2_ref[j * fc:(j + 1) * fc, :])
        acc = t if acc is None else acc + t
    out = x + gt2_ref[...] * acc
    if final:
        ms = jnp.mean(out * out, axis=-1, keepdims=True)
        out = out * lax.rsqrt(ms + EPS) * fg_ref[...]
    out_ref[0] = out


def _mix_mlp(x, y_hy, hm, z_a, hr_f, hr_b, mods, prm, l, row, final_g, final, tm, hm_col_major):
    bsz, seq, d = x.shape
    r = np.arange(HY_WIDTH) // (HY_WIDTH // HY_GROUPS)
    m64 = jnp.asarray((r[:, None] == r[None, :]) / float(HY_WIDTH // HY_GROUPS), dtype=BF16)
    tok = lambda w, j: pl.BlockSpec((1, tm, w), lambda i, b: (b, i, j))
    if hm_col_major:
        hm_rows = tm // GRID_W
        hm = hm.reshape(bsz, GRID_W, seq // GRID_W, ML_WIDTH)
        hm_spec = pl.BlockSpec((1, GRID_W, hm_rows, ML_WIDTH), lambda i, b: (b, 0, i, 0))
    else:
        hm_rows = 0
        hm_spec = tok(ML_WIDTH, 0)
    hr_spec = pl.BlockSpec((tm // SUBLANES, None, SUBLANES, RG_WIDTH), lambda i, b: (i, b, 0, 0))
    return pl.pallas_call(
        functools.partial(_mix_mlp_kernel, final=final, fc=1024, hm_rows=hm_rows),
        grid=(seq // tm, bsz),
        in_specs=[tok(d, 0),
                  pl.BlockSpec((tm, HY_WIDTH), lambda i, b: (i, b)),
                  hm_spec,
                  tok(ML_WIDTH, 2),
                  hr_spec, hr_spec,
                  tok(RG_WIDTH, 6),
                  _layer_spec(l, (1, d)),
                  pl.BlockSpec((HY_WIDTH, HY_WIDTH), lambda i, b: (0, 0)),
                  _layer_spec(l, (d, d), single=True), _mod_spec(l, 2, row),
                  _mod_spec(l, 3, row), _mod_spec(l, 4, row), _mod_spec(l, 5, row),
                  _layer_spec(l, (1, d)),
                  _layer_spec(l, (d, D_FF), single=True), _layer_spec(l, (D_FF, d), single=True),
                  pl.BlockSpec((1, d), lambda i, b: (0, 0))],
        out_specs=tok(d, 0),
        out_shape=jax.ShapeDtypeStruct((bsz, seq, d), F32),
        compiler_params=_cparams(("parallel", "parallel")),
        name="mix_mlp",
    )(x, y_hy, hm, z_a, hr_f, hr_b, z_a, prm['mix_g'], m64, prm['w_o'], mods, mods, mods, mods, prm['g2'],
      prm['w1'], prm['w2'], final_g)


def _split_w_in(w_in):
    wb = w_in.astype(BF16)
    rg = OFF_RG
    o0 = OFF_ML + 3 * ML_WIDTH
    w_a = jnp.concatenate([wb[..., :HY_COLS], wb[..., rg:rg + RG_WIDTH],
                           wb[..., o0:o0 + ML_WIDTH], wb[..., rg + RG_WIDTH:rg + 2 * RG_WIDTH]], axis=-1)
    g0 = OFF_ML + 4 * ML_WIDTH
    pad = jnp.zeros(wb.shape[:-1] + (ZM_COLS - 3 * ML_WIDTH - ML_GATE_COLS,), BF16)
    w_m = jnp.concatenate([wb[..., OFF_ML:o0], wb[..., g0:g0 + ML_GATE_COLS], pad], axis=-1)
    return w_a, w_m


def _prepare(norm1_g, norm2_g, w_in, hy_conv, hy_w1, hy_b1, hy_w2, hy_b2, hy_w3, hy_b3, hy_freq, hy_bias,
             ml_conv, ml_gate_b, rg_conv, rg_gate_w, rg_gate_b, rg_lambda, mix_norm_g, w_out, mlp_w1, mlp_w2):
    w_a, w_m = _split_w_in(w_in)
    rg_wg, rg_bg = _rg_gate_weights(rg_gate_w, rg_gate_b)
    return dict(
        g1=norm1_g[:, None, :], g2=norm2_g[:, None, :], mix_g=mix_norm_g[:, None, :],
        w_a=w_a, w_m=w_m, w_o=w_out.astype(BF16), w1=mlp_w1.astype(BF16), w2=mlp_w2.astype(BF16),
        hy_conv=hy_conv, hy_bias=hy_bias[:, None, :],
        hy_w1=jnp.pad(hy_w1, ((0, 0), (0, LANES - HY_POS_DIM), (0, 0))), hy_b1=hy_b1[:, None, :],
        hy_w2=hy_w2, hy_b2=hy_b2[:, None, :], hy_w3=hy_w3, hy_b3=hy_b3[:, None, :], hy_freq=hy_freq,
        ml_conv=ml_conv, ml_gate_b=ml_gate_b,
        rg_conv=rg_conv, rg_wg=rg_wg, rg_bg=rg_bg, rg_lambda=rg_lambda)


def kernel(x, c, ctx, c_ctx, ada_w, ada_b, norm1_g, norm2_g, w_in, hy_conv, hy_w1, hy_b1, hy_w2, hy_b2, hy_w3, hy_b3, hy_freq, hy_bias, ml_conv, ml_gate_b, rg_conv, rg_gate_w, rg_gate_b, rg_lambda, mix_norm_g, w_out, mlp_w1, mlp_w2, final_g):
    bsz, seq, d = x.shape
    clen = ctx.shape[1]
    prm = _prepare(norm1_g, norm2_g, w_in, hy_conv, hy_w1, hy_b1, hy_w2, hy_b2, hy_w3, hy_b3, hy_freq,
                   hy_bias, ml_conv, ml_gate_b, rg_conv, rg_gate_w, rg_gate_b, rg_lambda, mix_norm_g,
                   w_out, mlp_w1, mlp_w2)

    cc = jnp.concatenate([c, c_ctx[None], jnp.zeros((MOD_ROWS - bsz - 1, d), F32)], axis=0)
    mods = _ada(cc, ada_w, ada_b).reshape(DEPTH, MOD_ROWS, N_MOD, 1, d)
    ctx_row = bsz

    fwd_l, inv_l = _dft_tables(seq)
    fwd_c, inv_c = _dft_tables(clen)
    dh, nh = ML_HEAD_DIM, ML_HEADS
    ml_zero = (jnp.zeros((bsz, 2, nh, dh, 2 * dh), F32), jnp.zeros((bsz, 2, nh, 1, dh), F32))
    rg_zero = jnp.zeros((2, bsz, RG_WIDTH), F32)
    fg = final_g[None]

    ctx_s = ctx
    for l in range(DEPTH):
        need_ctx = l < DEPTH - 1
        zc_a, zc_m, zc_r = _inproj_seq(ctx_s, mods, prm, l, ctx_row, clen)
        hm_c, ml_c, ml_m = _mlstm(zc_m, prm, l, ml_zero)
        hrf_c, hrb_c, rg_state = _rglru(zc_r, prm, l, rg_zero)

        z_a, z_m, z_r = _inproj_grid(x, mods, prm, l, 16)
        y_hy = _hy_conv(z_a, prm, l, _hy_filter(seq, prm, l, fwd_l), fwd_l, inv_l)
        hm, _, _ = _mlstm(z_m, prm, l, (ml_c, ml_m))
        hr_f, hr_b, _ = _rglru(z_r, prm, l, rg_state)
        x = _mix_mlp(x, y_hy, hm, z_a, hr_f, hr_b, mods, prm, l, None, fg, final=(l == DEPTH - 1), tm=512,
                     hm_col_major=True)

        if need_ctx:
            y_hy_c = _hy_conv(zc_a, prm, l, _hy_filter(clen, prm, l, fwd_c), fwd_c, inv_c)
            ctx_s = _mix_mlp(ctx_s, y_hy_c, hm_c, zc_a, hrf_c, hrb_c, mods, prm, l, ctx_row, fg, final=False,
                             tm=clen, hm_col_major=False)
    return x
```

```python
import functools
import math

import numpy as np
import jax
import jax.numpy as jnp
from jax import lax
from jax.experimental import pallas as pl
from jax.experimental.pallas import tpu as pltpu

F32 = jnp.float32
BF16 = jnp.bfloat16

D_MODEL = 1024
DEPTH = 2
GRID_W = 64
HY_WIDTH = 256
HY_GROUPS = 4
HY_SHORT = 3
HY_POS_DIM = 33
HY_FFN = 64
HY_FAST_DECAY = 0.3
HY_SLOW_DECAY = 1.5
HY_TARGET = 1e-2
ML_WIDTH = 512
ML_HEADS = 4
ML_HEAD_DIM = 128
ML_CONV = 4
ML_CHUNK = 128
RG_WIDTH = 256
RG_HEADS = 4
RG_BLOCK = 64
RG_CONV = 4
RG_C = 8.0
D_FF = 4096
N_MOD = 6
EPS = 1e-6
HY_COLS = 3 * HY_WIDTH
ML_GATE_COLS = 4 * ML_HEADS
ML_COLS = 4 * ML_WIDTH + ML_GATE_COLS
OFF_ML = HY_COLS
OFF_RG = HY_COLS + ML_COLS

LANES = 128
SUBLANES = 8
VMEM_LIMIT = 56 * 1024 * 1024

ZA_COLS = 1792
ZM_COLS = 1664
PAD = SUBLANES
MOD_ROWS = 2 * SUBLANES
ML_HEADS_PER_STEP = 2


def _cparams(sem):
    return pltpu.CompilerParams(dimension_semantics=sem, vmem_limit_bytes=VMEM_LIMIT)


def _const_spec(shape):
    nd = len(shape)
    return pl.BlockSpec(shape, lambda *_: (0,) * nd, pipeline_mode=pl.Buffered(1))


def _layer_spec(l, shape, single=False):
    nd = len(shape)
    mode = dict(pipeline_mode=pl.Buffered(1)) if single else {}
    return pl.BlockSpec((None,) + tuple(shape), lambda *_: (l,) + (0,) * nd, **mode)


def _mod_spec(l, idx, row):
    blk = (None, None, None, 1, D_MODEL)
    if row is None:
        return pl.BlockSpec(blk, lambda i, b: (l, b, idx, 0, 0))
    return pl.BlockSpec(blk, lambda i, b: (l, row, idx, 0, 0))


def _split3(v):
    hi = v.astype(BF16)
    r = v - hi.astype(F32)
    mid = r.astype(BF16)
    lo = (r - mid.astype(F32)).astype(BF16)
    return hi, mid, lo


def _dot(a, b):
    return jnp.dot(a, b, preferred_element_type=F32)


def _dot_f32(a, b):
    ah, am, al = _split3(a)
    bh, bm, bl = _split3(b)
    return (_dot(ah, bh) + (_dot(ah, bm) + _dot(am, bh))
            + (_dot(ah, bl) + _dot(am, bm) + _dot(al, bh)))


def _dot_2x(a, b):
    ah = a.astype(BF16)
    al = (a - ah.astype(F32)).astype(BF16)
    bh = b.astype(BF16)
    bl = (b - bh.astype(F32)).astype(BF16)
    return _dot(ah, bh) + (_dot(ah, bl) + _dot(al, bh))


def _sigmoid(v):
    return 0.5 + 0.5 * jnp.tanh(0.5 * v)


def _silu(v):
    h = 0.5 * v
    return h + h * jnp.tanh(h)


def _modulated_norm(x, g, shift, scale):
    ms = jnp.mean(x * x, axis=-1, keepdims=True)
    return (x * lax.rsqrt(ms + EPS) * g) * (1.0 + scale) + shift


def _dwconv(pad_ref, w_ref, start, nrows, ksize):
    left = (ksize - 1) // 2
    acc = None
    for j in range(ksize):
        lo = PAD + start + j - left
        term = w_ref[j:j + 1, :] * pad_ref[lo:lo + nrows, :]
        acc = term if acc is None else acc + term
    return acc


def _fill_padded(pad_ref, rows, seq):
    width = pad_ref.shape[1]
    pad_ref[0:PAD, :] = jnp.zeros((PAD, width), F32)
    pad_ref[PAD + seq:2 * PAD + seq, :] = jnp.zeros((PAD, width), F32)
    pad_ref[PAD:PAD + seq, :] = rows


def _ada_kernel(c_ref, w_ref, b_ref, o_ref):
    o_ref[0] = _dot_2x(_silu(c_ref[...]), w_ref[0]) + b_ref[0]


def _ada(cc, ada_w, ada_b):
    depth, d, n = ada_w.shape
    tn = 1536
    return pl.pallas_call(
        _ada_kernel,
        grid=(depth, n // tn),
        in_specs=[pl.BlockSpec(cc.shape, lambda l, j: (0, 0)),
                  pl.BlockSpec((1, d, tn), lambda l, j: (l, 0, j)),
                  pl.BlockSpec((1, 1, tn), lambda l, j: (l, 0, j))],
        out_specs=pl.BlockSpec((1, cc.shape[0], tn), lambda l, j: (l, 0, j)),
        out_shape=jax.ShapeDtypeStruct((depth, cc.shape[0], n), F32),
        compiler_params=_cparams(("parallel", "parallel")),
        name="ada",
    )(cc, ada_w, ada_b.reshape(depth, 1, n))


def _inproj_seq_kernel(x_ref, sh_ref, sc_ref, g_ref, wa_ref, wm_ref, za_ref, zm_ref, zr_ref):
    b = pl.program_id(1)
    h = _modulated_norm(x_ref[0], g_ref[...], sh_ref[...], sc_ref[...]).astype(BF16)
    za = _dot(h, wa_ref[...])
    za_ref[0] = za
    zm_ref[0] = _dot(h, wm_ref[...])
    zr_ref[:, pl.ds(b, 1), :] = za[:, HY_COLS:HY_COLS + RG_WIDTH].reshape(za.shape[0], 1, RG_WIDTH)


def _inproj_seq(x, mods, prm, l, row, tm):
    bsz, seq, d = x.shape
    return pl.pallas_call(
        _inproj_seq_kernel,
        grid=(seq // tm, bsz),
        in_specs=[pl.BlockSpec((1, tm, d), lambda i, b: (b, i, 0)),
                  _mod_spec(l, 0, row), _mod_spec(l, 1, row),
                  _layer_spec(l, (1, d)),
                  _layer_spec(l, (d, ZA_COLS), single=True), _layer_spec(l, (d, ZM_COLS), single=True)],
        out_specs=[pl.BlockSpec((1, tm, ZA_COLS), lambda i, b: (b, i, 0)),
                   pl.BlockSpec((1, tm, ZM_COLS), lambda i, b: (b, i, 0)),
                   pl.BlockSpec((tm, bsz, RG_WIDTH), lambda i, b: (i, 0, 0))],
        out_shape=[jax.ShapeDtypeStruct((bsz, seq, ZA_COLS), F32),
                   jax.ShapeDtypeStruct((bsz, seq, ZM_COLS), F32),
                   jax.ShapeDtypeStruct((seq, bsz, RG_WIDTH), F32)],
        compiler_params=_cparams(("parallel", "arbitrary")),
        name="inproj_seq",
    )(x, mods, mods, prm['g1'], prm['w_a'], prm['w_m'])


def _inproj_grid_kernel(x_ref, sh_ref, sc_ref, g_ref, wa_ref, wm_ref, perm_ref, za_ref, zm_ref, zr_ref,
                        *, rows, ncol):
    b = pl.program_id(1)
    d = g_ref.shape[1]
    x = x_ref[0].reshape(rows * ncol, d)
    h = _modulated_norm(x, g_ref[...], sh_ref[...], sc_ref[...]).astype(BF16)
    za = _dot(h, wa_ref[...])
    za_ref[0] = za.reshape(rows, ncol, ZA_COLS)
    zr_ref[:, :, pl.ds(b, 1), :] = za[:, HY_COLS:HY_COLS + RG_WIDTH].reshape(rows, ncol, 1, RG_WIDTH)
    hp = _dot(perm_ref[...], h).astype(BF16)
    zm_ref[0] = _dot(hp, wm_ref[...])


def _inproj_grid(x, mods, prm, l, ncol):
    bsz, seq, d = x.shape
    rows = seq // GRID_W
    tm = rows * ncol
    src = np.arange(tm)
    r, c = src // ncol, src % ncol
    perm = np.zeros((tm, tm), np.float32)
    perm[c * rows + r, src] = 1.0
    za, zm, zr = pl.pallas_call(
        functools.partial(_inproj_grid_kernel, rows=rows, ncol=ncol),
        grid=(GRID_W // ncol, bsz),
        in_specs=[pl.BlockSpec((1, rows, ncol, d), lambda i, b: (b, 0, i, 0)),
                  _mod_spec(l, 0, None), _mod_spec(l, 1, None),
                  _layer_spec(l, (1, d)),
                  _layer_spec(l, (d, ZA_COLS), single=True), _layer_spec(l, (d, ZM_COLS), single=True),
                  _const_spec((tm, tm))],
        out_specs=[pl.BlockSpec((1, rows, ncol, ZA_COLS), lambda i, b: (b, 0, i, 0)),
                   pl.BlockSpec((1, tm, ZM_COLS), lambda i, b: (b, i, 0)),
                   pl.BlockSpec((rows, ncol, bsz, RG_WIDTH), lambda i, b: (0, i, 0, 0))],
        out_shape=[jax.ShapeDtypeStruct((bsz, rows, GRID_W, ZA_COLS), F32),
                   jax.ShapeDtypeStruct((bsz, seq, ZM_COLS), F32),
                   jax.ShapeDtypeStruct((rows, GRID_W, bsz, RG_WIDTH), F32)],
        compiler_params=_cparams(("parallel", "arbitrary")),
        name="inproj_grid",
    )(x.reshape(bsz, rows, GRID_W, d), mods, mods, prm['g1'], prm['w_a'], prm['w_m'],
      jnp.asarray(perm, dtype=BF16))
    return za.reshape(bsz, seq, ZA_COLS), zm, zr.reshape(seq, bsz, RG_WIDTH)


def _dft_tables(seq):
    n = 2 * seq
    k = np.arange(seq, dtype=np.int64)[:, None]
    s = np.arange(seq, dtype=np.int64)[None, :]
    ang = (2.0 * np.pi / n) * ((k * s) % n).astype(np.float64)
    top = np.cos(ang)
    bot = -np.sin(ang)
    bot[0, :] = np.where(np.arange(seq) % 2 == 0, 1.0, -1.0)
    fwd = np.concatenate([top, bot], axis=0)
    scale = np.full((n, 1), 2.0 / n)
    scale[0, 0] = 1.0 / n
    scale[seq, 0] = 1.0 / n
    inv = (fwd * scale).T
    return jnp.asarray(fwd.astype(np.float32)), jnp.asarray(inv.astype(np.float32))


def _hy_feature_tables(seq):
    pos = np.arange(seq, dtype=np.float64)[:, None]
    t = pos / max(seq - 1, 1)
    bands = (HY_POS_DIM - 1) // 2
    fr = np.linspace(1e-4, bands - 1, bands).astype(np.float32).astype(np.float64)[None]
    ang = (2.0 * math.pi / seq) * fr * pos
    feats = np.concatenate([t, np.cos(ang), -np.sin(ang)], axis=-1)
    feats = np.pad(feats, ((0, 0), (0, LANES - HY_POS_DIM)))
    max_decay = math.log(HY_TARGET) / HY_FAST_DECAY
    min_decay = math.log(HY_TARGET) / HY_SLOW_DECAY
    deltas = np.linspace(min_decay, max_decay, HY_WIDTH).astype(np.float32).astype(np.float64)
    decay = np.exp(-t * np.abs(deltas)[None, :])
    return jnp.asarray(feats, dtype=F32), jnp.asarray(decay, dtype=F32)


def _hy_filter_kernel(feats_ref, w1_ref, b1_ref, w2_ref, b2_ref, w3_ref, b3_ref, fq_ref,
                      decay_ref, wf_ref, f_ref, h_scr, *, seq, tr):
    i = pl.program_id(0)

    @pl.when(i == 0)
    def _():
        h = jnp.sin(fq_ref[0:1, :] * (_dot_f32(feats_ref[...], w1_ref[...]) + b1_ref[...]))
        h = jnp.sin(fq_ref[1:2, :] * (_dot_f32(h, w2_ref[...]) + b2_ref[...]))
        h = _dot_f32(h, w3_ref[...]) + b3_ref[...]
        dec = decay_ref[...]
        hf = h[:, :HY_WIDTH] * dec
        hb = h[:, HY_WIDTH:] * dec
        row = lax.broadcasted_iota(jnp.int32, hb.shape, 0)
        hb = jnp.where(row == 0, 0.0, hb)
        h_scr[:, :HY_WIDTH] = hf.astype(BF16)
        h_scr[:, HY_WIDTH:] = hb.astype(BF16)

    g = _dot(wf_ref[...].astype(BF16), h_scr[...])
    gf = g[:, :HY_WIDTH]
    gb = g[:, HY_WIDTH:]
    row = i * tr + lax.broadcasted_iota(jnp.int32, gf.shape, 0)
    f_ref[...] = jnp.where(row <= seq, gf + gb, gf - gb)


def _hy_filter(seq, prm, l, fwd):
    feats, decay = _hy_feature_tables(seq)
    tr = min(seq, 512)
    full = lambda shp: pl.BlockSpec(shp, lambda i: (0,) * len(shp))
    return pl.pallas_call(
        functools.partial(_hy_filter_kernel, seq=seq, tr=tr),
        grid=(2 * seq // tr,),
        in_specs=[full((seq, LANES)), _layer_spec(l, (LANES, HY_FFN)), _layer_spec(l, (1, HY_FFN)),
                  _layer_spec(l, (HY_FFN, HY_FFN)), _layer_spec(l, (1, HY_FFN)),
                  _layer_spec(l, (HY_FFN, 2 * HY_WIDTH)), _layer_spec(l, (1, 2 * HY_WIDTH)),
                  _layer_spec(l, (2, HY_FFN)), full((seq, HY_WIDTH)),
                  pl.BlockSpec((tr, seq), lambda i: (i, 0))],
        out_specs=pl.BlockSpec((tr, HY_WIDTH), lambda i: (i, 0)),
        out_shape=jax.ShapeDtypeStruct((2 * seq, HY_WIDTH), F32),
        scratch_shapes=[pltpu.VMEM((seq, 2 * HY_WIDTH), BF16)],
        compiler_params=_cparams(("arbitrary",)),
        name="hy_filter",
    )(feats, prm['hy_w1'], prm['hy_b1'], prm['hy_w2'], prm['hy_b2'], prm['hy_w3'], prm['hy_b3'],
      prm['hy_freq'], decay, fwd)


def _hy_pre_kernel(z_ref, cw_ref, bias_ref, x0_ref, e_ref, u_ref, pad_ref, *, seq):
    _fill_padded(pad_ref, z_ref[0], seq)
    zc = _dwconv(pad_ref, cw_ref, 0, seq, HY_SHORT)
    x0 = zc[:, :HY_WIDTH]
    u = zc[:, HY_WIDTH:2 * HY_WIDTH] * zc[:, 2 * HY_WIDTH:]
    x0_ref[...] = x0
    e_ref[...] = x0 * (u * bias_ref[...])
    u_ref[...] = u.astype(BF16)


def _hy_fwd_kernel(u_ref, wt_ref, wb_ref, ft_ref, fb_ref, yt_ref, yb_ref, *, nb):
    ft, fb = ft_ref[...], fb_ref[...]
    first = pl.program_id(0) == 0
    row0 = jnp.logical_and(lax.broadcasted_iota(jnp.int32, ft.shape, 0) == 0, first)
    f_im = jnp.where(row0, 0.0, fb)
    f_re2 = jnp.where(row0, fb, ft)
    wt = wt_ref[...].astype(BF16)
    wb = wb_ref[...].astype(BF16)
    for b in range(nb):
        cols = slice(b * HY_WIDTH, (b + 1) * HY_WIDTH)
        ub = u_ref[:, cols]
        pr = _dot(wt, ub)
        pi = _dot(wb, ub)
        yt_ref[:, cols] = (pr * ft - pi * f_im).astype(BF16)
        yb_ref[:, cols] = (pr * f_im + pi * f_re2).astype(BF16)


def _hy_inv_kernel(yt_ref, yb_ref, wt_ref, wb_ref, x0_ref, e_ref, o_ref, *, nb):
    wt = wt_ref[...].astype(BF16)
    wb = wb_ref[...].astype(BF16)
    for b in range(nb):
        cols = slice(b * HY_WIDTH, (b + 1) * HY_WIDTH)
        y = _dot(wt, yt_ref[:, cols]) + _dot(wb, yb_ref[:, cols])
        o_ref[:, cols] = x0_ref[:, cols] * y + e_ref[:, cols]


def _hy_conv(z_a, prm, l, filt, fwd, inv):
    bsz, seq, _ = z_a.shape
    wide = bsz * HY_WIDTH
    bcol = pl.BlockSpec((seq, HY_WIDTH), lambda b: (0, b))
    x0, e, u = pl.pallas_call(
        functools.partial(_hy_pre_kernel, seq=seq),
        grid=(bsz,),
        in_specs=[pl.BlockSpec((1, seq, HY_COLS), lambda b: (b, 0, 0)),
                  _layer_spec(l, (HY_SHORT, HY_COLS)), _layer_spec(l, (1, HY_WIDTH))],
        out_specs=[bcol, bcol, bcol],
        out_shape=[jax.ShapeDtypeStruct((seq, wide), F32), jax.ShapeDtypeStruct((seq, wide), F32),
                   jax.ShapeDtypeStruct((seq, wide), BF16)],
        scratch_shapes=[pltpu.VMEM((seq + 2 * PAD, HY_COLS), F32)],
        compiler_params=_cparams(("parallel",)),
        name="hy_pre",
    )(z_a, prm['hy_conv'], prm['hy_bias'])

    tk = min(seq, 512)
    kt = seq // tk
    yt, yb = pl.pallas_call(
        functools.partial(_hy_fwd_kernel, nb=bsz),
        grid=(kt,),
        in_specs=[_const_spec((seq, wide)),
                  pl.BlockSpec((tk, seq), lambda i: (i, 0)),
                  pl.BlockSpec((tk, seq), lambda i: (kt + i, 0)),
                  pl.BlockSpec((tk, HY_WIDTH), lambda i: (i, 0)),
                  pl.BlockSpec((tk, HY_WIDTH), lambda i: (kt + i, 0))],
        out_specs=[pl.BlockSpec((tk, wide), lambda i: (i, 0)), pl.BlockSpec((tk, wide), lambda i: (i, 0))],
        out_shape=[jax.ShapeDtypeStruct((seq, wide), BF16), jax.ShapeDtypeStruct((seq, wide), BF16)],
        compiler_params=_cparams(("parallel",)),
        name="hy_fwd",
    )(u, fwd, fwd, filt, filt)

    tt = min(seq, 256)
    return pl.pallas_call(
        functools.partial(_hy_inv_kernel, nb=bsz),
        grid=(seq // tt,),
        in_specs=[_const_spec((seq, wide)), _const_spec((seq, wide)),
                  pl.BlockSpec((tt, seq), lambda i: (i, 0)),
                  pl.BlockSpec((tt, seq), lambda i: (i, 1)),
                  pl.BlockSpec((tt, wide), lambda i: (i, 0)),
                  pl.BlockSpec((tt, wide), lambda i: (i, 0))],
        out_specs=pl.BlockSpec((tt, wide), lambda i: (i, 0)),
        out_shape=jax.ShapeDtypeStruct((seq, wide), F32),
        compiler_params=_cparams(("parallel",)),
        name="hy_inv",
    )(yt, yb, inv, inv, x0, e)


def _running_max(v, reverse):
    row = lax.broadcasted_iota(jnp.int32, v.shape, 0)
    k = 1
    while k < ML_CHUNK:
        if reverse:
            sh = jnp.where(row < ML_CHUNK - k, pltpu.roll(v, ML_CHUNK - k, 0), -jnp.inf)
        else:
            sh = jnp.where(row >= k, pltpu.roll(v, k, 0), -jnp.inf)
        v = jnp.maximum(v, sh)
        k *= 2
    return v


def _ml_gate_kernel(x_ref, b_ref, low_ref, up_ref, o_ref):
    for d, tri_ref in ((0, low_ref), (1, up_ref)):
        li = x_ref[2 * d] + b_ref[2 * d]
        pre = x_ref[2 * d + 1] + b_ref[2 * d + 1]
        lf = jnp.minimum(pre, 0.0) - jnp.log(1.0 + jnp.exp(-jnp.abs(pre)))
        hi, mid, lo = _split3(lf)
        tri = tri_ref[...]
        cum = _dot(tri, hi) + _dot(tri, mid) + _dot(tri, lo)
        r = li - cum
        o_ref[3 * d] = r
        o_ref[3 * d + 1] = _running_max(r, reverse=(d == 1))
        o_ref[3 * d + 2] = cum
    o_ref[6] = jnp.zeros(o_ref.shape[1:], F32)
    o_ref[7] = jnp.zeros(o_ref.shape[1:], F32)


def _ml_gates(gates, gate_b):
    bsz, seq, _ = gates.shape
    n = seq // ML_CHUNK
    m = bsz * ML_HEADS * n
    x = gates.reshape(bsz, n, ML_CHUNK, 4, ML_HEADS).transpose(3, 2, 0, 4, 1).reshape(4, ML_CHUNK, m)
    bias = jnp.broadcast_to(gate_b.reshape(4, 1, 1, ML_HEADS, 1), (4, 1, bsz, ML_HEADS, n)).reshape(4, 1, m)
    r = np.arange(ML_CHUNK)
    low = jnp.asarray(r[:, None] >= r[None, :], dtype=BF16)
    up = jnp.asarray(r[:, None] <= r[None, :], dtype=BF16)
    full = lambda shp: pl.BlockSpec(shp, lambda: (0,) * len(shp))
    out = pl.pallas_call(
        _ml_gate_kernel,
        in_specs=[full((4, ML_CHUNK, m)), full((4, 1, m)), full((ML_CHUNK, ML_CHUNK)),
                  full((ML_CHUNK, ML_CHUNK))],
        out_specs=full((SUBLANES, ML_CHUNK, m)),
        out_shape=jax.ShapeDtypeStruct((SUBLANES, ML_CHUNK, m), F32),
        compiler_params=pltpu.CompilerParams(vmem_limit_bytes=VMEM_LIMIT),
        name="ml_gates",
    )(x, bias, low, up)
    return out.reshape(SUBLANES, ML_CHUNK, bsz, ML_HEADS, n).transpose(2, 3, 4, 0, 1)


def _ml_column(rows, j):
    if j:
        rows = pltpu.roll(rows, SUBLANES - j, 0)
    return rows.T[:, 0:1]


def _ml_state_step(d, hh, n, kt_scr, va_scr, row_ref, c_ref, m_ref, cseq_scr, mseq_scr):
    sl = pl.ds(pl.multiple_of(n * ML_CHUNK, ML_CHUNK), ML_CHUNK)
    dh = ML_HEAD_DIM
    kt = kt_scr[hh, n]
    va = va_scr[sl, hh * 2 * dh:(hh + 1) * 2 * dh]
    rows = row_ref[0, hh, n]
    r_r = rows[3 * d:3 * d + 1, :]
    last = ML_CHUNK - 1 if d == 0 else 0
    mx_all = rows[3 * d + 1:3 * d + 2, last:last + 1]
    b_all = rows[3 * d + 2:3 * d + 3, last:last + 1]
    c0 = c_ref[0, d, hh]
    m0v = m_ref[0, d, hh]
    m0 = m0v[:, 0:1]
    cseq_scr[d, hh, n] = c0.astype(BF16)
    mseq_scr[d, hh, n] = m0v
    w_r = jnp.exp(r_r - mx_all)
    upd = _dot(kt * w_r.astype(BF16), va)
    m_loc = b_all + mx_all
    m_new = jnp.maximum(b_all + m0, m_loc)
    a = jnp.exp(b_all + m0 - m_new)
    sc = jnp.exp(m_loc - m_new)
    c_ref[0, d, hh] = a * c0 + sc * upd
    m_ref[0, d, hh] = jnp.broadcast_to(m_new, (1, dh))


def _ml_output(hh, n, q_scr, kt_scr, va_scr, row_ref, cseq_scr, mseq_scr, h_ref):
    dh = ML_HEAD_DIM
    sl = pl.ds(pl.multiple_of(n * ML_CHUNK, ML_CHUNK), ML_CHUNK)
    q = q_scr[sl, hh * dh:(hh + 1) * dh]
    va = va_scr[sl, hh * 2 * dh:(hh + 1) * 2 * dh]
    qk = _dot(q, kt_scr[hh, n])
    tt = lax.broadcasted_iota(jnp.int32, (ML_CHUNK, ML_CHUNK), 0)
    ss = lax.broadcasted_iota(jnp.int32, (ML_CHUNK, ML_CHUNK), 1)
    rows = row_ref[0, hh, n]
    h = None
    for d in range(2):
        r_r = rows[3 * d:3 * d + 1, :]
        mx_c = _ml_column(rows, 3 * d + 1)
        b_c = _ml_column(rows, 3 * d + 2)
        m0 = mseq_scr[d, hh, n][:, 0:1]
        seen = (ss <= tt) if d == 0 else (ss >= tt)
        mu_c = jnp.maximum(m0, mx_c)
        mu = jnp.broadcast_to(mu_c, (ML_CHUNK, ML_CHUNK))
        floor = jnp.broadcast_to(jnp.exp(-(b_c + mu_c)), (ML_CHUNK, ML_CHUNK))
        e = jnp.where(seen, jnp.exp(r_r - mu), 0.0)
        wi = jnp.exp(m0 - mu).astype(BF16)
        s = (qk * e).astype(BF16)
        res = _dot(jnp.concatenate([s, wi * q], axis=1), jnp.concatenate([va, cseq_scr[d, hh, n]], axis=0))
        hd = res[:, :dh] / jnp.maximum(jnp.abs(res[:, dh:]), floor)
        h = hd if h is None else h + hd
    h_ref[0, sl, hh * dh:(hh + 1) * dh] = h


def _ml_kernel(zq_ref, zk_ref, zv_ref, cwq_ref, cwk_ref, row_ref, c0_ref, m0_ref,
               h_ref, c_ref, m_ref, pad_ref, q_scr, kt_scr, va_scr, cseq_scr, mseq_scr, *, seq):
    nchunk = seq // ML_CHUNK
    dh = ML_HEAD_DIM
    hp = ML_HEADS_PER_STEP
    _fill_padded(pad_ref, zq_ref[0], seq)
    q_scr[...] = _silu(_dwconv(pad_ref, cwq_ref, 0, seq, ML_CONV)).astype(BF16)
    pad_ref[PAD:PAD + seq, :] = zk_ref[0]
    for n in range(nchunk):
        rows = slice(n * ML_CHUNK, (n + 1) * ML_CHUNK)
        kk = _silu(_dwconv(pad_ref, cwk_ref, n * ML_CHUNK, ML_CHUNK, ML_CONV)) * (dh ** -0.5)
        for hh in range(hp):
            kt_scr[hh, n] = kk[:, hh * dh:(hh + 1) * dh].T.astype(BF16)
            va_scr[rows, hh * 2 * dh:hh * 2 * dh + dh] = zv_ref[0, rows, hh * dh:(hh + 1) * dh].astype(BF16)
            va_scr[rows, hh * 2 * dh + dh:(hh + 1) * 2 * dh] = jnp.ones((ML_CHUNK, dh), BF16)
    c_ref[...] = c0_ref[...]
    m_ref[...] = m0_ref[...]

    def state_body(i, carry):
        for hh in range(hp):
            for d, n in ((0, i), (1, nchunk - 1 - i)):
                _ml_state_step(d, hh, n, kt_scr, va_scr, row_ref, c_ref, m_ref, cseq_scr, mseq_scr)
        return carry

    lax.fori_loop(0, nchunk, state_body, 0, unroll=2)

    def out_body(n, carry):
        for hh in range(hp):
            _ml_output(hh, n, q_scr, kt_scr, va_scr, row_ref, cseq_scr, mseq_scr, h_ref)
        return carry

    lax.fori_loop(0, nchunk, out_body, 0, unroll=min(8, nchunk))


def _mlstm(z_ml, prm, l, state):
    bsz, seq, _ = z_ml.shape
    nchunk = seq // ML_CHUNK
    row = _ml_gates(z_ml[..., 3 * ML_WIDTH:3 * ML_WIDTH + ML_GATE_COLS], prm['ml_gate_b'][l])
    dh, nh, hp = ML_HEAD_DIM, ML_HEADS, ML_HEADS_PER_STEP
    ng = nh // hp
    w = hp * dh
    zspec = lambda off: pl.BlockSpec((1, seq, w), lambda b, g: (b, 0, off + g))
    cspec = pl.BlockSpec((1, 2, hp, dh, 2 * dh), lambda b, g: (b, 0, g, 0, 0))
    mspec = pl.BlockSpec((1, 2, hp, 1, dh), lambda b, g: (b, 0, g, 0, 0))
    return pl.pallas_call(
        functools.partial(_ml_kernel, seq=seq),
        grid=(bsz, ng),
        in_specs=[zspec(0), zspec(ng), zspec(2 * ng),
                  pl.BlockSpec((None, ML_CONV, w), lambda b, g: (l, 0, g)),
                  pl.BlockSpec((None, ML_CONV, w), lambda b, g: (l, 0, ng + g)),
                  pl.BlockSpec((1, hp, nchunk, SUBLANES, ML_CHUNK), lambda b, g: (b, g, 0, 0, 0)),
                  cspec, mspec],
        out_specs=[pl.BlockSpec((1, seq, w), lambda b, g: (b, 0, g)), cspec, mspec],
        out_shape=[jax.ShapeDtypeStruct((bsz, seq, ML_WIDTH), F32),
                   jax.ShapeDtypeStruct((bsz, 2, nh, dh, 2 * dh), F32),
                   jax.ShapeDtypeStruct((bsz, 2, nh, 1, dh), F32)],
        scratch_shapes=[pltpu.VMEM((seq + 2 * PAD, w), F32),
                        pltpu.VMEM((seq, w), BF16),
                        pltpu.VMEM((hp, nchunk, dh, ML_CHUNK), BF16),
                        pltpu.VMEM((seq, 2 * w), BF16),
                        pltpu.VMEM((2, hp, nchunk, dh, 2 * dh), BF16),
                        pltpu.VMEM((2, hp, nchunk, 1, dh), F32)],
        compiler_params=_cparams(("parallel", "parallel")),
        name="mlstm",
    )(z_ml, z_ml, z_ml, prm['ml_conv'], prm['ml_conv'], row, *state)


RG_TILE = 256


def _rg_kernel(z_ref, cw_ref, wg_ref, bg_ref, lam_ref, h0_ref, of_ref, ob_ref, st_ref,
               zp_scr, a_scr, b_scr, h_scr, carry_scr, *, seq, tt):
    i = pl.program_id(0)
    nt = seq // tt
    nb = z_ref.shape[1]
    c = RG_WIDTH

    @pl.when(i == 0)
    def _():
        carry_scr[...] = h0_ref[...]

    lam = lam_ref[...]
    sp = jnp.maximum(-lam, 0.0) + jnp.log(1.0 + jnp.exp(-jnp.abs(lam)))
    coef = (-0.5 * RG_C * math.log2(math.e)) * sp
    left = (RG_CONV - 1) // 2
    right = RG_CONV - 1 - left
    for d, tile in ((0, i), (1, nt - 1 - i)):
        t0 = tile * tt
        zp_scr[left:left + tt] = z_ref[pl.ds(t0, tt)]
        for k in range(left):
            src = t0 - left + k
            zp_scr[k:k + 1] = z_ref[pl.ds(jnp.maximum(src, 0), 1)] * (src >= 0).astype(F32)
        for k in range(right):
            src = t0 + tt + k
            zp_scr[left + tt + k:left + tt + k + 1] = (
                z_ref[pl.ds(jnp.minimum(src, seq - 1), 1)] * (src < seq).astype(F32))
        xr = None
        for j in range(RG_CONV):
            term = cw_ref[j:j + 1, :] * zp_scr[j:j + tt]
            xr = term if xr is None else xr + term
        half_pre = (_dot(xr.reshape(tt * nb, c).astype(BF16), wg_ref[:, 2 * d * c:2 * (d + 1) * c])
                    + bg_ref[:, 2 * d * c:2 * (d + 1) * c])
        t_r = jnp.tanh(half_pre[:, :c])
        t_i = jnp.tanh(half_pre[:, c:])
        cd = coef[d:d + 1, :]
        a = jnp.exp2(cd + cd * t_r)
        y = 1.0 - a * a
        root = jnp.where(y > 0.0, y * lax.rsqrt(y), 0.0)
        half_x = 0.5 * xr.reshape(tt * nb, c)
        a_scr[d] = a.reshape(tt, nb, c)
        b_scr[d] = (root * (half_x + half_x * t_i)).reshape(tt, nb, c)

    def body(s, carry):
        hf, hb = carry
        hf = a_scr[0, s] * hf + b_scr[0, s]
        h_scr[0, s] = hf
        sb = tt - 1 - s
        hb = a_scr[1, sb] * hb + b_scr[1, sb]
        h_scr[1, sb] = hb
        return hf, hb

    hf, hb = lax.fori_loop(0, tt, body, (carry_scr[0], carry_scr[1]), unroll=8)
    carry_scr[0] = hf
    carry_scr[1] = hb
    st_ref[0] = hf
    st_ref[1] = hb
    for d, o_ref in ((0, of_ref), (1, ob_ref)):
        o_ref[...] = jnp.swapaxes(h_scr[d].reshape(tt // SUBLANES, SUBLANES, nb, c), 1, 2)


def _rg_gate_weights(rg_gate_w, rg_gate_b):
    depth = rg_gate_w.shape[0]
    eye = jnp.eye(RG_HEADS, dtype=rg_gate_w.dtype)
    full = jnp.einsum('ldghij,hk->ldghikj', 0.5 * rg_gate_w, eye)
    full = full.reshape(depth, 2, 2, RG_WIDTH, RG_WIDTH)
    wg = full.transpose(0, 3, 1, 2, 4).reshape(depth, RG_WIDTH, 4 * RG_WIDTH).astype(BF16)
    return wg, 0.5 * rg_gate_b.reshape(depth, 1, 4 * RG_WIDTH)


def _rglru(zr, prm, l, h0):
    seq, bsz, c = zr.shape
    tt = min(seq, RG_TILE)
    nt = seq // tt
    ng = tt // SUBLANES
    state_spec = pl.BlockSpec((2, bsz, c), lambda i: (0, 0, 0))
    out_sds = jax.ShapeDtypeStruct((seq // SUBLANES, bsz, SUBLANES, c), F32)
    return pl.pallas_call(
        functools.partial(_rg_kernel, seq=seq, tt=tt),
        grid=(nt,),
        in_specs=[_const_spec((seq, bsz, c)),
                  _layer_spec(l, (RG_CONV, c)), _layer_spec(l, (c, 4 * c)), _layer_spec(l, (1, 4 * c)),
                  _layer_spec(l, (2, c)), state_spec],
        out_specs=[pl.BlockSpec((ng, bsz, SUBLANES, c), lambda i: (i, 0, 0, 0)),
                   pl.BlockSpec((ng, bsz, SUBLANES, c), lambda i: (nt - 1 - i, 0, 0, 0)),
                   state_spec],
        out_shape=[out_sds, out_sds, jax.ShapeDtypeStruct((2, bsz, c), F32)],
        scratch_shapes=[pltpu.VMEM((tt + RG_CONV - 1, bsz, c), F32),
                        pltpu.VMEM((2, tt, bsz, c), F32), pltpu.VMEM((2, tt, bsz, c), F32),
                        pltpu.VMEM((2, tt, bsz, c), F32), pltpu.VMEM((2, bsz, c), F32)],
        compiler_params=_cparams(("arbitrary",)),
        name="rglru",
    )(zr, prm['rg_conv'], prm['rg_wg'], prm['rg_bg'], prm['rg_lambda'], h0)


def _group_mean_sq(y, group):
    assert 2 * group == LANES and y.shape[1] % LANES == 0
    sq = y * y
    outs = []
    for t in range(y.shape[1] // LANES):
        blk = sq[:, t * LANES:(t + 1) * LANES]
        low = lax.broadcasted_iota(jnp.int32, blk.shape, 1) < group
        s_low = jnp.sum(jnp.where(low, blk, 0.0), axis=-1, keepdims=True)
        s_high = jnp.sum(jnp.where(low, 0.0, blk), axis=-1, keepdims=True)
        outs.append(jnp.where(low, s_low, s_high) * (1.0 / group))
    return jnp.concatenate(outs, axis=-1)


def _mix_mlp_kernel(x_ref, yhy_ref, hm_ref, o_ref, hrf_ref, hrb_ref, gate_ref, g_ref, wo_ref,
                    gt_ref, sh_ref, sc_ref, gt2_ref, g2_ref, w1_ref, w2_ref, fg_ref, out_ref,
                    *, final, fc, hm_rows):
    g = g_ref[...]
    yhy = yhy_ref[...]
    yhy = yhy * lax.rsqrt(_group_mean_sq(yhy, HY_WIDTH // HY_GROUPS) + EPS) * g[:, :HY_WIDTH]
    parts = [yhy.astype(BF16)]
    if hm_rows:
        hm = jnp.concatenate([hm_ref[0, :, r, :] for r in range(hm_rows)], axis=0)
    else:
        hm = hm_ref[0]
    o_gate = _sigmoid(o_ref[0])
    for h in range(ML_HEADS):
        sl = slice(h * ML_HEAD_DIM, (h + 1) * ML_HEAD_DIM)
        hh = hm[:, sl]
        ms = jnp.mean(hh * hh, axis=-1, keepdims=True)
        yh = hh * lax.rsqrt(ms + EPS) * g[:, HY_WIDTH + h * ML_HEAD_DIM:HY_WIDTH + (h + 1) * ML_HEAD_DIM]
        parts.append((yh * o_gate[:, sl]).astype(BF16))
    gp = gate_ref[0]
    gelu = 0.5 * gp * (1.0 + jnp.tanh(math.sqrt(2.0 / math.pi) * (gp + 0.044715 * (gp * gp * gp))))
    hr = hrf_ref[...] + hrb_ref[...]
    yrg = hr.reshape(hr.shape[0] * SUBLANES, RG_WIDTH) * gelu
    yrg = yrg * lax.rsqrt(_group_mean_sq(yrg, RG_BLOCK) + EPS) * g[:, HY_WIDTH + ML_WIDTH:]
    parts.append(yrg.astype(BF16))
    y = jnp.concatenate(parts, axis=-1)
    x = x_ref[0] + gt_ref[...] * _dot(y, wo_ref[...])

    h = _modulated_norm(x, g2_ref[...], sh_ref[...], sc_ref[...]).astype(BF16)
    acc = None
    for j in range(D_FF // fc):
        a = jnp.maximum(_dot(h, w1_ref[:, j * fc:(j + 1) * fc]), 0.0)
        t = _dot((a * a).astype(BF16), w2_ref[j * fc:(j + 1) * fc, :])
        acc = t if acc is None else acc + t
    out = x + gt2_ref[...] * acc
    if final:
        ms = jnp.mean(out * out, axis=-1, keepdims=True)
        out = out * lax.rsqrt(ms + EPS) * fg_ref[...]
    out_ref[0] = out


def _mix_mlp(x, y_hy, hm, z_a, hr_f, hr_b, mods, prm, l, row, final_g, final, tm, hm_col_major):
    bsz, seq, d = x.shape
    tok = lambda w, j: pl.BlockSpec((1, tm, w), lambda i, b: (b, i, j))
    if hm_col_major:
        hm_rows = tm // GRID_W
        hm = hm.reshape(bsz, GRID_W, seq // GRID_W, ML_WIDTH)
        hm_spec = pl.BlockSpec((1, GRID_W, hm_rows, ML_WIDTH), lambda i, b: (b, 0, i, 0))
    else:
        hm_rows = 0
        hm_spec = tok(ML_WIDTH, 0)
    hr_spec = pl.BlockSpec((tm // SUBLANES, None, SUBLANES, RG_WIDTH), lambda i, b: (i, b, 0, 0))
    return pl.pallas_call(
        functools.partial(_mix_mlp_kernel, final=final, fc=1024, hm_rows=hm_rows),
        grid=(seq // tm, bsz),
        in_specs=[tok(d, 0),
                  pl.BlockSpec((tm, HY_WIDTH), lambda i, b: (i, b)),
                  hm_spec,
                  tok(ML_WIDTH, 2),
                  hr_spec, hr_spec,
                  tok(RG_WIDTH, 6),
                  _layer_spec(l, (1, d)),
                  _layer_spec(l, (d, d), single=True), _mod_spec(l, 2, row),
                  _mod_spec(l, 3, row), _mod_spec(l, 4, row), _mod_spec(l, 5, row),
                  _layer_spec(l, (1, d)),
                  _layer_spec(l, (d, D_FF), single=True), _layer_spec(l, (D_FF, d), single=True),
                  pl.BlockSpec((1, d), lambda i, b: (0, 0))],
        out_specs=tok(d, 0),
        out_shape=jax.ShapeDtypeStruct((bsz, seq, d), F32),
        compiler_params=_cparams(("parallel", "parallel")),
        name="mix_mlp",
    )(x, y_hy, hm, z_a, hr_f, hr_b, z_a, prm['mix_g'], prm['w_o'], mods, mods, mods, mods, prm['g2'],
      prm['w1'], prm['w2'], final_g)


def _split_w_in(w_in):
    wb = w_in.astype(BF16)
    rg = OFF_RG
    o0 = OFF_ML + 3 * ML_WIDTH
    w_a = jnp.concatenate([wb[..., :HY_COLS], wb[..., rg:rg + RG_WIDTH],
                           wb[..., o0:o0 + ML_WIDTH], wb[..., rg + RG_WIDTH:rg + 2 * RG_WIDTH]], axis=-1)
    g0 = OFF_ML + 4 * ML_WIDTH
    pad = jnp.zeros(wb.shape[:-1] + (ZM_COLS - 3 * ML_WIDTH - ML_GATE_COLS,), BF16)
    w_m = jnp.concatenate([wb[..., OFF_ML:o0], wb[..., g0:g0 + ML_GATE_COLS], pad], axis=-1)
    return w_a, w_m


def _prepare(norm1_g, norm2_g, w_in, hy_conv, hy_w1, hy_b1, hy_w2, hy_b2, hy_w3, hy_b3, hy_freq, hy_bias,
             ml_conv, ml_gate_b, rg_conv, rg_gate_w, rg_gate_b, rg_lambda, mix_norm_g, w_out, mlp_w1, mlp_w2):
    w_a, w_m = _split_w_in(w_in)
    rg_wg, rg_bg = _rg_gate_weights(rg_gate_w, rg_gate_b)
    return dict(
        g1=norm1_g[:, None, :], g2=norm2_g[:, None, :], mix_g=mix_norm_g[:, None, :],
        w_a=w_a, w_m=w_m, w_o=w_out.astype(BF16), w1=mlp_w1.astype(BF16), w2=mlp_w2.astype(BF16),
        hy_conv=hy_conv, hy_bias=hy_bias[:, None, :],
        hy_w1=jnp.pad(hy_w1, ((0, 0), (0, LANES - HY_POS_DIM), (0, 0))), hy_b1=hy_b1[:, None, :],
        hy_w2=hy_w2, hy_b2=hy_b2[:, None, :], hy_w3=hy_w3, hy_b3=hy_b3[:, None, :], hy_freq=hy_freq,
        ml_conv=ml_conv, ml_gate_b=ml_gate_b,
        rg_conv=rg_conv, rg_wg=rg_wg, rg_bg=rg_bg, rg_lambda=rg_lambda)


def kernel(x, c, ctx, c_ctx, ada_w, ada_b, norm1_g, norm2_g, w_in, hy_conv, hy_w1, hy_b1, hy_w2, hy_b2, hy_w3, hy_b3, hy_freq, hy_bias, ml_conv, ml_gate_b, rg_conv, rg_gate_w, rg_gate_b, rg_lambda, mix_norm_g, w_out, mlp_w1, mlp_w2, final_g):
    bsz, seq, d = x.shape
    clen = ctx.shape[1]
    prm = _prepare(norm1_g, norm2_g, w_in, hy_conv, hy_w1, hy_b1, hy_w2, hy_b2, hy_w3, hy_b3, hy_freq,
                   hy_bias, ml_conv, ml_gate_b, rg_conv, rg_gate_w, rg_gate_b, rg_lambda, mix_norm_g,
                   w_out, mlp_w1, mlp_w2)

    cc = jnp.concatenate([c, c_ctx[None], jnp.zeros((MOD_ROWS - bsz - 1, d), F32)], axis=0)
    mods = _ada(cc, ada_w, ada_b).reshape(DEPTH, MOD_ROWS, N_MOD, 1, d)
    ctx_row = bsz

    fwd_l, inv_l = _dft_tables(seq)
    fwd_c, inv_c = _dft_tables(clen)
    dh, nh = ML_HEAD_DIM, ML_HEADS
    ml_zero = (jnp.zeros((bsz, 2, nh, dh, 2 * dh), F32), jnp.zeros((bsz, 2, nh, 1, dh), F32))
    rg_zero = jnp.zeros((2, bsz, RG_WIDTH), F32)
    fg = final_g[None]

    ctx_s = ctx
    for l in range(DEPTH):
        need_ctx = l < DEPTH - 1
        zc_a, zc_m, zc_r = _inproj_seq(ctx_s, mods, prm, l, ctx_row, clen)
        hm_c, ml_c, ml_m = _mlstm(zc_m, prm, l, ml_zero)
        hrf_c, hrb_c, rg_state = _rglru(zc_r, prm, l, rg_zero)

        z_a, z_m, z_r = _inproj_grid(x, mods, prm, l, 16)
        y_hy = _hy_conv(z_a, prm, l, _hy_filter(seq, prm, l, fwd_l), fwd_l, inv_l)
        hm, _, _ = _mlstm(z_m, prm, l, (ml_c, ml_m))
        hr_f, hr_b, _ = _rglru(z_r, prm, l, rg_state)
        x = _mix_mlp(x, y_hy, hm, z_a, hr_f, hr_b, mods, prm, l, None, fg, final=(l == DEPTH - 1), tm=512,
                     hm_col_major=True)

        if need_ctx:
            y_hy_c = _hy_conv(zc_a, prm, l, _hy_filter(clen, prm, l, fwd_c), fwd_c, inv_c)
            ctx_s = _mix_mlp(ctx_s, y_hy_c, hm_c, zc_a, hrf_c, hrb_c, mods, prm, l, ctx_row, fg, final=False,
                             tm=clen, hm_col_major=False)
    return x
```

```python
import functools
import math

import numpy as np
import jax
import jax.numpy as jnp
from jax import lax
from jax.experimental import pallas as pl
from jax.experimental.pallas import tpu as pltpu

F32 = jnp.float32
BF16 = jnp.bfloat16

D_MODEL = 1024
DEPTH = 2
GRID_W = 64
HY_WIDTH = 256
HY_GROUPS = 4
HY_SHORT = 3
HY_POS_DIM = 33
HY_FFN = 64
HY_FAST_DECAY = 0.3
HY_SLOW_DECAY = 1.5
HY_TARGET = 1e-2
ML_WIDTH = 512
ML_HEADS = 4
ML_HEAD_DIM = 128
ML_CONV = 4
ML_CHUNK = 128
RG_WIDTH = 256
RG_HEADS = 4
RG_BLOCK = 64
RG_CONV = 4
RG_C = 8.0
D_FF = 4096
N_MOD = 6
EPS = 1e-6
HY_COLS = 3 * HY_WIDTH
ML_GATE_COLS = 4 * ML_HEADS
ML_COLS = 4 * ML_WIDTH + ML_GATE_COLS
OFF_ML = HY_COLS
OFF_RG = HY_COLS + ML_COLS

LANES = 128
SUBLANES = 8
VMEM_LIMIT = 56 * 1024 * 1024

ZA_COLS = 1792
ZM_COLS = 1664
PAD = SUBLANES
MOD_ROWS = 2 * SUBLANES
ML_HEADS_PER_STEP = 2


def _cparams(sem):
    return pltpu.CompilerParams(dimension_semantics=sem, vmem_limit_bytes=VMEM_LIMIT)


def _const_spec(shape):
    nd = len(shape)
    return pl.BlockSpec(shape, lambda *_: (0,) * nd, pipeline_mode=pl.Buffered(1))


def _layer_spec(l, shape, single=False):
    nd = len(shape)
    mode = dict(pipeline_mode=pl.Buffered(1)) if single else {}
    return pl.BlockSpec((None,) + tuple(shape), lambda *_: (l,) + (0,) * nd, **mode)


def _mod_spec(l, idx, row):
    blk = (None, None, None, 1, D_MODEL)
    if row is None:
        return pl.BlockSpec(blk, lambda i, b: (l, b, idx, 0, 0))
    return pl.BlockSpec(blk, lambda i, b: (l, row, idx, 0, 0))


def _split3(v):
    hi = v.astype(BF16)
    r = v - hi.astype(F32)
    mid = r.astype(BF16)
    lo = (r - mid.astype(F32)).astype(BF16)
    return hi, mid, lo


def _dot(a, b):
    return jnp.dot(a, b, preferred_element_type=F32)


def _dot_f32(a, b):
    ah, am, al = _split3(a)
    bh, bm, bl = _split3(b)
    return (_dot(ah, bh) + (_dot(ah, bm) + _dot(am, bh))
            + (_dot(ah, bl) + _dot(am, bm) + _dot(al, bh)))


def _dot_2x(a, b):
    ah = a.astype(BF16)
    al = (a - ah.astype(F32)).astype(BF16)
    bh = b.astype(BF16)
    bl = (b - bh.astype(F32)).astype(BF16)
    return _dot(ah, bh) + (_dot(ah, bl) + _dot(al, bh))


def _sigmoid(v):
    return 0.5 + 0.5 * jnp.tanh(0.5 * v)


def _silu(v):
    h = 0.5 * v
    return h + h * jnp.tanh(h)


def _modulated_norm(x, g, shift, scale):
    ms = jnp.mean(x * x, axis=-1, keepdims=True)
    return (x * lax.rsqrt(ms + EPS) * g) * (1.0 + scale) + shift


def _dwconv(pad_ref, w_ref, start, nrows, ksize):
    left = (ksize - 1) // 2
    acc = None
    for j in range(ksize):
        lo = PAD + start + j - left
        term = w_ref[j:j + 1, :] * pad_ref[lo:lo + nrows, :]
        acc = term if acc is None else acc + term
    return acc


def _fill_padded(pad_ref, rows, seq):
    width = pad_ref.shape[1]
    pad_ref[0:PAD, :] = jnp.zeros((PAD, width), F32)
    pad_ref[PAD + seq:2 * PAD + seq, :] = jnp.zeros((PAD, width), F32)
    pad_ref[PAD:PAD + seq, :] = rows


def _ada_kernel(c_ref, w_ref, b_ref, o_ref):
    o_ref[0] = _dot_2x(_silu(c_ref[...]), w_ref[0]) + b_ref[0]


def _ada(cc, ada_w, ada_b):
    depth, d, n = ada_w.shape
    tn = 1536
    return pl.pallas_call(
        _ada_kernel,
        grid=(depth, n // tn),
        in_specs=[pl.BlockSpec(cc.shape, lambda l, j: (0, 0)),
                  pl.BlockSpec((1, d, tn), lambda l, j: (l, 0, j)),
                  pl.BlockSpec((1, 1, tn), lambda l, j: (l, 0, j))],
        out_specs=pl.BlockSpec((1, cc.shape[0], tn), lambda l, j: (l, 0, j)),
        out_shape=jax.ShapeDtypeStruct((depth, cc.shape[0], n), F32),
        compiler_params=_cparams(("parallel", "parallel")),
        name="ada",
    )(cc, ada_w, ada_b.reshape(depth, 1, n))


def _inproj_seq_kernel(x_ref, sh_ref, sc_ref, g_ref, wa_ref, wm_ref, za_ref, zm_ref, zr_ref):
    b = pl.program_id(1)
    h = _modulated_norm(x_ref[0], g_ref[...], sh_ref[...], sc_ref[...]).astype(BF16)
    za = _dot(h, wa_ref[...])
    za_ref[0] = za
    zm_ref[0] = _dot(h, wm_ref[...])
    zr_ref[:, pl.ds(b, 1), :] = za[:, HY_COLS:HY_COLS + RG_WIDTH].reshape(za.shape[0], 1, RG_WIDTH)


def _inproj_seq(x, mods, prm, l, row, tm):
    bsz, seq, d = x.shape
    return pl.pallas_call(
        _inproj_seq_kernel,
        grid=(seq // tm, bsz),
        in_specs=[pl.BlockSpec((1, tm, d), lambda i, b: (b, i, 0)),
                  _mod_spec(l, 0, row), _mod_spec(l, 1, row),
                  _layer_spec(l, (1, d)),
                  _layer_spec(l, (d, ZA_COLS), single=True), _layer_spec(l, (d, ZM_COLS), single=True)],
        out_specs=[pl.BlockSpec((1, tm, ZA_COLS), lambda i, b: (b, i, 0)),
                   pl.BlockSpec((1, tm, ZM_COLS), lambda i, b: (b, i, 0)),
                   pl.BlockSpec((tm, bsz, RG_WIDTH), lambda i, b: (i, 0, 0))],
        out_shape=[jax.ShapeDtypeStruct((bsz, seq, ZA_COLS), F32),
                   jax.ShapeDtypeStruct((bsz, seq, ZM_COLS), F32),
                   jax.ShapeDtypeStruct((seq, bsz, RG_WIDTH), F32)],
        compiler_params=_cparams(("parallel", "arbitrary")),
        name="inproj_seq",
    )(x, mods, mods, prm['g1'], prm['w_a'], prm['w_m'])


def _inproj_grid_kernel(x_ref, sh_ref, sc_ref, g_ref, wa_ref, wm_ref, perm_ref, za_ref, zm_ref, zr_ref,
                        *, rows, ncol):
    b = pl.program_id(1)
    d = g_ref.shape[1]
    x = x_ref[0].reshape(rows * ncol, d)
    h = _modulated_norm(x, g_ref[...], sh_ref[...], sc_ref[...]).astype(BF16)
    za = _dot(h, wa_ref[...])
    za_ref[0] = za.reshape(rows, ncol, ZA_COLS)
    zr_ref[:, :, pl.ds(b, 1), :] = za[:, HY_COLS:HY_COLS + RG_WIDTH].reshape(rows, ncol, 1, RG_WIDTH)
    hp = _dot(perm_ref[...], h).astype(BF16)
    zm_ref[0] = _dot(hp, wm_ref[...])


def _inproj_grid(x, mods, prm, l, ncol):
    bsz, seq, d = x.shape
    rows = seq // GRID_W
    tm = rows * ncol
    src = np.arange(tm)
    r, c = src // ncol, src % ncol
    perm = np.zeros((tm, tm), np.float32)
    perm[c * rows + r, src] = 1.0
    za, zm, zr = pl.pallas_call(
        functools.partial(_inproj_grid_kernel, rows=rows, ncol=ncol),
        grid=(GRID_W // ncol, bsz),
        in_specs=[pl.BlockSpec((1, rows, ncol, d), lambda i, b: (b, 0, i, 0)),
                  _mod_spec(l, 0, None), _mod_spec(l, 1, None),
                  _layer_spec(l, (1, d)),
                  _layer_spec(l, (d, ZA_COLS), single=True), _layer_spec(l, (d, ZM_COLS), single=True),
                  _const_spec((tm, tm))],
        out_specs=[pl.BlockSpec((1, rows, ncol, ZA_COLS), lambda i, b: (b, 0, i, 0)),
                   pl.BlockSpec((1, tm, ZM_COLS), lambda i, b: (b, i, 0)),
                   pl.BlockSpec((rows, ncol, bsz, RG_WIDTH), lambda i, b: (0, i, 0, 0))],
        out_shape=[jax.ShapeDtypeStruct((bsz, rows, GRID_W, ZA_COLS), F32),
                   jax.ShapeDtypeStruct((bsz, seq, ZM_COLS), F32),
                   jax.ShapeDtypeStruct((rows, GRID_W, bsz, RG_WIDTH), F32)],
        compiler_params=_cparams(("parallel", "arbitrary")),
        name="inproj_grid",
    )(x.reshape(bsz, rows, GRID_W, d), mods, mods, prm['g1'], prm['w_a'], prm['w_m'],
      jnp.asarray(perm, dtype=BF16))
    return za.reshape(bsz, seq, ZA_COLS), zm, zr.reshape(seq, bsz, RG_WIDTH)


def _dft_tables(seq):
    n = 2 * seq
    k = np.arange(seq, dtype=np.int64)[:, None]
    s = np.arange(seq, dtype=np.int64)[None, :]
    ang = (2.0 * np.pi / n) * ((k * s) % n).astype(np.float64)
    top = np.cos(ang)
    bot = -np.sin(ang)
    bot[0, :] = np.where(np.arange(seq) % 2 == 0, 1.0, -1.0)
    fwd = np.concatenate([top, bot], axis=0)
    scale = np.full((n, 1), 2.0 / n)
    scale[0, 0] = 1.0 / n
    scale[seq, 0] = 1.0 / n
    inv = (fwd * scale).T
    return jnp.asarray(fwd.astype(np.float32)), jnp.asarray(inv.astype(np.float32))


def _hy_feature_tables(seq):
    pos = np.arange(seq, dtype=np.float64)[:, None]
    t = pos / max(seq - 1, 1)
    bands = (HY_POS_DIM - 1) // 2
    fr = np.linspace(1e-4, bands - 1, bands).astype(np.float32).astype(np.float64)[None]
    ang = (2.0 * math.pi / seq) * fr * pos
    feats = np.concatenate([t, np.cos(ang), -np.sin(ang)], axis=-1)
    feats = np.pad(feats, ((0, 0), (0, LANES - HY_POS_DIM)))
    max_decay = math.log(HY_TARGET) / HY_FAST_DECAY
    min_decay = math.log(HY_TARGET) / HY_SLOW_DECAY
    deltas = np.linspace(min_decay, max_decay, HY_WIDTH).astype(np.float32).astype(np.float64)
    decay = np.exp(-t * np.abs(deltas)[None, :])
    return jnp.asarray(feats, dtype=F32), jnp.asarray(decay, dtype=F32)


def _hy_filter_kernel(feats_ref, w1_ref, b1_ref, w2_ref, b2_ref, w3_ref, b3_ref, fq_ref,
                      decay_ref, wf_ref, f_ref, h_scr, *, seq, tr):
    i = pl.program_id(0)
    depth = w1_ref.shape[0]
    fw = 2 * HY_WIDTH

    @pl.when(i == 0)
    def _():
        dec = decay_ref[...]
        for l in range(depth):
            h = jnp.sin(fq_ref[l, 0:1, :] * (_dot_f32(feats_ref[...], w1_ref[l]) + b1_ref[l]))
            h = jnp.sin(fq_ref[l, 1:2, :] * (_dot_f32(h, w2_ref[l]) + b2_ref[l]))
            h = _dot_f32(h, w3_ref[l]) + b3_ref[l]
            hf = h[:, :HY_WIDTH] * dec
            hb = h[:, HY_WIDTH:] * dec
            row = lax.broadcasted_iota(jnp.int32, hb.shape, 0)
            hb = jnp.where(row == 0, 0.0, hb)
            h_scr[:, l * fw:l * fw + HY_WIDTH] = hf.astype(BF16)
            h_scr[:, l * fw + HY_WIDTH:(l + 1) * fw] = hb.astype(BF16)

    g = _dot(wf_ref[...].astype(BF16), h_scr[...])
    row = i * tr + lax.broadcasted_iota(jnp.int32, (tr, HY_WIDTH), 0)
    for l in range(depth):
        gf = g[:, l * fw:l * fw + HY_WIDTH]
        gb = g[:, l * fw + HY_WIDTH:(l + 1) * fw]
        f_ref[l] = jnp.where(row <= seq, gf + gb, gf - gb)


def _hy_filter(seq, prm, fwd):
    feats, decay = _hy_feature_tables(seq)
    depth = prm['hy_w1'].shape[0]
    tr = min(seq, 512)
    full = lambda shp: pl.BlockSpec(shp, lambda i: (0,) * len(shp))
    names = ('hy_w1', 'hy_b1', 'hy_w2', 'hy_b2', 'hy_w3', 'hy_b3', 'hy_freq')
    return pl.pallas_call(
        functools.partial(_hy_filter_kernel, seq=seq, tr=tr),
        grid=(2 * seq // tr,),
        in_specs=[full((seq, LANES))] + [full(prm[k].shape) for k in names]
                 + [full((seq, HY_WIDTH)), pl.BlockSpec((tr, seq), lambda i: (i, 0))],
        out_specs=pl.BlockSpec((depth, tr, HY_WIDTH), lambda i: (0, i, 0)),
        out_shape=jax.ShapeDtypeStruct((depth, 2 * seq, HY_WIDTH), F32),
        scratch_shapes=[pltpu.VMEM((seq, depth * 2 * HY_WIDTH), BF16)],
        compiler_params=_cparams(("arbitrary",)),
        name="hy_filter",
    )(feats, *[prm[k] for k in names], decay, fwd)


def _hy_pre_kernel(z_ref, cw_ref, bias_ref, x0_ref, e_ref, u_ref, pad_ref, *, seq):
    _fill_padded(pad_ref, z_ref[0], seq)
    zc = _dwconv(pad_ref, cw_ref, 0, seq, HY_SHORT)
    x0 = zc[:, :HY_WIDTH]
    u = zc[:, HY_WIDTH:2 * HY_WIDTH] * zc[:, 2 * HY_WIDTH:]
    x0_ref[...] = x0
    e_ref[...] = x0 * (u * bias_ref[...])
    u_ref[...] = u.astype(BF16)


def _hy_fwd_kernel(u_ref, wt_ref, wb_ref, ft_ref, fb_ref, yt_ref, yb_ref, *, nb):
    ft, fb = ft_ref[...], fb_ref[...]
    first = pl.program_id(0) == 0
    row0 = jnp.logical_and(lax.broadcasted_iota(jnp.int32, ft.shape, 0) == 0, first)
    f_im = jnp.where(row0, 0.0, fb)
    f_re2 = jnp.where(row0, fb, ft)
    wt = wt_ref[...].astype(BF16)
    wb = wb_ref[...].astype(BF16)
    for b in range(nb):
        cols = slice(b * HY_WIDTH, (b + 1) * HY_WIDTH)
        ub = u_ref[:, cols]
        pr = _dot(wt, ub)
        pi = _dot(wb, ub)
        yt_ref[:, cols] = (pr * ft - pi * f_im).astype(BF16)
        yb_ref[:, cols] = (pr * f_im + pi * f_re2).astype(BF16)


def _hy_inv_kernel(yt_ref, yb_ref, wt_ref, wb_ref, x0_ref, e_ref, o_ref, *, nb):
    wt = wt_ref[...].astype(BF16)
    wb = wb_ref[...].astype(BF16)
    for b in range(nb):
        cols = slice(b * HY_WIDTH, (b + 1) * HY_WIDTH)
        y = _dot(wt, yt_ref[:, cols]) + _dot(wb, yb_ref[:, cols])
        o_ref[:, cols] = x0_ref[:, cols] * y + e_ref[:, cols]


def _hy_conv(z_a, prm, l, filt, fwd, inv):
    bsz, seq, _ = z_a.shape
    wide = bsz * HY_WIDTH
    bcol = pl.BlockSpec((seq, HY_WIDTH), lambda b: (0, b))
    x0, e, u = pl.pallas_call(
        functools.partial(_hy_pre_kernel, seq=seq),
        grid=(bsz,),
        in_specs=[pl.BlockSpec((1, seq, HY_COLS), lambda b: (b, 0, 0)),
                  _layer_spec(l, (HY_SHORT, HY_COLS)), _layer_spec(l, (1, HY_WIDTH))],
        out_specs=[bcol, bcol, bcol],
        out_shape=[jax.ShapeDtypeStruct((seq, wide), F32), jax.ShapeDtypeStruct((seq, wide), F32),
                   jax.ShapeDtypeStruct((seq, wide), BF16)],
        scratch_shapes=[pltpu.VMEM((seq + 2 * PAD, HY_COLS), F32)],
        compiler_params=_cparams(("parallel",)),
        name="hy_pre",
    )(z_a, prm['hy_conv'], prm['hy_bias'])

    tk = min(seq, 512)
    kt = seq // tk
    yt, yb = pl.pallas_call(
        functools.partial(_hy_fwd_kernel, nb=bsz),
        grid=(kt,),
        in_specs=[_const_spec((seq, wide)),
                  pl.BlockSpec((tk, seq), lambda i: (i, 0)),
                  pl.BlockSpec((tk, seq), lambda i: (kt + i, 0)),
                  pl.BlockSpec((None, tk, HY_WIDTH), lambda i: (l, i, 0)),
                  pl.BlockSpec((None, tk, HY_WIDTH), lambda i: (l, kt + i, 0))],
        out_specs=[pl.BlockSpec((tk, wide), lambda i: (i, 0)), pl.BlockSpec((tk, wide), lambda i: (i, 0))],
        out_shape=[jax.ShapeDtypeStruct((seq, wide), BF16), jax.ShapeDtypeStruct((seq, wide), BF16)],
        compiler_params=_cparams(("parallel",)),
        name="hy_fwd",
    )(u, fwd, fwd, filt, filt)

    tt = min(seq, 256)
    return pl.pallas_call(
        functools.partial(_hy_inv_kernel, nb=bsz),
        grid=(seq // tt,),
        in_specs=[_const_spec((seq, wide)), _const_spec((seq, wide)),
                  pl.BlockSpec((tt, seq), lambda i: (i, 0)),
                  pl.BlockSpec((tt, seq), lambda i: (i, 1)),
                  pl.BlockSpec((tt, wide), lambda i: (i, 0)),
                  pl.BlockSpec((tt, wide), lambda i: (i, 0))],
        out_specs=pl.BlockSpec((tt, wide), lambda i: (i, 0)),
        out_shape=jax.ShapeDtypeStruct((seq, wide), F32),
        compiler_params=_cparams(("parallel",)),
        name="hy_inv",
    )(yt, yb, inv, inv, x0, e)


def _running_max(v, reverse):
    row = lax.broadcasted_iota(jnp.int32, v.shape, 0)
    k = 1
    while k < ML_CHUNK:
        if reverse:
            sh = jnp.where(row < ML_CHUNK - k, pltpu.roll(v, ML_CHUNK - k, 0), -jnp.inf)
        else:
            sh = jnp.where(row >= k, pltpu.roll(v, k, 0), -jnp.inf)
        v = jnp.maximum(v, sh)
        k *= 2
    return v


def _ml_gate_kernel(x_ref, b_ref, low_ref, up_ref, o_ref):
    for d, tri_ref in ((0, low_ref), (1, up_ref)):
        li = x_ref[2 * d] + b_ref[2 * d]
        pre = x_ref[2 * d + 1] + b_ref[2 * d + 1]
        lf = jnp.minimum(pre, 0.0) - jnp.log(1.0 + jnp.exp(-jnp.abs(pre)))
        hi, mid, lo = _split3(lf)
        tri = tri_ref[...]
        cum = _dot(tri, hi) + _dot(tri, mid) + _dot(tri, lo)
        r = li - cum
        o_ref[3 * d] = r
        o_ref[3 * d + 1] = _running_max(r, reverse=(d == 1))
        o_ref[3 * d + 2] = cum
    o_ref[6] = jnp.zeros(o_ref.shape[1:], F32)
    o_ref[7] = jnp.zeros(o_ref.shape[1:], F32)


def _ml_gates(gates, gate_b):
    bsz, seq, _ = gates.shape
    n = seq // ML_CHUNK
    m = bsz * ML_HEADS * n
    x = gates.reshape(bsz, n, ML_CHUNK, 4, ML_HEADS).transpose(3, 2, 0, 4, 1).reshape(4, ML_CHUNK, m)
    bias = jnp.broadcast_to(gate_b.reshape(4, 1, 1, ML_HEADS, 1), (4, 1, bsz, ML_HEADS, n)).reshape(4, 1, m)
    r = np.arange(ML_CHUNK)
    low = jnp.asarray(r[:, None] >= r[None, :], dtype=BF16)
    up = jnp.asarray(r[:, None] <= r[None, :], dtype=BF16)
    full = lambda shp: pl.BlockSpec(shp, lambda: (0,) * len(shp))
    out = pl.pallas_call(
        _ml_gate_kernel,
        in_specs=[full((4, ML_CHUNK, m)), full((4, 1, m)), full((ML_CHUNK, ML_CHUNK)),
                  full((ML_CHUNK, ML_CHUNK))],
        out_specs=full((SUBLANES, ML_CHUNK, m)),
        out_shape=jax.ShapeDtypeStruct((SUBLANES, ML_CHUNK, m), F32),
        compiler_params=pltpu.CompilerParams(vmem_limit_bytes=VMEM_LIMIT),
        name="ml_gates",
    )(x, bias, low, up)
    return out.reshape(SUBLANES, ML_CHUNK, bsz, ML_HEADS, n).transpose(2, 3, 4, 0, 1)


def _ml_column(rows, j):
    if j:
        rows = pltpu.roll(rows, SUBLANES - j, 0)
    return rows.T[:, 0:1]


def _ml_state_step(d, hh, n, kt_scr, va_scr, row_ref, c_ref, m_ref, cseq_scr, mseq_scr):
    sl = pl.ds(pl.multiple_of(n * ML_CHUNK, ML_CHUNK), ML_CHUNK)
    dh = ML_HEAD_DIM
    kt = kt_scr[hh, n]
    va = va_scr[sl, hh * 2 * dh:(hh + 1) * 2 * dh]
    rows = row_ref[0, hh, n]
    r_r = rows[3 * d:3 * d + 1, :]
    last = ML_CHUNK - 1 if d == 0 else 0
    mx_all = rows[3 * d + 1:3 * d + 2, last:last + 1]
    b_all = rows[3 * d + 2:3 * d + 3, last:last + 1]
    m0v = m_ref[0, d, hh]
    m0 = m0v[:, 0:1]
    mseq_scr[d, hh, n] = m0v
    w_r = jnp.exp(r_r - mx_all)
    kw = kt * w_r.astype(BF16)
    m_loc = b_all + mx_all
    m_new = jnp.maximum(b_all + m0, m_loc)
    a = jnp.exp(b_all + m0 - m_new)
    sc = jnp.exp(m_loc - m_new)
    for half in range(2):
        cols = slice(half * dh, (half + 1) * dh)
        c0 = c_ref[0, d, hh, :, cols]
        cseq_scr[d, hh, n, :, cols] = c0.astype(BF16)
        c_ref[0, d, hh, :, cols] = a * c0 + sc * _dot(kw, va[:, cols])
    m_ref[0, d, hh] = jnp.broadcast_to(m_new, (1, dh))


def _ml_output(hh, n, q_scr, kt_scr, va_scr, row_ref, cseq_scr, mseq_scr, h_ref):
    dh = ML_HEAD_DIM
    sl = pl.ds(pl.multiple_of(n * ML_CHUNK, ML_CHUNK), ML_CHUNK)
    q = q_scr[sl, hh * dh:(hh + 1) * dh]
    va = va_scr[sl, hh * 2 * dh:(hh + 1) * 2 * dh]
    qk = _dot(q, kt_scr[hh, n])
    tt = lax.broadcasted_iota(jnp.int32, (ML_CHUNK, ML_CHUNK), 0)
    ss = lax.broadcasted_iota(jnp.int32, (ML_CHUNK, ML_CHUNK), 1)
    rows = row_ref[0, hh, n]
    h = None
    for d in range(2):
        r_r = rows[3 * d:3 * d + 1, :]
        mx_c = _ml_column(rows, 3 * d + 1)
        b_c = _ml_column(rows, 3 * d + 2)
        m0 = mseq_scr[d, hh, n][:, 0:1]
        seen = (ss <= tt) if d == 0 else (ss >= tt)
        mu_c = jnp.maximum(m0, mx_c)
        mu = jnp.broadcast_to(mu_c, (ML_CHUNK, ML_CHUNK))
        floor = jnp.broadcast_to(jnp.exp(-(b_c + mu_c)), (ML_CHUNK, ML_CHUNK))
        e = jnp.where(seen, jnp.exp(r_r - mu), 0.0)
        wi = jnp.exp(m0 - mu).astype(BF16)
        s = (qk * e).astype(BF16)
        res = _dot(jnp.concatenate([s, wi * q], axis=1), jnp.concatenate([va, cseq_scr[d, hh, n]], axis=0))
        hd = res[:, :dh] / jnp.maximum(jnp.abs(res[:, dh:]), floor)
        h = hd if h is None else h + hd
    h_ref[0, sl, hh * dh:(hh + 1) * dh] = h


def _ml_kernel(zq_ref, zk_ref, zv_ref, cwq_ref, cwk_ref, row_ref, c0_ref, m0_ref,
               h_ref, c_ref, m_ref, pad_ref, q_scr, kt_scr, va_scr, cseq_scr, mseq_scr, *, seq):
    nchunk = seq // ML_CHUNK
    dh = ML_HEAD_DIM
    hp = ML_HEADS_PER_STEP
    _fill_padded(pad_ref, zq_ref[0], seq)
    q_scr[...] = _silu(_dwconv(pad_ref, cwq_ref, 0, seq, ML_CONV)).astype(BF16)
    pad_ref[PAD:PAD + seq, :] = zk_ref[0]
    for n in range(nchunk):
        rows = slice(n * ML_CHUNK, (n + 1) * ML_CHUNK)
        kk = _silu(_dwconv(pad_ref, cwk_ref, n * ML_CHUNK, ML_CHUNK, ML_CONV)) * (dh ** -0.5)
        for hh in range(hp):
            kt_scr[hh, n] = kk[:, hh * dh:(hh + 1) * dh].T.astype(BF16)
            va_scr[rows, hh * 2 * dh:hh * 2 * dh + dh] = zv_ref[0, rows, hh * dh:(hh + 1) * dh].astype(BF16)
            va_scr[rows, hh * 2 * dh + dh:(hh + 1) * 2 * dh] = jnp.ones((ML_CHUNK, dh), BF16)
    c_ref[...] = c0_ref[...]
    m_ref[...] = m0_ref[...]

    def state_body(i, carry):
        for hh in range(hp):
            for d, n in ((0, i), (1, nchunk - 1 - i)):
                _ml_state_step(d, hh, n, kt_scr, va_scr, row_ref, c_ref, m_ref, cseq_scr, mseq_scr)
        return carry

    lax.fori_loop(0, nchunk, state_body, 0, unroll=2)

    def out_body(n, carry):
        for hh in range(hp):
            _ml_output(hh, n, q_scr, kt_scr, va_scr, row_ref, cseq_scr, mseq_scr, h_ref)
        return carry

    lax.fori_loop(0, nchunk, out_body, 0, unroll=min(8, nchunk))


def _mlstm(z_ml, prm, l, state):
    bsz, seq, _ = z_ml.shape
    nchunk = seq // ML_CHUNK
    row = _ml_gates(z_ml[..., 3 * ML_WIDTH:3 * ML_WIDTH + ML_GATE_COLS], prm['ml_gate_b'][l])
    dh, nh, hp = ML_HEAD_DIM, ML_HEADS, ML_HEADS_PER_STEP
    ng = nh // hp
    w = hp * dh
    zspec = lambda off: pl.BlockSpec((1, seq, w), lambda b, g: (b, 0, off + g))
    cspec = pl.BlockSpec((1, 2, hp, dh, 2 * dh), lambda b, g: (b, 0, g, 0, 0))
    mspec = pl.BlockSpec((1, 2, hp, 1, dh), lambda b, g: (b, 0, g, 0, 0))
    return pl.pallas_call(
        functools.partial(_ml_kernel, seq=seq),
        grid=(bsz, ng),
        in_specs=[zspec(0), zspec(ng), zspec(2 * ng),
                  pl.BlockSpec((None, ML_CONV, w), lambda b, g: (l, 0, g)),
                  pl.BlockSpec((None, ML_CONV, w), lambda b, g: (l, 0, ng + g)),
                  pl.BlockSpec((1, hp, nchunk, SUBLANES, ML_CHUNK), lambda b, g: (b, g, 0, 0, 0)),
                  cspec, mspec],
        out_specs=[pl.BlockSpec((1, seq, w), lambda b, g: (b, 0, g)), cspec, mspec],
        out_shape=[jax.ShapeDtypeStruct((bsz, seq, ML_WIDTH), F32),
                   jax.ShapeDtypeStruct((bsz, 2, nh, dh, 2 * dh), F32),
                   jax.ShapeDtypeStruct((bsz, 2, nh, 1, dh), F32)],
        scratch_shapes=[pltpu.VMEM((seq + 2 * PAD, w), F32),
                        pltpu.VMEM((seq, w), BF16),
                        pltpu.VMEM((hp, nchunk, dh, ML_CHUNK), BF16),
                        pltpu.VMEM((seq, 2 * w), BF16),
                        pltpu.VMEM((2, hp, nchunk, dh, 2 * dh), BF16),
                        pltpu.VMEM((2, hp, nchunk, 1, dh), F32)],
        compiler_params=_cparams(("parallel", "parallel")),
        name="mlstm",
    )(z_ml, z_ml, z_ml, prm['ml_conv'], prm['ml_conv'], row, *state)


RG_TILE = 256


def _rg_kernel(z_ref, cw_ref, wg_ref, bg_ref, lam_ref, h0_ref, of_ref, ob_ref, st_ref,
               zp_scr, a_scr, b_scr, h_scr, carry_scr, *, seq, tt):
    i = pl.program_id(0)
    nt = seq // tt
    nb = z_ref.shape[1]
    c = RG_WIDTH

    @pl.when(i == 0)
    def _():
        carry_scr[...] = h0_ref[...]

    lam = lam_ref[...]
    sp = jnp.maximum(-lam, 0.0) + jnp.log(1.0 + jnp.exp(-jnp.abs(lam)))
    coef = (-0.5 * RG_C * math.log2(math.e)) * sp
    left = (RG_CONV - 1) // 2
    right = RG_CONV - 1 - left
    for d, tile in ((0, i), (1, nt - 1 - i)):
        t0 = tile * tt
        zp_scr[left:left + tt] = z_ref[pl.ds(t0, tt)]
        for k in range(left):
            src = t0 - left + k
            zp_scr[k:k + 1] = z_ref[pl.ds(jnp.maximum(src, 0), 1)] * (src >= 0).astype(F32)
        for k in range(right):
            src = t0 + tt + k
            zp_scr[left + tt + k:left + tt + k + 1] = (
                z_ref[pl.ds(jnp.minimum(src, seq - 1), 1)] * (src < seq).astype(F32))
        xr = None
        for j in range(RG_CONV):
            term = cw_ref[j:j + 1, :] * zp_scr[j:j + tt]
            xr = term if xr is None else xr + term
        half_pre = (_dot(xr.reshape(tt * nb, c).astype(BF16), wg_ref[:, 2 * d * c:2 * (d + 1) * c])
                    + bg_ref[:, 2 * d * c:2 * (d + 1) * c])
        t_r = jnp.tanh(half_pre[:, :c])
        t_i = jnp.tanh(half_pre[:, c:])
        cd = coef[d:d + 1, :]
        a = jnp.exp2(cd + cd * t_r)
        y = 1.0 - a * a
        root = jnp.where(y > 0.0, y * lax.rsqrt(y), 0.0)
        half_x = 0.5 * xr.reshape(tt * nb, c)
        a_scr[d] = a.reshape(tt, nb, c)
        b_scr[d] = (root * (half_x + half_x * t_i)).reshape(tt, nb, c)

    def body(s, carry):
        hf, hb = carry
        hf = a_scr[0, s] * hf + b_scr[0, s]
        h_scr[0, s] = hf
        sb = tt - 1 - s
        hb = a_scr[1, sb] * hb + b_scr[1, sb]
        h_scr[1, sb] = hb
        return hf, hb

    hf, hb = lax.fori_loop(0, tt, body, (carry_scr[0], carry_scr[1]), unroll=8)
    carry_scr[0] = hf
    carry_scr[1] = hb
    st_ref[0] = hf
    st_ref[1] = hb
    for d, o_ref in ((0, of_ref), (1, ob_ref)):
        o_ref[...] = jnp.swapaxes(h_scr[d].reshape(tt // SUBLANES, SUBLANES, nb, c), 1, 2)


def _rg_gate_weights(rg_gate_w, rg_gate_b):
    depth = rg_gate_w.shape[0]
    eye = jnp.eye(RG_HEADS, dtype=rg_gate_w.dtype)
    full = jnp.einsum('ldghij,hk->ldghikj', 0.5 * rg_gate_w, eye)
    full = full.reshape(depth, 2, 2, RG_WIDTH, RG_WIDTH)
    wg = full.transpose(0, 3, 1, 2, 4).reshape(depth, RG_WIDTH, 4 * RG_WIDTH).astype(BF16)
    return wg, 0.5 * rg_gate_b.reshape(depth, 1, 4 * RG_WIDTH)


def _rglru(zr, prm, l, h0):
    seq, bsz, c = zr.shape
    tt = min(seq, RG_TILE)
    nt = seq // tt
    ng = tt // SUBLANES
    state_spec = pl.BlockSpec((2, bsz, c), lambda i: (0, 0, 0))
    out_sds = jax.ShapeDtypeStruct((seq // SUBLANES, bsz, SUBLANES, c), F32)
    return pl.pallas_call(
        functools.partial(_rg_kernel, seq=seq, tt=tt),
        grid=(nt,),
        in_specs=[_const_spec((seq, bsz, c)),
                  _layer_spec(l, (RG_CONV, c)), _layer_spec(l, (c, 4 * c)), _layer_spec(l, (1, 4 * c)),
                  _layer_spec(l, (2, c)), state_spec],
        out_specs=[pl.BlockSpec((ng, bsz, SUBLANES, c), lambda i: (i, 0, 0, 0)),
                   pl.BlockSpec((ng, bsz, SUBLANES, c), lambda i: (nt - 1 - i, 0, 0, 0)),
                   state_spec],
        out_shape=[out_sds, out_sds, jax.ShapeDtypeStruct((2, bsz, c), F32)],
        scratch_shapes=[pltpu.VMEM((tt + RG_CONV - 1, bsz, c), F32),
                        pltpu.VMEM((2, tt, bsz, c), F32), pltpu.VMEM((2, tt, bsz, c), F32),
                        pltpu.VMEM((2, tt, bsz, c), F32), pltpu.VMEM((2, bsz, c), F32)],
        compiler_params=_cparams(("arbitrary",)),
        name="rglru",
    )(zr, prm['rg_conv'], prm['rg_wg'], prm['rg_bg'], prm['rg_lambda'], h0)


def _group_mean_sq(y, group):
    assert 2 * group == LANES and y.shape[1] % LANES == 0
    sq = y * y
    outs = []
    for t in range(y.shape[1] // LANES):
        blk = sq[:, t * LANES:(t + 1) * LANES]
        low = lax.broadcasted_iota(jnp.int32, blk.shape, 1) < group
        s_low = jnp.sum(jnp.where(low, blk, 0.0), axis=-1, keepdims=True)
        s_high = jnp.sum(jnp.where(low, 0.0, blk), axis=-1, keepdims=True)
        outs.append(jnp.where(low, s_low, s_high) * (1.0 / group))
    return jnp.concatenate(outs, axis=-1)


def _mix_mlp_kernel(x_ref, yhy_ref, hm_ref, o_ref, hrf_ref, hrb_ref, gate_ref, g_ref, wo_ref,
                    gt_ref, sh_ref, sc_ref, gt2_ref, g2_ref, w1_ref, w2_ref, fg_ref, out_ref,
                    *, final, fc, hm_rows):
    g = g_ref[...]
    yhy = yhy_ref[...]
    yhy = yhy * lax.rsqrt(_group_mean_sq(yhy, HY_WIDTH // HY_GROUPS) + EPS) * g[:, :HY_WIDTH]
    parts = [yhy.astype(BF16)]
    if hm_rows:
        hm = jnp.concatenate([hm_ref[0, :, r, :] for r in range(hm_rows)], axis=0)
    else:
        hm = hm_ref[0]
    o_gate = _sigmoid(o_ref[0])
    for h in range(ML_HEADS):
        sl = slice(h * ML_HEAD_DIM, (h + 1) * ML_HEAD_DIM)
        hh = hm[:, sl]
        ms = jnp.mean(hh * hh, axis=-1, keepdims=True)
        yh = hh * lax.rsqrt(ms + EPS) * g[:, HY_WIDTH + h * ML_HEAD_DIM:HY_WIDTH + (h + 1) * ML_HEAD_DIM]
        parts.append((yh * o_gate[:, sl]).astype(BF16))
    gp = gate_ref[0]
    gelu = 0.5 * gp * (1.0 + jnp.tanh(math.sqrt(2.0 / math.pi) * (gp + 0.044715 * (gp * gp * gp))))
    hr = hrf_ref[...] + hrb_ref[...]
    yrg = hr.reshape(hr.shape[0] * SUBLANES, RG_WIDTH) * gelu
    yrg = yrg * lax.rsqrt(_group_mean_sq(yrg, RG_BLOCK) + EPS) * g[:, HY_WIDTH + ML_WIDTH:]
    parts.append(yrg.astype(BF16))
    y = jnp.concatenate(parts, axis=-1)
    x = x_ref[0] + gt_ref[...] * _dot(y, wo_ref[...])

    h = _modulated_norm(x, g2_ref[...], sh_ref[...], sc_ref[...]).astype(BF16)
    acc = None
    for j in range(D_FF // fc):
        a = jnp.maximum(_dot(h, w1_ref[:, j * fc:(j + 1) * fc]), 0.0)
        t = _dot((a * a).astype(BF16), w2_ref[j * fc:(j + 1) * fc, :])
        acc = t if acc is None else acc + t
    out = x + gt2_ref[...] * acc
    if final:
        ms = jnp.mean(out * out, axis=-1, keepdims=True)
        out = out * lax.rsqrt(ms + EPS) * fg_ref[...]
    out_ref[0] = out


def _mix_mlp(x, y_hy, hm, z_a, hr_f, hr_b, mods, prm, l, row, final_g, final, tm, hm_col_major):
    bsz, seq, d = x.shape
    tok = lambda w, j: pl.BlockSpec((1, tm, w), lambda i, b: (b, i, j))
    if hm_col_major:
        hm_rows = tm // GRID_W
        hm = hm.reshape(bsz, GRID_W, seq // GRID_W, ML_WIDTH)
        hm_spec = pl.BlockSpec((1, GRID_W, hm_rows, ML_WIDTH), lambda i, b: (b, 0, i, 0))
    else:
        hm_rows = 0
        hm_spec = tok(ML_WIDTH, 0)
    hr_spec = pl.BlockSpec((tm // SUBLANES, None, SUBLANES, RG_WIDTH), lambda i, b: (i, b, 0, 0))
    return pl.pallas_call(
        functools.partial(_mix_mlp_kernel, final=final, fc=1024, hm_rows=hm_rows),
        grid=(seq // tm, bsz),
        in_specs=[tok(d, 0),
                  pl.BlockSpec((tm, HY_WIDTH), lambda i, b: (i, b)),
                  hm_spec,
                  tok(ML_WIDTH, 2),
                  hr_spec, hr_spec,
                  tok(RG_WIDTH, 6),
                  _layer_spec(l, (1, d)),
                  _layer_spec(l, (d, d), single=True), _mod_spec(l, 2, row),
                  _mod_spec(l, 3, row), _mod_spec(l, 4, row), _mod_spec(l, 5, row),
                  _layer_spec(l, (1, d)),
                  _layer_spec(l, (d, D_FF), single=True), _layer_spec(l, (D_FF, d), single=True),
                  pl.BlockSpec((1, d), lambda i, b: (0, 0))],
        out_specs=tok(d, 0),
        out_shape=jax.ShapeDtypeStruct((bsz, seq, d), F32),
        compiler_params=_cparams(("parallel", "parallel")),
        name="mix_mlp",
    )(x, y_hy, hm, z_a, hr_f, hr_b, z_a, prm['mix_g'], prm['w_o'], mods, mods, mods, mods, prm['g2'],
      prm['w1'], prm['w2'], final_g)


def _split_w_in(w_in):
    wb = w_in.astype(BF16)
    rg = OFF_RG
    o0 = OFF_ML + 3 * ML_WIDTH
    w_a = jnp.concatenate([wb[..., :HY_COLS], wb[..., rg:rg + RG_WIDTH],
                           wb[..., o0:o0 + ML_WIDTH], wb[..., rg + RG_WIDTH:rg + 2 * RG_WIDTH]], axis=-1)
    g0 = OFF_ML + 4 * ML_WIDTH
    pad = jnp.zeros(wb.shape[:-1] + (ZM_COLS - 3 * ML_WIDTH - ML_GATE_COLS,), BF16)
    w_m = jnp.concatenate([wb[..., OFF_ML:o0], wb[..., g0:g0 + ML_GATE_COLS], pad], axis=-1)
    return w_a, w_m


def _prepare(norm1_g, norm2_g, w_in, hy_conv, hy_w1, hy_b1, hy_w2, hy_b2, hy_w3, hy_b3, hy_freq, hy_bias,
             ml_conv, ml_gate_b, rg_conv, rg_gate_w, rg_gate_b, rg_lambda, mix_norm_g, w_out, mlp_w1, mlp_w2):
    w_a, w_m = _split_w_in(w_in)
    rg_wg, rg_bg = _rg_gate_weights(rg_gate_w, rg_gate_b)
    return dict(
        g1=norm1_g[:, None, :], g2=norm2_g[:, None, :], mix_g=mix_norm_g[:, None, :],
        w_a=w_a, w_m=w_m, w_o=w_out.astype(BF16), w1=mlp_w1.astype(BF16), w2=mlp_w2.astype(BF16),
        hy_conv=hy_conv, hy_bias=hy_bias[:, None, :],
        hy_w1=jnp.pad(hy_w1, ((0, 0), (0, LANES - HY_POS_DIM), (0, 0))), hy_b1=hy_b1[:, None, :],
        hy_w2=hy_w2, hy_b2=hy_b2[:, None, :], hy_w3=hy_w3, hy_b3=hy_b3[:, None, :], hy_freq=hy_freq,
        ml_conv=ml_conv, ml_gate_b=ml_gate_b,
        rg_conv=rg_conv, rg_wg=rg_wg, rg_bg=rg_bg, rg_lambda=rg_lambda)


def kernel(x, c, ctx, c_ctx, ada_w, ada_b, norm1_g, norm2_g, w_in, hy_conv, hy_w1, hy_b1, hy_w2, hy_b2, hy_w3, hy_b3, hy_freq, hy_bias, ml_conv, ml_gate_b, rg_conv, rg_gate_w, rg_gate_b, rg_lambda, mix_norm_g, w_out, mlp_w1, mlp_w2, final_g):
    bsz, seq, d = x.shape
    clen = ctx.shape[1]
    prm = _prepare(norm1_g, norm2_g, w_in, hy_conv, hy_w1, hy_b1, hy_w2, hy_b2, hy_w3, hy_b3, hy_freq,
                   hy_bias, ml_conv, ml_gate_b, rg_conv, rg_gate_w, rg_gate_b, rg_lambda, mix_norm_g,
                   w_out, mlp_w1, mlp_w2)

    cc = jnp.concatenate([c, c_ctx[None], jnp.zeros((MOD_ROWS - bsz - 1, d), F32)], axis=0)
    mods = _ada(cc, ada_w, ada_b).reshape(DEPTH, MOD_ROWS, N_MOD, 1, d)
    ctx_row = bsz

    fwd_l, inv_l = _dft_tables(seq)
    fwd_c, inv_c = _dft_tables(clen)
    filt_l = _hy_filter(seq, prm, fwd_l)
    filt_c = _hy_filter(clen, prm, fwd_c)
    dh, nh = ML_HEAD_DIM, ML_HEADS
    ml_zero = (jnp.zeros((bsz, 2, nh, dh, 2 * dh), F32), jnp.zeros((bsz, 2, nh, 1, dh), F32))
    rg_zero = jnp.zeros((2, bsz, RG_WIDTH), F32)
    fg = final_g[None]

    ctx_s = ctx
    for l in range(DEPTH):
        need_ctx = l < DEPTH - 1
        zc_a, zc_m, zc_r = _inproj_seq(ctx_s, mods, prm, l, ctx_row, clen)
        hm_c, ml_c, ml_m = _mlstm(zc_m, prm, l, ml_zero)
        hrf_c, hrb_c, rg_state = _rglru(zc_r, prm, l, rg_zero)

        z_a, z_m, z_r = _inproj_grid(x, mods, prm, l, 16)
        y_hy = _hy_conv(z_a, prm, l, filt_l, fwd_l, inv_l)
        hm, _, _ = _mlstm(z_m, prm, l, (ml_c, ml_m))
        hr_f, hr_b, _ = _rglru(z_r, prm, l, rg_state)
        x = _mix_mlp(x, y_hy, hm, z_a, hr_f, hr_b, mods, prm, l, None, fg, final=(l == DEPTH - 1), tm=512,
                     hm_col_major=True)

        if need_ctx:
            y_hy_c = _hy_conv(zc_a, prm, l, filt_c, fwd_c, inv_c)
            ctx_s = _mix_mlp(ctx_s, y_hy_c, hm_c, zc_a, hrf_c, hrb_c, mods, prm, l, ctx_row, fg, final=False,
                             tm=clen, hm_col_major=False)
    return x
```

```python
import functools
import math

import numpy as np
import jax
import jax.numpy as jnp
from jax import lax
from jax.experimental import pallas as pl
from jax.experimental.pallas import tpu as pltpu

F32 = jnp.float32
BF16 = jnp.bfloat16

D_MODEL = 1024
DEPTH = 2
GRID_W = 64
HY_WIDTH = 256
HY_GROUPS = 4
HY_SHORT = 3
HY_POS_DIM = 33
HY_FFN = 64
HY_FAST_DECAY = 0.3
HY_SLOW_DECAY = 1.5
HY_TARGET = 1e-2
ML_WIDTH = 512
ML_HEADS = 4
ML_HEAD_DIM = 128
ML_CONV = 4
ML_CHUNK = 128
RG_WIDTH = 256
RG_HEADS = 4
RG_BLOCK = 64
RG_CONV = 4
RG_C = 8.0
D_FF = 4096
N_MOD = 6
EPS = 1e-6
HY_COLS = 3 * HY_WIDTH
ML_GATE_COLS = 4 * ML_HEADS
ML_COLS = 4 * ML_WIDTH + ML_GATE_COLS
OFF_ML = HY_COLS
OFF_RG = HY_COLS + ML_COLS

LANES = 128
SUBLANES = 8
VMEM_LIMIT = 56 * 1024 * 1024

ZA_COLS = 1792
ZM_COLS = 1664
ZM_MAIN = 3 * ML_WIDTH
PAD = SUBLANES
MOD_ROWS = 2 * SUBLANES
ML_HEADS_PER_STEP = 2


def _cparams(sem):
    return pltpu.CompilerParams(dimension_semantics=sem, vmem_limit_bytes=VMEM_LIMIT)


def _const_spec(shape):
    nd = len(shape)
    return pl.BlockSpec(shape, lambda *_: (0,) * nd, pipeline_mode=pl.Buffered(1))


def _layer_spec(l, shape, single=False):
    nd = len(shape)
    mode = dict(pipeline_mode=pl.Buffered(1)) if single else {}
    return pl.BlockSpec((None,) + tuple(shape), lambda *_: (l,) + (0,) * nd, **mode)


def _mod_spec(l, idx, row):
    blk = (None, None, None, 1, D_MODEL)
    if row is None:
        return pl.BlockSpec(blk, lambda i, b: (l, b, idx, 0, 0))
    return pl.BlockSpec(blk, lambda i, b: (l, row, idx, 0, 0))


def _split3(v):
    hi = v.astype(BF16)
    r = v - hi.astype(F32)
    mid = r.astype(BF16)
    lo = (r - mid.astype(F32)).astype(BF16)
    return hi, mid, lo


def _dot(a, b):
    return jnp.dot(a, b, preferred_element_type=F32)


def _dot_f32(a, b):
    ah, am, al = _split3(a)
    bh, bm, bl = _split3(b)
    return (_dot(ah, bh) + (_dot(ah, bm) + _dot(am, bh))
            + (_dot(ah, bl) + _dot(am, bm) + _dot(al, bh)))


def _dot_2x(a, b):
    ah = a.astype(BF16)
    al = (a - ah.astype(F32)).astype(BF16)
    bh = b.astype(BF16)
    bl = (b - bh.astype(F32)).astype(BF16)
    return _dot(ah, bh) + (_dot(ah, bl) + _dot(al, bh))


def _sigmoid(v):
    return 0.5 + 0.5 * jnp.tanh(0.5 * v)


def _silu(v):
    h = 0.5 * v
    return h + h * jnp.tanh(h)


def _modulated_norm(x, g, shift, scale):
    ms = jnp.mean(x * x, axis=-1, keepdims=True)
    return (x * lax.rsqrt(ms + EPS) * g) * (1.0 + scale) + shift


def _dwconv(pad_ref, w_ref, start, nrows, ksize):
    left = (ksize - 1) // 2
    acc = None
    for j in range(ksize):
        lo = PAD + start + j - left
        term = w_ref[j:j + 1, :] * pad_ref[lo:lo + nrows, :]
        acc = term if acc is None else acc + term
    return acc


def _fill_padded(pad_ref, rows, seq):
    width = pad_ref.shape[1]
    pad_ref[0:PAD, :] = jnp.zeros((PAD, width), F32)
    pad_ref[PAD + seq:2 * PAD + seq, :] = jnp.zeros((PAD, width), F32)
    pad_ref[PAD:PAD + seq, :] = rows


def _ada_kernel(c_ref, w_ref, b_ref, o_ref):
    o_ref[0] = _dot_2x(_silu(c_ref[...]), w_ref[0]) + b_ref[0]


def _ada(cc, ada_w, ada_b):
    depth, d, n = ada_w.shape
    tn = 1536
    return pl.pallas_call(
        _ada_kernel,
        grid=(depth, n // tn),
        in_specs=[pl.BlockSpec(cc.shape, lambda l, j: (0, 0)),
                  pl.BlockSpec((1, d, tn), lambda l, j: (l, 0, j)),
                  pl.BlockSpec((1, 1, tn), lambda l, j: (l, 0, j))],
        out_specs=pl.BlockSpec((1, cc.shape[0], tn), lambda l, j: (l, 0, j)),
        out_shape=jax.ShapeDtypeStruct((depth, cc.shape[0], n), F32),
        compiler_params=_cparams(("parallel", "parallel")),
        name="ada",
    )(cc, ada_w, ada_b.reshape(depth, 1, n))


def _inproj_seq_kernel(x_ref, sh_ref, sc_ref, g_ref, wa_ref, wm_ref, za_ref, zm_ref, zg_ref, zr_ref):
    b = pl.program_id(1)
    h = _modulated_norm(x_ref[0], g_ref[...], sh_ref[...], sc_ref[...]).astype(BF16)
    za = _dot(h, wa_ref[...])
    za_ref[0] = za
    zm = _dot(h, wm_ref[...])
    zm_ref[0] = zm[:, :ZM_MAIN]
    zg_ref[0] = zm[:, ZM_MAIN:]
    zr_ref[:, pl.ds(b, 1), :] = za[:, HY_COLS:HY_COLS + RG_WIDTH].reshape(za.shape[0], 1, RG_WIDTH)


def _inproj_seq(x, mods, prm, l, row, tm):
    bsz, seq, d = x.shape
    return pl.pallas_call(
        _inproj_seq_kernel,
        grid=(seq // tm, bsz),
        in_specs=[pl.BlockSpec((1, tm, d), lambda i, b: (b, i, 0)),
                  _mod_spec(l, 0, row), _mod_spec(l, 1, row),
                  _layer_spec(l, (1, d)),
                  _layer_spec(l, (d, ZA_COLS), single=True), _layer_spec(l, (d, ZM_COLS), single=True)],
        out_specs=[pl.BlockSpec((1, tm, ZA_COLS), lambda i, b: (b, i, 0)),
                   pl.BlockSpec((1, tm, ZM_MAIN), lambda i, b: (b, i, 0)),
                   pl.BlockSpec((1, tm, ZM_COLS - ZM_MAIN), lambda i, b: (b, i, 0)),
                   pl.BlockSpec((tm, bsz, RG_WIDTH), lambda i, b: (i, 0, 0))],
        out_shape=[jax.ShapeDtypeStruct((bsz, seq, ZA_COLS), F32),
                   jax.ShapeDtypeStruct((bsz, seq, ZM_MAIN), F32),
                   jax.ShapeDtypeStruct((bsz, seq, ZM_COLS - ZM_MAIN), F32),
                   jax.ShapeDtypeStruct((seq, bsz, RG_WIDTH), F32)],
        compiler_params=_cparams(("parallel", "arbitrary")),
        name="inproj_seq",
    )(x, mods, mods, prm['g1'], prm['w_a'], prm['w_m'])


def _inproj_grid_kernel(x_ref, sh_ref, sc_ref, g_ref, wa_ref, wm_ref, perm_ref, za_ref, zm_ref, zg_ref,
                        zr_ref, *, rows, ncol):
    b = pl.program_id(1)
    d = g_ref.shape[1]
    x = x_ref[0].reshape(rows * ncol, d)
    h = _modulated_norm(x, g_ref[...], sh_ref[...], sc_ref[...]).astype(BF16)
    za = _dot(h, wa_ref[...])
    za_ref[0] = za.reshape(rows, ncol, ZA_COLS)
    zr_ref[:, :, pl.ds(b, 1), :] = za[:, HY_COLS:HY_COLS + RG_WIDTH].reshape(rows, ncol, 1, RG_WIDTH)
    hp = _dot(perm_ref[...], h).astype(BF16)
    zm = _dot(hp, wm_ref[...])
    zm_ref[0] = zm[:, :ZM_MAIN]
    zg_ref[0] = zm[:, ZM_MAIN:]


def _inproj_grid(x, mods, prm, l, ncol):
    bsz, seq, d = x.shape
    rows = seq // GRID_W
    tm = rows * ncol
    src = np.arange(tm)
    r, c = src // ncol, src % ncol
    perm = np.zeros((tm, tm), np.float32)
    perm[c * rows + r, src] = 1.0
    za, zm, zg, zr = pl.pallas_call(
        functools.partial(_inproj_grid_kernel, rows=rows, ncol=ncol),
        grid=(GRID_W // ncol, bsz),
        in_specs=[pl.BlockSpec((1, rows, ncol, d), lambda i, b: (b, 0, i, 0)),
                  _mod_spec(l, 0, None), _mod_spec(l, 1, None),
                  _layer_spec(l, (1, d)),
                  _layer_spec(l, (d, ZA_COLS), single=True), _layer_spec(l, (d, ZM_COLS), single=True),
                  _const_spec((tm, tm))],
        out_specs=[pl.BlockSpec((1, rows, ncol, ZA_COLS), lambda i, b: (b, 0, i, 0)),
                   pl.BlockSpec((1, tm, ZM_MAIN), lambda i, b: (b, i, 0)),
                   pl.BlockSpec((1, tm, ZM_COLS - ZM_MAIN), lambda i, b: (b, i, 0)),
                   pl.BlockSpec((rows, ncol, bsz, RG_WIDTH), lambda i, b: (0, i, 0, 0))],
        out_shape=[jax.ShapeDtypeStruct((bsz, rows, GRID_W, ZA_COLS), F32),
                   jax.ShapeDtypeStruct((bsz, seq, ZM_MAIN), F32),
                   jax.ShapeDtypeStruct((bsz, seq, ZM_COLS - ZM_MAIN), F32),
                   jax.ShapeDtypeStruct((rows, GRID_W, bsz, RG_WIDTH), F32)],
        compiler_params=_cparams(("parallel", "arbitrary")),
        name="inproj_grid",
    )(x.reshape(bsz, rows, GRID_W, d), mods, mods, prm['g1'], prm['w_a'], prm['w_m'],
      jnp.asarray(perm, dtype=BF16))
    return za.reshape(bsz, seq, ZA_COLS), zm, zg, zr.reshape(seq, bsz, RG_WIDTH)


def _dft_tables(seq):
    n = 2 * seq
    k = np.arange(seq, dtype=np.int64)[:, None]
    s = np.arange(seq, dtype=np.int64)[None, :]
    ang = (2.0 * np.pi / n) * ((k * s) % n).astype(np.float64)
    top = np.cos(ang)
    bot = -np.sin(ang)
    bot[0, :] = np.where(np.arange(seq) % 2 == 0, 1.0, -1.0)
    fwd = np.concatenate([top, bot], axis=0)
    scale = np.full((n, 1), 2.0 / n)
    scale[0, 0] = 1.0 / n
    scale[seq, 0] = 1.0 / n
    inv = (fwd * scale).T
    return jnp.asarray(fwd.astype(np.float32)), jnp.asarray(inv.astype(np.float32))


def _hy_feature_tables(seq):
    pos = np.arange(seq, dtype=np.float64)[:, None]
    t = pos / max(seq - 1, 1)
    bands = (HY_POS_DIM - 1) // 2
    fr = np.linspace(1e-4, bands - 1, bands).astype(np.float32).astype(np.float64)[None]
    ang = (2.0 * math.pi / seq) * fr * pos
    feats = np.concatenate([t, np.cos(ang), -np.sin(ang)], axis=-1)
    feats = np.pad(feats, ((0, 0), (0, LANES - HY_POS_DIM)))
    max_decay = math.log(HY_TARGET) / HY_FAST_DECAY
    min_decay = math.log(HY_TARGET) / HY_SLOW_DECAY
    deltas = np.linspace(min_decay, max_decay, HY_WIDTH).astype(np.float32).astype(np.float64)
    decay = np.exp(-t * np.abs(deltas)[None, :])
    return jnp.asarray(feats, dtype=F32), jnp.asarray(decay, dtype=F32)


def _hy_filter_kernel(feats_ref, w1_ref, b1_ref, w2_ref, b2_ref, w3_ref, b3_ref, fq_ref,
                      decay_ref, wf_ref, f_ref, h_scr, *, seq, tr):
    i = pl.program_id(0)
    depth = w1_ref.shape[0]
    fw = 2 * HY_WIDTH

    @pl.when(i == 0)
    def _():
        dec = decay_ref[...]
        for l in range(depth):
            h = jnp.sin(fq_ref[l, 0:1, :] * (_dot_f32(feats_ref[...], w1_ref[l]) + b1_ref[l]))
            h = jnp.sin(fq_ref[l, 1:2, :] * (_dot_f32(h, w2_ref[l]) + b2_ref[l]))
            h = _dot_f32(h, w3_ref[l]) + b3_ref[l]
            hf = h[:, :HY_WIDTH] * dec
            hb = h[:, HY_WIDTH:] * dec
            row = lax.broadcasted_iota(jnp.int32, hb.shape, 0)
            hb = jnp.where(row == 0, 0.0, hb)
            h_scr[:, l * fw:l * fw + HY_WIDTH] = hf.astype(BF16)
            h_scr[:, l * fw + HY_WIDTH:(l + 1) * fw] = hb.astype(BF16)

    g = _dot(wf_ref[...].astype(BF16), h_scr[...])
    row = i * tr + lax.broadcasted_iota(jnp.int32, (tr, HY_WIDTH), 0)
    for l in range(depth):
        gf = g[:, l * fw:l * fw + HY_WIDTH]
        gb = g[:, l * fw + HY_WIDTH:(l + 1) * fw]
        f_ref[l] = jnp.where(row <= seq, gf + gb, gf - gb)


def _hy_filter(seq, prm, fwd):
    feats, decay = _hy_feature_tables(seq)
    depth = prm['hy_w1'].shape[0]
    tr = min(seq, 512)
    full = lambda shp: pl.BlockSpec(shp, lambda i: (0,) * len(shp))
    names = ('hy_w1', 'hy_b1', 'hy_w2', 'hy_b2', 'hy_w3', 'hy_b3', 'hy_freq')
    return pl.pallas_call(
        functools.partial(_hy_filter_kernel, seq=seq, tr=tr),
        grid=(2 * seq // tr,),
        in_specs=[full((seq, LANES))] + [full(prm[k].shape) for k in names]
                 + [full((seq, HY_WIDTH)), pl.BlockSpec((tr, seq), lambda i: (i, 0))],
        out_specs=pl.BlockSpec((depth, tr, HY_WIDTH), lambda i: (0, i, 0)),
        out_shape=jax.ShapeDtypeStruct((depth, 2 * seq, HY_WIDTH), F32),
        scratch_shapes=[pltpu.VMEM((seq, depth * 2 * HY_WIDTH), BF16)],
        compiler_params=_cparams(("arbitrary",)),
        name="hy_filter",
    )(feats, *[prm[k] for k in names], decay, fwd)


def _hy_pre_kernel(z_ref, cw_ref, bias_ref, x0_ref, e_ref, u_ref, pad_ref, *, seq):
    _fill_padded(pad_ref, z_ref[0], seq)
    zc = _dwconv(pad_ref, cw_ref, 0, seq, HY_SHORT)
    x0 = zc[:, :HY_WIDTH]
    u = zc[:, HY_WIDTH:2 * HY_WIDTH] * zc[:, 2 * HY_WIDTH:]
    x0_ref[...] = x0
    e_ref[...] = x0 * (u * bias_ref[...])
    u_ref[...] = u.astype(BF16)


def _hy_fwd_kernel(u_ref, wt_ref, wb_ref, ft_ref, fb_ref, yt_ref, yb_ref, *, nb):
    ft, fb = ft_ref[...], fb_ref[...]
    first = pl.program_id(0) == 0
    row0 = jnp.logical_and(lax.broadcasted_iota(jnp.int32, ft.shape, 0) == 0, first)
    f_im = jnp.where(row0, 0.0, fb)
    f_re2 = jnp.where(row0, fb, ft)
    wt = wt_ref[...].astype(BF16)
    wb = wb_ref[...].astype(BF16)
    for b in range(nb):
        cols = slice(b * HY_WIDTH, (b + 1) * HY_WIDTH)
        ub = u_ref[:, cols]
        pr = _dot(wt, ub)
        pi = _dot(wb, ub)
        yt_ref[:, cols] = (pr * ft - pi * f_im).astype(BF16)
        yb_ref[:, cols] = (pr * f_im + pi * f_re2).astype(BF16)


def _hy_inv_kernel(yt_ref, yb_ref, wt_ref, wb_ref, x0_ref, e_ref, o_ref, *, nb):
    wt = wt_ref[...].astype(BF16)
    wb = wb_ref[...].astype(BF16)
    for b in range(nb):
        cols = slice(b * HY_WIDTH, (b + 1) * HY_WIDTH)
        y = _dot(wt, yt_ref[:, cols]) + _dot(wb, yb_ref[:, cols])
        o_ref[:, cols] = x0_ref[:, cols] * y + e_ref[:, cols]


def _hy_conv(z_a, prm, l, filt, fwd, inv):
    bsz, seq, _ = z_a.shape
    wide = bsz * HY_WIDTH
    bcol = pl.BlockSpec((seq, HY_WIDTH), lambda b: (0, b))
    x0, e, u = pl.pallas_call(
        functools.partial(_hy_pre_kernel, seq=seq),
        grid=(bsz,),
        in_specs=[pl.BlockSpec((1, seq, HY_COLS), lambda b: (b, 0, 0)),
                  _layer_spec(l, (HY_SHORT, HY_COLS)), _layer_spec(l, (1, HY_WIDTH))],
        out_specs=[bcol, bcol, bcol],
        out_shape=[jax.ShapeDtypeStruct((seq, wide), F32), jax.ShapeDtypeStruct((seq, wide), F32),
                   jax.ShapeDtypeStruct((seq, wide), BF16)],
        scratch_shapes=[pltpu.VMEM((seq + 2 * PAD, HY_COLS), F32)],
        compiler_params=_cparams(("parallel",)),
        name="hy_pre",
    )(z_a, prm['hy_conv'], prm['hy_bias'])

    tk = min(seq, 512)
    kt = seq // tk
    yt, yb = pl.pallas_call(
        functools.partial(_hy_fwd_kernel, nb=bsz),
        grid=(kt,),
        in_specs=[_const_spec((seq, wide)),
                  pl.BlockSpec((tk, seq), lambda i: (i, 0)),
                  pl.BlockSpec((tk, seq), lambda i: (kt + i, 0)),
                  pl.BlockSpec((None, tk, HY_WIDTH), lambda i: (l, i, 0)),
                  pl.BlockSpec((None, tk, HY_WIDTH), lambda i: (l, kt + i, 0))],
        out_specs=[pl.BlockSpec((tk, wide), lambda i: (i, 0)), pl.BlockSpec((tk, wide), lambda i: (i, 0))],
        out_shape=[jax.ShapeDtypeStruct((seq, wide), BF16), jax.ShapeDtypeStruct((seq, wide), BF16)],
        compiler_params=_cparams(("parallel",)),
        name="hy_fwd",
    )(u, fwd, fwd, filt, filt)

    tt = min(seq, 256)
    return pl.pallas_call(
        functools.partial(_hy_inv_kernel, nb=bsz),
        grid=(seq // tt,),
        in_specs=[_const_spec((seq, wide)), _const_spec((seq, wide)),
                  pl.BlockSpec((tt, seq), lambda i: (i, 0)),
                  pl.BlockSpec((tt, seq), lambda i: (i, 1)),
                  pl.BlockSpec((tt, wide), lambda i: (i, 0)),
                  pl.BlockSpec((tt, wide), lambda i: (i, 0))],
        out_specs=pl.BlockSpec((tt, wide), lambda i: (i, 0)),
        out_shape=jax.ShapeDtypeStruct((seq, wide), F32),
        compiler_params=_cparams(("parallel",)),
        name="hy_inv",
    )(yt, yb, inv, inv, x0, e)


def _running_max(v, reverse):
    row = lax.broadcasted_iota(jnp.int32, v.shape, 0)
    k = 1
    while k < ML_CHUNK:
        if reverse:
            sh = jnp.where(row < ML_CHUNK - k, pltpu.roll(v, ML_CHUNK - k, 0), -jnp.inf)
        else:
            sh = jnp.where(row >= k, pltpu.roll(v, k, 0), -jnp.inf)
        v = jnp.maximum(v, sh)
        k *= 2
    return v


def _ml_gate_kernel(x_ref, b_ref, low_ref, up_ref, o_ref):
    for d, tri_ref in ((0, low_ref), (1, up_ref)):
        li = x_ref[2 * d] + b_ref[2 * d]
        pre = x_ref[2 * d + 1] + b_ref[2 * d + 1]
        lf = jnp.minimum(pre, 0.0) - jnp.log(1.0 + jnp.exp(-jnp.abs(pre)))
        hi, mid, lo = _split3(lf)
        tri = tri_ref[...]
        cum = _dot(tri, hi) + _dot(tri, mid) + _dot(tri, lo)
        r = li - cum
        o_ref[3 * d] = r
        o_ref[3 * d + 1] = _running_max(r, reverse=(d == 1))
        o_ref[3 * d + 2] = cum
    o_ref[6] = jnp.zeros(o_ref.shape[1:], F32)
    o_ref[7] = jnp.zeros(o_ref.shape[1:], F32)


def _ml_gates(gates, gate_b):
    bsz, seq, _ = gates.shape
    n = seq // ML_CHUNK
    m = bsz * ML_HEADS * n
    x = gates.reshape(bsz, n, ML_CHUNK, 4, ML_HEADS).transpose(3, 2, 0, 4, 1).reshape(4, ML_CHUNK, m)
    bias = jnp.broadcast_to(gate_b.reshape(4, 1, 1, ML_HEADS, 1), (4, 1, bsz, ML_HEADS, n)).reshape(4, 1, m)
    r = np.arange(ML_CHUNK)
    low = jnp.asarray(r[:, None] >= r[None, :], dtype=BF16)
    up = jnp.asarray(r[:, None] <= r[None, :], dtype=BF16)
    full = lambda shp: pl.BlockSpec(shp, lambda: (0,) * len(shp))
    out = pl.pallas_call(
        _ml_gate_kernel,
        in_specs=[full((4, ML_CHUNK, m)), full((4, 1, m)), full((ML_CHUNK, ML_CHUNK)),
                  full((ML_CHUNK, ML_CHUNK))],
        out_specs=full((SUBLANES, ML_CHUNK, m)),
        out_shape=jax.ShapeDtypeStruct((SUBLANES, ML_CHUNK, m), F32),
        compiler_params=pltpu.CompilerParams(vmem_limit_bytes=VMEM_LIMIT),
        name="ml_gates",
    )(x, bias, low, up)
    return out.reshape(SUBLANES, ML_CHUNK, bsz, ML_HEADS, n).transpose(2, 3, 4, 0, 1)


def _ml_column(rows, j):
    if j:
        rows = pltpu.roll(rows, SUBLANES - j, 0)
    return rows.T[:, 0:1]


def _ml_state_step(d, hh, n, kt_scr, va_scr, row_ref, c_ref, m_ref, cseq_scr, mseq_scr):
    sl = pl.ds(pl.multiple_of(n * ML_CHUNK, ML_CHUNK), ML_CHUNK)
    dh = ML_HEAD_DIM
    kt = kt_scr[hh, n]
    va = va_scr[sl, hh * 2 * dh:(hh + 1) * 2 * dh]
    rows = row_ref[0, hh, n]
    r_r = rows[3 * d:3 * d + 1, :]
    last = ML_CHUNK - 1 if d == 0 else 0
    mx_all = rows[3 * d + 1:3 * d + 2, last:last + 1]
    b_all = rows[3 * d + 2:3 * d + 3, last:last + 1]
    c0 = c_ref[0, d, hh]
    m0v = m_ref[0, d, hh]
    m0 = m0v[:, 0:1]
    cseq_scr[d, hh, n] = c0.astype(BF16)
    mseq_scr[d, hh, n] = m0v
    w_r = jnp.exp(r_r - mx_all)
    upd = _dot(kt * w_r.astype(BF16), va)
    m_loc = b_all + mx_all
    m_new = jnp.maximum(b_all + m0, m_loc)
    a = jnp.exp(b_all + m0 - m_new)
    sc = jnp.exp(m_loc - m_new)
    c_ref[0, d, hh] = a * c0 + sc * upd
    m_ref[0, d, hh] = jnp.broadcast_to(m_new, (1, dh))


def _ml_output(hh, n, q_scr, kt_scr, va_scr, row_ref, cseq_scr, mseq_scr, h_ref):
    dh = ML_HEAD_DIM
    sl = pl.ds(pl.multiple_of(n * ML_CHUNK, ML_CHUNK), ML_CHUNK)
    q = q_scr[sl, hh * dh:(hh + 1) * dh]
    va = va_scr[sl, hh * 2 * dh:(hh + 1) * 2 * dh]
    qk = _dot(q, kt_scr[hh, n])
    tt = lax.broadcasted_iota(jnp.int32, (ML_CHUNK, ML_CHUNK), 0)
    ss = lax.broadcasted_iota(jnp.int32, (ML_CHUNK, ML_CHUNK), 1)
    rows = row_ref[0, hh, n]
    h = None
    for d in range(2):
        r_r = rows[3 * d:3 * d + 1, :]
        mx_c = _ml_column(rows, 3 * d + 1)
        b_c = _ml_column(rows, 3 * d + 2)
        m0 = mseq_scr[d, hh, n][:, 0:1]
        seen = (ss <= tt) if d == 0 else (ss >= tt)
        mu_c = jnp.maximum(m0, mx_c)
        mu = jnp.broadcast_to(mu_c, (ML_CHUNK, ML_CHUNK))
        floor = jnp.broadcast_to(jnp.exp(-(b_c + mu_c)), (ML_CHUNK, ML_CHUNK))
        e = jnp.where(seen, jnp.exp(r_r - mu), 0.0)
        wi = jnp.exp(m0 - mu).astype(BF16)
        s = (qk * e).astype(BF16)
        res = _dot(jnp.concatenate([s, wi * q], axis=1), jnp.concatenate([va, cseq_scr[d, hh, n]], axis=0))
        hd = res[:, :dh] / jnp.maximum(jnp.abs(res[:, dh:]), floor)
        h = hd if h is None else h + hd
    h_ref[0, sl, hh * dh:(hh + 1) * dh] = h


def _ml_kernel(zq_ref, zk_ref, zv_ref, cwq_ref, cwk_ref, row_ref, c0_ref, m0_ref,
               h_ref, c_ref, m_ref, pad_ref, q_scr, kt_scr, va_scr, cseq_scr, mseq_scr, *, seq):
    nchunk = seq // ML_CHUNK
    dh = ML_HEAD_DIM
    hp = ML_HEADS_PER_STEP
    _fill_padded(pad_ref, zq_ref[0], seq)
    q_scr[...] = _silu(_dwconv(pad_ref, cwq_ref, 0, seq, ML_CONV)).astype(BF16)
    pad_ref[PAD:PAD + seq, :] = zk_ref[0]
    for n in range(nchunk):
        rows = slice(n * ML_CHUNK, (n + 1) * ML_CHUNK)
        kk = _silu(_dwconv(pad_ref, cwk_ref, n * ML_CHUNK, ML_CHUNK, ML_CONV)) * (dh ** -0.5)
        for hh in range(hp):
            kt_scr[hh, n] = kk[:, hh * dh:(hh + 1) * dh].T.astype(BF16)
            va_scr[rows, hh * 2 * dh:hh * 2 * dh + dh] = zv_ref[0, rows, hh * dh:(hh + 1) * dh].astype(BF16)
            va_scr[rows, hh * 2 * dh + dh:(hh + 1) * 2 * dh] = jnp.ones((ML_CHUNK, dh), BF16)
    c_ref[...] = c0_ref[...]
    m_ref[...] = m0_ref[...]

    def state_body(i, carry):
        for hh in range(hp):
            for d, n in ((0, i), (1, nchunk - 1 - i)):
                _ml_state_step(d, hh, n, kt_scr, va_scr, row_ref, c_ref, m_ref, cseq_scr, mseq_scr)
        return carry

    lax.fori_loop(0, nchunk, state_body, 0, unroll=2)

    def out_body(n, carry):
        for hh in range(hp):
            _ml_output(hh, n, q_scr, kt_scr, va_scr, row_ref, cseq_scr, mseq_scr, h_ref)
        return carry

    lax.fori_loop(0, nchunk, out_body, 0, unroll=min(8, nchunk))


def _mlstm(z_ml, z_gate, prm, l, state):
    bsz, seq, _ = z_ml.shape
    nchunk = seq // ML_CHUNK
    row = _ml_gates(z_gate[..., :ML_GATE_COLS], prm['ml_gate_b'][l])
    dh, nh, hp = ML_HEAD_DIM, ML_HEADS, ML_HEADS_PER_STEP
    ng = nh // hp
    w = hp * dh
    zspec = lambda off: pl.BlockSpec((1, seq, w), lambda b, g: (b, 0, off + g))
    cspec = pl.BlockSpec((1, 2, hp, dh, 2 * dh), lambda b, g: (b, 0, g, 0, 0))
    mspec = pl.BlockSpec((1, 2, hp, 1, dh), lambda b, g: (b, 0, g, 0, 0))
    return pl.pallas_call(
        functools.partial(_ml_kernel, seq=seq),
        grid=(bsz, ng),
        in_specs=[zspec(0), zspec(ng), zspec(2 * ng),
                  pl.BlockSpec((None, ML_CONV, w), lambda b, g: (l, 0, g)),
                  pl.BlockSpec((None, ML_CONV, w), lambda b, g: (l, 0, ng + g)),
                  pl.BlockSpec((1, hp, nchunk, SUBLANES, ML_CHUNK), lambda b, g: (b, g, 0, 0, 0)),
                  cspec, mspec],
        out_specs=[pl.BlockSpec((1, seq, w), lambda b, g: (b, 0, g)), cspec, mspec],
        out_shape=[jax.ShapeDtypeStruct((bsz, seq, ML_WIDTH), F32),
                   jax.ShapeDtypeStruct((bsz, 2, nh, dh, 2 * dh), F32),
                   jax.ShapeDtypeStruct((bsz, 2, nh, 1, dh), F32)],
        scratch_shapes=[pltpu.VMEM((seq + 2 * PAD, w), F32),
                        pltpu.VMEM((seq, w), BF16),
                        pltpu.VMEM((hp, nchunk, dh, ML_CHUNK), BF16),
                        pltpu.VMEM((seq, 2 * w), BF16),
                        pltpu.VMEM((2, hp, nchunk, dh, 2 * dh), BF16),
                        pltpu.VMEM((2, hp, nchunk, 1, dh), F32)],
        compiler_params=_cparams(("parallel", "parallel")),
        name="mlstm",
    )(z_ml, z_ml, z_ml, prm['ml_conv'], prm['ml_conv'], row, *state)


RG_TILE = 256


def _rg_kernel(z_ref, cw_ref, wg_ref, bg_ref, lam_ref, h0_ref, of_ref, ob_ref, st_ref,
               zp_scr, a_scr, b_scr, h_scr, carry_scr, *, seq, tt):
    i = pl.program_id(0)
    nt = seq // tt
    nb = z_ref.shape[1]
    c = RG_WIDTH

    @pl.when(i == 0)
    def _():
        carry_scr[...] = h0_ref[...]

    lam = lam_ref[...]
    sp = jnp.maximum(-lam, 0.0) + jnp.log(1.0 + jnp.exp(-jnp.abs(lam)))
    coef = (-0.5 * RG_C * math.log2(math.e)) * sp
    left = (RG_CONV - 1) // 2
    right = RG_CONV - 1 - left
    for d, tile in ((0, i), (1, nt - 1 - i)):
        t0 = tile * tt
        zp_scr[left:left + tt] = z_ref[pl.ds(t0, tt)]
        for k in range(left):
            src = t0 - left + k
            zp_scr[k:k + 1] = z_ref[pl.ds(jnp.maximum(src, 0), 1)] * (src >= 0).astype(F32)
        for k in range(right):
            src = t0 + tt + k
            zp_scr[left + tt + k:left + tt + k + 1] = (
                z_ref[pl.ds(jnp.minimum(src, seq - 1), 1)] * (src < seq).astype(F32))
        xr = None
        for j in range(RG_CONV):
            term = cw_ref[j:j + 1, :] * zp_scr[j:j + tt]
            xr = term if xr is None else xr + term
        half_pre = (_dot(xr.reshape(tt * nb, c).astype(BF16), wg_ref[:, 2 * d * c:2 * (d + 1) * c])
                    + bg_ref[:, 2 * d * c:2 * (d + 1) * c])
        t_r = jnp.tanh(half_pre[:, :c])
        t_i = jnp.tanh(half_pre[:, c:])
        cd = coef[d:d + 1, :]
        a = jnp.exp2(cd + cd * t_r)
        y = 1.0 - a * a
        root = jnp.where(y > 0.0, y * lax.rsqrt(y), 0.0)
        half_x = 0.5 * xr.reshape(tt * nb, c)
        a_scr[d] = a.reshape(tt, nb, c)
        b_scr[d] = (root * (half_x + half_x * t_i)).reshape(tt, nb, c)

    def body(s, carry):
        hf, hb = carry
        hf = a_scr[0, s] * hf + b_scr[0, s]
        h_scr[0, s] = hf
        sb = tt - 1 - s
        hb = a_scr[1, sb] * hb + b_scr[1, sb]
        h_scr[1, sb] = hb
        return hf, hb

    hf, hb = lax.fori_loop(0, tt, body, (carry_scr[0], carry_scr[1]), unroll=8)
    carry_scr[0] = hf
    carry_scr[1] = hb
    st_ref[0] = hf
    st_ref[1] = hb
    for d, o_ref in ((0, of_ref), (1, ob_ref)):
        o_ref[...] = jnp.swapaxes(h_scr[d].reshape(tt // SUBLANES, SUBLANES, nb, c), 1, 2)


def _rg_gate_weights(rg_gate_w, rg_gate_b):
    depth = rg_gate_w.shape[0]
    eye = jnp.eye(RG_HEADS, dtype=rg_gate_w.dtype)
    full = jnp.einsum('ldghij,hk->ldghikj', 0.5 * rg_gate_w, eye)
    full = full.reshape(depth, 2, 2, RG_WIDTH, RG_WIDTH)
    wg = full.transpose(0, 3, 1, 2, 4).reshape(depth, RG_WIDTH, 4 * RG_WIDTH).astype(BF16)
    return wg, 0.5 * rg_gate_b.reshape(depth, 1, 4 * RG_WIDTH)


def _rglru(zr, prm, l, h0):
    seq, bsz, c = zr.shape
    tt = min(seq, RG_TILE)
    nt = seq // tt
    ng = tt // SUBLANES
    state_spec = pl.BlockSpec((2, bsz, c), lambda i: (0, 0, 0))
    out_sds = jax.ShapeDtypeStruct((seq // SUBLANES, bsz, SUBLANES, c), F32)
    return pl.pallas_call(
        functools.partial(_rg_kernel, seq=seq, tt=tt),
        grid=(nt,),
        in_specs=[_const_spec((seq, bsz, c)),
                  _layer_spec(l, (RG_CONV, c)), _layer_spec(l, (c, 4 * c)), _layer_spec(l, (1, 4 * c)),
                  _layer_spec(l, (2, c)), state_spec],
        out_specs=[pl.BlockSpec((ng, bsz, SUBLANES, c), lambda i: (i, 0, 0, 0)),
                   pl.BlockSpec((ng, bsz, SUBLANES, c), lambda i: (nt - 1 - i, 0, 0, 0)),
                   state_spec],
        out_shape=[out_sds, out_sds, jax.ShapeDtypeStruct((2, bsz, c), F32)],
        scratch_shapes=[pltpu.VMEM((tt + RG_CONV - 1, bsz, c), F32),
                        pltpu.VMEM((2, tt, bsz, c), F32), pltpu.VMEM((2, tt, bsz, c), F32),
                        pltpu.VMEM((2, tt, bsz, c), F32), pltpu.VMEM((2, bsz, c), F32)],
        compiler_params=_cparams(("arbitrary",)),
        name="rglru",
    )(zr, prm['rg_conv'], prm['rg_wg'], prm['rg_bg'], prm['rg_lambda'], h0)


def _group_mean_sq(y, group):
    assert 2 * group == LANES and y.shape[1] % LANES == 0
    sq = y * y
    outs = []
    for t in range(y.shape[1] // LANES):
        blk = sq[:, t * LANES:(t + 1) * LANES]
        low = lax.broadcasted_iota(jnp.int32, blk.shape, 1) < group
        s_low = jnp.sum(jnp.where(low, blk, 0.0), axis=-1, keepdims=True)
        s_high = jnp.sum(jnp.where(low, 0.0, blk), axis=-1, keepdims=True)
        outs.append(jnp.where(low, s_low, s_high) * (1.0 / group))
    return jnp.concatenate(outs, axis=-1)


def _mix_mlp_kernel(x_ref, yhy_ref, hm_ref, o_ref, hrf_ref, hrb_ref, gate_ref, g_ref, wo_ref,
                    gt_ref, sh_ref, sc_ref, gt2_ref, g2_ref, w1_ref, w2_ref, fg_ref, out_ref,
                    *, final, fc, hm_rows):
    g = g_ref[...]
    yhy = yhy_ref[...]
    yhy = yhy * lax.rsqrt(_group_mean_sq(yhy, HY_WIDTH // HY_GROUPS) + EPS) * g[:, :HY_WIDTH]
    parts = [yhy.astype(BF16)]
    if hm_rows:
        hm = jnp.concatenate([hm_ref[0, :, r, :] for r in range(hm_rows)], axis=0)
    else:
        hm = hm_ref[0]
    o_gate = _sigmoid(o_ref[0])
    for h in range(ML_HEADS):
        sl = slice(h * ML_HEAD_DIM, (h + 1) * ML_HEAD_DIM)
        hh = hm[:, sl]
        ms = jnp.mean(hh * hh, axis=-1, keepdims=True)
        yh = hh * lax.rsqrt(ms + EPS) * g[:, HY_WIDTH + h * ML_HEAD_DIM:HY_WIDTH + (h + 1) * ML_HEAD_DIM]
        parts.append((yh * o_gate[:, sl]).astype(BF16))
    gp = gate_ref[0]
    gelu = 0.5 * gp * (1.0 + jnp.tanh(math.sqrt(2.0 / math.pi) * (gp + 0.044715 * (gp * gp * gp))))
    hr = hrf_ref[...] + hrb_ref[...]
    yrg = hr.reshape(hr.shape[0] * SUBLANES, RG_WIDTH) * gelu
    yrg = yrg * lax.rsqrt(_group_mean_sq(yrg, RG_BLOCK) + EPS) * g[:, HY_WIDTH + ML_WIDTH:]
    parts.append(yrg.astype(BF16))
    y = jnp.concatenate(parts, axis=-1)
    x = x_ref[0] + gt_ref[...] * _dot(y, wo_ref[...])

    h = _modulated_norm(x, g2_ref[...], sh_ref[...], sc_ref[...]).astype(BF16)
    acc = None
    for j in range(D_FF // fc):
        a = jnp.maximum(_dot(h, w1_ref[:, j * fc:(j + 1) * fc]), 0.0)
        t = _dot((a * a).astype(BF16), w2_ref[j * fc:(j + 1) * fc, :])
        acc = t if acc is None else acc + t
    out = x + gt2_ref[...] * acc
    if final:
        ms = jnp.mean(out * out, axis=-1, keepdims=True)
        out = out * lax.rsqrt(ms + EPS) * fg_ref[...]
    out_ref[0] = out


def _mix_mlp(x, y_hy, hm, z_a, hr_f, hr_b, mods, prm, l, row, final_g, final, tm, hm_col_major):
    bsz, seq, d = x.shape
    tok = lambda w, j: pl.BlockSpec((1, tm, w), lambda i, b: (b, i, j))
    if hm_col_major:
        hm_rows = tm // GRID_W
        hm = hm.reshape(bsz, GRID_W, seq // GRID_W, ML_WIDTH)
        hm_spec = pl.BlockSpec((1, GRID_W, hm_rows, ML_WIDTH), lambda i, b: (b, 0, i, 0))
    else:
        hm_rows = 0
        hm_spec = tok(ML_WIDTH, 0)
    hr_spec = pl.BlockSpec((tm // SUBLANES, None, SUBLANES, RG_WIDTH), lambda i, b: (i, b, 0, 0))
    return pl.pallas_call(
        functools.partial(_mix_mlp_kernel, final=final, fc=1024, hm_rows=hm_rows),
        grid=(seq // tm, bsz),
        in_specs=[tok(d, 0),
                  pl.BlockSpec((tm, HY_WIDTH), lambda i, b: (i, b)),
                  hm_spec,
                  tok(ML_WIDTH, 2),
                  hr_spec, hr_spec,
                  tok(RG_WIDTH, 6),
                  _layer_spec(l, (1, d)),
                  _layer_spec(l, (d, d), single=True), _mod_spec(l, 2, row),
                  _mod_spec(l, 3, row), _mod_spec(l, 4, row), _mod_spec(l, 5, row),
                  _layer_spec(l, (1, d)),
                  _layer_spec(l, (d, D_FF), single=True), _layer_spec(l, (D_FF, d), single=True),
                  pl.BlockSpec((1, d), lambda i, b: (0, 0))],
        out_specs=tok(d, 0),
        out_shape=jax.ShapeDtypeStruct((bsz, seq, d), F32),
        compiler_params=_cparams(("parallel", "parallel")),
        name="mix_mlp",
    )(x, y_hy, hm, z_a, hr_f, hr_b, z_a, prm['mix_g'], prm['w_o'], mods, mods, mods, mods, prm['g2'],
      prm['w1'], prm['w2'], final_g)


def _split_w_in_kernel(w_ref, wa_ref, wm_ref):
    w = w_ref[0]
    rg = OFF_RG
    o0 = OFF_ML + 3 * ML_WIDTH
    g0 = OFF_ML + 4 * ML_WIDTH
    wa_ref[0] = jnp.concatenate([w[:, :HY_COLS], w[:, rg:rg + RG_WIDTH], w[:, o0:o0 + ML_WIDTH],
                                 w[:, rg + RG_WIDTH:rg + 2 * RG_WIDTH]], axis=1).astype(BF16)
    pad = jnp.zeros((w.shape[0], ZM_COLS - ZM_MAIN - ML_GATE_COLS), F32)
    wm_ref[0] = jnp.concatenate([w[:, OFF_ML:o0], w[:, g0:g0 + ML_GATE_COLS], pad], axis=1).astype(BF16)


def _split_w_in(w_in):
    depth, d, p_in = w_in.shape
    tr = 256
    return pl.pallas_call(
        _split_w_in_kernel,
        grid=(depth, d // tr),
        in_specs=[pl.BlockSpec((1, tr, p_in), lambda l, i: (l, i, 0))],
        out_specs=[pl.BlockSpec((1, tr, ZA_COLS), lambda l, i: (l, i, 0)),
                   pl.BlockSpec((1, tr, ZM_COLS), lambda l, i: (l, i, 0))],
        out_shape=[jax.ShapeDtypeStruct((depth, d, ZA_COLS), BF16),
                   jax.ShapeDtypeStruct((depth, d, ZM_COLS), BF16)],
        compiler_params=_cparams(("parallel", "parallel")),
        name="split_w_in",
    )(w_in)


def _prepare(norm1_g, norm2_g, w_in, hy_conv, hy_w1, hy_b1, hy_w2, hy_b2, hy_w3, hy_b3, hy_freq, hy_bias,
             ml_conv, ml_gate_b, rg_conv, rg_gate_w, rg_gate_b, rg_lambda, mix_norm_g, w_out, mlp_w1, mlp_w2):
    w_a, w_m = _split_w_in(w_in)
    rg_wg, rg_bg = _rg_gate_weights(rg_gate_w, rg_gate_b)
    return dict(
        g1=norm1_g[:, None, :], g2=norm2_g[:, None, :], mix_g=mix_norm_g[:, None, :],
        w_a=w_a, w_m=w_m, w_o=w_out.astype(BF16), w1=mlp_w1.astype(BF16), w2=mlp_w2.astype(BF16),
        hy_conv=hy_conv, hy_bias=hy_bias[:, None, :],
        hy_w1=jnp.pad(hy_w1, ((0, 0), (0, LANES - HY_POS_DIM), (0, 0))), hy_b1=hy_b1[:, None, :],
        hy_w2=hy_w2, hy_b2=hy_b2[:, None, :], hy_w3=hy_w3, hy_b3=hy_b3[:, None, :], hy_freq=hy_freq,
        ml_conv=ml_conv, ml_gate_b=ml_gate_b,
        rg_conv=rg_conv, rg_wg=rg_wg, rg_bg=rg_bg, rg_lambda=rg_lambda)


def kernel(x, c, ctx, c_ctx, ada_w, ada_b, norm1_g, norm2_g, w_in, hy_conv, hy_w1, hy_b1, hy_w2, hy_b2, hy_w3, hy_b3, hy_freq, hy_bias, ml_conv, ml_gate_b, rg_conv, rg_gate_w, rg_gate_b, rg_lambda, mix_norm_g, w_out, mlp_w1, mlp_w2, final_g):
    bsz, seq, d = x.shape
    clen = ctx.shape[1]
    prm = _prepare(norm1_g, norm2_g, w_in, hy_conv, hy_w1, hy_b1, hy_w2, hy_b2, hy_w3, hy_b3, hy_freq,
                   hy_bias, ml_conv, ml_gate_b, rg_conv, rg_gate_w, rg_gate_b, rg_lambda, mix_norm_g,
                   w_out, mlp_w1, mlp_w2)

    cc = jnp.concatenate([c, c_ctx[None], jnp.zeros((MOD_ROWS - bsz - 1, d), F32)], axis=0)
    mods = _ada(cc, ada_w, ada_b).reshape(DEPTH, MOD_ROWS, N_MOD, 1, d)
    ctx_row = bsz

    fwd_l, inv_l = _dft_tables(seq)
    fwd_c, inv_c = _dft_tables(clen)
    filt_l = _hy_filter(seq, prm, fwd_l)
    filt_c = _hy_filter(clen, prm, fwd_c)
    dh, nh = ML_HEAD_DIM, ML_HEADS
    ml_zero = (jnp.zeros((bsz, 2, nh, dh, 2 * dh), F32), jnp.zeros((bsz, 2, nh, 1, dh), F32))
    rg_zero = jnp.zeros((2, bsz, RG_WIDTH), F32)
    fg = final_g[None]

    ctx_s = ctx
    for l in range(DEPTH):
        need_ctx = l < DEPTH - 1
        zc_a, zc_m, zc_g, zc_r = _inproj_seq(ctx_s, mods, prm, l, ctx_row, clen)
        hm_c, ml_c, ml_m = _mlstm(zc_m, zc_g, prm, l, ml_zero)
        hrf_c, hrb_c, rg_state = _rglru(zc_r, prm, l, rg_zero)

        z_a, z_m, z_g, z_r = _inproj_grid(x, mods, prm, l, 16)
        y_hy = _hy_conv(z_a, prm, l, filt_l, fwd_l, inv_l)
        hm, _, _ = _mlstm(z_m, z_g, prm, l, (ml_c, ml_m))
        hr_f, hr_b, _ = _rglru(z_r, prm, l, rg_state)
        x = _mix_mlp(x, y_hy, hm, z_a, hr_f, hr_b, mods, prm, l, None, fg, final=(l == DEPTH - 1), tm=512,
                     hm_col_major=True)

        if need_ctx:
            y_hy_c = _hy_conv(zc_a, prm, l, filt_c, fwd_c, inv_c)
            ctx_s = _mix_mlp(ctx_s, y_hy_c, hm_c, zc_a, hrf_c, hrb_c, mods, prm, l, ctx_row, fg, final=False,
                             tm=clen, hm_col_major=False)
    return x
```

```python
import functools
import math

import numpy as np
import jax
import jax.numpy as jnp
from jax import lax
from jax.experimental import pallas as pl
from jax.experimental.pallas import tpu as pltpu

F32 = jnp.float32
BF16 = jnp.bfloat16

D_MODEL = 1024
DEPTH = 2
GRID_W = 64
HY_WIDTH = 256
HY_GROUPS = 4
HY_SHORT = 3
HY_POS_DIM = 33
HY_FFN = 64
HY_FAST_DECAY = 0.3
HY_SLOW_DECAY = 1.5
HY_TARGET = 1e-2
ML_WIDTH = 512
ML_HEADS = 4
ML_HEAD_DIM = 128
ML_CONV = 4
ML_CHUNK = 128
RG_WIDTH = 256
RG_HEADS = 4
RG_BLOCK = 64
RG_CONV = 4
RG_C = 8.0
D_FF = 4096
N_MOD = 6
EPS = 1e-6
HY_COLS = 3 * HY_WIDTH
ML_GATE_COLS = 4 * ML_HEADS
ML_COLS = 4 * ML_WIDTH + ML_GATE_COLS
OFF_ML = HY_COLS
OFF_RG = HY_COLS + ML_COLS

LANES = 128
SUBLANES = 8
VMEM_LIMIT = 56 * 1024 * 1024

ZA_COLS = 1792
ZM_COLS = 1664
ZM_MAIN = 3 * ML_WIDTH
PAD = SUBLANES
MOD_ROWS = 2 * SUBLANES
ML_HEADS_PER_STEP = 2


def _cparams(sem):
    return pltpu.CompilerParams(dimension_semantics=sem, vmem_limit_bytes=VMEM_LIMIT)


def _const_spec(shape):
    nd = len(shape)
    return pl.BlockSpec(shape, lambda *_: (0,) * nd, pipeline_mode=pl.Buffered(1))


def _layer_spec(l, shape, single=False):
    nd = len(shape)
    mode = dict(pipeline_mode=pl.Buffered(1)) if single else {}
    return pl.BlockSpec((None,) + tuple(shape), lambda *_: (l,) + (0,) * nd, **mode)


def _mod_spec(l, idx, row):
    blk = (None, None, None, 1, D_MODEL)
    if row is None:
        return pl.BlockSpec(blk, lambda i, b: (l, b, idx, 0, 0))
    return pl.BlockSpec(blk, lambda i, b: (l, row, idx, 0, 0))


def _split3(v):
    hi = v.astype(BF16)
    r = v - hi.astype(F32)
    mid = r.astype(BF16)
    lo = (r - mid.astype(F32)).astype(BF16)
    return hi, mid, lo


def _dot(a, b):
    return jnp.dot(a, b, preferred_element_type=F32)


def _dot_f32(a, b):
    ah, am, al = _split3(a)
    bh, bm, bl = _split3(b)
    return (_dot(ah, bh) + (_dot(ah, bm) + _dot(am, bh))
            + (_dot(ah, bl) + _dot(am, bm) + _dot(al, bh)))


def _dot_2x(a, b):
    ah = a.astype(BF16)
    al = (a - ah.astype(F32)).astype(BF16)
    bh = b.astype(BF16)
    bl = (b - bh.astype(F32)).astype(BF16)
    return _dot(ah, bh) + (_dot(ah, bl) + _dot(al, bh))


def _sigmoid(v):
    return 0.5 + 0.5 * jnp.tanh(0.5 * v)


def _silu(v):
    h = 0.5 * v
    return h + h * jnp.tanh(h)


def _modulated_norm(x, g, shift, scale):
    ms = jnp.mean(x * x, axis=-1, keepdims=True)
    return (x * lax.rsqrt(ms + EPS) * g) * (1.0 + scale) + shift


def _dwconv(pad_ref, w_ref, start, nrows, ksize):
    left = (ksize - 1) // 2
    acc = None
    for j in range(ksize):
        lo = PAD + start + j - left
        term = w_ref[j:j + 1, :] * pad_ref[lo:lo + nrows, :]
        acc = term if acc is None else acc + term
    return acc


def _fill_padded(pad_ref, rows, seq):
    width = pad_ref.shape[1]
    pad_ref[0:PAD, :] = jnp.zeros((PAD, width), F32)
    pad_ref[PAD + seq:2 * PAD + seq, :] = jnp.zeros((PAD, width), F32)
    pad_ref[PAD:PAD + seq, :] = rows


def _ada_kernel(c_ref, w_ref, b_ref, o_ref):
    o_ref[0] = _dot_2x(_silu(c_ref[...]), w_ref[0]) + b_ref[0]


def _ada(cc, ada_w, ada_b):
    depth, d, n = ada_w.shape
    tn = 1536
    return pl.pallas_call(
        _ada_kernel,
        grid=(depth, n // tn),
        in_specs=[pl.BlockSpec(cc.shape, lambda l, j: (0, 0)),
                  pl.BlockSpec((1, d, tn), lambda l, j: (l, 0, j)),
                  pl.BlockSpec((1, 1, tn), lambda l, j: (l, 0, j))],
        out_specs=pl.BlockSpec((1, cc.shape[0], tn), lambda l, j: (l, 0, j)),
        out_shape=jax.ShapeDtypeStruct((depth, cc.shape[0], n), F32),
        compiler_params=_cparams(("parallel", "parallel")),
        name="ada",
    )(cc, ada_w, ada_b.reshape(depth, 1, n))


def _inproj_seq_kernel(x_ref, sh_ref, sc_ref, g_ref, wa_ref, wm_ref, za_ref, zm_ref, zg_ref, zr_ref):
    b = pl.program_id(1)
    h = _modulated_norm(x_ref[0], g_ref[...], sh_ref[...], sc_ref[...]).astype(BF16)
    za = _dot(h, wa_ref[...])
    za_ref[0] = za
    zm = _dot(h, wm_ref[...])
    zm_ref[0] = zm[:, :ZM_MAIN]
    zg_ref[0] = zm[:, ZM_MAIN:]
    zr_ref[:, pl.ds(b, 1), :] = za[:, HY_COLS:HY_COLS + RG_WIDTH].reshape(za.shape[0], 1, RG_WIDTH)


def _inproj_seq(x, mods, prm, l, row, tm):
    bsz, seq, d = x.shape
    return pl.pallas_call(
        _inproj_seq_kernel,
        grid=(seq // tm, bsz),
        in_specs=[pl.BlockSpec((1, tm, d), lambda i, b: (b, i, 0)),
                  _mod_spec(l, 0, row), _mod_spec(l, 1, row),
                  _layer_spec(l, (1, d)),
                  _layer_spec(l, (d, ZA_COLS), single=True), _layer_spec(l, (d, ZM_COLS), single=True)],
        out_specs=[pl.BlockSpec((1, tm, ZA_COLS), lambda i, b: (b, i, 0)),
                   pl.BlockSpec((1, tm, ZM_MAIN), lambda i, b: (b, i, 0)),
                   pl.BlockSpec((1, tm, ZM_COLS - ZM_MAIN), lambda i, b: (b, i, 0)),
                   pl.BlockSpec((tm, bsz, RG_WIDTH), lambda i, b: (i, 0, 0))],
        out_shape=[jax.ShapeDtypeStruct((bsz, seq, ZA_COLS), F32),
                   jax.ShapeDtypeStruct((bsz, seq, ZM_MAIN), F32),
                   jax.ShapeDtypeStruct((bsz, seq, ZM_COLS - ZM_MAIN), F32),
                   jax.ShapeDtypeStruct((seq, bsz, RG_WIDTH), F32)],
        compiler_params=_cparams(("parallel", "arbitrary")),
        name="inproj_seq",
    )(x, mods, mods, prm['g1'], prm['w_a'], prm['w_m'])


def _inproj_grid_kernel(x_ref, sh_ref, sc_ref, g_ref, wa_ref, wm_ref, perm_ref, za_ref, zm_ref, zg_ref,
                        zr_ref, *, rows, ncol):
    b = pl.program_id(1)
    d = g_ref.shape[1]
    x = x_ref[0].reshape(rows * ncol, d)
    h = _modulated_norm(x, g_ref[...], sh_ref[...], sc_ref[...]).astype(BF16)
    za = _dot(h, wa_ref[...])
    za_ref[0] = za.reshape(rows, ncol, ZA_COLS)
    zr_ref[:, :, pl.ds(b, 1), :] = za[:, HY_COLS:HY_COLS + RG_WIDTH].reshape(rows, ncol, 1, RG_WIDTH)
    hp = _dot(perm_ref[...], h).astype(BF16)
    zm = _dot(hp, wm_ref[...])
    zm_ref[0] = zm[:, :ZM_MAIN]
    zg_ref[0] = zm[:, ZM_MAIN:]


def _inproj_grid(x, mods, prm, l, ncol):
    bsz, seq, d = x.shape
    rows = seq // GRID_W
    tm = rows * ncol
    src = np.arange(tm)
    r, c = src // ncol, src % ncol
    perm = np.zeros((tm, tm), np.float32)
    perm[c * rows + r, src] = 1.0
    za, zm, zg, zr = pl.pallas_call(
        functools.partial(_inproj_grid_kernel, rows=rows, ncol=ncol),
        grid=(GRID_W // ncol, bsz),
        in_specs=[pl.BlockSpec((1, rows, ncol, d), lambda i, b: (b, 0, i, 0)),
                  _mod_spec(l, 0, None), _mod_spec(l, 1, None),
                  _layer_spec(l, (1, d)),
                  _layer_spec(l, (d, ZA_COLS), single=True), _layer_spec(l, (d, ZM_COLS), single=True),
                  _const_spec((tm, tm))],
        out_specs=[pl.BlockSpec((1, rows, ncol, ZA_COLS), lambda i, b: (b, 0, i, 0)),
                   pl.BlockSpec((1, tm, ZM_MAIN), lambda i, b: (b, i, 0)),
                   pl.BlockSpec((1, tm, ZM_COLS - ZM_MAIN), lambda i, b: (b, i, 0)),
                   pl.BlockSpec((rows, ncol, bsz, RG_WIDTH), lambda i, b: (0, i, 0, 0))],
        out_shape=[jax.ShapeDtypeStruct((bsz, rows, GRID_W, ZA_COLS), F32),
                   jax.ShapeDtypeStruct((bsz, seq, ZM_MAIN), F32),
                   jax.ShapeDtypeStruct((bsz, seq, ZM_COLS - ZM_MAIN), F32),
                   jax.ShapeDtypeStruct((rows, GRID_W, bsz, RG_WIDTH), F32)],
        compiler_params=_cparams(("parallel", "arbitrary")),
        name="inproj_grid",
    )(x.reshape(bsz, rows, GRID_W, d), mods, mods, prm['g1'], prm['w_a'], prm['w_m'],
      jnp.asarray(perm, dtype=BF16))
    return za.reshape(bsz, seq, ZA_COLS), zm, zg, zr.reshape(seq, bsz, RG_WIDTH)


def _dft_tables(seq):
    n = 2 * seq
    k = np.arange(seq, dtype=np.int64)[:, None]
    s = np.arange(seq, dtype=np.int64)[None, :]
    ang = (2.0 * np.pi / n) * ((k * s) % n).astype(np.float64)
    top = np.cos(ang)
    bot = -np.sin(ang)
    bot[0, :] = np.where(np.arange(seq) % 2 == 0, 1.0, -1.0)
    fwd = np.concatenate([top, bot], axis=0)
    scale = np.full((n, 1), 2.0 / n)
    scale[0, 0] = 1.0 / n
    scale[seq, 0] = 1.0 / n
    inv = (fwd * scale).T
    return jnp.asarray(fwd.astype(np.float32)), jnp.asarray(inv.astype(np.float32))


def _hy_feature_tables(seq):
    pos = np.arange(seq, dtype=np.float64)[:, None]
    t = pos / max(seq - 1, 1)
    bands = (HY_POS_DIM - 1) // 2
    fr = np.linspace(1e-4, bands - 1, bands).astype(np.float32).astype(np.float64)[None]
    ang = (2.0 * math.pi / seq) * fr * pos
    feats = np.concatenate([t, np.cos(ang), -np.sin(ang)], axis=-1)
    feats = np.pad(feats, ((0, 0), (0, LANES - HY_POS_DIM))).T
    max_decay = math.log(HY_TARGET) / HY_FAST_DECAY
    min_decay = math.log(HY_TARGET) / HY_SLOW_DECAY
    deltas = np.linspace(min_decay, max_decay, HY_WIDTH).astype(np.float32).astype(np.float64)
    decay = np.exp(-t * np.abs(deltas)[None, :])
    return jnp.asarray(feats, dtype=F32), jnp.asarray(decay, dtype=F32)


def _hy_filter_kernel(feats_ref, w1_ref, b1_ref, w2_ref, b2_ref, w3_ref, b3_ref, fq_ref,
                      decay_ref, wf_ref, f_ref, h_scr, *, seq, tr):
    i = pl.program_id(0)
    depth = w1_ref.shape[0]
    fw = 2 * HY_WIDTH

    @pl.when(i == 0)
    def _():
        dec = decay_ref[...]
        for l in range(depth):
            h = jnp.sin(fq_ref[l, 0] * (_dot_f32(w1_ref[l], feats_ref[...]) + b1_ref[l]))
            h = jnp.sin(fq_ref[l, 1] * (_dot_f32(w2_ref[l], h) + b2_ref[l]))
            h = _dot_f32(w3_ref[l], h).T + b3_ref[l]
            hf = h[:, :HY_WIDTH] * dec
            hb = h[:, HY_WIDTH:] * dec
            row = lax.broadcasted_iota(jnp.int32, hb.shape, 0)
            hb = jnp.where(row == 0, 0.0, hb)
            h_scr[:, l * fw:l * fw + HY_WIDTH] = hf.astype(BF16)
            h_scr[:, l * fw + HY_WIDTH:(l + 1) * fw] = hb.astype(BF16)

    g = _dot(wf_ref[...].astype(BF16), h_scr[...])
    row = i * tr + lax.broadcasted_iota(jnp.int32, (tr, HY_WIDTH), 0)
    for l in range(depth):
        gf = g[:, l * fw:l * fw + HY_WIDTH]
        gb = g[:, l * fw + HY_WIDTH:(l + 1) * fw]
        f_ref[l] = jnp.where(row <= seq, gf + gb, gf - gb)


def _hy_filter(seq, prm, fwd):
    feats, decay = _hy_feature_tables(seq)
    depth = prm['hy_w1'].shape[0]
    tr = min(seq, 512)
    full = lambda shp: pl.BlockSpec(shp, lambda i: (0,) * len(shp))
    names = ('hy_w1', 'hy_b1', 'hy_w2', 'hy_b2', 'hy_w3', 'hy_b3', 'hy_freq')
    return pl.pallas_call(
        functools.partial(_hy_filter_kernel, seq=seq, tr=tr),
        grid=(2 * seq // tr,),
        in_specs=[full((LANES, seq))] + [full(prm[k].shape) for k in names]
                 + [full((seq, HY_WIDTH)), pl.BlockSpec((tr, seq), lambda i: (i, 0))],
        out_specs=pl.BlockSpec((depth, tr, HY_WIDTH), lambda i: (0, i, 0)),
        out_shape=jax.ShapeDtypeStruct((depth, 2 * seq, HY_WIDTH), F32),
        scratch_shapes=[pltpu.VMEM((seq, depth * 2 * HY_WIDTH), BF16)],
        compiler_params=_cparams(("arbitrary",)),
        name="hy_filter",
    )(feats, *[prm[k] for k in names], decay, fwd)


def _hy_pre_kernel(z_ref, cw_ref, bias_ref, x0_ref, e_ref, u_ref, pad_ref, *, seq):
    _fill_padded(pad_ref, z_ref[0], seq)
    zc = _dwconv(pad_ref, cw_ref, 0, seq, HY_SHORT)
    x0 = zc[:, :HY_WIDTH]
    u = zc[:, HY_WIDTH:2 * HY_WIDTH] * zc[:, 2 * HY_WIDTH:]
    x0_ref[...] = x0
    e_ref[...] = x0 * (u * bias_ref[...])
    u_ref[...] = u.astype(BF16)


def _hy_fwd_kernel(u_ref, wt_ref, wb_ref, ft_ref, fb_ref, yt_ref, yb_ref, *, nb):
    ft, fb = ft_ref[...], fb_ref[...]
    first = pl.program_id(0) == 0
    row0 = jnp.logical_and(lax.broadcasted_iota(jnp.int32, ft.shape, 0) == 0, first)
    f_im = jnp.where(row0, 0.0, fb)
    f_re2 = jnp.where(row0, fb, ft)
    wt = wt_ref[...].astype(BF16)
    wb = wb_ref[...].astype(BF16)
    for b in range(nb):
        cols = slice(b * HY_WIDTH, (b + 1) * HY_WIDTH)
        ub = u_ref[:, cols]
        pr = _dot(wt, ub)
        pi = _dot(wb, ub)
        yt_ref[:, cols] = (pr * ft - pi * f_im).astype(BF16)
        yb_ref[:, cols] = (pr * f_im + pi * f_re2).astype(BF16)


def _hy_inv_kernel(yt_ref, yb_ref, wt_ref, wb_ref, x0_ref, e_ref, o_ref, *, nb):
    wt = wt_ref[...].astype(BF16)
    wb = wb_ref[...].astype(BF16)
    for b in range(nb):
        cols = slice(b * HY_WIDTH, (b + 1) * HY_WIDTH)
        y = _dot(wt, yt_ref[:, cols]) + _dot(wb, yb_ref[:, cols])
        o_ref[:, cols] = x0_ref[:, cols] * y + e_ref[:, cols]


def _hy_conv(z_a, prm, l, filt, fwd, inv):
    bsz, seq, _ = z_a.shape
    wide = bsz * HY_WIDTH
    bcol = pl.BlockSpec((seq, HY_WIDTH), lambda b: (0, b))
    x0, e, u = pl.pallas_call(
        functools.partial(_hy_pre_kernel, seq=seq),
        grid=(bsz,),
        in_specs=[pl.BlockSpec((1, seq, HY_COLS), lambda b: (b, 0, 0)),
                  _layer_spec(l, (HY_SHORT, HY_COLS)), _layer_spec(l, (1, HY_WIDTH))],
        out_specs=[bcol, bcol, bcol],
        out_shape=[jax.ShapeDtypeStruct((seq, wide), F32), jax.ShapeDtypeStruct((seq, wide), F32),
                   jax.ShapeDtypeStruct((seq, wide), BF16)],
        scratch_shapes=[pltpu.VMEM((seq + 2 * PAD, HY_COLS), F32)],
        compiler_params=_cparams(("parallel",)),
        name="hy_pre",
    )(z_a, prm['hy_conv'], prm['hy_bias'])

    tk = min(seq, 512)
    kt = seq // tk
    yt, yb = pl.pallas_call(
        functools.partial(_hy_fwd_kernel, nb=bsz),
        grid=(kt,),
        in_specs=[_const_spec((seq, wide)),
                  pl.BlockSpec((tk, seq), lambda i: (i, 0)),
                  pl.BlockSpec((tk, seq), lambda i: (kt + i, 0)),
                  pl.BlockSpec((None, tk, HY_WIDTH), lambda i: (l, i, 0)),
                  pl.BlockSpec((None, tk, HY_WIDTH), lambda i: (l, kt + i, 0))],
        out_specs=[pl.BlockSpec((tk, wide), lambda i: (i, 0)), pl.BlockSpec((tk, wide), lambda i: (i, 0))],
        out_shape=[jax.ShapeDtypeStruct((seq, wide), BF16), jax.ShapeDtypeStruct((seq, wide), BF16)],
        compiler_params=_cparams(("parallel",)),
        name="hy_fwd",
    )(u, fwd, fwd, filt, filt)

    tt = min(seq, 256)
    return pl.pallas_call(
        functools.partial(_hy_inv_kernel, nb=bsz),
        grid=(seq // tt,),
        in_specs=[_const_spec((seq, wide)), _const_spec((seq, wide)),
                  pl.BlockSpec((tt, seq), lambda i: (i, 0)),
                  pl.BlockSpec((tt, seq), lambda i: (i, 1)),
                  pl.BlockSpec((tt, wide), lambda i: (i, 0)),
                  pl.BlockSpec((tt, wide), lambda i: (i, 0))],
        out_specs=pl.BlockSpec((tt, wide), lambda i: (i, 0)),
        out_shape=jax.ShapeDtypeStruct((seq, wide), F32),
        compiler_params=_cparams(("parallel",)),
        name="hy_inv",
    )(yt, yb, inv, inv, x0, e)


def _running_max(v, reverse):
    row = lax.broadcasted_iota(jnp.int32, v.shape, 0)
    k = 1
    while k < ML_CHUNK:
        if reverse:
            sh = jnp.where(row < ML_CHUNK - k, pltpu.roll(v, ML_CHUNK - k, 0), -jnp.inf)
        else:
            sh = jnp.where(row >= k, pltpu.roll(v, k, 0), -jnp.inf)
        v = jnp.maximum(v, sh)
        k *= 2
    return v


def _ml_gate_kernel(x_ref, b_ref, low_ref, up_ref, o_ref):
    for d, tri_ref in ((0, low_ref), (1, up_ref)):
        li = x_ref[2 * d] + b_ref[2 * d]
        pre = x_ref[2 * d + 1] + b_ref[2 * d + 1]
        lf = jnp.minimum(pre, 0.0) - jnp.log(1.0 + jnp.exp(-jnp.abs(pre)))
        hi, mid, lo = _split3(lf)
        tri = tri_ref[...]
        cum = _dot(tri, hi) + _dot(tri, mid) + _dot(tri, lo)
        r = li - cum
        o_ref[3 * d] = r
        o_ref[3 * d + 1] = _running_max(r, reverse=(d == 1))
        o_ref[3 * d + 2] = cum
    o_ref[6] = jnp.zeros(o_ref.shape[1:], F32)
    o_ref[7] = jnp.zeros(o_ref.shape[1:], F32)


def _ml_gates(gates, gate_b):
    bsz, seq, _ = gates.shape
    n = seq // ML_CHUNK
    m = bsz * ML_HEADS * n
    x = gates.reshape(bsz, n, ML_CHUNK, 4, ML_HEADS).transpose(3, 2, 0, 4, 1).reshape(4, ML_CHUNK, m)
    bias = jnp.broadcast_to(gate_b.reshape(4, 1, 1, ML_HEADS, 1), (4, 1, bsz, ML_HEADS, n)).reshape(4, 1, m)
    r = np.arange(ML_CHUNK)
    low = jnp.asarray(r[:, None] >= r[None, :], dtype=BF16)
    up = jnp.asarray(r[:, None] <= r[None, :], dtype=BF16)
    full = lambda shp: pl.BlockSpec(shp, lambda: (0,) * len(shp))
    out = pl.pallas_call(
        _ml_gate_kernel,
        in_specs=[full((4, ML_CHUNK, m)), full((4, 1, m)), full((ML_CHUNK, ML_CHUNK)),
                  full((ML_CHUNK, ML_CHUNK))],
        out_specs=full((SUBLANES, ML_CHUNK, m)),
        out_shape=jax.ShapeDtypeStruct((SUBLANES, ML_CHUNK, m), F32),
        compiler_params=pltpu.CompilerParams(vmem_limit_bytes=VMEM_LIMIT),
        name="ml_gates",
    )(x, bias, low, up)
    return out.reshape(SUBLANES, ML_CHUNK, bsz, ML_HEADS, n).transpose(2, 3, 4, 0, 1)


def _ml_column(rows, j):
    if j:
        rows = pltpu.roll(rows, SUBLANES - j, 0)
    return rows.T[:, 0:1]


def _ml_state_step(d, hh, n, kt_scr, va_scr, row_ref, c_ref, m_ref, cseq_scr, mseq_scr):
    sl = pl.ds(pl.multiple_of(n * ML_CHUNK, ML_CHUNK), ML_CHUNK)
    dh = ML_HEAD_DIM
    kt = kt_scr[hh, n]
    va = va_scr[sl, hh * 2 * dh:(hh + 1) * 2 * dh]
    rows = row_ref[0, hh, n]
    r_r = rows[3 * d:3 * d + 1, :]
    last = ML_CHUNK - 1 if d == 0 else 0
    mx_all = rows[3 * d + 1:3 * d + 2, last:last + 1]
    b_all = rows[3 * d + 2:3 * d + 3, last:last + 1]
    c0 = c_ref[0, d, hh]
    m0v = m_ref[0, d, hh]
    m0 = m0v[:, 0:1]
    cseq_scr[d, hh, n] = c0.astype(BF16)
    mseq_scr[d, hh, n] = m0v
    w_r = jnp.exp(r_r - mx_all)
    upd = _dot(kt * w_r.astype(BF16), va)
    m_loc = b_all + mx_all
    m_new = jnp.maximum(b_all + m0, m_loc)
    a = jnp.exp(b_all + m0 - m_new)
    sc = jnp.exp(m_loc - m_new)
    c_ref[0, d, hh] = a * c0 + sc * upd
    m_ref[0, d, hh] = jnp.broadcast_to(m_new, (1, dh))


def _ml_output(hh, n, q_scr, kt_scr, va_scr, row_ref, cseq_scr, mseq_scr, h_ref):
    dh = ML_HEAD_DIM
    sl = pl.ds(pl.multiple_of(n * ML_CHUNK, ML_CHUNK), ML_CHUNK)
    q = q_scr[sl, hh * dh:(hh + 1) * dh]
    va = va_scr[sl, hh * 2 * dh:(hh + 1) * 2 * dh]
    qk = _dot(q, kt_scr[hh, n])
    tt = lax.broadcasted_iota(jnp.int32, (ML_CHUNK, ML_CHUNK), 0)
    ss = lax.broadcasted_iota(jnp.int32, (ML_CHUNK, ML_CHUNK), 1)
    rows = row_ref[0, hh, n]
    h = None
    for d in range(2):
        r_r = rows[3 * d:3 * d + 1, :]
        mx_c = _ml_column(rows, 3 * d + 1)
        b_c = _ml_column(rows, 3 * d + 2)
        m0 = mseq_scr[d, hh, n][:, 0:1]
        seen = (ss <= tt) if d == 0 else (ss >= tt)
        mu_c = jnp.maximum(m0, mx_c)
        mu = jnp.broadcast_to(mu_c, (ML_CHUNK, ML_CHUNK))
        floor = jnp.broadcast_to(jnp.exp(-(b_c + mu_c)), (ML_CHUNK, ML_CHUNK))
        e = jnp.where(seen, jnp.exp(r_r - mu), 0.0)
        wi = jnp.exp(m0 - mu).astype(BF16)
        s = (qk * e).astype(BF16)
        res = _dot(jnp.concatenate([s, wi * q], axis=1), jnp.concatenate([va, cseq_scr[d, hh, n]], axis=0))
        hd = res[:, :dh] / jnp.maximum(jnp.abs(res[:, dh:]), floor)
        h = hd if h is None else h + hd
    h_ref[0, sl, hh * dh:(hh + 1) * dh] = h


def _ml_kernel(zq_ref, zk_ref, zv_ref, cwq_ref, cwk_ref, row_ref, c0_ref, m0_ref,
               h_ref, c_ref, m_ref, pad_ref, q_scr, kt_scr, va_scr, cseq_scr, mseq_scr, *, seq):
    nchunk = seq // ML_CHUNK
    dh = ML_HEAD_DIM
    hp = ML_HEADS_PER_STEP
    _fill_padded(pad_ref, zq_ref[0], seq)
    q_scr[...] = _silu(_dwconv(pad_ref, cwq_ref, 0, seq, ML_CONV)).astype(BF16)
    pad_ref[PAD:PAD + seq, :] = zk_ref[0]
    for n in range(nchunk):
        rows = slice(n * ML_CHUNK, (n + 1) * ML_CHUNK)
        kk = _silu(_dwconv(pad_ref, cwk_ref, n * ML_CHUNK, ML_CHUNK, ML_CONV)) * (dh ** -0.5)
        for hh in range(hp):
            kt_scr[hh, n] = kk[:, hh * dh:(hh + 1) * dh].T.astype(BF16)
            va_scr[rows, hh * 2 * dh:hh * 2 * dh + dh] = zv_ref[0, rows, hh * dh:(hh + 1) * dh].astype(BF16)
            va_scr[rows, hh * 2 * dh + dh:(hh + 1) * 2 * dh] = jnp.ones((ML_CHUNK, dh), BF16)
    c_ref[...] = c0_ref[...]
    m_ref[...] = m0_ref[...]

    def state_body(i, carry):
        for hh in range(hp):
            for d, n in ((0, i), (1, nchunk - 1 - i)):
                _ml_state_step(d, hh, n, kt_scr, va_scr, row_ref, c_ref, m_ref, cseq_scr, mseq_scr)
        return carry

    lax.fori_loop(0, nchunk, state_body, 0, unroll=2)

    def out_body(n, carry):
        for hh in range(hp):
            _ml_output(hh, n, q_scr, kt_scr, va_scr, row_ref, cseq_scr, mseq_scr, h_ref)
        return carry

    lax.fori_loop(0, nchunk, out_body, 0, unroll=min(8, nchunk))


def _mlstm(z_ml, z_gate, prm, l, state):
    bsz, seq, _ = z_ml.shape
    nchunk = seq // ML_CHUNK
    row = _ml_gates(z_gate[..., :ML_GATE_COLS], prm['ml_gate_b'][l])
    dh, nh, hp = ML_HEAD_DIM, ML_HEADS, ML_HEADS_PER_STEP
    ng = nh // hp
    w = hp * dh
    zspec = lambda off: pl.BlockSpec((1, seq, w), lambda b, g: (b, 0, off + g))
    cspec = pl.BlockSpec((1, 2, hp, dh, 2 * dh), lambda b, g: (b, 0, g, 0, 0))
    mspec = pl.BlockSpec((1, 2, hp, 1, dh), lambda b, g: (b, 0, g, 0, 0))
    return pl.pallas_call(
        functools.partial(_ml_kernel, seq=seq),
        grid=(bsz, ng),
        in_specs=[zspec(0), zspec(ng), zspec(2 * ng),
                  pl.BlockSpec((None, ML_CONV, w), lambda b, g: (l, 0, g)),
                  pl.BlockSpec((None, ML_CONV, w), lambda b, g: (l, 0, ng + g)),
                  pl.BlockSpec((1, hp, nchunk, SUBLANES, ML_CHUNK), lambda b, g: (b, g, 0, 0, 0)),
                  cspec, mspec],
        out_specs=[pl.BlockSpec((1, seq, w), lambda b, g: (b, 0, g)), cspec, mspec],
        out_shape=[jax.ShapeDtypeStruct((bsz, seq, ML_WIDTH), F32),
                   jax.ShapeDtypeStruct((bsz, 2, nh, dh, 2 * dh), F32),
                   jax.ShapeDtypeStruct((bsz, 2, nh, 1, dh), F32)],
        scratch_shapes=[pltpu.VMEM((seq + 2 * PAD, w), F32),
                        pltpu.VMEM((seq, w), BF16),
                        pltpu.VMEM((hp, nchunk, dh, ML_CHUNK), BF16),
                        pltpu.VMEM((seq, 2 * w), BF16),
                        pltpu.VMEM((2, hp, nchunk, dh, 2 * dh), BF16),
                        pltpu.VMEM((2, hp, nchunk, 1, dh), F32)],
        compiler_params=_cparams(("parallel", "parallel")),
        name="mlstm",
    )(z_ml, z_ml, z_ml, prm['ml_conv'], prm['ml_conv'], row, *state)


RG_TILE = 256


def _rg_kernel(z_ref, cw_ref, wg_ref, bg_ref, lam_ref, h0_ref, of_ref, ob_ref, st_ref,
               zp_scr, a_scr, b_scr, h_scr, carry_scr, *, seq, tt):
    i = pl.program_id(0)
    nt = seq // tt
    nb = z_ref.shape[1]
    c = RG_WIDTH

    @pl.when(i == 0)
    def _():
        carry_scr[...] = h0_ref[...]

    lam = lam_ref[...]
    sp = jnp.maximum(-lam, 0.0) + jnp.log(1.0 + jnp.exp(-jnp.abs(lam)))
    coef = (-0.5 * RG_C * math.log2(math.e)) * sp
    left = (RG_CONV - 1) // 2
    right = RG_CONV - 1 - left
    for d, tile in ((0, i), (1, nt - 1 - i)):
        t0 = tile * tt
        zp_scr[left:left + tt] = z_ref[pl.ds(t0, tt)]
        for k in range(left):
            src = t0 - left + k
            zp_scr[k:k + 1] = z_ref[pl.ds(jnp.maximum(src, 0), 1)] * (src >= 0).astype(F32)
        for k in range(right):
            src = t0 + tt + k
            zp_scr[left + tt + k:left + tt + k + 1] = (
                z_ref[pl.ds(jnp.minimum(src, seq - 1), 1)] * (src < seq).astype(F32))
        xr = None
        for j in range(RG_CONV):
            term = cw_ref[j:j + 1, :] * zp_scr[j:j + tt]
            xr = term if xr is None else xr + term
        half_pre = (_dot(xr.reshape(tt * nb, c).astype(BF16), wg_ref[:, 2 * d * c:2 * (d + 1) * c])
                    + bg_ref[:, 2 * d * c:2 * (d + 1) * c])
        t_r = jnp.tanh(half_pre[:, :c])
        t_i = jnp.tanh(half_pre[:, c:])
        cd = coef[d:d + 1, :]
        a = jnp.exp2(cd + cd * t_r)
        y = 1.0 - a * a
        root = jnp.where(y > 0.0, y * lax.rsqrt(y), 0.0)
        half_x = 0.5 * xr.reshape(tt * nb, c)
        a_scr[d] = a.reshape(tt, nb, c)
        b_scr[d] = (root * (half_x + half_x * t_i)).reshape(tt, nb, c)

    def body(s, carry):
        hf, hb = carry
        hf = a_scr[0, s] * hf + b_scr[0, s]
        h_scr[0, s] = hf
        sb = tt - 1 - s
        hb = a_scr[1, sb] * hb + b_scr[1, sb]
        h_scr[1, sb] = hb
        return hf, hb

    hf, hb = lax.fori_loop(0, tt, body, (carry_scr[0], carry_scr[1]), unroll=8)
    carry_scr[0] = hf
    carry_scr[1] = hb
    st_ref[0] = hf
    st_ref[1] = hb
    for d, o_ref in ((0, of_ref), (1, ob_ref)):
        o_ref[...] = jnp.swapaxes(h_scr[d].reshape(tt // SUBLANES, SUBLANES, nb, c), 1, 2)


def _rg_gate_weights(rg_gate_w, rg_gate_b):
    depth = rg_gate_w.shape[0]
    eye = jnp.eye(RG_HEADS, dtype=rg_gate_w.dtype)
    full = jnp.einsum('ldghij,hk->ldghikj', 0.5 * rg_gate_w, eye)
    full = full.reshape(depth, 2, 2, RG_WIDTH, RG_WIDTH)
    wg = full.transpose(0, 3, 1, 2, 4).reshape(depth, RG_WIDTH, 4 * RG_WIDTH).astype(BF16)
    return wg, 0.5 * rg_gate_b.reshape(depth, 1, 4 * RG_WIDTH)


def _rglru(zr, prm, l, h0):
    seq, bsz, c = zr.shape
    tt = min(seq, RG_TILE)
    nt = seq // tt
    ng = tt // SUBLANES
    state_spec = pl.BlockSpec((2, bsz, c), lambda i: (0, 0, 0))
    out_sds = jax.ShapeDtypeStruct((seq // SUBLANES, bsz, SUBLANES, c), F32)
    return pl.pallas_call(
        functools.partial(_rg_kernel, seq=seq, tt=tt),
        grid=(nt,),
        in_specs=[_const_spec((seq, bsz, c)),
                  _layer_spec(l, (RG_CONV, c)), _layer_spec(l, (c, 4 * c)), _layer_spec(l, (1, 4 * c)),
                  _layer_spec(l, (2, c)), state_spec],
        out_specs=[pl.BlockSpec((ng, bsz, SUBLANES, c), lambda i: (i, 0, 0, 0)),
                   pl.BlockSpec((ng, bsz, SUBLANES, c), lambda i: (nt - 1 - i, 0, 0, 0)),
                   state_spec],
        out_shape=[out_sds, out_sds, jax.ShapeDtypeStruct((2, bsz, c), F32)],
        scratch_shapes=[pltpu.VMEM((tt + RG_CONV - 1, bsz, c), F32),
                        pltpu.VMEM((2, tt, bsz, c), F32), pltpu.VMEM((2, tt, bsz, c), F32),
                        pltpu.VMEM((2, tt, bsz, c), F32), pltpu.VMEM((2, bsz, c), F32)],
        compiler_params=_cparams(("arbitrary",)),
        name="rglru",
    )(zr, prm['rg_conv'], prm['rg_wg'], prm['rg_bg'], prm['rg_lambda'], h0)


def _group_mean_sq(y, group):
    assert 2 * group == LANES and y.shape[1] % LANES == 0
    sq = y * y
    outs = []
    for t in range(y.shape[1] // LANES):
        blk = sq[:, t * LANES:(t + 1) * LANES]
        low = lax.broadcasted_iota(jnp.int32, blk.shape, 1) < group
        s_low = jnp.sum(jnp.where(low, blk, 0.0), axis=-1, keepdims=True)
        s_high = jnp.sum(jnp.where(low, 0.0, blk), axis=-1, keepdims=True)
        outs.append(jnp.where(low, s_low, s_high) * (1.0 / group))
    return jnp.concatenate(outs, axis=-1)


def _mix_mlp_kernel(x_ref, yhy_ref, hm_ref, o_ref, hrf_ref, hrb_ref, gate_ref, g_ref, wo_ref,
                    gt_ref, sh_ref, sc_ref, gt2_ref, g2_ref, w1_ref, w2_ref, fg_ref, out_ref,
                    *, final, fc, hm_rows):
    g = g_ref[...]
    yhy = yhy_ref[...]
    yhy = yhy * lax.rsqrt(_group_mean_sq(yhy, HY_WIDTH // HY_GROUPS) + EPS) * g[:, :HY_WIDTH]
    parts = [yhy.astype(BF16)]
    if hm_rows:
        hm = jnp.concatenate([hm_ref[0, :, r, :] for r in range(hm_rows)], axis=0)
    else:
        hm = hm_ref[0]
    o_gate = _sigmoid(o_ref[0])
    for h in range(ML_HEADS):
        sl = slice(h * ML_HEAD_DIM, (h + 1) * ML_HEAD_DIM)
        hh = hm[:, sl]
        ms = jnp.mean(hh * hh, axis=-1, keepdims=True)
        yh = hh * lax.rsqrt(ms + EPS) * g[:, HY_WIDTH + h * ML_HEAD_DIM:HY_WIDTH + (h + 1) * ML_HEAD_DIM]
        parts.append((yh * o_gate[:, sl]).astype(BF16))
    gp = gate_ref[0]
    gelu = 0.5 * gp * (1.0 + jnp.tanh(math.sqrt(2.0 / math.pi) * (gp + 0.044715 * (gp * gp * gp))))
    hr = hrf_ref[...] + hrb_ref[...]
    yrg = hr.reshape(hr.shape[0] * SUBLANES, RG_WIDTH) * gelu
    yrg = yrg * lax.rsqrt(_group_mean_sq(yrg, RG_BLOCK) + EPS) * g[:, HY_WIDTH + ML_WIDTH:]
    parts.append(yrg.astype(BF16))
    y = jnp.concatenate(parts, axis=-1)
    x = x_ref[0] + gt_ref[...] * _dot(y, wo_ref[...])

    h = _modulated_norm(x, g2_ref[...], sh_ref[...], sc_ref[...]).astype(BF16)
    acc = None
    for j in range(D_FF // fc):
        a = jnp.maximum(_dot(h, w1_ref[:, j * fc:(j + 1) * fc]), 0.0)
        t = _dot((a * a).astype(BF16), w2_ref[j * fc:(j + 1) * fc, :])
        acc = t if acc is None else acc + t
    out = x + gt2_ref[...] * acc
    if final:
        ms = jnp.mean(out * out, axis=-1, keepdims=True)
        out = out * lax.rsqrt(ms + EPS) * fg_ref[...]
    out_ref[0] = out


def _mix_mlp(x, y_hy, hm, z_a, hr_f, hr_b, mods, prm, l, row, final_g, final, tm, hm_col_major):
    bsz, seq, d = x.shape
    tok = lambda w, j: pl.BlockSpec((1, tm, w), lambda i, b: (b, i, j))
    if hm_col_major:
        hm_rows = tm // GRID_W
        hm = hm.reshape(bsz, GRID_W, seq // GRID_W, ML_WIDTH)
        hm_spec = pl.BlockSpec((1, GRID_W, hm_rows, ML_WIDTH), lambda i, b: (b, 0, i, 0))
    else:
        hm_rows = 0
        hm_spec = tok(ML_WIDTH, 0)
    hr_spec = pl.BlockSpec((tm // SUBLANES, None, SUBLANES, RG_WIDTH), lambda i, b: (i, b, 0, 0))
    return pl.pallas_call(
        functools.partial(_mix_mlp_kernel, final=final, fc=1024, hm_rows=hm_rows),
        grid=(seq // tm, bsz),
        in_specs=[tok(d, 0),
                  pl.BlockSpec((tm, HY_WIDTH), lambda i, b: (i, b)),
                  hm_spec,
                  tok(ML_WIDTH, 2),
                  hr_spec, hr_spec,
                  tok(RG_WIDTH, 6),
                  _layer_spec(l, (1, d)),
                  _layer_spec(l, (d, d), single=True), _mod_spec(l, 2, row),
                  _mod_spec(l, 3, row), _mod_spec(l, 4, row), _mod_spec(l, 5, row),
                  _layer_spec(l, (1, d)),
                  _layer_spec(l, (d, D_FF), single=True), _layer_spec(l, (D_FF, d), single=True),
                  pl.BlockSpec((1, d), lambda i, b: (0, 0))],
        out_specs=tok(d, 0),
        out_shape=jax.ShapeDtypeStruct((bsz, seq, d), F32),
        compiler_params=_cparams(("parallel", "parallel")),
        name="mix_mlp",
    )(x, y_hy, hm, z_a, hr_f, hr_b, z_a, prm['mix_g'], prm['w_o'], mods, mods, mods, mods, prm['g2'],
      prm['w1'], prm['w2'], final_g)


def _split_w_in_kernel(w_ref, wa_ref, wm_ref):
    w = w_ref[0]
    rg = OFF_RG
    o0 = OFF_ML + 3 * ML_WIDTH
    g0 = OFF_ML + 4 * ML_WIDTH
    wa_ref[0] = jnp.concatenate([w[:, :HY_COLS], w[:, rg:rg + RG_WIDTH], w[:, o0:o0 + ML_WIDTH],
                                 w[:, rg + RG_WIDTH:rg + 2 * RG_WIDTH]], axis=1).astype(BF16)
    pad = jnp.zeros((w.shape[0], ZM_COLS - ZM_MAIN - ML_GATE_COLS), F32)
    wm_ref[0] = jnp.concatenate([w[:, OFF_ML:o0], w[:, g0:g0 + ML_GATE_COLS], pad], axis=1).astype(BF16)


def _split_w_in(w_in):
    depth, d, p_in = w_in.shape
    tr = 256
    return pl.pallas_call(
        _split_w_in_kernel,
        grid=(depth, d // tr),
        in_specs=[pl.BlockSpec((1, tr, p_in), lambda l, i: (l, i, 0))],
        out_specs=[pl.BlockSpec((1, tr, ZA_COLS), lambda l, i: (l, i, 0)),
                   pl.BlockSpec((1, tr, ZM_COLS), lambda l, i: (l, i, 0))],
        out_shape=[jax.ShapeDtypeStruct((depth, d, ZA_COLS), BF16),
                   jax.ShapeDtypeStruct((depth, d, ZM_COLS), BF16)],
        compiler_params=_cparams(("parallel", "parallel")),
        name="split_w_in",
    )(w_in)


def _prepare(norm1_g, norm2_g, w_in, hy_conv, hy_w1, hy_b1, hy_w2, hy_b2, hy_w3, hy_b3, hy_freq, hy_bias,
             ml_conv, ml_gate_b, rg_conv, rg_gate_w, rg_gate_b, rg_lambda, mix_norm_g, w_out, mlp_w1, mlp_w2):
    w_a, w_m = _split_w_in(w_in)
    rg_wg, rg_bg = _rg_gate_weights(rg_gate_w, rg_gate_b)
    return dict(
        g1=norm1_g[:, None, :], g2=norm2_g[:, None, :], mix_g=mix_norm_g[:, None, :],
        w_a=w_a, w_m=w_m, w_o=w_out.astype(BF16), w1=mlp_w1.astype(BF16), w2=mlp_w2.astype(BF16),
        hy_conv=hy_conv, hy_bias=hy_bias[:, None, :],
        hy_w1=jnp.swapaxes(jnp.pad(hy_w1, ((0, 0), (0, LANES - HY_POS_DIM), (0, 0))), 1, 2),
        hy_b1=hy_b1[:, :, None], hy_w2=jnp.swapaxes(hy_w2, 1, 2), hy_b2=hy_b2[:, :, None],
        hy_w3=jnp.swapaxes(hy_w3, 1, 2), hy_b3=hy_b3[:, None, :], hy_freq=hy_freq[:, :, :, None],
        ml_conv=ml_conv, ml_gate_b=ml_gate_b,
        rg_conv=rg_conv, rg_wg=rg_wg, rg_bg=rg_bg, rg_lambda=rg_lambda)


def kernel(x, c, ctx, c_ctx, ada_w, ada_b, norm1_g, norm2_g, w_in, hy_conv, hy_w1, hy_b1, hy_w2, hy_b2, hy_w3, hy_b3, hy_freq, hy_bias, ml_conv, ml_gate_b, rg_conv, rg_gate_w, rg_gate_b, rg_lambda, mix_norm_g, w_out, mlp_w1, mlp_w2, final_g):
    bsz, seq, d = x.shape
    clen = ctx.shape[1]
    prm = _prepare(norm1_g, norm2_g, w_in, hy_conv, hy_w1, hy_b1, hy_w2, hy_b2, hy_w3, hy_b3, hy_freq,
                   hy_bias, ml_conv, ml_gate_b, rg_conv, rg_gate_w, rg_gate_b, rg_lambda, mix_norm_g,
                   w_out, mlp_w1, mlp_w2)

    cc = jnp.concatenate([c, c_ctx[None], jnp.zeros((MOD_ROWS - bsz - 1, d), F32)], axis=0)
    mods = _ada(cc, ada_w, ada_b).reshape(DEPTH, MOD_ROWS, N_MOD, 1, d)
    ctx_row = bsz

    fwd_l, inv_l = _dft_tables(seq)
    fwd_c, inv_c = _dft_tables(clen)
    filt_l = _hy_filter(seq, prm, fwd_l)
    filt_c = _hy_filter(clen, prm, fwd_c)
    dh, nh = ML_HEAD_DIM, ML_HEADS
    ml_zero = (jnp.zeros((bsz, 2, nh, dh, 2 * dh), F32), jnp.zeros((bsz, 2, nh, 1, dh), F32))
    rg_zero = jnp.zeros((2, bsz, RG_WIDTH), F32)
    fg = final_g[None]

    ctx_s = ctx
    for l in range(DEPTH):
        need_ctx = l < DEPTH - 1
        zc_a, zc_m, zc_g, zc_r = _inproj_seq(ctx_s, mods, prm, l, ctx_row, clen)
        hm_c, ml_c, ml_m = _mlstm(zc_m, zc_g, prm, l, ml_zero)
        hrf_c, hrb_c, rg_state = _rglru(zc_r, prm, l, rg_zero)

        z_a, z_m, z_g, z_r = _inproj_grid(x, mods, prm, l, 16)
        y_hy = _hy_conv(z_a, prm, l, filt_l, fwd_l, inv_l)
        hm, _, _ = _mlstm(z_m, z_g, prm, l, (ml_c, ml_m))
        hr_f, hr_b, _ = _rglru(z_r, prm, l, rg_state)
        x = _mix_mlp(x, y_hy, hm, z_a, hr_f, hr_b, mods, prm, l, None, fg, final=(l == DEPTH - 1), tm=512,
                     hm_col_major=True)

        if need_ctx:
            y_hy_c = _hy_conv(zc_a, prm, l, filt_c, fwd_c, inv_c)
            ctx_s = _mix_mlp(ctx_s, y_hy_c, hm_c, zc_a, hrf_c, hrb_c, mods, prm, l, ctx_row, fg, final=False,
                             tm=clen, hm_col_major=False)
    return x
```

```python
import functools
import math

import numpy as np
import jax
import jax.numpy as jnp
from jax import lax
from jax.experimental import pallas as pl
from jax.experimental.pallas import tpu as pltpu

F32 = jnp.float32
BF16 = jnp.bfloat16

D_MODEL = 1024
DEPTH = 2
GRID_W = 64
HY_WIDTH = 256
HY_GROUPS = 4
HY_SHORT = 3
HY_POS_DIM = 33
HY_FFN = 64
HY_FAST_DECAY = 0.3
HY_SLOW_DECAY = 1.5
HY_TARGET = 1e-2
ML_WIDTH = 512
ML_HEADS = 4
ML_HEAD_DIM = 128
ML_CONV = 4
ML_CHUNK = 128
RG_WIDTH = 256
RG_HEADS = 4
RG_BLOCK = 64
RG_CONV = 4
RG_C = 8.0
D_FF = 4096
N_MOD = 6
EPS = 1e-6
HY_COLS = 3 * HY_WIDTH
ML_GATE_COLS = 4 * ML_HEADS
ML_COLS = 4 * ML_WIDTH + ML_GATE_COLS
OFF_ML = HY_COLS
OFF_RG = HY_COLS + ML_COLS

LANES = 128
SUBLANES = 8
VMEM_LIMIT = 56 * 1024 * 1024

ZA_COLS = 1792
ZM_COLS = 1664
ZM_MAIN = 3 * ML_WIDTH
PAD = SUBLANES
MOD_ROWS = 2 * SUBLANES
ML_HEADS_PER_STEP = 4


def _cparams(sem):
    return pltpu.CompilerParams(dimension_semantics=sem, vmem_limit_bytes=VMEM_LIMIT)


def _const_spec(shape):
    nd = len(shape)
    return pl.BlockSpec(shape, lambda *_: (0,) * nd, pipeline_mode=pl.Buffered(1))


def _layer_spec(l, shape, single=False):
    nd = len(shape)
    mode = dict(pipeline_mode=pl.Buffered(1)) if single else {}
    return pl.BlockSpec((None,) + tuple(shape), lambda *_: (l,) + (0,) * nd, **mode)


def _mod_spec(l, idx, row):
    blk = (None, None, None, 1, D_MODEL)
    if row is None:
        return pl.BlockSpec(blk, lambda i, b: (l, b, idx, 0, 0))
    return pl.BlockSpec(blk, lambda i, b: (l, row, idx, 0, 0))


def _split3(v):
    hi = v.astype(BF16)
    r = v - hi.astype(F32)
    mid = r.astype(BF16)
    lo = (r - mid.astype(F32)).astype(BF16)
    return hi, mid, lo


def _dot(a, b):
    return jnp.dot(a, b, preferred_element_type=F32)


def _dot_f32(a, b):
    ah, am, al = _split3(a)
    bh, bm, bl = _split3(b)
    return (_dot(ah, bh) + (_dot(ah, bm) + _dot(am, bh))
            + (_dot(ah, bl) + _dot(am, bm) + _dot(al, bh)))


def _dot_2x(a, b):
    ah = a.astype(BF16)
    al = (a - ah.astype(F32)).astype(BF16)
    bh = b.astype(BF16)
    bl = (b - bh.astype(F32)).astype(BF16)
    return _dot(ah, bh) + (_dot(ah, bl) + _dot(al, bh))


def _sigmoid(v):
    return 0.5 + 0.5 * jnp.tanh(0.5 * v)


def _silu(v):
    h = 0.5 * v
    return h + h * jnp.tanh(h)


def _modulated_norm(x, g, shift, scale):
    ms = jnp.mean(x * x, axis=-1, keepdims=True)
    return (x * lax.rsqrt(ms + EPS) * g) * (1.0 + scale) + shift


def _dwconv(pad_ref, w_ref, start, nrows, ksize):
    left = (ksize - 1) // 2
    acc = None
    for j in range(ksize):
        lo = PAD + start + j - left
        term = w_ref[j:j + 1, :] * pad_ref[lo:lo + nrows, :]
        acc = term if acc is None else acc + term
    return acc


def _fill_padded(pad_ref, rows, seq):
    width = pad_ref.shape[1]
    pad_ref[0:PAD, :] = jnp.zeros((PAD, width), F32)
    pad_ref[PAD + seq:2 * PAD + seq, :] = jnp.zeros((PAD, width), F32)
    pad_ref[PAD:PAD + seq, :] = rows


def _ada_kernel(c_ref, w_ref, b_ref, o_ref):
    o_ref[0] = _dot_2x(_silu(c_ref[...]), w_ref[0]) + b_ref[0]


def _ada(cc, ada_w, ada_b):
    depth, d, n = ada_w.shape
    tn = 1536
    return pl.pallas_call(
        _ada_kernel,
        grid=(depth, n // tn),
        in_specs=[pl.BlockSpec(cc.shape, lambda l, j: (0, 0)),
                  pl.BlockSpec((1, d, tn), lambda l, j: (l, 0, j)),
                  pl.BlockSpec((1, 1, tn), lambda l, j: (l, 0, j))],
        out_specs=pl.BlockSpec((1, cc.shape[0], tn), lambda l, j: (l, 0, j)),
        out_shape=jax.ShapeDtypeStruct((depth, cc.shape[0], n), F32),
        compiler_params=_cparams(("parallel", "parallel")),
        name="ada",
    )(cc, ada_w, ada_b.reshape(depth, 1, n))


def _inproj_seq_kernel(x_ref, sh_ref, sc_ref, g_ref, wa_ref, wm_ref, za_ref, zm_ref, zg_ref, zr_ref):
    b = pl.program_id(1)
    h = _modulated_norm(x_ref[0], g_ref[...], sh_ref[...], sc_ref[...]).astype(BF16)
    za = _dot(h, wa_ref[...])
    za_ref[0] = za
    zm = _dot(h, wm_ref[...])
    zm_ref[0] = zm[:, :ZM_MAIN].astype(BF16)
    zg_ref[0] = zm[:, ZM_MAIN:]
    zr_ref[:, pl.ds(b, 1), :] = za[:, HY_COLS:HY_COLS + RG_WIDTH].reshape(za.shape[0], 1, RG_WIDTH)


def _inproj_seq(x, mods, prm, l, row, tm):
    bsz, seq, d = x.shape
    return pl.pallas_call(
        _inproj_seq_kernel,
        grid=(seq // tm, bsz),
        in_specs=[pl.BlockSpec((1, tm, d), lambda i, b: (b, i, 0)),
                  _mod_spec(l, 0, row), _mod_spec(l, 1, row),
                  _layer_spec(l, (1, d)),
                  _layer_spec(l, (d, ZA_COLS), single=True), _layer_spec(l, (d, ZM_COLS), single=True)],
        out_specs=[pl.BlockSpec((1, tm, ZA_COLS), lambda i, b: (b, i, 0)),
                   pl.BlockSpec((1, tm, ZM_MAIN), lambda i, b: (b, i, 0)),
                   pl.BlockSpec((1, tm, ZM_COLS - ZM_MAIN), lambda i, b: (b, i, 0)),
                   pl.BlockSpec((tm, bsz, RG_WIDTH), lambda i, b: (i, 0, 0))],
        out_shape=[jax.ShapeDtypeStruct((bsz, seq, ZA_COLS), F32),
                   jax.ShapeDtypeStruct((bsz, seq, ZM_MAIN), BF16),
                   jax.ShapeDtypeStruct((bsz, seq, ZM_COLS - ZM_MAIN), F32),
                   jax.ShapeDtypeStruct((seq, bsz, RG_WIDTH), F32)],
        compiler_params=_cparams(("parallel", "arbitrary")),
        name="inproj_seq",
    )(x, mods, mods, prm['g1'], prm['w_a'], prm['w_m'])


def _inproj_grid_kernel(x_ref, sh_ref, sc_ref, g_ref, wa_ref, wm_ref, perm_ref, za_ref, zm_ref, zg_ref,
                        zr_ref, *, rows, ncol):
    b = pl.program_id(1)
    d = g_ref.shape[1]
    x = x_ref[0].reshape(rows * ncol, d)
    h = _modulated_norm(x, g_ref[...], sh_ref[...], sc_ref[...]).astype(BF16)
    za = _dot(h, wa_ref[...])
    za_ref[0] = za.reshape(rows, ncol, ZA_COLS)
    zr_ref[:, :, pl.ds(b, 1), :] = za[:, HY_COLS:HY_COLS + RG_WIDTH].reshape(rows, ncol, 1, RG_WIDTH)
    hp = _dot(perm_ref[...], h).astype(BF16)
    zm = _dot(hp, wm_ref[...])
    zm_ref[0] = zm[:, :ZM_MAIN].astype(BF16)
    zg_ref[0] = zm[:, ZM_MAIN:]


def _inproj_grid(x, mods, prm, l, ncol):
    bsz, seq, d = x.shape
    rows = seq // GRID_W
    tm = rows * ncol
    src = np.arange(tm)
    r, c = src // ncol, src % ncol
    perm = np.zeros((tm, tm), np.float32)
    perm[c * rows + r, src] = 1.0
    za, zm, zg, zr = pl.pallas_call(
        functools.partial(_inproj_grid_kernel, rows=rows, ncol=ncol),
        grid=(GRID_W // ncol, bsz),
        in_specs=[pl.BlockSpec((1, rows, ncol, d), lambda i, b: (b, 0, i, 0)),
                  _mod_spec(l, 0, None), _mod_spec(l, 1, None),
                  _layer_spec(l, (1, d)),
                  _layer_spec(l, (d, ZA_COLS), single=True), _layer_spec(l, (d, ZM_COLS), single=True),
                  _const_spec((tm, tm))],
        out_specs=[pl.BlockSpec((1, rows, ncol, ZA_COLS), lambda i, b: (b, 0, i, 0)),
                   pl.BlockSpec((1, tm, ZM_MAIN), lambda i, b: (b, i, 0)),
                   pl.BlockSpec((1, tm, ZM_COLS - ZM_MAIN), lambda i, b: (b, i, 0)),
                   pl.BlockSpec((rows, ncol, bsz, RG_WIDTH), lambda i, b: (0, i, 0, 0))],
        out_shape=[jax.ShapeDtypeStruct((bsz, rows, GRID_W, ZA_COLS), F32),
                   jax.ShapeDtypeStruct((bsz, seq, ZM_MAIN), BF16),
                   jax.ShapeDtypeStruct((bsz, seq, ZM_COLS - ZM_MAIN), F32),
                   jax.ShapeDtypeStruct((rows, GRID_W, bsz, RG_WIDTH), F32)],
        compiler_params=_cparams(("parallel", "arbitrary")),
        name="inproj_grid",
    )(x.reshape(bsz, rows, GRID_W, d), mods, mods, prm['g1'], prm['w_a'], prm['w_m'],
      jnp.asarray(perm, dtype=BF16))
    return za.reshape(bsz, seq, ZA_COLS), zm, zg, zr.reshape(seq, bsz, RG_WIDTH)


def _dft_tables(seq):
    n = 2 * seq
    k = np.arange(seq, dtype=np.int64)[:, None]
    s = np.arange(seq, dtype=np.int64)[None, :]
    ang = (2.0 * np.pi / n) * ((k * s) % n).astype(np.float64)
    top = np.cos(ang)
    bot = -np.sin(ang)
    bot[0, :] = np.where(np.arange(seq) % 2 == 0, 1.0, -1.0)
    fwd = np.concatenate([top, bot], axis=0)
    scale = np.full((n, 1), 2.0 / n)
    scale[0, 0] = 1.0 / n
    scale[seq, 0] = 1.0 / n
    inv = (fwd * scale).T
    return jnp.asarray(fwd.astype(np.float32)), jnp.asarray(inv.astype(np.float32))


def _hy_feature_tables(seq):
    pos = np.arange(seq, dtype=np.float64)[:, None]
    t = pos / max(seq - 1, 1)
    bands = (HY_POS_DIM - 1) // 2
    fr = np.linspace(1e-4, bands - 1, bands).astype(np.float32).astype(np.float64)[None]
    ang = (2.0 * math.pi / seq) * fr * pos
    feats = np.concatenate([t, np.cos(ang), -np.sin(ang)], axis=-1)
    feats = np.pad(feats, ((0, 0), (0, LANES - HY_POS_DIM))).T
    max_decay = math.log(HY_TARGET) / HY_FAST_DECAY
    min_decay = math.log(HY_TARGET) / HY_SLOW_DECAY
    deltas = np.linspace(min_decay, max_decay, HY_WIDTH).astype(np.float32).astype(np.float64)
    decay = np.exp(-t * np.abs(deltas)[None, :])
    return jnp.asarray(feats, dtype=F32), jnp.asarray(decay, dtype=F32)


def _hy_filter_kernel(feats_ref, w1_ref, b1_ref, w2_ref, b2_ref, w3_ref, b3_ref, fq_ref,
                      decay_ref, wf_ref, f_ref, h_scr, *, seq, tr):
    i = pl.program_id(0)
    depth = w1_ref.shape[0]
    fw = 2 * HY_WIDTH

    @pl.when(i == 0)
    def _():
        dec = decay_ref[...]
        for l in range(depth):
            h = jnp.sin(fq_ref[l, 0] * (_dot_f32(w1_ref[l], feats_ref[...]) + b1_ref[l]))
            h = jnp.sin(fq_ref[l, 1] * (_dot_f32(w2_ref[l], h) + b2_ref[l]))
            h = _dot_f32(w3_ref[l], h).T + b3_ref[l]
            hf = h[:, :HY_WIDTH] * dec
            hb = h[:, HY_WIDTH:] * dec
            row = lax.broadcasted_iota(jnp.int32, hb.shape, 0)
            hb = jnp.where(row == 0, 0.0, hb)
            h_scr[:, l * fw:l * fw + HY_WIDTH] = hf.astype(BF16)
            h_scr[:, l * fw + HY_WIDTH:(l + 1) * fw] = hb.astype(BF16)

    g = _dot(wf_ref[...].astype(BF16), h_scr[...])
    row = i * tr + lax.broadcasted_iota(jnp.int32, (tr, HY_WIDTH), 0)
    for l in range(depth):
        gf = g[:, l * fw:l * fw + HY_WIDTH]
        gb = g[:, l * fw + HY_WIDTH:(l + 1) * fw]
        f_ref[l] = jnp.where(row <= seq, gf + gb, gf - gb)


def _hy_filter(seq, prm, fwd):
    feats, decay = _hy_feature_tables(seq)
    depth = prm['hy_w1'].shape[0]
    tr = min(seq, 512)
    full = lambda shp: pl.BlockSpec(shp, lambda i: (0,) * len(shp))
    names = ('hy_w1', 'hy_b1', 'hy_w2', 'hy_b2', 'hy_w3', 'hy_b3', 'hy_freq')
    return pl.pallas_call(
        functools.partial(_hy_filter_kernel, seq=seq, tr=tr),
        grid=(2 * seq // tr,),
        in_specs=[full((LANES, seq))] + [full(prm[k].shape) for k in names]
                 + [full((seq, HY_WIDTH)), pl.BlockSpec((tr, seq), lambda i: (i, 0))],
        out_specs=pl.BlockSpec((depth, tr, HY_WIDTH), lambda i: (0, i, 0)),
        out_shape=jax.ShapeDtypeStruct((depth, 2 * seq, HY_WIDTH), F32),
        scratch_shapes=[pltpu.VMEM((seq, depth * 2 * HY_WIDTH), BF16)],
        compiler_params=_cparams(("arbitrary",)),
        name="hy_filter",
    )(feats, *[prm[k] for k in names], decay, fwd)


def _hy_pre_kernel(z_ref, cw_ref, bias_ref, x0_ref, e_ref, u_ref, pad_ref, *, seq):
    _fill_padded(pad_ref, z_ref[0], seq)
    zc = _dwconv(pad_ref, cw_ref, 0, seq, HY_SHORT)
    x0 = zc[:, :HY_WIDTH]
    u = zc[:, HY_WIDTH:2 * HY_WIDTH] * zc[:, 2 * HY_WIDTH:]
    x0_ref[...] = x0
    e_ref[...] = x0 * (u * bias_ref[...])
    u_ref[...] = u.astype(BF16)


def _hy_fwd_kernel(u_ref, wt_ref, wb_ref, ft_ref, fb_ref, yt_ref, yb_ref, *, nb):
    ft, fb = ft_ref[...], fb_ref[...]
    first = pl.program_id(0) == 0
    row0 = jnp.logical_and(lax.broadcasted_iota(jnp.int32, ft.shape, 0) == 0, first)
    f_im = jnp.where(row0, 0.0, fb)
    f_re2 = jnp.where(row0, fb, ft)
    wt = wt_ref[...].astype(BF16)
    wb = wb_ref[...].astype(BF16)
    for b in range(nb):
        cols = slice(b * HY_WIDTH, (b + 1) * HY_WIDTH)
        ub = u_ref[:, cols]
        pr = _dot(wt, ub)
        pi = _dot(wb, ub)
        yt_ref[:, cols] = (pr * ft - pi * f_im).astype(BF16)
        yb_ref[:, cols] = (pr * f_im + pi * f_re2).astype(BF16)


def _hy_inv_kernel(yt_ref, yb_ref, wt_ref, wb_ref, x0_ref, e_ref, o_ref, *, nb):
    wt = wt_ref[...].astype(BF16)
    wb = wb_ref[...].astype(BF16)
    for b in range(nb):
        cols = slice(b * HY_WIDTH, (b + 1) * HY_WIDTH)
        y = _dot(wt, yt_ref[:, cols]) + _dot(wb, yb_ref[:, cols])
        o_ref[:, cols] = x0_ref[:, cols] * y + e_ref[:, cols]


def _hy_conv(z_a, prm, l, filt, fwd, inv):
    bsz, seq, _ = z_a.shape
    wide = bsz * HY_WIDTH
    bcol = pl.BlockSpec((seq, HY_WIDTH), lambda b: (0, b))
    x0, e, u = pl.pallas_call(
        functools.partial(_hy_pre_kernel, seq=seq),
        grid=(bsz,),
        in_specs=[pl.BlockSpec((1, seq, HY_COLS), lambda b: (b, 0, 0)),
                  _layer_spec(l, (HY_SHORT, HY_COLS)), _layer_spec(l, (1, HY_WIDTH))],
        out_specs=[bcol, bcol, bcol],
        out_shape=[jax.ShapeDtypeStruct((seq, wide), F32), jax.ShapeDtypeStruct((seq, wide), F32),
                   jax.ShapeDtypeStruct((seq, wide), BF16)],
        scratch_shapes=[pltpu.VMEM((seq + 2 * PAD, HY_COLS), F32)],
        compiler_params=_cparams(("parallel",)),
        name="hy_pre",
    )(z_a, prm['hy_conv'], prm['hy_bias'])

    tk = min(seq, 512)
    kt = seq // tk
    yt, yb = pl.pallas_call(
        functools.partial(_hy_fwd_kernel, nb=bsz),
        grid=(kt,),
        in_specs=[_const_spec((seq, wide)),
                  pl.BlockSpec((tk, seq), lambda i: (i, 0)),
                  pl.BlockSpec((tk, seq), lambda i: (kt + i, 0)),
                  pl.BlockSpec((None, tk, HY_WIDTH), lambda i: (l, i, 0)),
                  pl.BlockSpec((None, tk, HY_WIDTH), lambda i: (l, kt + i, 0))],
        out_specs=[pl.BlockSpec((tk, wide), lambda i: (i, 0)), pl.BlockSpec((tk, wide), lambda i: (i, 0))],
        out_shape=[jax.ShapeDtypeStruct((seq, wide), BF16), jax.ShapeDtypeStruct((seq, wide), BF16)],
        compiler_params=_cparams(("parallel",)),
        name="hy_fwd",
    )(u, fwd, fwd, filt, filt)

    tt = min(seq, 256)
    return pl.pallas_call(
        functools.partial(_hy_inv_kernel, nb=bsz),
        grid=(seq // tt,),
        in_specs=[_const_spec((seq, wide)), _const_spec((seq, wide)),
                  pl.BlockSpec((tt, seq), lambda i: (i, 0)),
                  pl.BlockSpec((tt, seq), lambda i: (i, 1)),
                  pl.BlockSpec((tt, wide), lambda i: (i, 0)),
                  pl.BlockSpec((tt, wide), lambda i: (i, 0))],
        out_specs=pl.BlockSpec((tt, wide), lambda i: (i, 0)),
        out_shape=jax.ShapeDtypeStruct((seq, wide), F32),
        compiler_params=_cparams(("parallel",)),
        name="hy_inv",
    )(yt, yb, inv, inv, x0, e)


def _running_max(v, reverse):
    row = lax.broadcasted_iota(jnp.int32, v.shape, 0)
    k = 1
    while k < ML_CHUNK:
        if reverse:
            sh = jnp.where(row < ML_CHUNK - k, pltpu.roll(v, ML_CHUNK - k, 0), -jnp.inf)
        else:
            sh = jnp.where(row >= k, pltpu.roll(v, k, 0), -jnp.inf)
        v = jnp.maximum(v, sh)
        k *= 2
    return v


def _ml_gate_kernel(x_ref, b_ref, low_ref, up_ref, o_ref):
    for d, tri_ref in ((0, low_ref), (1, up_ref)):
        li = x_ref[2 * d] + b_ref[2 * d]
        pre = x_ref[2 * d + 1] + b_ref[2 * d + 1]
        lf = jnp.minimum(pre, 0.0) - jnp.log(1.0 + jnp.exp(-jnp.abs(pre)))
        hi, mid, lo = _split3(lf)
        tri = tri_ref[...]
        cum = _dot(tri, hi) + _dot(tri, mid) + _dot(tri, lo)
        r = li - cum
        o_ref[3 * d] = r
        o_ref[3 * d + 1] = _running_max(r, reverse=(d == 1))
        o_ref[3 * d + 2] = cum
    o_ref[6] = jnp.zeros(o_ref.shape[1:], F32)
    o_ref[7] = jnp.zeros(o_ref.shape[1:], F32)


def _ml_gates(gates, gate_b):
    bsz, seq, _ = gates.shape
    n = seq // ML_CHUNK
    m = bsz * ML_HEADS * n
    x = gates.reshape(bsz, n, ML_CHUNK, 4, ML_HEADS).transpose(3, 2, 0, 4, 1).reshape(4, ML_CHUNK, m)
    bias = jnp.broadcast_to(gate_b.reshape(4, 1, 1, ML_HEADS, 1), (4, 1, bsz, ML_HEADS, n)).reshape(4, 1, m)
    r = np.arange(ML_CHUNK)
    low = jnp.asarray(r[:, None] >= r[None, :], dtype=BF16)
    up = jnp.asarray(r[:, None] <= r[None, :], dtype=BF16)
    full = lambda shp: pl.BlockSpec(shp, lambda: (0,) * len(shp))
    out = pl.pallas_call(
        _ml_gate_kernel,
        in_specs=[full((4, ML_CHUNK, m)), full((4, 1, m)), full((ML_CHUNK, ML_CHUNK)),
                  full((ML_CHUNK, ML_CHUNK))],
        out_specs=full((SUBLANES, ML_CHUNK, m)),
        out_shape=jax.ShapeDtypeStruct((SUBLANES, ML_CHUNK, m), F32),
        compiler_params=pltpu.CompilerParams(vmem_limit_bytes=VMEM_LIMIT),
        name="ml_gates",
    )(x, bias, low, up)
    return out.reshape(SUBLANES, ML_CHUNK, bsz, ML_HEADS, n).transpose(2, 3, 4, 0, 1)


def _ml_column(rows, j):
    if j:
        rows = pltpu.roll(rows, SUBLANES - j, 0)
    return rows.T[:, 0:1]


def _ml_state_step(d, hh, n, kt_scr, va_scr, row_ref, c_ref, m_ref, cseq_scr, mseq_scr):
    sl = pl.ds(pl.multiple_of(n * ML_CHUNK, ML_CHUNK), ML_CHUNK)
    dh = ML_HEAD_DIM
    kt = kt_scr[hh, n]
    va = va_scr[sl, hh * 2 * dh:(hh + 1) * 2 * dh]
    rows = row_ref[0, hh, n]
    r_r = rows[3 * d:3 * d + 1, :]
    last = ML_CHUNK - 1 if d == 0 else 0
    mx_all = rows[3 * d + 1:3 * d + 2, last:last + 1]
    b_all = rows[3 * d + 2:3 * d + 3, last:last + 1]
    c0 = c_ref[0, d, hh]
    m0v = m_ref[0, d, hh]
    m0 = m0v[:, 0:1]
    cseq_scr[d, hh, n] = c0.astype(BF16)
    mseq_scr[d, hh, n] = m0v
    w_r = jnp.exp(r_r - mx_all)
    upd = _dot(kt * w_r.astype(BF16), va)
    m_loc = b_all + mx_all
    m_new = jnp.maximum(b_all + m0, m_loc)
    a = jnp.exp(b_all + m0 - m_new)
    sc = jnp.exp(m_loc - m_new)
    c_ref[0, d, hh] = a * c0 + sc * upd
    m_ref[0, d, hh] = jnp.broadcast_to(m_new, (1, dh))


def _ml_output(hh, n, q_scr, kt_scr, va_scr, row_ref, cseq_scr, mseq_scr, h_ref):
    dh = ML_HEAD_DIM
    sl = pl.ds(pl.multiple_of(n * ML_CHUNK, ML_CHUNK), ML_CHUNK)
    q = q_scr[sl, hh * dh:(hh + 1) * dh]
    va = va_scr[sl, hh * 2 * dh:(hh + 1) * 2 * dh]
    qk = _dot(q, kt_scr[hh, n])
    tt = lax.broadcasted_iota(jnp.int32, (ML_CHUNK, ML_CHUNK), 0)
    ss = lax.broadcasted_iota(jnp.int32, (ML_CHUNK, ML_CHUNK), 1)
    rows = row_ref[0, hh, n]
    h = None
    for d in range(2):
        r_r = rows[3 * d:3 * d + 1, :]
        mx_c = _ml_column(rows, 3 * d + 1)
        b_c = _ml_column(rows, 3 * d + 2)
        m0 = mseq_scr[d, hh, n][:, 0:1]
        seen = (ss <= tt) if d == 0 else (ss >= tt)
        mu_c = jnp.maximum(m0, mx_c)
        mu = jnp.broadcast_to(mu_c, (ML_CHUNK, ML_CHUNK))
        floor = jnp.broadcast_to(jnp.exp(-(b_c + mu_c)), (ML_CHUNK, ML_CHUNK))
        e = jnp.where(seen, jnp.exp(r_r - mu), 0.0)
        wi = jnp.exp(m0 - mu).astype(BF16)
        s = (qk * e).astype(BF16)
        res = _dot(jnp.concatenate([s, wi * q], axis=1), jnp.concatenate([va, cseq_scr[d, hh, n]], axis=0))
        hd = res[:, :dh] / jnp.maximum(jnp.abs(res[:, dh:]), floor)
        h = hd if h is None else h + hd
    h_ref[0, sl, hh * dh:(hh + 1) * dh] = h


def _ml_kernel(zq_ref, zk_ref, zv_ref, cwq_ref, cwk_ref, row_ref, c0_ref, m0_ref,
               h_ref, c_ref, m_ref, pad_ref, q_scr, kt_scr, va_scr, cseq_scr, mseq_scr, *, seq):
    nchunk = seq // ML_CHUNK
    dh = ML_HEAD_DIM
    hp = ML_HEADS_PER_STEP
    _fill_padded(pad_ref, zq_ref[0].astype(F32), seq)
    q_scr[...] = _silu(_dwconv(pad_ref, cwq_ref, 0, seq, ML_CONV)).astype(BF16)
    pad_ref[PAD:PAD + seq, :] = zk_ref[0].astype(F32)
    for n in range(nchunk):
        rows = slice(n * ML_CHUNK, (n + 1) * ML_CHUNK)
        kk = _silu(_dwconv(pad_ref, cwk_ref, n * ML_CHUNK, ML_CHUNK, ML_CONV)) * (dh ** -0.5)
        for hh in range(hp):
            kt_scr[hh, n] = kk[:, hh * dh:(hh + 1) * dh].T.astype(BF16)
            va_scr[rows, hh * 2 * dh:hh * 2 * dh + dh] = zv_ref[0, rows, hh * dh:(hh + 1) * dh].astype(BF16)
            va_scr[rows, hh * 2 * dh + dh:(hh + 1) * 2 * dh] = jnp.ones((ML_CHUNK, dh), BF16)
    c_ref[...] = c0_ref[...]
    m_ref[...] = m0_ref[...]

    def state_body(i, carry):
        for hh in range(hp):
            for d, n in ((0, i), (1, nchunk - 1 - i)):
                _ml_state_step(d, hh, n, kt_scr, va_scr, row_ref, c_ref, m_ref, cseq_scr, mseq_scr)
        return carry

    lax.fori_loop(0, nchunk, state_body, 0, unroll=2)

    def out_body(n, carry):
        for hh in range(hp):
            _ml_output(hh, n, q_scr, kt_scr, va_scr, row_ref, cseq_scr, mseq_scr, h_ref)
        return carry

    lax.fori_loop(0, nchunk, out_body, 0, unroll=min(8, nchunk))


def _mlstm(z_ml, z_gate, prm, l, state):
    bsz, seq, _ = z_ml.shape
    nchunk = seq // ML_CHUNK
    row = _ml_gates(z_gate[..., :ML_GATE_COLS], prm['ml_gate_b'][l])
    dh, nh, hp = ML_HEAD_DIM, ML_HEADS, ML_HEADS_PER_STEP
    ng = nh // hp
    w = hp * dh
    zspec = lambda off: pl.BlockSpec((1, seq, w), lambda b, g: (b, 0, off + g))
    cspec = pl.BlockSpec((1, 2, hp, dh, 2 * dh), lambda b, g: (b, 0, g, 0, 0))
    mspec = pl.BlockSpec((1, 2, hp, 1, dh), lambda b, g: (b, 0, g, 0, 0))
    return pl.pallas_call(
        functools.partial(_ml_kernel, seq=seq),
        grid=(bsz, ng),
        in_specs=[zspec(0), zspec(ng), zspec(2 * ng),
                  pl.BlockSpec((None, ML_CONV, w), lambda b, g: (l, 0, g)),
                  pl.BlockSpec((None, ML_CONV, w), lambda b, g: (l, 0, ng + g)),
                  pl.BlockSpec((1, hp, nchunk, SUBLANES, ML_CHUNK), lambda b, g: (b, g, 0, 0, 0)),
                  cspec, mspec],
        out_specs=[pl.BlockSpec((1, seq, w), lambda b, g: (b, 0, g)), cspec, mspec],
        out_shape=[jax.ShapeDtypeStruct((bsz, seq, ML_WIDTH), F32),
                   jax.ShapeDtypeStruct((bsz, 2, nh, dh, 2 * dh), F32),
                   jax.ShapeDtypeStruct((bsz, 2, nh, 1, dh), F32)],
        scratch_shapes=[pltpu.VMEM((seq + 2 * PAD, w), F32),
                        pltpu.VMEM((seq, w), BF16),
                        pltpu.VMEM((hp, nchunk, dh, ML_CHUNK), BF16),
                        pltpu.VMEM((seq, 2 * w), BF16),
                        pltpu.VMEM((2, hp, nchunk, dh, 2 * dh), BF16),
                        pltpu.VMEM((2, hp, nchunk, 1, dh), F32)],
        compiler_params=_cparams(("parallel", "parallel")),
        name="mlstm",
    )(z_ml, z_ml, z_ml, prm['ml_conv'], prm['ml_conv'], row, *state)


RG_TILE = 256


def _rg_kernel(z_ref, cw_ref, wg_ref, bg_ref, lam_ref, h0_ref, of_ref, ob_ref, st_ref,
               zp_scr, a_scr, b_scr, h_scr, carry_scr, *, seq, tt):
    i = pl.program_id(0)
    nt = seq // tt
    nb = z_ref.shape[1]
    c = RG_WIDTH

    @pl.when(i == 0)
    def _():
        carry_scr[...] = h0_ref[...]

    lam = lam_ref[...]
    sp = jnp.maximum(-lam, 0.0) + jnp.log(1.0 + jnp.exp(-jnp.abs(lam)))
    coef = (-0.5 * RG_C * math.log2(math.e)) * sp
    left = (RG_CONV - 1) // 2
    right = RG_CONV - 1 - left
    for d, tile in ((0, i), (1, nt - 1 - i)):
        t0 = tile * tt
        zp_scr[left:left + tt] = z_ref[pl.ds(t0, tt)]
        for k in range(left):
            src = t0 - left + k
            zp_scr[k:k + 1] = z_ref[pl.ds(jnp.maximum(src, 0), 1)] * (src >= 0).astype(F32)
        for k in range(right):
            src = t0 + tt + k
            zp_scr[left + tt + k:left + tt + k + 1] = (
                z_ref[pl.ds(jnp.minimum(src, seq - 1), 1)] * (src < seq).astype(F32))
        xr = None
        for j in range(RG_CONV):
            term = cw_ref[j:j + 1, :] * zp_scr[j:j + tt]
            xr = term if xr is None else xr + term
        half_pre = (_dot(xr.reshape(tt * nb, c).astype(BF16), wg_ref[:, 2 * d * c:2 * (d + 1) * c])
                    + bg_ref[:, 2 * d * c:2 * (d + 1) * c])
        t_r = jnp.tanh(half_pre[:, :c])
        t_i = jnp.tanh(half_pre[:, c:])
        cd = coef[d:d + 1, :]
        a = jnp.exp2(cd + cd * t_r)
        y = 1.0 - a * a
        root = jnp.where(y > 0.0, y * lax.rsqrt(y), 0.0)
        half_x = 0.5 * xr.reshape(tt * nb, c)
        a_scr[d] = a.reshape(tt, nb, c)
        b_scr[d] = (root * (half_x + half_x * t_i)).reshape(tt, nb, c)

    def body(s, carry):
        hf, hb = carry
        hf = a_scr[0, s] * hf + b_scr[0, s]
        h_scr[0, s] = hf
        sb = tt - 1 - s
        hb = a_scr[1, sb] * hb + b_scr[1, sb]
        h_scr[1, sb] = hb
        return hf, hb

    hf, hb = lax.fori_loop(0, tt, body, (carry_scr[0], carry_scr[1]), unroll=8)
    carry_scr[0] = hf
    carry_scr[1] = hb
    st_ref[0] = hf
    st_ref[1] = hb
    for d, o_ref in ((0, of_ref), (1, ob_ref)):
        o_ref[...] = jnp.swapaxes(h_scr[d].reshape(tt // SUBLANES, SUBLANES, nb, c), 1, 2)


def _rg_gate_weights(rg_gate_w, rg_gate_b):
    depth = rg_gate_w.shape[0]
    eye = jnp.eye(RG_HEADS, dtype=rg_gate_w.dtype)
    full = jnp.einsum('ldghij,hk->ldghikj', 0.5 * rg_gate_w, eye)
    full = full.reshape(depth, 2, 2, RG_WIDTH, RG_WIDTH)
    wg = full.transpose(0, 3, 1, 2, 4).reshape(depth, RG_WIDTH, 4 * RG_WIDTH).astype(BF16)
    return wg, 0.5 * rg_gate_b.reshape(depth, 1, 4 * RG_WIDTH)


def _rglru(zr, prm, l, h0):
    seq, bsz, c = zr.shape
    tt = min(seq, RG_TILE)
    nt = seq // tt
    ng = tt // SUBLANES
    state_spec = pl.BlockSpec((2, bsz, c), lambda i: (0, 0, 0))
    out_sds = jax.ShapeDtypeStruct((seq // SUBLANES, bsz, SUBLANES, c), F32)
    return pl.pallas_call(
        functools.partial(_rg_kernel, seq=seq, tt=tt),
        grid=(nt,),
        in_specs=[_const_spec((seq, bsz, c)),
                  _layer_spec(l, (RG_CONV, c)), _layer_spec(l, (c, 4 * c)), _layer_spec(l, (1, 4 * c)),
                  _layer_spec(l, (2, c)), state_spec],
        out_specs=[pl.BlockSpec((ng, bsz, SUBLANES, c), lambda i: (i, 0, 0, 0)),
                   pl.BlockSpec((ng, bsz, SUBLANES, c), lambda i: (nt - 1 - i, 0, 0, 0)),
                   state_spec],
        out_shape=[out_sds, out_sds, jax.ShapeDtypeStruct((2, bsz, c), F32)],
        scratch_shapes=[pltpu.VMEM((tt + RG_CONV - 1, bsz, c), F32),
                        pltpu.VMEM((2, tt, bsz, c), F32), pltpu.VMEM((2, tt, bsz, c), F32),
                        pltpu.VMEM((2, tt, bsz, c), F32), pltpu.VMEM((2, bsz, c), F32)],
        compiler_params=_cparams(("arbitrary",)),
        name="rglru",
    )(zr, prm['rg_conv'], prm['rg_wg'], prm['rg_bg'], prm['rg_lambda'], h0)


def _group_mean_sq(y, group):
    assert 2 * group == LANES and y.shape[1] % LANES == 0
    sq = y * y
    outs = []
    for t in range(y.shape[1] // LANES):
        blk = sq[:, t * LANES:(t + 1) * LANES]
        low = lax.broadcasted_iota(jnp.int32, blk.shape, 1) < group
        s_low = jnp.sum(jnp.where(low, blk, 0.0), axis=-1, keepdims=True)
        s_high = jnp.sum(jnp.where(low, 0.0, blk), axis=-1, keepdims=True)
        outs.append(jnp.where(low, s_low, s_high) * (1.0 / group))
    return jnp.concatenate(outs, axis=-1)


def _mix_mlp_kernel(x_ref, yhy_ref, hm_ref, o_ref, hrf_ref, hrb_ref, gate_ref, g_ref, wo_ref,
                    gt_ref, sh_ref, sc_ref, gt2_ref, g2_ref, w1_ref, w2_ref, fg_ref, out_ref,
                    *, final, fc, hm_rows):
    g = g_ref[...]
    yhy = yhy_ref[...]
    yhy = yhy * lax.rsqrt(_group_mean_sq(yhy, HY_WIDTH // HY_GROUPS) + EPS) * g[:, :HY_WIDTH]
    parts = [yhy.astype(BF16)]
    if hm_rows:
        hm = jnp.concatenate([hm_ref[0, :, r, :] for r in range(hm_rows)], axis=0)
    else:
        hm = hm_ref[0]
    o_gate = _sigmoid(o_ref[0])
    for h in range(ML_HEADS):
        sl = slice(h * ML_HEAD_DIM, (h + 1) * ML_HEAD_DIM)
        hh = hm[:, sl]
        ms = jnp.mean(hh * hh, axis=-1, keepdims=True)
        yh = hh * lax.rsqrt(ms + EPS) * g[:, HY_WIDTH + h * ML_HEAD_DIM:HY_WIDTH + (h + 1) * ML_HEAD_DIM]
        parts.append((yh * o_gate[:, sl]).astype(BF16))
    gp = gate_ref[0]
    gelu = 0.5 * gp * (1.0 + jnp.tanh(math.sqrt(2.0 / math.pi) * (gp + 0.044715 * (gp * gp * gp))))
    hr = hrf_ref[...] + hrb_ref[...]
    yrg = hr.reshape(hr.shape[0] * SUBLANES, RG_WIDTH) * gelu
    yrg = yrg * lax.rsqrt(_group_mean_sq(yrg, RG_BLOCK) + EPS) * g[:, HY_WIDTH + ML_WIDTH:]
    parts.append(yrg.astype(BF16))
    y = jnp.concatenate(parts, axis=-1)
    x = x_ref[0] + gt_ref[...] * _dot(y, wo_ref[...])

    h = _modulated_norm(x, g2_ref[...], sh_ref[...], sc_ref[...]).astype(BF16)
    acc = None
    for j in range(D_FF // fc):
        a = jnp.maximum(_dot(h, w1_ref[:, j * fc:(j + 1) * fc]), 0.0)
        t = _dot((a * a).astype(BF16), w2_ref[j * fc:(j + 1) * fc, :])
        acc = t if acc is None else acc + t
    out = x + gt2_ref[...] * acc
    if final:
        ms = jnp.mean(out * out, axis=-1, keepdims=True)
        out = out * lax.rsqrt(ms + EPS) * fg_ref[...]
    out_ref[0] = out


def _mix_mlp(x, y_hy, hm, z_a, hr_f, hr_b, mods, prm, l, row, final_g, final, tm, hm_col_major):
    bsz, seq, d = x.shape
    tok = lambda w, j: pl.BlockSpec((1, tm, w), lambda i, b: (b, i, j))
    if hm_col_major:
        hm_rows = tm // GRID_W
        hm = hm.reshape(bsz, GRID_W, seq // GRID_W, ML_WIDTH)
        hm_spec = pl.BlockSpec((1, GRID_W, hm_rows, ML_WIDTH), lambda i, b: (b, 0, i, 0))
    else:
        hm_rows = 0
        hm_spec = tok(ML_WIDTH, 0)
    hr_spec = pl.BlockSpec((tm // SUBLANES, None, SUBLANES, RG_WIDTH), lambda i, b: (i, b, 0, 0))
    return pl.pallas_call(
        functools.partial(_mix_mlp_kernel, final=final, fc=1024, hm_rows=hm_rows),
        grid=(seq // tm, bsz),
        in_specs=[tok(d, 0),
                  pl.BlockSpec((tm, HY_WIDTH), lambda i, b: (i, b)),
                  hm_spec,
                  tok(ML_WIDTH, 2),
                  hr_spec, hr_spec,
                  tok(RG_WIDTH, 6),
                  _layer_spec(l, (1, d)),
                  _layer_spec(l, (d, d), single=True), _mod_spec(l, 2, row),
                  _mod_spec(l, 3, row), _mod_spec(l, 4, row), _mod_spec(l, 5, row),
                  _layer_spec(l, (1, d)),
                  _layer_spec(l, (d, D_FF), single=True), _layer_spec(l, (D_FF, d), single=True),
                  pl.BlockSpec((1, d), lambda i, b: (0, 0))],
        out_specs=tok(d, 0),
        out_shape=jax.ShapeDtypeStruct((bsz, seq, d), F32),
        compiler_params=_cparams(("parallel", "parallel")),
        name="mix_mlp",
    )(x, y_hy, hm, z_a, hr_f, hr_b, z_a, prm['mix_g'], prm['w_o'], mods, mods, mods, mods, prm['g2'],
      prm['w1'], prm['w2'], final_g)


def _split_w_in_kernel(w_ref, wa_ref, wm_ref):
    w = w_ref[0]
    rg = OFF_RG
    o0 = OFF_ML + 3 * ML_WIDTH
    g0 = OFF_ML + 4 * ML_WIDTH
    wa_ref[0] = jnp.concatenate([w[:, :HY_COLS], w[:, rg:rg + RG_WIDTH], w[:, o0:o0 + ML_WIDTH],
                                 w[:, rg + RG_WIDTH:rg + 2 * RG_WIDTH]], axis=1).astype(BF16)
    pad = jnp.zeros((w.shape[0], ZM_COLS - ZM_MAIN - ML_GATE_COLS), F32)
    wm_ref[0] = jnp.concatenate([w[:, OFF_ML:o0], w[:, g0:g0 + ML_GATE_COLS], pad], axis=1).astype(BF16)


def _split_w_in(w_in):
    depth, d, p_in = w_in.shape
    tr = 256
    return pl.pallas_call(
        _split_w_in_kernel,
        grid=(depth, d // tr),
        in_specs=[pl.BlockSpec((1, tr, p_in), lambda l, i: (l, i, 0))],
        out_specs=[pl.BlockSpec((1, tr, ZA_COLS), lambda l, i: (l, i, 0)),
                   pl.BlockSpec((1, tr, ZM_COLS), lambda l, i: (l, i, 0))],
        out_shape=[jax.ShapeDtypeStruct((depth, d, ZA_COLS), BF16),
                   jax.ShapeDtypeStruct((depth, d, ZM_COLS), BF16)],
        compiler_params=_cparams(("parallel", "parallel")),
        name="split_w_in",
    )(w_in)


def _prepare(norm1_g, norm2_g, w_in, hy_conv, hy_w1, hy_b1, hy_w2, hy_b2, hy_w3, hy_b3, hy_freq, hy_bias,
             ml_conv, ml_gate_b, rg_conv, rg_gate_w, rg_gate_b, rg_lambda, mix_norm_g, w_out, mlp_w1, mlp_w2):
    w_a, w_m = _split_w_in(w_in)
    rg_wg, rg_bg = _rg_gate_weights(rg_gate_w, rg_gate_b)
    return dict(
        g1=norm1_g[:, None, :], g2=norm2_g[:, None, :], mix_g=mix_norm_g[:, None, :],
        w_a=w_a, w_m=w_m, w_o=w_out.astype(BF16), w1=mlp_w1.astype(BF16), w2=mlp_w2.astype(BF16),
        hy_conv=hy_conv, hy_bias=hy_bias[:, None, :],
        hy_w1=jnp.swapaxes(jnp.pad(hy_w1, ((0, 0), (0, LANES - HY_POS_DIM), (0, 0))), 1, 2),
        hy_b1=hy_b1[:, :, None], hy_w2=jnp.swapaxes(hy_w2, 1, 2), hy_b2=hy_b2[:, :, None],
        hy_w3=jnp.swapaxes(hy_w3, 1, 2), hy_b3=hy_b3[:, None, :], hy_freq=hy_freq[:, :, :, None],
        ml_conv=ml_conv, ml_gate_b=ml_gate_b,
        rg_conv=rg_conv, rg_wg=rg_wg, rg_bg=rg_bg, rg_lambda=rg_lambda)


def kernel(x, c, ctx, c_ctx, ada_w, ada_b, norm1_g, norm2_g, w_in, hy_conv, hy_w1, hy_b1, hy_w2, hy_b2, hy_w3, hy_b3, hy_freq, hy_bias, ml_conv, ml_gate_b, rg_conv, rg_gate_w, rg_gate_b, rg_lambda, mix_norm_g, w_out, mlp_w1, mlp_w2, final_g):
    bsz, seq, d = x.shape
    clen = ctx.shape[1]
    prm = _prepare(norm1_g, norm2_g, w_in, hy_conv, hy_w1, hy_b1, hy_w2, hy_b2, hy_w3, hy_b3, hy_freq,
                   hy_bias, ml_conv, ml_gate_b, rg_conv, rg_gate_w, rg_gate_b, rg_lambda, mix_norm_g,
                   w_out, mlp_w1, mlp_w2)

    cc = jnp.concatenate([c, c_ctx[None], jnp.zeros((MOD_ROWS - bsz - 1, d), F32)], axis=0)
    mods = _ada(cc, ada_w, ada_b).reshape(DEPTH, MOD_ROWS, N_MOD, 1, d)
    ctx_row = bsz

    fwd_l, inv_l = _dft_tables(seq)
    fwd_c, inv_c = _dft_tables(clen)
    filt_l = _hy_filter(seq, prm, fwd_l)
    filt_c = _hy_filter(clen, prm, fwd_c)
    dh, nh = ML_HEAD_DIM, ML_HEADS
    ml_zero = (jnp.zeros((bsz, 2, nh, dh, 2 * dh), F32), jnp.zeros((bsz, 2, nh, 1, dh), F32))
    rg_zero = jnp.zeros((2, bsz, RG_WIDTH), F32)
    fg = final_g[None]

    ctx_s = ctx
    for l in range(DEPTH):
        need_ctx = l < DEPTH - 1
        zc_a, zc_m, zc_g, zc_r = _inproj_seq(ctx_s, mods, prm, l, ctx_row, clen)
        hm_c, ml_c, ml_m = _mlstm(zc_m, zc_g, prm, l, ml_zero)
        hrf_c, hrb_c, rg_state = _rglru(zc_r, prm, l, rg_zero)

        z_a, z_m, z_g, z_r = _inproj_grid(x, mods, prm, l, 16)
        y_hy = _hy_conv(z_a, prm, l, filt_l, fwd_l, inv_l)
        hm, _, _ = _mlstm(z_m, z_g, prm, l, (ml_c, ml_m))
        hr_f, hr_b, _ = _rglru(z_r, prm, l, rg_state)
        x = _mix_mlp(x, y_hy, hm, z_a, hr_f, hr_b, mods, prm, l, None, fg, final=(l == DEPTH - 1), tm=512,
                     hm_col_major=True)

        if need_ctx:
            y_hy_c = _hy_conv(zc_a, prm, l, filt_c, fwd_c, inv_c)
            ctx_s = _mix_mlp(ctx_s, y_hy_c, hm_c, zc_a, hrf_c, hrb_c, mods, prm, l, ctx_row, fg, final=False,
                             tm=clen, hm_col_major=False)
    return x
```

```python
import functools
import math

import numpy as np
import jax
import jax.numpy as jnp
from jax import lax
from jax.experimental import pallas as pl
from jax.experimental.pallas import tpu as pltpu

F32 = jnp.float32
BF16 = jnp.bfloat16

D_MODEL = 1024
DEPTH = 2
GRID_W = 64
HY_WIDTH = 256
HY_GROUPS = 4
HY_SHORT = 3
HY_POS_DIM = 33
HY_FFN = 64
HY_FAST_DECAY = 0.3
HY_SLOW_DECAY = 1.5
HY_TARGET = 1e-2
ML_WIDTH = 512
ML_HEADS = 4
ML_HEAD_DIM = 128
ML_CONV = 4
ML_CHUNK = 128
RG_WIDTH = 256
RG_HEADS = 4
RG_BLOCK = 64
RG_CONV = 4
RG_C = 8.0
D_FF = 4096
N_MOD = 6
EPS = 1e-6
HY_COLS = 3 * HY_WIDTH
ML_GATE_COLS = 4 * ML_HEADS
ML_COLS = 4 * ML_WIDTH + ML_GATE_COLS
OFF_ML = HY_COLS
OFF_RG = HY_COLS + ML_COLS

LANES = 128
SUBLANES = 8
VMEM_LIMIT = 56 * 1024 * 1024

ZA_COLS = 1792
ZM_COLS = 1664
ZM_MAIN = 3 * ML_WIDTH
ZO_COLS = ML_WIDTH + RG_WIDTH
PAD = SUBLANES
MOD_ROWS = 2 * SUBLANES
ML_HEADS_PER_STEP = 4


def _cparams(sem):
    return pltpu.CompilerParams(dimension_semantics=sem, vmem_limit_bytes=VMEM_LIMIT)


def _const_spec(shape):
    nd = len(shape)
    return pl.BlockSpec(shape, lambda *_: (0,) * nd, pipeline_mode=pl.Buffered(1))


def _layer_spec(l, shape, single=False):
    nd = len(shape)
    mode = dict(pipeline_mode=pl.Buffered(1)) if single else {}
    return pl.BlockSpec((None,) + tuple(shape), lambda *_: (l,) + (0,) * nd, **mode)


def _mod_spec(l, idx, row):
    blk = (None, None, None, 1, D_MODEL)
    if row is None:
        return pl.BlockSpec(blk, lambda i, b: (l, b, idx, 0, 0))
    return pl.BlockSpec(blk, lambda i, b: (l, row, idx, 0, 0))


def _split3(v):
    hi = v.astype(BF16)
    r = v - hi.astype(F32)
    mid = r.astype(BF16)
    lo = (r - mid.astype(F32)).astype(BF16)
    return hi, mid, lo


def _dot(a, b):
    return jnp.dot(a, b, preferred_element_type=F32)


def _dot_f32(a, b):
    ah, am, al = _split3(a)
    bh, bm, bl = _split3(b)
    return (_dot(ah, bh) + (_dot(ah, bm) + _dot(am, bh))
            + (_dot(ah, bl) + _dot(am, bm) + _dot(al, bh)))


def _dot_2x(a, b):
    ah = a.astype(BF16)
    al = (a - ah.astype(F32)).astype(BF16)
    bh = b.astype(BF16)
    bl = (b - bh.astype(F32)).astype(BF16)
    return _dot(ah, bh) + (_dot(ah, bl) + _dot(al, bh))


def _sigmoid(v):
    return 0.5 + 0.5 * jnp.tanh(0.5 * v)


def _silu(v):
    h = 0.5 * v
    return h + h * jnp.tanh(h)


def _modulated_norm(x, g, shift, scale):
    ms = jnp.mean(x * x, axis=-1, keepdims=True)
    return (x * lax.rsqrt(ms + EPS) * g) * (1.0 + scale) + shift


def _dwconv(pad_ref, w_ref, start, nrows, ksize):
    left = (ksize - 1) // 2
    acc = None
    for j in range(ksize):
        lo = PAD + start + j - left
        term = w_ref[j:j + 1, :] * pad_ref[lo:lo + nrows, :]
        acc = term if acc is None else acc + term
    return acc


def _fill_padded(pad_ref, rows, seq):
    width = pad_ref.shape[1]
    pad_ref[0:PAD, :] = jnp.zeros((PAD, width), F32)
    pad_ref[PAD + seq:2 * PAD + seq, :] = jnp.zeros((PAD, width), F32)
    pad_ref[PAD:PAD + seq, :] = rows


def _ada_kernel(c_ref, w_ref, b_ref, o_ref):
    o_ref[0] = _dot_2x(_silu(c_ref[...]), w_ref[0]) + b_ref[0]


def _ada(cc, ada_w, ada_b):
    depth, d, n = ada_w.shape
    tn = 1536
    return pl.pallas_call(
        _ada_kernel,
        grid=(depth, n // tn),
        in_specs=[pl.BlockSpec(cc.shape, lambda l, j: (0, 0)),
                  pl.BlockSpec((1, d, tn), lambda l, j: (l, 0, j)),
                  pl.BlockSpec((1, 1, tn), lambda l, j: (l, 0, j))],
        out_specs=pl.BlockSpec((1, cc.shape[0], tn), lambda l, j: (l, 0, j)),
        out_shape=jax.ShapeDtypeStruct((depth, cc.shape[0], n), F32),
        compiler_params=_cparams(("parallel", "parallel")),
        name="ada",
    )(cc, ada_w, ada_b.reshape(depth, 1, n))


def _inproj_seq_kernel(x_ref, sh_ref, sc_ref, g_ref, wa_ref, wm_ref, zh_ref, zo_ref, zm_ref, zg_ref, zr_ref):
    b = pl.program_id(1)
    h = _modulated_norm(x_ref[0], g_ref[...], sh_ref[...], sc_ref[...]).astype(BF16)
    za = _dot(h, wa_ref[...])
    zh_ref[0] = za[:, :HY_COLS].astype(BF16)
    zo_ref[0] = za[:, HY_COLS + RG_WIDTH:]
    zm = _dot(h, wm_ref[...])
    zm_ref[0] = zm[:, :ZM_MAIN].astype(BF16)
    zg_ref[0] = zm[:, ZM_MAIN:]
    zr_ref[:, pl.ds(b, 1), :] = za[:, HY_COLS:HY_COLS + RG_WIDTH].reshape(za.shape[0], 1, RG_WIDTH)


def _inproj_seq(x, mods, prm, l, row, tm):
    bsz, seq, d = x.shape
    return pl.pallas_call(
        _inproj_seq_kernel,
        grid=(seq // tm, bsz),
        in_specs=[pl.BlockSpec((1, tm, d), lambda i, b: (b, i, 0)),
                  _mod_spec(l, 0, row), _mod_spec(l, 1, row),
                  _layer_spec(l, (1, d)),
                  _layer_spec(l, (d, ZA_COLS), single=True), _layer_spec(l, (d, ZM_COLS), single=True)],
        out_specs=[pl.BlockSpec((1, tm, HY_COLS), lambda i, b: (b, i, 0)),
                   pl.BlockSpec((1, tm, ZO_COLS), lambda i, b: (b, i, 0)),
                   pl.BlockSpec((1, tm, ZM_MAIN), lambda i, b: (b, i, 0)),
                   pl.BlockSpec((1, tm, ZM_COLS - ZM_MAIN), lambda i, b: (b, i, 0)),
                   pl.BlockSpec((tm, bsz, RG_WIDTH), lambda i, b: (i, 0, 0))],
        out_shape=[jax.ShapeDtypeStruct((bsz, seq, HY_COLS), BF16),
                   jax.ShapeDtypeStruct((bsz, seq, ZO_COLS), F32),
                   jax.ShapeDtypeStruct((bsz, seq, ZM_MAIN), BF16),
                   jax.ShapeDtypeStruct((bsz, seq, ZM_COLS - ZM_MAIN), F32),
                   jax.ShapeDtypeStruct((seq, bsz, RG_WIDTH), F32)],
        compiler_params=_cparams(("parallel", "arbitrary")),
        name="inproj_seq",
    )(x, mods, mods, prm['g1'], prm['w_a'], prm['w_m'])


def _inproj_grid_kernel(x_ref, sh_ref, sc_ref, g_ref, wa_ref, wm_ref, perm_ref, zh_ref, zo_ref, zm_ref,
                        zg_ref, zr_ref, *, rows, ncol):
    b = pl.program_id(1)
    d = g_ref.shape[1]
    x = x_ref[0].reshape(rows * ncol, d)
    h = _modulated_norm(x, g_ref[...], sh_ref[...], sc_ref[...]).astype(BF16)
    za = _dot(h, wa_ref[...])
    zh_ref[0] = za[:, :HY_COLS].astype(BF16).reshape(rows, ncol, HY_COLS)
    zo_ref[0] = za[:, HY_COLS + RG_WIDTH:].reshape(rows, ncol, ZO_COLS)
    zr_ref[:, :, pl.ds(b, 1), :] = za[:, HY_COLS:HY_COLS + RG_WIDTH].reshape(rows, ncol, 1, RG_WIDTH)
    hp = _dot(perm_ref[...], h).astype(BF16)
    zm = _dot(hp, wm_ref[...])
    zm_ref[0] = zm[:, :ZM_MAIN].astype(BF16)
    zg_ref[0] = zm[:, ZM_MAIN:]


def _inproj_grid(x, mods, prm, l, ncol):
    bsz, seq, d = x.shape
    rows = seq // GRID_W
    tm = rows * ncol
    src = np.arange(tm)
    r, c = src // ncol, src % ncol
    perm = np.zeros((tm, tm), np.float32)
    perm[c * rows + r, src] = 1.0
    zh, zo, zm, zg, zr = pl.pallas_call(
        functools.partial(_inproj_grid_kernel, rows=rows, ncol=ncol),
        grid=(GRID_W // ncol, bsz),
        in_specs=[pl.BlockSpec((1, rows, ncol, d), lambda i, b: (b, 0, i, 0)),
                  _mod_spec(l, 0, None), _mod_spec(l, 1, None),
                  _layer_spec(l, (1, d)),
                  _layer_spec(l, (d, ZA_COLS), single=True), _layer_spec(l, (d, ZM_COLS), single=True),
                  _const_spec((tm, tm))],
        out_specs=[pl.BlockSpec((1, rows, ncol, HY_COLS), lambda i, b: (b, 0, i, 0)),
                   pl.BlockSpec((1, rows, ncol, ZO_COLS), lambda i, b: (b, 0, i, 0)),
                   pl.BlockSpec((1, tm, ZM_MAIN), lambda i, b: (b, i, 0)),
                   pl.BlockSpec((1, tm, ZM_COLS - ZM_MAIN), lambda i, b: (b, i, 0)),
                   pl.BlockSpec((rows, ncol, bsz, RG_WIDTH), lambda i, b: (0, i, 0, 0))],
        out_shape=[jax.ShapeDtypeStruct((bsz, rows, GRID_W, HY_COLS), BF16),
                   jax.ShapeDtypeStruct((bsz, rows, GRID_W, ZO_COLS), F32),
                   jax.ShapeDtypeStruct((bsz, seq, ZM_MAIN), BF16),
                   jax.ShapeDtypeStruct((bsz, seq, ZM_COLS - ZM_MAIN), F32),
                   jax.ShapeDtypeStruct((rows, GRID_W, bsz, RG_WIDTH), F32)],
        compiler_params=_cparams(("parallel", "arbitrary")),
        name="inproj_grid",
    )(x.reshape(bsz, rows, GRID_W, d), mods, mods, prm['g1'], prm['w_a'], prm['w_m'],
      jnp.asarray(perm, dtype=BF16))
    return (zh.reshape(bsz, seq, HY_COLS), zo.reshape(bsz, seq, ZO_COLS), zm, zg,
            zr.reshape(seq, bsz, RG_WIDTH))


def _dft_tables(seq):
    n = 2 * seq
    k = np.arange(seq, dtype=np.int64)[:, None]
    s = np.arange(seq, dtype=np.int64)[None, :]
    ang = (2.0 * np.pi / n) * ((k * s) % n).astype(np.float64)
    top = np.cos(ang)
    bot = -np.sin(ang)
    bot[0, :] = np.where(np.arange(seq) % 2 == 0, 1.0, -1.0)
    fwd = np.concatenate([top, bot], axis=0)
    scale = np.full((n, 1), 2.0 / n)
    scale[0, 0] = 1.0 / n
    scale[seq, 0] = 1.0 / n
    inv = (fwd * scale).T
    return jnp.asarray(fwd.astype(np.float32)), jnp.asarray(inv.astype(np.float32))


def _hy_feature_tables(seq):
    pos = np.arange(seq, dtype=np.float64)[:, None]
    t = pos / max(seq - 1, 1)
    bands = (HY_POS_DIM - 1) // 2
    fr = np.linspace(1e-4, bands - 1, bands).astype(np.float32).astype(np.float64)[None]
    ang = (2.0 * math.pi / seq) * fr * pos
    feats = np.concatenate([t, np.cos(ang), -np.sin(ang)], axis=-1)
    feats = np.pad(feats, ((0, 0), (0, LANES - HY_POS_DIM))).T
    max_decay = math.log(HY_TARGET) / HY_FAST_DECAY
    min_decay = math.log(HY_TARGET) / HY_SLOW_DECAY
    deltas = np.linspace(min_decay, max_decay, HY_WIDTH).astype(np.float32).astype(np.float64)
    decay = np.exp(-t * np.abs(deltas)[None, :])
    return jnp.asarray(feats, dtype=F32), jnp.asarray(decay, dtype=F32)


def _hy_filter_kernel(feats_ref, w1_ref, b1_ref, w2_ref, b2_ref, w3_ref, b3_ref, fq_ref,
                      decay_ref, wf_ref, f_ref, h_scr, *, seq, tr):
    i = pl.program_id(0)
    depth = w1_ref.shape[0]
    fw = 2 * HY_WIDTH

    @pl.when(i == 0)
    def _():
        dec = decay_ref[...]
        for l in range(depth):
            h = jnp.sin(fq_ref[l, 0] * (_dot_f32(w1_ref[l], feats_ref[...]) + b1_ref[l]))
            h = jnp.sin(fq_ref[l, 1] * (_dot_f32(w2_ref[l], h) + b2_ref[l]))
            h = _dot_f32(w3_ref[l], h).T + b3_ref[l]
            hf = h[:, :HY_WIDTH] * dec
            hb = h[:, HY_WIDTH:] * dec
            row = lax.broadcasted_iota(jnp.int32, hb.shape, 0)
            hb = jnp.where(row == 0, 0.0, hb)
            h_scr[:, l * fw:l * fw + HY_WIDTH] = hf.astype(BF16)
            h_scr[:, l * fw + HY_WIDTH:(l + 1) * fw] = hb.astype(BF16)

    g = _dot(wf_ref[...].astype(BF16), h_scr[...])
    row = i * tr + lax.broadcasted_iota(jnp.int32, (tr, HY_WIDTH), 0)
    for l in range(depth):
        gf = g[:, l * fw:l * fw + HY_WIDTH]
        gb = g[:, l * fw + HY_WIDTH:(l + 1) * fw]
        f_ref[l] = jnp.where(row <= seq, gf + gb, gf - gb)


def _hy_filter(seq, prm, fwd):
    feats, decay = _hy_feature_tables(seq)
    depth = prm['hy_w1'].shape[0]
    tr = min(seq, 512)
    full = lambda shp: pl.BlockSpec(shp, lambda i: (0,) * len(shp))
    names = ('hy_w1', 'hy_b1', 'hy_w2', 'hy_b2', 'hy_w3', 'hy_b3', 'hy_freq')
    return pl.pallas_call(
        functools.partial(_hy_filter_kernel, seq=seq, tr=tr),
        grid=(2 * seq // tr,),
        in_specs=[full((LANES, seq))] + [full(prm[k].shape) for k in names]
                 + [full((seq, HY_WIDTH)), pl.BlockSpec((tr, seq), lambda i: (i, 0))],
        out_specs=pl.BlockSpec((depth, tr, HY_WIDTH), lambda i: (0, i, 0)),
        out_shape=jax.ShapeDtypeStruct((depth, 2 * seq, HY_WIDTH), F32),
        scratch_shapes=[pltpu.VMEM((seq, depth * 2 * HY_WIDTH), BF16)],
        compiler_params=_cparams(("arbitrary",)),
        name="hy_filter",
    )(feats, *[prm[k] for k in names], decay, fwd)


def _hy_pre_kernel(z_ref, cw_ref, bias_ref, x0_ref, e_ref, u_ref, pad_ref, *, seq):
    _fill_padded(pad_ref, z_ref[0].astype(F32), seq)
    zc = _dwconv(pad_ref, cw_ref, 0, seq, HY_SHORT)
    x0 = zc[:, :HY_WIDTH]
    u = zc[:, HY_WIDTH:2 * HY_WIDTH] * zc[:, 2 * HY_WIDTH:]
    x0_ref[...] = x0
    e_ref[...] = x0 * (u * bias_ref[...])
    u_ref[...] = u.astype(BF16)


def _hy_fwd_kernel(u_ref, wt_ref, wb_ref, ft_ref, fb_ref, yt_ref, yb_ref, *, nb):
    ft, fb = ft_ref[...], fb_ref[...]
    first = pl.program_id(0) == 0
    row0 = jnp.logical_and(lax.broadcasted_iota(jnp.int32, ft.shape, 0) == 0, first)
    f_im = jnp.where(row0, 0.0, fb)
    f_re2 = jnp.where(row0, fb, ft)
    wt = wt_ref[...].astype(BF16)
    wb = wb_ref[...].astype(BF16)
    for b in range(nb):
        cols = slice(b * HY_WIDTH, (b + 1) * HY_WIDTH)
        ub = u_ref[:, cols]
        pr = _dot(wt, ub)
        pi = _dot(wb, ub)
        yt_ref[:, cols] = (pr * ft - pi * f_im).astype(BF16)
        yb_ref[:, cols] = (pr * f_im + pi * f_re2).astype(BF16)


def _hy_inv_kernel(yt_ref, yb_ref, wt_ref, wb_ref, x0_ref, e_ref, o_ref, *, nb):
    wt = wt_ref[...].astype(BF16)
    wb = wb_ref[...].astype(BF16)
    for b in range(nb):
        cols = slice(b * HY_WIDTH, (b + 1) * HY_WIDTH)
        y = _dot(wt, yt_ref[:, cols]) + _dot(wb, yb_ref[:, cols])
        o_ref[:, cols] = x0_ref[:, cols] * y + e_ref[:, cols]


def _hy_conv(z_a, prm, l, filt, fwd, inv):
    bsz, seq, _ = z_a.shape
    wide = bsz * HY_WIDTH
    bcol = pl.BlockSpec((seq, HY_WIDTH), lambda b: (0, b))
    x0, e, u = pl.pallas_call(
        functools.partial(_hy_pre_kernel, seq=seq),
        grid=(bsz,),
        in_specs=[pl.BlockSpec((1, seq, HY_COLS), lambda b: (b, 0, 0)),
                  _layer_spec(l, (HY_SHORT, HY_COLS)), _layer_spec(l, (1, HY_WIDTH))],
        out_specs=[bcol, bcol, bcol],
        out_shape=[jax.ShapeDtypeStruct((seq, wide), F32), jax.ShapeDtypeStruct((seq, wide), F32),
                   jax.ShapeDtypeStruct((seq, wide), BF16)],
        scratch_shapes=[pltpu.VMEM((seq + 2 * PAD, HY_COLS), F32)],
        compiler_params=_cparams(("parallel",)),
        name="hy_pre",
    )(z_a, prm['hy_conv'], prm['hy_bias'])

    tk = min(seq, 512)
    kt = seq // tk
    yt, yb = pl.pallas_call(
        functools.partial(_hy_fwd_kernel, nb=bsz),
        grid=(kt,),
        in_specs=[_const_spec((seq, wide)),
                  pl.BlockSpec((tk, seq), lambda i: (i, 0)),
                  pl.BlockSpec((tk, seq), lambda i: (kt + i, 0)),
                  pl.BlockSpec((None, tk, HY_WIDTH), lambda i: (l, i, 0)),
                  pl.BlockSpec((None, tk, HY_WIDTH), lambda i: (l, kt + i, 0))],
        out_specs=[pl.BlockSpec((tk, wide), lambda i: (i, 0)), pl.BlockSpec((tk, wide), lambda i: (i, 0))],
        out_shape=[jax.ShapeDtypeStruct((seq, wide), BF16), jax.ShapeDtypeStruct((seq, wide), BF16)],
        compiler_params=_cparams(("parallel",)),
        name="hy_fwd",
    )(u, fwd, fwd, filt, filt)

    tt = min(seq, 256)
    return pl.pallas_call(
        functools.partial(_hy_inv_kernel, nb=bsz),
        grid=(seq // tt,),
        in_specs=[_const_spec((seq, wide)), _const_spec((seq, wide)),
                  pl.BlockSpec((tt, seq), lambda i: (i, 0)),
                  pl.BlockSpec((tt, seq), lambda i: (i, 1)),
                  pl.BlockSpec((tt, wide), lambda i: (i, 0)),
                  pl.BlockSpec((tt, wide), lambda i: (i, 0))],
        out_specs=pl.BlockSpec((tt, wide), lambda i: (i, 0)),
        out_shape=jax.ShapeDtypeStruct((seq, wide), F32),
        compiler_params=_cparams(("parallel",)),
        name="hy_inv",
    )(yt, yb, inv, inv, x0, e)


def _running_max(v, reverse):
    row = lax.broadcasted_iota(jnp.int32, v.shape, 0)
    k = 1
    while k < ML_CHUNK:
        if reverse:
            sh = jnp.where(row < ML_CHUNK - k, pltpu.roll(v, ML_CHUNK - k, 0), -jnp.inf)
        else:
            sh = jnp.where(row >= k, pltpu.roll(v, k, 0), -jnp.inf)
        v = jnp.maximum(v, sh)
        k *= 2
    return v


def _ml_gate_kernel(x_ref, b_ref, low_ref, up_ref, o_ref):
    for d, tri_ref in ((0, low_ref), (1, up_ref)):
        li = x_ref[2 * d] + b_ref[2 * d]
        pre = x_ref[2 * d + 1] + b_ref[2 * d + 1]
        lf = jnp.minimum(pre, 0.0) - jnp.log(1.0 + jnp.exp(-jnp.abs(pre)))
        hi, mid, lo = _split3(lf)
        tri = tri_ref[...]
        cum = _dot(tri, hi) + _dot(tri, mid) + _dot(tri, lo)
        r = li - cum
        o_ref[3 * d] = r
        o_ref[3 * d + 1] = _running_max(r, reverse=(d == 1))
        o_ref[3 * d + 2] = cum
    o_ref[6] = jnp.zeros(o_ref.shape[1:], F32)
    o_ref[7] = jnp.zeros(o_ref.shape[1:], F32)


def _ml_gates(gates, gate_b):
    bsz, seq, _ = gates.shape
    n = seq // ML_CHUNK
    m = bsz * ML_HEADS * n
    x = gates.reshape(bsz, n, ML_CHUNK, 4, ML_HEADS).transpose(3, 2, 0, 4, 1).reshape(4, ML_CHUNK, m)
    bias = jnp.broadcast_to(gate_b.reshape(4, 1, 1, ML_HEADS, 1), (4, 1, bsz, ML_HEADS, n)).reshape(4, 1, m)
    r = np.arange(ML_CHUNK)
    low = jnp.asarray(r[:, None] >= r[None, :], dtype=BF16)
    up = jnp.asarray(r[:, None] <= r[None, :], dtype=BF16)
    full = lambda shp: pl.BlockSpec(shp, lambda: (0,) * len(shp))
    out = pl.pallas_call(
        _ml_gate_kernel,
        in_specs=[full((4, ML_CHUNK, m)), full((4, 1, m)), full((ML_CHUNK, ML_CHUNK)),
                  full((ML_CHUNK, ML_CHUNK))],
        out_specs=full((SUBLANES, ML_CHUNK, m)),
        out_shape=jax.ShapeDtypeStruct((SUBLANES, ML_CHUNK, m), F32),
        compiler_params=pltpu.CompilerParams(vmem_limit_bytes=VMEM_LIMIT),
        name="ml_gates",
    )(x, bias, low, up)
    return out.reshape(SUBLANES, ML_CHUNK, bsz, ML_HEADS, n).transpose(2, 3, 4, 0, 1)


def _ml_column(rows, j):
    if j:
        rows = pltpu.roll(rows, SUBLANES - j, 0)
    return rows.T[:, 0:1]


def _ml_state_step(d, hh, n, kt_scr, va_scr, row_ref, c_ref, m_ref, cseq_scr, mseq_scr):
    sl = pl.ds(pl.multiple_of(n * ML_CHUNK, ML_CHUNK), ML_CHUNK)
    dh = ML_HEAD_DIM
    kt = kt_scr[hh, n]
    va = va_scr[sl, hh * 2 * dh:(hh + 1) * 2 * dh]
    rows = row_ref[0, hh, n]
    r_r = rows[3 * d:3 * d + 1, :]
    last = ML_CHUNK - 1 if d == 0 else 0
    mx_all = rows[3 * d + 1:3 * d + 2, last:last + 1]
    b_all = rows[3 * d + 2:3 * d + 3, last:last + 1]
    c0 = c_ref[0, d, hh]
    m0v = m_ref[0, d, hh]
    m0 = m0v[:, 0:1]
    cseq_scr[d, hh, n] = c0.astype(BF16)
    mseq_scr[d, hh, n] = m0v
    w_r = jnp.exp(r_r - mx_all)
    upd = _dot(kt * w_r.astype(BF16), va)
    m_loc = b_all + mx_all
    m_new = jnp.maximum(b_all + m0, m_loc)
    a = jnp.exp(b_all + m0 - m_new)
    sc = jnp.exp(m_loc - m_new)
    c_ref[0, d, hh] = a * c0 + sc * upd
    m_ref[0, d, hh] = jnp.broadcast_to(m_new, (1, dh))


def _ml_output(hh, n, q_scr, kt_scr, va_scr, row_ref, cseq_scr, mseq_scr, h_ref):
    dh = ML_HEAD_DIM
    sl = pl.ds(pl.multiple_of(n * ML_CHUNK, ML_CHUNK), ML_CHUNK)
    q = q_scr[sl, hh * dh:(hh + 1) * dh]
    va = va_scr[sl, hh * 2 * dh:(hh + 1) * 2 * dh]
    qk = _dot(q, kt_scr[hh, n])
    tt = lax.broadcasted_iota(jnp.int32, (ML_CHUNK, ML_CHUNK), 0)
    ss = lax.broadcasted_iota(jnp.int32, (ML_CHUNK, ML_CHUNK), 1)
    rows = row_ref[0, hh, n]
    h = None
    for d in range(2):
        r_r = rows[3 * d:3 * d + 1, :]
        mx_c = _ml_column(rows, 3 * d + 1)
        b_c = _ml_column(rows, 3 * d + 2)
        m0 = mseq_scr[d, hh, n][:, 0:1]
        seen = (ss <= tt) if d == 0 else (ss >= tt)
        mu_c = jnp.maximum(m0, mx_c)
        mu = jnp.broadcast_to(mu_c, (ML_CHUNK, ML_CHUNK))
        floor = jnp.broadcast_to(jnp.exp(-(b_c + mu_c)), (ML_CHUNK, ML_CHUNK))
        e = jnp.where(seen, jnp.exp(r_r - mu), 0.0)
        wi = jnp.exp(m0 - mu).astype(BF16)
        s = (qk * e).astype(BF16)
        res = _dot(jnp.concatenate([s, wi * q], axis=1), jnp.concatenate([va, cseq_scr[d, hh, n]], axis=0))
        hd = res[:, :dh] / jnp.maximum(jnp.abs(res[:, dh:]), floor)
        h = hd if h is None else h + hd
    h_ref[0, sl, hh * dh:(hh + 1) * dh] = h


def _ml_kernel(zq_ref, zk_ref, zv_ref, cwq_ref, cwk_ref, row_ref, c0_ref, m0_ref,
               h_ref, c_ref, m_ref, pad_ref, q_scr, kt_scr, va_scr, cseq_scr, mseq_scr, *, seq):
    nchunk = seq // ML_CHUNK
    dh = ML_HEAD_DIM
    hp = ML_HEADS_PER_STEP
    _fill_padded(pad_ref, zq_ref[0].astype(F32), seq)
    q_scr[...] = _silu(_dwconv(pad_ref, cwq_ref, 0, seq, ML_CONV)).astype(BF16)
    pad_ref[PAD:PAD + seq, :] = zk_ref[0].astype(F32)
    for n in range(nchunk):
        rows = slice(n * ML_CHUNK, (n + 1) * ML_CHUNK)
        kk = _silu(_dwconv(pad_ref, cwk_ref, n * ML_CHUNK, ML_CHUNK, ML_CONV)) * (dh ** -0.5)
        for hh in range(hp):
            kt_scr[hh, n] = kk[:, hh * dh:(hh + 1) * dh].T.astype(BF16)
            va_scr[rows, hh * 2 * dh:hh * 2 * dh + dh] = zv_ref[0, rows, hh * dh:(hh + 1) * dh].astype(BF16)
            va_scr[rows, hh * 2 * dh + dh:(hh + 1) * 2 * dh] = jnp.ones((ML_CHUNK, dh), BF16)
    c_ref[...] = c0_ref[...]
    m_ref[...] = m0_ref[...]

    def state_body(i, carry):
        for hh in range(hp):
            for d, n in ((0, i), (1, nchunk - 1 - i)):
                _ml_state_step(d, hh, n, kt_scr, va_scr, row_ref, c_ref, m_ref, cseq_scr, mseq_scr)
        return carry

    lax.fori_loop(0, nchunk, state_body, 0, unroll=2)

    def out_body(n, carry):
        for hh in range(hp):
            _ml_output(hh, n, q_scr, kt_scr, va_scr, row_ref, cseq_scr, mseq_scr, h_ref)
        return carry

    lax.fori_loop(0, nchunk, out_body, 0, unroll=min(8, nchunk))


def _mlstm(z_ml, z_gate, prm, l, state):
    bsz, seq, _ = z_ml.shape
    nchunk = seq // ML_CHUNK
    row = _ml_gates(z_gate[..., :ML_GATE_COLS], prm['ml_gate_b'][l])
    dh, nh, hp = ML_HEAD_DIM, ML_HEADS, ML_HEADS_PER_STEP
    ng = nh // hp
    w = hp * dh
    zspec = lambda off: pl.BlockSpec((1, seq, w), lambda b, g: (b, 0, off + g))
    cspec = pl.BlockSpec((1, 2, hp, dh, 2 * dh), lambda b, g: (b, 0, g, 0, 0))
    mspec = pl.BlockSpec((1, 2, hp, 1, dh), lambda b, g: (b, 0, g, 0, 0))
    return pl.pallas_call(
        functools.partial(_ml_kernel, seq=seq),
        grid=(bsz, ng),
        in_specs=[zspec(0), zspec(ng), zspec(2 * ng),
                  pl.BlockSpec((None, ML_CONV, w), lambda b, g: (l, 0, g)),
                  pl.BlockSpec((None, ML_CONV, w), lambda b, g: (l, 0, ng + g)),
                  pl.BlockSpec((1, hp, nchunk, SUBLANES, ML_CHUNK), lambda b, g: (b, g, 0, 0, 0)),
                  cspec, mspec],
        out_specs=[pl.BlockSpec((1, seq, w), lambda b, g: (b, 0, g)), cspec, mspec],
        out_shape=[jax.ShapeDtypeStruct((bsz, seq, ML_WIDTH), F32),
                   jax.ShapeDtypeStruct((bsz, 2, nh, dh, 2 * dh), F32),
                   jax.ShapeDtypeStruct((bsz, 2, nh, 1, dh), F32)],
        scratch_shapes=[pltpu.VMEM((seq + 2 * PAD, w), F32),
                        pltpu.VMEM((seq, w), BF16),
                        pltpu.VMEM((hp, nchunk, dh, ML_CHUNK), BF16),
                        pltpu.VMEM((seq, 2 * w), BF16),
                        pltpu.VMEM((2, hp, nchunk, dh, 2 * dh), BF16),
                        pltpu.VMEM((2, hp, nchunk, 1, dh), F32)],
        compiler_params=_cparams(("parallel", "parallel")),
        name="mlstm",
    )(z_ml, z_ml, z_ml, prm['ml_conv'], prm['ml_conv'], row, *state)


RG_TILE = 256


def _rg_kernel(z_ref, cw_ref, wg_ref, bg_ref, lam_ref, h0_ref, of_ref, ob_ref, st_ref,
               zp_scr, a_scr, b_scr, h_scr, carry_scr, *, seq, tt):
    i = pl.program_id(0)
    nt = seq // tt
    nb = z_ref.shape[1]
    c = RG_WIDTH

    @pl.when(i == 0)
    def _():
        carry_scr[...] = h0_ref[...]

    lam = lam_ref[...]
    sp = jnp.maximum(-lam, 0.0) + jnp.log(1.0 + jnp.exp(-jnp.abs(lam)))
    coef = (-0.5 * RG_C * math.log2(math.e)) * sp
    left = (RG_CONV - 1) // 2
    right = RG_CONV - 1 - left
    for d, tile in ((0, i), (1, nt - 1 - i)):
        t0 = tile * tt
        zp_scr[left:left + tt] = z_ref[pl.ds(t0, tt)]
        for k in range(left):
            src = t0 - left + k
            zp_scr[k:k + 1] = z_ref[pl.ds(jnp.maximum(src, 0), 1)] * (src >= 0).astype(F32)
        for k in range(right):
            src = t0 + tt + k
            zp_scr[left + tt + k:left + tt + k + 1] = (
                z_ref[pl.ds(jnp.minimum(src, seq - 1), 1)] * (src < seq).astype(F32))
        xr = None
        for j in range(RG_CONV):
            term = cw_ref[j:j + 1, :] * zp_scr[j:j + tt]
            xr = term if xr is None else xr + term
        half_pre = (_dot(xr.reshape(tt * nb, c).astype(BF16), wg_ref[:, 2 * d * c:2 * (d + 1) * c])
                    + bg_ref[:, 2 * d * c:2 * (d + 1) * c])
        t_r = jnp.tanh(half_pre[:, :c])
        t_i = jnp.tanh(half_pre[:, c:])
        cd = coef[d:d + 1, :]
        a = jnp.exp2(cd + cd * t_r)
        y = 1.0 - a * a
        root = jnp.where(y > 0.0, y * lax.rsqrt(y), 0.0)
        half_x = 0.5 * xr.reshape(tt * nb, c)
        a_scr[d] = a.reshape(tt, nb, c)
        b_scr[d] = (root * (half_x + half_x * t_i)).reshape(tt, nb, c)

    def body(s, carry):
        hf, hb = carry
        hf = a_scr[0, s] * hf + b_scr[0, s]
        h_scr[0, s] = hf
        sb = tt - 1 - s
        hb = a_scr[1, sb] * hb + b_scr[1, sb]
        h_scr[1, sb] = hb
        return hf, hb

    hf, hb = lax.fori_loop(0, tt, body, (carry_scr[0], carry_scr[1]), unroll=8)
    carry_scr[0] = hf
    carry_scr[1] = hb
    st_ref[0] = hf
    st_ref[1] = hb
    for d, o_ref in ((0, of_ref), (1, ob_ref)):
        o_ref[...] = jnp.swapaxes(h_scr[d].reshape(tt // SUBLANES, SUBLANES, nb, c), 1, 2)


def _rg_gate_weights(rg_gate_w, rg_gate_b):
    depth = rg_gate_w.shape[0]
    eye = jnp.eye(RG_HEADS, dtype=rg_gate_w.dtype)
    full = jnp.einsum('ldghij,hk->ldghikj', 0.5 * rg_gate_w, eye)
    full = full.reshape(depth, 2, 2, RG_WIDTH, RG_WIDTH)
    wg = full.transpose(0, 3, 1, 2, 4).reshape(depth, RG_WIDTH, 4 * RG_WIDTH).astype(BF16)
    return wg, 0.5 * rg_gate_b.reshape(depth, 1, 4 * RG_WIDTH)


def _rglru(zr, prm, l, h0):
    seq, bsz, c = zr.shape
    tt = min(seq, RG_TILE)
    nt = seq // tt
    ng = tt // SUBLANES
    state_spec = pl.BlockSpec((2, bsz, c), lambda i: (0, 0, 0))
    out_sds = jax.ShapeDtypeStruct((seq // SUBLANES, bsz, SUBLANES, c), F32)
    return pl.pallas_call(
        functools.partial(_rg_kernel, seq=seq, tt=tt),
        grid=(nt,),
        in_specs=[_const_spec((seq, bsz, c)),
                  _layer_spec(l, (RG_CONV, c)), _layer_spec(l, (c, 4 * c)), _layer_spec(l, (1, 4 * c)),
                  _layer_spec(l, (2, c)), state_spec],
        out_specs=[pl.BlockSpec((ng, bsz, SUBLANES, c), lambda i: (i, 0, 0, 0)),
                   pl.BlockSpec((ng, bsz, SUBLANES, c), lambda i: (nt - 1 - i, 0, 0, 0)),
                   state_spec],
        out_shape=[out_sds, out_sds, jax.ShapeDtypeStruct((2, bsz, c), F32)],
        scratch_shapes=[pltpu.VMEM((tt + RG_CONV - 1, bsz, c), F32),
                        pltpu.VMEM((2, tt, bsz, c), F32), pltpu.VMEM((2, tt, bsz, c), F32),
                        pltpu.VMEM((2, tt, bsz, c), F32), pltpu.VMEM((2, bsz, c), F32)],
        compiler_params=_cparams(("arbitrary",)),
        name="rglru",
    )(zr, prm['rg_conv'], prm['rg_wg'], prm['rg_bg'], prm['rg_lambda'], h0)


def _group_mean_sq(y, group):
    assert 2 * group == LANES and y.shape[1] % LANES == 0
    sq = y * y
    outs = []
    for t in range(y.shape[1] // LANES):
        blk = sq[:, t * LANES:(t + 1) * LANES]
        low = lax.broadcasted_iota(jnp.int32, blk.shape, 1) < group
        s_low = jnp.sum(jnp.where(low, blk, 0.0), axis=-1, keepdims=True)
        s_high = jnp.sum(jnp.where(low, 0.0, blk), axis=-1, keepdims=True)
        outs.append(jnp.where(low, s_low, s_high) * (1.0 / group))
    return jnp.concatenate(outs, axis=-1)


def _mix_mlp_kernel(x_ref, yhy_ref, hm_ref, o_ref, hrf_ref, hrb_ref, gate_ref, g_ref, wo_ref,
                    gt_ref, sh_ref, sc_ref, gt2_ref, g2_ref, w1_ref, w2_ref, fg_ref, out_ref,
                    *, final, fc, hm_rows):
    g = g_ref[...]
    yhy = yhy_ref[...]
    yhy = yhy * lax.rsqrt(_group_mean_sq(yhy, HY_WIDTH // HY_GROUPS) + EPS) * g[:, :HY_WIDTH]
    parts = [yhy.astype(BF16)]
    if hm_rows:
        hm = jnp.concatenate([hm_ref[0, :, r, :] for r in range(hm_rows)], axis=0)
    else:
        hm = hm_ref[0]
    o_gate = _sigmoid(o_ref[0])
    for h in range(ML_HEADS):
        sl = slice(h * ML_HEAD_DIM, (h + 1) * ML_HEAD_DIM)
        hh = hm[:, sl]
        ms = jnp.mean(hh * hh, axis=-1, keepdims=True)
        yh = hh * lax.rsqrt(ms + EPS) * g[:, HY_WIDTH + h * ML_HEAD_DIM:HY_WIDTH + (h + 1) * ML_HEAD_DIM]
        parts.append((yh * o_gate[:, sl]).astype(BF16))
    gp = gate_ref[0]
    gelu = 0.5 * gp * (1.0 + jnp.tanh(math.sqrt(2.0 / math.pi) * (gp + 0.044715 * (gp * gp * gp))))
    hr = hrf_ref[...] + hrb_ref[...]
    yrg = hr.reshape(hr.shape[0] * SUBLANES, RG_WIDTH) * gelu
    yrg = yrg * lax.rsqrt(_group_mean_sq(yrg, RG_BLOCK) + EPS) * g[:, HY_WIDTH + ML_WIDTH:]
    parts.append(yrg.astype(BF16))
    y = jnp.concatenate(parts, axis=-1)
    x = x_ref[0] + gt_ref[...] * _dot(y, wo_ref[...])

    h = _modulated_norm(x, g2_ref[...], sh_ref[...], sc_ref[...]).astype(BF16)
    acc = None
    for j in range(D_FF // fc):
        a = jnp.maximum(_dot(h, w1_ref[:, j * fc:(j + 1) * fc]), 0.0)
        t = _dot((a * a).astype(BF16), w2_ref[j * fc:(j + 1) * fc, :])
        acc = t if acc is None else acc + t
    out = x + gt2_ref[...] * acc
    if final:
        ms = jnp.mean(out * out, axis=-1, keepdims=True)
        out = out * lax.rsqrt(ms + EPS) * fg_ref[...]
    out_ref[0] = out


def _mix_mlp(x, y_hy, hm, z_a, hr_f, hr_b, mods, prm, l, row, final_g, final, tm, hm_col_major):
    bsz, seq, d = x.shape
    tok = lambda w, j: pl.BlockSpec((1, tm, w), lambda i, b: (b, i, j))
    if hm_col_major:
        hm_rows = tm // GRID_W
        hm = hm.reshape(bsz, GRID_W, seq // GRID_W, ML_WIDTH)
        hm_spec = pl.BlockSpec((1, GRID_W, hm_rows, ML_WIDTH), lambda i, b: (b, 0, i, 0))
    else:
        hm_rows = 0
        hm_spec = tok(ML_WIDTH, 0)
    hr_spec = pl.BlockSpec((tm // SUBLANES, None, SUBLANES, RG_WIDTH), lambda i, b: (i, b, 0, 0))
    return pl.pallas_call(
        functools.partial(_mix_mlp_kernel, final=final, fc=1024, hm_rows=hm_rows),
        grid=(seq // tm, bsz),
        in_specs=[tok(d, 0),
                  pl.BlockSpec((tm, HY_WIDTH), lambda i, b: (i, b)),
                  hm_spec,
                  tok(ML_WIDTH, 0),
                  hr_spec, hr_spec,
                  tok(RG_WIDTH, 2),
                  _layer_spec(l, (1, d)),
                  _layer_spec(l, (d, d), single=True), _mod_spec(l, 2, row),
                  _mod_spec(l, 3, row), _mod_spec(l, 4, row), _mod_spec(l, 5, row),
                  _layer_spec(l, (1, d)),
                  _layer_spec(l, (d, D_FF), single=True), _layer_spec(l, (D_FF, d), single=True),
                  pl.BlockSpec((1, d), lambda i, b: (0, 0))],
        out_specs=tok(d, 0),
        out_shape=jax.ShapeDtypeStruct((bsz, seq, d), F32),
        compiler_params=_cparams(("parallel", "parallel")),
        name="mix_mlp",
    )(x, y_hy, hm, z_a, hr_f, hr_b, z_a, prm['mix_g'], prm['w_o'], mods, mods, mods, mods, prm['g2'],
      prm['w1'], prm['w2'], final_g)


def _split_w_in_kernel(w_ref, wa_ref, wm_ref):
    w = w_ref[0]
    rg = OFF_RG
    o0 = OFF_ML + 3 * ML_WIDTH
    g0 = OFF_ML + 4 * ML_WIDTH
    wa_ref[0] = jnp.concatenate([w[:, :HY_COLS], w[:, rg:rg + RG_WIDTH], w[:, o0:o0 + ML_WIDTH],
                                 w[:, rg + RG_WIDTH:rg + 2 * RG_WIDTH]], axis=1).astype(BF16)
    pad = jnp.zeros((w.shape[0], ZM_COLS - ZM_MAIN - ML_GATE_COLS), F32)
    wm_ref[0] = jnp.concatenate([w[:, OFF_ML:o0], w[:, g0:g0 + ML_GATE_COLS], pad], axis=1).astype(BF16)


def _split_w_in(w_in):
    depth, d, p_in = w_in.shape
    tr = 256
    return pl.pallas_call(
        _split_w_in_kernel,
        grid=(depth, d // tr),
        in_specs=[pl.BlockSpec((1, tr, p_in), lambda l, i: (l, i, 0))],
        out_specs=[pl.BlockSpec((1, tr, ZA_COLS), lambda l, i: (l, i, 0)),
                   pl.BlockSpec((1, tr, ZM_COLS), lambda l, i: (l, i, 0))],
        out_shape=[jax.ShapeDtypeStruct((depth, d, ZA_COLS), BF16),
                   jax.ShapeDtypeStruct((depth, d, ZM_COLS), BF16)],
        compiler_params=_cparams(("parallel", "parallel")),
        name="split_w_in",
    )(w_in)


def _prepare(norm1_g, norm2_g, w_in, hy_conv, hy_w1, hy_b1, hy_w2, hy_b2, hy_w3, hy_b3, hy_freq, hy_bias,
             ml_conv, ml_gate_b, rg_conv, rg_gate_w, rg_gate_b, rg_lambda, mix_norm_g, w_out, mlp_w1, mlp_w2):
    w_a, w_m = _split_w_in(w_in)
    rg_wg, rg_bg = _rg_gate_weights(rg_gate_w, rg_gate_b)
    return dict(
        g1=norm1_g[:, None, :], g2=norm2_g[:, None, :], mix_g=mix_norm_g[:, None, :],
        w_a=w_a, w_m=w_m, w_o=w_out.astype(BF16), w1=mlp_w1.astype(BF16), w2=mlp_w2.astype(BF16),
        hy_conv=hy_conv, hy_bias=hy_bias[:, None, :],
        hy_w1=jnp.swapaxes(jnp.pad(hy_w1, ((0, 0), (0, LANES - HY_POS_DIM), (0, 0))), 1, 2),
        hy_b1=hy_b1[:, :, None], hy_w2=jnp.swapaxes(hy_w2, 1, 2), hy_b2=hy_b2[:, :, None],
        hy_w3=jnp.swapaxes(hy_w3, 1, 2), hy_b3=hy_b3[:, None, :], hy_freq=hy_freq[:, :, :, None],
        ml_conv=ml_conv, ml_gate_b=ml_gate_b,
        rg_conv=rg_conv, rg_wg=rg_wg, rg_bg=rg_bg, rg_lambda=rg_lambda)


def kernel(x, c, ctx, c_ctx, ada_w, ada_b, norm1_g, norm2_g, w_in, hy_conv, hy_w1, hy_b1, hy_w2, hy_b2, hy_w3, hy_b3, hy_freq, hy_bias, ml_conv, ml_gate_b, rg_conv, rg_gate_w, rg_gate_b, rg_lambda, mix_norm_g, w_out, mlp_w1, mlp_w2, final_g):
    bsz, seq, d = x.shape
    clen = ctx.shape[1]
    prm = _prepare(norm1_g, norm2_g, w_in, hy_conv, hy_w1, hy_b1, hy_w2, hy_b2, hy_w3, hy_b3, hy_freq,
                   hy_bias, ml_conv, ml_gate_b, rg_conv, rg_gate_w, rg_gate_b, rg_lambda, mix_norm_g,
                   w_out, mlp_w1, mlp_w2)

    cc = jnp.concatenate([c, c_ctx[None], jnp.zeros((MOD_ROWS - bsz - 1, d), F32)], axis=0)
    mods = _ada(cc, ada_w, ada_b).reshape(DEPTH, MOD_ROWS, N_MOD, 1, d)
    ctx_row = bsz

    fwd_l, inv_l = _dft_tables(seq)
    fwd_c, inv_c = _dft_tables(clen)
    filt_l = _hy_filter(seq, prm, fwd_l)
    filt_c = _hy_filter(clen, prm, fwd_c)
    dh, nh = ML_HEAD_DIM, ML_HEADS
    ml_zero = (jnp.zeros((bsz, 2, nh, dh, 2 * dh), F32), jnp.zeros((bsz, 2, nh, 1, dh), F32))
    rg_zero = jnp.zeros((2, bsz, RG_WIDTH), F32)
    fg = final_g[None]

    ctx_s = ctx
    for l in range(DEPTH):
        need_ctx = l < DEPTH - 1
        zc_h, zc_a, zc_m, zc_g, zc_r = _inproj_seq(ctx_s, mods, prm, l, ctx_row, clen)
        hm_c, ml_c, ml_m = _mlstm(zc_m, zc_g, prm, l, ml_zero)
        hrf_c, hrb_c, rg_state = _rglru(zc_r, prm, l, rg_zero)

        z_h, z_a, z_m, z_g, z_r = _inproj_grid(x, mods, prm, l, 16)
        y_hy = _hy_conv(z_h, prm, l, filt_l, fwd_l, inv_l)
        hm, _, _ = _mlstm(z_m, z_g, prm, l, (ml_c, ml_m))
        hr_f, hr_b, _ = _rglru(z_r, prm, l, rg_state)
        x = _mix_mlp(x, y_hy, hm, z_a, hr_f, hr_b, mods, prm, l, None, fg, final=(l == DEPTH - 1), tm=512,
                     hm_col_major=True)

        if need_ctx:
            y_hy_c = _hy_conv(zc_h, prm, l, filt_c, fwd_c, inv_c)
            ctx_s = _mix_mlp(ctx_s, y_hy_c, hm_c, zc_a, hrf_c, hrb_c, mods, prm, l, ctx_row, fg, final=False,
                             tm=clen, hm_col_major=False)
    return x
```

```python
import functools
import math

import numpy as np
import jax
import jax.numpy as jnp
from jax import lax
from jax.experimental import pallas as pl
from jax.experimental.pallas import tpu as pltpu

F32 = jnp.float32
BF16 = jnp.bfloat16

D_MODEL = 1024
DEPTH = 2
GRID_W = 64
HY_WIDTH = 256
HY_GROUPS = 4
HY_SHORT = 3
HY_POS_DIM = 33
HY_FFN = 64
HY_FAST_DECAY = 0.3
HY_SLOW_DECAY = 1.5
HY_TARGET = 1e-2
ML_WIDTH = 512
ML_HEADS = 4
ML_HEAD_DIM = 128
ML_CONV = 4
ML_CHUNK = 128
RG_WIDTH = 256
RG_HEADS = 4
RG_BLOCK = 64
RG_CONV = 4
RG_C = 8.0
D_FF = 4096
N_MOD = 6
EPS = 1e-6
HY_COLS = 3 * HY_WIDTH
ML_GATE_COLS = 4 * ML_HEADS
ML_COLS = 4 * ML_WIDTH + ML_GATE_COLS
OFF_ML = HY_COLS
OFF_RG = HY_COLS + ML_COLS

LANES = 128
SUBLANES = 8
VMEM_LIMIT = 56 * 1024 * 1024

ZA_COLS = 1792
ZM_COLS = 1664
ZM_MAIN = 3 * ML_WIDTH
ZO_COLS = ML_WIDTH + RG_WIDTH
PAD = SUBLANES
MOD_ROWS = 2 * SUBLANES
ML_HEADS_PER_STEP = 4


def _cparams(sem):
    return pltpu.CompilerParams(dimension_semantics=sem, vmem_limit_bytes=VMEM_LIMIT)


def _const_spec(shape):
    nd = len(shape)
    return pl.BlockSpec(shape, lambda *_: (0,) * nd, pipeline_mode=pl.Buffered(1))


def _layer_spec(l, shape, single=False):
    nd = len(shape)
    mode = dict(pipeline_mode=pl.Buffered(1)) if single else {}
    return pl.BlockSpec((None,) + tuple(shape), lambda *_: (l,) + (0,) * nd, **mode)


def _mod_spec(l, idx, row):
    blk = (None, None, None, 1, D_MODEL)
    if row is None:
        return pl.BlockSpec(blk, lambda i, b: (l, b, idx, 0, 0))
    return pl.BlockSpec(blk, lambda i, b: (l, row, idx, 0, 0))


def _split3(v):
    hi = v.astype(BF16)
    r = v - hi.astype(F32)
    mid = r.astype(BF16)
    lo = (r - mid.astype(F32)).astype(BF16)
    return hi, mid, lo


def _dot(a, b):
    return jnp.dot(a, b, preferred_element_type=F32)


def _dot_f32(a, b):
    ah, am, al = _split3(a)
    bh, bm, bl = _split3(b)
    return (_dot(ah, bh) + (_dot(ah, bm) + _dot(am, bh))
            + (_dot(ah, bl) + _dot(am, bm) + _dot(al, bh)))


def _dot_2x(a, b):
    ah = a.astype(BF16)
    al = (a - ah.astype(F32)).astype(BF16)
    bh = b.astype(BF16)
    bl = (b - bh.astype(F32)).astype(BF16)
    return _dot(ah, bh) + (_dot(ah, bl) + _dot(al, bh))


def _sigmoid(v):
    return 0.5 + 0.5 * jnp.tanh(0.5 * v)


def _silu(v):
    h = 0.5 * v
    return h + h * jnp.tanh(h)


def _modulated_norm(x, g, shift, scale):
    ms = jnp.mean(x * x, axis=-1, keepdims=True)
    return (x * lax.rsqrt(ms + EPS) * g) * (1.0 + scale) + shift


def _dwconv(pad_ref, w_ref, start, nrows, ksize):
    left = (ksize - 1) // 2
    acc = None
    for j in range(ksize):
        lo = PAD + start + j - left
        term = w_ref[j:j + 1, :] * pad_ref[lo:lo + nrows, :]
        acc = term if acc is None else acc + term
    return acc


def _fill_padded(pad_ref, rows, seq):
    width = pad_ref.shape[1]
    pad_ref[0:PAD, :] = jnp.zeros((PAD, width), F32)
    pad_ref[PAD + seq:2 * PAD + seq, :] = jnp.zeros((PAD, width), F32)
    pad_ref[PAD:PAD + seq, :] = rows


def _ada_kernel(c_ref, w_ref, b_ref, o_ref):
    o_ref[0] = _dot_2x(_silu(c_ref[...]), w_ref[0]) + b_ref[0]


def _ada(cc, ada_w, ada_b):
    depth, d, n = ada_w.shape
    tn = 1536
    return pl.pallas_call(
        _ada_kernel,
        grid=(depth, n // tn),
        in_specs=[pl.BlockSpec(cc.shape, lambda l, j: (0, 0)),
                  pl.BlockSpec((1, d, tn), lambda l, j: (l, 0, j)),
                  pl.BlockSpec((1, 1, tn), lambda l, j: (l, 0, j))],
        out_specs=pl.BlockSpec((1, cc.shape[0], tn), lambda l, j: (l, 0, j)),
        out_shape=jax.ShapeDtypeStruct((depth, cc.shape[0], n), F32),
        compiler_params=_cparams(("parallel", "parallel")),
        name="ada",
    )(cc, ada_w, ada_b.reshape(depth, 1, n))


def _inproj_seq_kernel(x_ref, sh_ref, sc_ref, g_ref, wa_ref, wm_ref, zh_ref, zo_ref, zm_ref, zg_ref, zr_ref):
    b = pl.program_id(1)
    h = _modulated_norm(x_ref[0], g_ref[...], sh_ref[...], sc_ref[...]).astype(BF16)
    za = _dot(h, wa_ref[...])
    zh_ref[0] = za[:, :HY_COLS].astype(BF16)
    zo_ref[0] = za[:, HY_COLS + RG_WIDTH:]
    zm = _dot(h, wm_ref[...])
    zm_ref[0] = zm[:, :ZM_MAIN].astype(BF16)
    zg_ref[0] = zm[:, ZM_MAIN:]
    zr_ref[:, pl.ds(b, 1), :] = za[:, HY_COLS:HY_COLS + RG_WIDTH].reshape(za.shape[0], 1, RG_WIDTH)


def _inproj_seq(x, mods, prm, l, row, tm):
    bsz, seq, d = x.shape
    return pl.pallas_call(
        _inproj_seq_kernel,
        grid=(seq // tm, bsz),
        in_specs=[pl.BlockSpec((1, tm, d), lambda i, b: (b, i, 0)),
                  _mod_spec(l, 0, row), _mod_spec(l, 1, row),
                  _layer_spec(l, (1, d)),
                  _layer_spec(l, (d, ZA_COLS), single=True), _layer_spec(l, (d, ZM_COLS), single=True)],
        out_specs=[pl.BlockSpec((1, tm, HY_COLS), lambda i, b: (b, i, 0)),
                   pl.BlockSpec((1, tm, ZO_COLS), lambda i, b: (b, i, 0)),
                   pl.BlockSpec((1, tm, ZM_MAIN), lambda i, b: (b, i, 0)),
                   pl.BlockSpec((1, tm, ZM_COLS - ZM_MAIN), lambda i, b: (b, i, 0)),
                   pl.BlockSpec((tm, bsz, RG_WIDTH), lambda i, b: (i, 0, 0))],
        out_shape=[jax.ShapeDtypeStruct((bsz, seq, HY_COLS), BF16),
                   jax.ShapeDtypeStruct((bsz, seq, ZO_COLS), F32),
                   jax.ShapeDtypeStruct((bsz, seq, ZM_MAIN), BF16),
                   jax.ShapeDtypeStruct((bsz, seq, ZM_COLS - ZM_MAIN), F32),
                   jax.ShapeDtypeStruct((seq, bsz, RG_WIDTH), F32)],
        compiler_params=_cparams(("parallel", "arbitrary")),
        name="inproj_seq",
    )(x, mods, mods, prm['g1'], prm['w_a'], prm['w_m'])


def _inproj_grid_kernel(x_ref, sh_ref, sc_ref, g_ref, wa_ref, wm_ref, perm_ref, zh_ref, zo_ref, zm_ref,
                        zg_ref, zr_ref, *, rows, ncol):
    b = pl.program_id(1)
    d = g_ref.shape[1]
    x = x_ref[0].reshape(rows * ncol, d)
    h = _modulated_norm(x, g_ref[...], sh_ref[...], sc_ref[...]).astype(BF16)
    za = _dot(h, wa_ref[...])
    zh_ref[0] = za[:, :HY_COLS].astype(BF16).reshape(rows, ncol, HY_COLS)
    zo_ref[0] = za[:, HY_COLS + RG_WIDTH:].reshape(rows, ncol, ZO_COLS)
    zr_ref[:, :, pl.ds(b, 1), :] = za[:, HY_COLS:HY_COLS + RG_WIDTH].reshape(rows, ncol, 1, RG_WIDTH)
    hp = _dot(perm_ref[...], h).astype(BF16)
    zm = _dot(hp, wm_ref[...])
    zm_ref[0] = zm[:, :ZM_MAIN].astype(BF16)
    zg_ref[0] = zm[:, ZM_MAIN:]


def _inproj_grid(x, mods, prm, l, ncol):
    bsz, seq, d = x.shape
    rows = seq // GRID_W
    tm = rows * ncol
    src = np.arange(tm)
    r, c = src // ncol, src % ncol
    perm = np.zeros((tm, tm), np.float32)
    perm[c * rows + r, src] = 1.0
    zh, zo, zm, zg, zr = pl.pallas_call(
        functools.partial(_inproj_grid_kernel, rows=rows, ncol=ncol),
        grid=(GRID_W // ncol, bsz),
        in_specs=[pl.BlockSpec((1, rows, ncol, d), lambda i, b: (b, 0, i, 0)),
                  _mod_spec(l, 0, None), _mod_spec(l, 1, None),
                  _layer_spec(l, (1, d)),
                  _layer_spec(l, (d, ZA_COLS), single=True), _layer_spec(l, (d, ZM_COLS), single=True),
                  _const_spec((tm, tm))],
        out_specs=[pl.BlockSpec((1, rows, ncol, HY_COLS), lambda i, b: (b, 0, i, 0)),
                   pl.BlockSpec((1, rows, ncol, ZO_COLS), lambda i, b: (b, 0, i, 0)),
                   pl.BlockSpec((1, tm, ZM_MAIN), lambda i, b: (b, i, 0)),
                   pl.BlockSpec((1, tm, ZM_COLS - ZM_MAIN), lambda i, b: (b, i, 0)),
                   pl.BlockSpec((rows, ncol, bsz, RG_WIDTH), lambda i, b: (0, i, 0, 0))],
        out_shape=[jax.ShapeDtypeStruct((bsz, rows, GRID_W, HY_COLS), BF16),
                   jax.ShapeDtypeStruct((bsz, rows, GRID_W, ZO_COLS), F32),
                   jax.ShapeDtypeStruct((bsz, seq, ZM_MAIN), BF16),
                   jax.ShapeDtypeStruct((bsz, seq, ZM_COLS - ZM_MAIN), F32),
                   jax.ShapeDtypeStruct((rows, GRID_W, bsz, RG_WIDTH), F32)],
        compiler_params=_cparams(("parallel", "arbitrary")),
        name="inproj_grid",
    )(x.reshape(bsz, rows, GRID_W, d), mods, mods, prm['g1'], prm['w_a'], prm['w_m'],
      jnp.asarray(perm, dtype=BF16))
    return (zh.reshape(bsz, seq, HY_COLS), zo.reshape(bsz, seq, ZO_COLS), zm, zg,
            zr.reshape(seq, bsz, RG_WIDTH))


def _dft_tables(seq):
    n = 2 * seq
    k = np.arange(seq, dtype=np.int64)[:, None]
    s = np.arange(seq, dtype=np.int64)[None, :]
    ang = (2.0 * np.pi / n) * ((k * s) % n).astype(np.float64)
    top = np.cos(ang)
    bot = -np.sin(ang)
    bot[0, :] = np.where(np.arange(seq) % 2 == 0, 1.0, -1.0)
    fwd = np.concatenate([top, bot], axis=0)
    scale = np.full((n, 1), 2.0 / n)
    scale[0, 0] = 1.0 / n
    scale[seq, 0] = 1.0 / n
    inv = (fwd * scale).T
    return jnp.asarray(fwd.astype(np.float32)), jnp.asarray(inv.astype(np.float32))


def _hy_feature_tables(seq):
    pos = np.arange(seq, dtype=np.float64)[:, None]
    t = pos / max(seq - 1, 1)
    bands = (HY_POS_DIM - 1) // 2
    fr = np.linspace(1e-4, bands - 1, bands).astype(np.float32).astype(np.float64)[None]
    ang = (2.0 * math.pi / seq) * fr * pos
    feats = np.concatenate([t, np.cos(ang), -np.sin(ang)], axis=-1)
    feats = np.pad(feats, ((0, 0), (0, LANES - HY_POS_DIM))).T
    max_decay = math.log(HY_TARGET) / HY_FAST_DECAY
    min_decay = math.log(HY_TARGET) / HY_SLOW_DECAY
    deltas = np.linspace(min_decay, max_decay, HY_WIDTH).astype(np.float32).astype(np.float64)
    decay = np.exp(-t * np.abs(deltas)[None, :])
    return jnp.asarray(feats, dtype=F32), jnp.asarray(decay, dtype=F32)


def _hy_filter_kernel(feats_ref, w1_ref, b1_ref, w2_ref, b2_ref, w3_ref, b3_ref, fq_ref,
                      decay_ref, wf_ref, f_ref, h_scr, *, seq, tr):
    i = pl.program_id(0)
    depth = w1_ref.shape[0]
    fw = 2 * HY_WIDTH

    @pl.when(i == 0)
    def _():
        dec = decay_ref[...]
        for l in range(depth):
            h = jnp.sin(fq_ref[l, 0] * (_dot_f32(w1_ref[l], feats_ref[...]) + b1_ref[l]))
            h = jnp.sin(fq_ref[l, 1] * (_dot_f32(w2_ref[l], h) + b2_ref[l]))
            h = _dot_f32(w3_ref[l], h).T + b3_ref[l]
            hf = h[:, :HY_WIDTH] * dec
            hb = h[:, HY_WIDTH:] * dec
            row = lax.broadcasted_iota(jnp.int32, hb.shape, 0)
            hb = jnp.where(row == 0, 0.0, hb)
            h_scr[:, l * fw:l * fw + HY_WIDTH] = hf.astype(BF16)
            h_scr[:, l * fw + HY_WIDTH:(l + 1) * fw] = hb.astype(BF16)

    g = _dot(wf_ref[...].astype(BF16), h_scr[...])
    row = i * tr + lax.broadcasted_iota(jnp.int32, (tr, HY_WIDTH), 0)
    for l in range(depth):
        gf = g[:, l * fw:l * fw + HY_WIDTH]
        gb = g[:, l * fw + HY_WIDTH:(l + 1) * fw]
        f_ref[l] = jnp.where(row <= seq, gf + gb, gf - gb)


def _hy_filter(seq, prm, fwd):
    feats, decay = _hy_feature_tables(seq)
    depth = prm['hy_w1'].shape[0]
    tr = min(seq, 512)
    full = lambda shp: pl.BlockSpec(shp, lambda i: (0,) * len(shp))
    names = ('hy_w1', 'hy_b1', 'hy_w2', 'hy_b2', 'hy_w3', 'hy_b3', 'hy_freq')
    return pl.pallas_call(
        functools.partial(_hy_filter_kernel, seq=seq, tr=tr),
        grid=(2 * seq // tr,),
        in_specs=[full((LANES, seq))] + [full(prm[k].shape) for k in names]
                 + [full((seq, HY_WIDTH)), pl.BlockSpec((tr, seq), lambda i: (i, 0))],
        out_specs=pl.BlockSpec((depth, tr, HY_WIDTH), lambda i: (0, i, 0)),
        out_shape=jax.ShapeDtypeStruct((depth, 2 * seq, HY_WIDTH), F32),
        scratch_shapes=[pltpu.VMEM((seq, depth * 2 * HY_WIDTH), BF16)],
        compiler_params=_cparams(("arbitrary",)),
        name="hy_filter",
    )(feats, *[prm[k] for k in names], decay, fwd)


def _hy_pre_kernel(z_ref, cw_ref, bias_ref, x0_ref, e_ref, u_ref, pad_ref, *, seq):
    _fill_padded(pad_ref, z_ref[0].astype(F32), seq)
    zc = _dwconv(pad_ref, cw_ref, 0, seq, HY_SHORT)
    x0 = zc[:, :HY_WIDTH]
    u = zc[:, HY_WIDTH:2 * HY_WIDTH] * zc[:, 2 * HY_WIDTH:]
    x0_ref[...] = x0
    e_ref[...] = x0 * (u * bias_ref[...])
    u_ref[...] = u.astype(BF16)


def _hy_fwd_kernel(u_ref, wt_ref, wb_ref, ft_ref, fb_ref, yt_ref, yb_ref, *, nb):
    ft, fb = ft_ref[...], fb_ref[...]
    first = pl.program_id(0) == 0
    row0 = jnp.logical_and(lax.broadcasted_iota(jnp.int32, ft.shape, 0) == 0, first)
    f_im = jnp.where(row0, 0.0, fb)
    f_re2 = jnp.where(row0, fb, ft)
    wt = wt_ref[...].astype(BF16)
    wb = wb_ref[...].astype(BF16)
    for b in range(nb):
        cols = slice(b * HY_WIDTH, (b + 1) * HY_WIDTH)
        ub = u_ref[:, cols]
        pr = _dot(wt, ub)
        pi = _dot(wb, ub)
        yt_ref[:, cols] = (pr * ft - pi * f_im).astype(BF16)
        yb_ref[:, cols] = (pr * f_im + pi * f_re2).astype(BF16)


def _hy_inv_kernel(yt_ref, yb_ref, wt_ref, wb_ref, x0_ref, e_ref, o_ref, *, nb):
    wt = wt_ref[...].astype(BF16)
    wb = wb_ref[...].astype(BF16)
    for b in range(nb):
        cols = slice(b * HY_WIDTH, (b + 1) * HY_WIDTH)
        y = _dot(wt, yt_ref[:, cols]) + _dot(wb, yb_ref[:, cols])
        o_ref[:, cols] = x0_ref[:, cols] * y + e_ref[:, cols]


def _hy_conv(z_h, prm, l, filt, fwd, inv):
    bsz, seq, _ = z_h.shape
    wide = bsz * HY_WIDTH
    bcol = pl.BlockSpec((seq, HY_WIDTH), lambda b: (0, b))
    x0, e, u = pl.pallas_call(
        functools.partial(_hy_pre_kernel, seq=seq),
        grid=(bsz,),
        in_specs=[pl.BlockSpec((1, seq, HY_COLS), lambda b: (b, 0, 0)),
                  _layer_spec(l, (HY_SHORT, HY_COLS)), _layer_spec(l, (1, HY_WIDTH))],
        out_specs=[bcol, bcol, bcol],
        out_shape=[jax.ShapeDtypeStruct((seq, wide), F32), jax.ShapeDtypeStruct((seq, wide), F32),
                   jax.ShapeDtypeStruct((seq, wide), BF16)],
        scratch_shapes=[pltpu.VMEM((seq + 2 * PAD, HY_COLS), F32)],
        compiler_params=_cparams(("parallel",)),
        name="hy_pre",
    )(z_h, prm['hy_conv'], prm['hy_bias'])

    tk = min(seq, 512)
    kt = seq // tk
    yt, yb = pl.pallas_call(
        functools.partial(_hy_fwd_kernel, nb=bsz),
        grid=(kt,),
        in_specs=[_const_spec((seq, wide)),
                  pl.BlockSpec((tk, seq), lambda i: (i, 0)),
                  pl.BlockSpec((tk, seq), lambda i: (kt + i, 0)),
                  pl.BlockSpec((None, tk, HY_WIDTH), lambda i: (l, i, 0)),
                  pl.BlockSpec((None, tk, HY_WIDTH), lambda i: (l, kt + i, 0))],
        out_specs=[pl.BlockSpec((tk, wide), lambda i: (i, 0)), pl.BlockSpec((tk, wide), lambda i: (i, 0))],
        out_shape=[jax.ShapeDtypeStruct((seq, wide), BF16), jax.ShapeDtypeStruct((seq, wide), BF16)],
        compiler_params=_cparams(("parallel",)),
        name="hy_fwd",
    )(u, fwd, fwd, filt, filt)

    tt = min(seq, 256)
    return pl.pallas_call(
        functools.partial(_hy_inv_kernel, nb=bsz),
        grid=(seq // tt,),
        in_specs=[_const_spec((seq, wide)), _const_spec((seq, wide)),
                  pl.BlockSpec((tt, seq), lambda i: (i, 0)),
                  pl.BlockSpec((tt, seq), lambda i: (i, 1)),
                  pl.BlockSpec((tt, wide), lambda i: (i, 0)),
                  pl.BlockSpec((tt, wide), lambda i: (i, 0))],
        out_specs=pl.BlockSpec((tt, wide), lambda i: (i, 0)),
        out_shape=jax.ShapeDtypeStruct((seq, wide), F32),
        compiler_params=_cparams(("parallel",)),
        name="hy_inv",
    )(yt, yb, inv, inv, x0, e)


def _running_max(v, reverse):
    row = lax.broadcasted_iota(jnp.int32, v.shape, 0)
    k = 1
    while k < ML_CHUNK:
        if reverse:
            sh = jnp.where(row < ML_CHUNK - k, pltpu.roll(v, ML_CHUNK - k, 0), -jnp.inf)
        else:
            sh = jnp.where(row >= k, pltpu.roll(v, k, 0), -jnp.inf)
        v = jnp.maximum(v, sh)
        k *= 2
    return v


def _ml_gate_kernel(x_ref, b_ref, low_ref, up_ref, o_ref):
    for d, tri_ref in ((0, low_ref), (1, up_ref)):
        li = x_ref[2 * d] + b_ref[2 * d]
        pre = x_ref[2 * d + 1] + b_ref[2 * d + 1]
        lf = jnp.minimum(pre, 0.0) - jnp.log(1.0 + jnp.exp(-jnp.abs(pre)))
        hi, mid, lo = _split3(lf)
        tri = tri_ref[...]
        cum = _dot(tri, hi) + _dot(tri, mid) + _dot(tri, lo)
        r = li - cum
        o_ref[3 * d] = r
        o_ref[3 * d + 1] = _running_max(r, reverse=(d == 1))
        o_ref[3 * d + 2] = cum
    o_ref[6] = jnp.zeros(o_ref.shape[1:], F32)
    o_ref[7] = jnp.zeros(o_ref.shape[1:], F32)


def _ml_gates(gates, gate_b):
    bsz, seq, _ = gates.shape
    n = seq // ML_CHUNK
    m = bsz * ML_HEADS * n
    x = gates.reshape(bsz, n, ML_CHUNK, 4, ML_HEADS).transpose(3, 2, 0, 4, 1).reshape(4, ML_CHUNK, m)
    bias = jnp.broadcast_to(gate_b.reshape(4, 1, 1, ML_HEADS, 1), (4, 1, bsz, ML_HEADS, n)).reshape(4, 1, m)
    r = np.arange(ML_CHUNK)
    low = jnp.asarray(r[:, None] >= r[None, :], dtype=BF16)
    up = jnp.asarray(r[:, None] <= r[None, :], dtype=BF16)
    full = lambda shp: pl.BlockSpec(shp, lambda: (0,) * len(shp))
    out = pl.pallas_call(
        _ml_gate_kernel,
        in_specs=[full((4, ML_CHUNK, m)), full((4, 1, m)), full((ML_CHUNK, ML_CHUNK)),
                  full((ML_CHUNK, ML_CHUNK))],
        out_specs=full((SUBLANES, ML_CHUNK, m)),
        out_shape=jax.ShapeDtypeStruct((SUBLANES, ML_CHUNK, m), F32),
        compiler_params=pltpu.CompilerParams(vmem_limit_bytes=VMEM_LIMIT),
        name="ml_gates",
    )(x, bias, low, up)
    return out.reshape(SUBLANES, ML_CHUNK, bsz, ML_HEADS, n).transpose(2, 3, 4, 0, 1)


def _ml_column(rows, j):
    if j:
        rows = pltpu.roll(rows, SUBLANES - j, 0)
    return rows.T[:, 0:1]


def _ml_state_step(d, hh, n, kt_scr, va_scr, row_ref, c_ref, m_ref, cseq_scr, mseq_scr):
    sl = pl.ds(pl.multiple_of(n * ML_CHUNK, ML_CHUNK), ML_CHUNK)
    dh = ML_HEAD_DIM
    kt = kt_scr[hh, n]
    va = va_scr[sl, hh * 2 * dh:(hh + 1) * 2 * dh]
    rows = row_ref[0, hh, n]
    r_r = rows[3 * d:3 * d + 1, :]
    last = ML_CHUNK - 1 if d == 0 else 0
    mx_all = rows[3 * d + 1:3 * d + 2, last:last + 1]
    b_all = rows[3 * d + 2:3 * d + 3, last:last + 1]
    c0 = c_ref[0, d, hh]
    m0v = m_ref[0, d, hh]
    m0 = m0v[:, 0:1]
    cseq_scr[d, hh, n] = c0.astype(BF16)
    mseq_scr[d, hh, n] = m0v
    w_r = jnp.exp(r_r - mx_all)
    upd = _dot(kt * w_r.astype(BF16), va)
    m_loc = b_all + mx_all
    m_new = jnp.maximum(b_all + m0, m_loc)
    a = jnp.exp(b_all + m0 - m_new)
    sc = jnp.exp(m_loc - m_new)
    c_ref[0, d, hh] = a * c0 + sc * upd
    m_ref[0, d, hh] = jnp.broadcast_to(m_new, (1, dh))


def _ml_output(hh, n, q_scr, kt_scr, va_scr, row_ref, cseq_scr, mseq_scr, h_ref):
    dh = ML_HEAD_DIM
    sl = pl.ds(pl.multiple_of(n * ML_CHUNK, ML_CHUNK), ML_CHUNK)
    q = q_scr[sl, hh * dh:(hh + 1) * dh]
    va = va_scr[sl, hh * 2 * dh:(hh + 1) * 2 * dh]
    qk = _dot(q, kt_scr[hh, n])
    tt = lax.broadcasted_iota(jnp.int32, (ML_CHUNK, ML_CHUNK), 0)
    ss = lax.broadcasted_iota(jnp.int32, (ML_CHUNK, ML_CHUNK), 1)
    rows = row_ref[0, hh, n]
    h = None
    for d in range(2):
        r_r = rows[3 * d:3 * d + 1, :]
        mx_c = _ml_column(rows, 3 * d + 1)
        b_c = _ml_column(rows, 3 * d + 2)
        m0 = mseq_scr[d, hh, n][:, 0:1]
        seen = (ss <= tt) if d == 0 else (ss >= tt)
        mu_c = jnp.maximum(m0, mx_c)
        mu = jnp.broadcast_to(mu_c, (ML_CHUNK, ML_CHUNK))
        floor = jnp.broadcast_to(jnp.exp(-(b_c + mu_c)), (ML_CHUNK, ML_CHUNK))
        e = jnp.where(seen, jnp.exp(r_r - mu), 0.0)
        wi = jnp.exp(m0 - mu).astype(BF16)
        s = (qk * e).astype(BF16)
        res = _dot(jnp.concatenate([s, wi * q], axis=1), jnp.concatenate([va, cseq_scr[d, hh, n]], axis=0))
        hd = res[:, :dh] / jnp.maximum(jnp.abs(res[:, dh:]), floor)
        h = hd if h is None else h + hd
    h_ref[0, sl, hh * dh:(hh + 1) * dh] = h


def _ml_kernel(zq_ref, zk_ref, zv_ref, cwq_ref, cwk_ref, row_ref, c0_ref, m0_ref,
               h_ref, c_ref, m_ref, pad_ref, q_scr, kt_scr, va_scr, cseq_scr, mseq_scr, *, seq):
    nchunk = seq // ML_CHUNK
    dh = ML_HEAD_DIM
    hp = ML_HEADS_PER_STEP
    _fill_padded(pad_ref, zq_ref[0].astype(F32), seq)
    q_scr[...] = _silu(_dwconv(pad_ref, cwq_ref, 0, seq, ML_CONV)).astype(BF16)
    pad_ref[PAD:PAD + seq, :] = zk_ref[0].astype(F32)
    for n in range(nchunk):
        rows = slice(n * ML_CHUNK, (n + 1) * ML_CHUNK)
        kk = _silu(_dwconv(pad_ref, cwk_ref, n * ML_CHUNK, ML_CHUNK, ML_CONV)) * (dh ** -0.5)
        for hh in range(hp):
            kt_scr[hh, n] = kk[:, hh * dh:(hh + 1) * dh].T.astype(BF16)
            va_scr[rows, hh * 2 * dh:hh * 2 * dh + dh] = zv_ref[0, rows, hh * dh:(hh + 1) * dh].astype(BF16)
            va_scr[rows, hh * 2 * dh + dh:(hh + 1) * 2 * dh] = jnp.ones((ML_CHUNK, dh), BF16)
    c_ref[...] = c0_ref[...]
    m_ref[...] = m0_ref[...]

    def state_body(i, carry):
        for hh in range(hp):
            for d, n in ((0, i), (1, nchunk - 1 - i)):
                _ml_state_step(d, hh, n, kt_scr, va_scr, row_ref, c_ref, m_ref, cseq_scr, mseq_scr)
        return carry

    lax.fori_loop(0, nchunk, state_body, 0, unroll=2)

    def out_body(n, carry):
        for hh in range(hp):
            _ml_output(hh, n, q_scr, kt_scr, va_scr, row_ref, cseq_scr, mseq_scr, h_ref)
        return carry

    lax.fori_loop(0, nchunk, out_body, 0, unroll=min(8, nchunk))


def _mlstm(z_ml, z_gate, prm, l, state):
    bsz, seq, _ = z_ml.shape
    nchunk = seq // ML_CHUNK
    row = _ml_gates(z_gate[..., :ML_GATE_COLS], prm['ml_gate_b'][l])
    dh, nh, hp = ML_HEAD_DIM, ML_HEADS, ML_HEADS_PER_STEP
    ng = nh // hp
    w = hp * dh
    zspec = lambda off: pl.BlockSpec((1, seq, w), lambda b, g: (b, 0, off + g))
    cspec = pl.BlockSpec((1, 2, hp, dh, 2 * dh), lambda b, g: (b, 0, g, 0, 0))
    mspec = pl.BlockSpec((1, 2, hp, 1, dh), lambda b, g: (b, 0, g, 0, 0))
    return pl.pallas_call(
        functools.partial(_ml_kernel, seq=seq),
        grid=(bsz, ng),
        in_specs=[zspec(0), zspec(ng), zspec(2 * ng),
                  pl.BlockSpec((None, ML_CONV, w), lambda b, g: (l, 0, g)),
                  pl.BlockSpec((None, ML_CONV, w), lambda b, g: (l, 0, ng + g)),
                  pl.BlockSpec((1, hp, nchunk, SUBLANES, ML_CHUNK), lambda b, g: (b, g, 0, 0, 0)),
                  cspec, mspec],
        out_specs=[pl.BlockSpec((1, seq, w), lambda b, g: (b, 0, g)), cspec, mspec],
        out_shape=[jax.ShapeDtypeStruct((bsz, seq, ML_WIDTH), F32),
                   jax.ShapeDtypeStruct((bsz, 2, nh, dh, 2 * dh), F32),
                   jax.ShapeDtypeStruct((bsz, 2, nh, 1, dh), F32)],
        scratch_shapes=[pltpu.VMEM((seq + 2 * PAD, w), F32),
                        pltpu.VMEM((seq, w), BF16),
                        pltpu.VMEM((hp, nchunk, dh, ML_CHUNK), BF16),
                        pltpu.VMEM((seq, 2 * w), BF16),
                        pltpu.VMEM((2, hp, nchunk, dh, 2 * dh), BF16),
                        pltpu.VMEM((2, hp, nchunk, 1, dh), F32)],
        compiler_params=_cparams(("parallel", "parallel")),
        name="mlstm",
    )(z_ml, z_ml, z_ml, prm['ml_conv'], prm['ml_conv'], row, *state)


RG_TILE = 256


def _rg_kernel(z_ref, cw_ref, wg_ref, bg_ref, lam_ref, h0_ref, of_ref, ob_ref, st_ref,
               zp_scr, a_scr, b_scr, h_scr, carry_scr, *, seq, tt):
    i = pl.program_id(0)
    nt = seq // tt
    nb = z_ref.shape[1]
    c = RG_WIDTH

    @pl.when(i == 0)
    def _():
        carry_scr[...] = h0_ref[...]

    lam = lam_ref[...]
    sp = jnp.maximum(-lam, 0.0) + jnp.log(1.0 + jnp.exp(-jnp.abs(lam)))
    coef = (-0.5 * RG_C * math.log2(math.e)) * sp
    left = (RG_CONV - 1) // 2
    right = RG_CONV - 1 - left
    for d, tile in ((0, i), (1, nt - 1 - i)):
        t0 = tile * tt
        zp_scr[left:left + tt] = z_ref[pl.ds(t0, tt)]
        for k in range(left):
            src = t0 - left + k
            zp_scr[k:k + 1] = z_ref[pl.ds(jnp.maximum(src, 0), 1)] * (src >= 0).astype(F32)
        for k in range(right):
            src = t0 + tt + k
            zp_scr[left + tt + k:left + tt + k + 1] = (
                z_ref[pl.ds(jnp.minimum(src, seq - 1), 1)] * (src < seq).astype(F32))
        xr = None
        for j in range(RG_CONV):
            term = cw_ref[j:j + 1, :] * zp_scr[j:j + tt]
            xr = term if xr is None else xr + term
        half_pre = (_dot(xr.reshape(tt * nb, c).astype(BF16), wg_ref[:, 2 * d * c:2 * (d + 1) * c])
                    + bg_ref[:, 2 * d * c:2 * (d + 1) * c])
        t_r = jnp.tanh(half_pre[:, :c])
        t_i = jnp.tanh(half_pre[:, c:])
        cd = coef[d:d + 1, :]
        a = jnp.exp2(cd + cd * t_r)
        y = 1.0 - a * a
        root = jnp.where(y > 0.0, y * lax.rsqrt(y), 0.0)
        half_x = 0.5 * xr.reshape(tt * nb, c)
        a_scr[d] = a.reshape(tt, nb, c)
        b_scr[d] = (root * (half_x + half_x * t_i)).reshape(tt, nb, c)

    def body(s, carry):
        hf, hb = carry
        hf = a_scr[0, s] * hf + b_scr[0, s]
        h_scr[0, s] = hf
        sb = tt - 1 - s
        hb = a_scr[1, sb] * hb + b_scr[1, sb]
        h_scr[1, sb] = hb
        return hf, hb

    hf, hb = lax.fori_loop(0, tt, body, (carry_scr[0], carry_scr[1]), unroll=8)
    carry_scr[0] = hf
    carry_scr[1] = hb
    st_ref[0] = hf
    st_ref[1] = hb
    for d, o_ref in ((0, of_ref), (1, ob_ref)):
        o_ref[...] = jnp.swapaxes(h_scr[d].reshape(tt // SUBLANES, SUBLANES, nb, c), 1, 2)


def _rg_gate_weights(rg_gate_w, rg_gate_b):
    depth = rg_gate_w.shape[0]
    eye = jnp.eye(RG_HEADS, dtype=rg_gate_w.dtype)
    full = jnp.einsum('ldghij,hk->ldghikj', 0.5 * rg_gate_w, eye)
    full = full.reshape(depth, 2, 2, RG_WIDTH, RG_WIDTH)
    wg = full.transpose(0, 3, 1, 2, 4).reshape(depth, RG_WIDTH, 4 * RG_WIDTH).astype(BF16)
    return wg, 0.5 * rg_gate_b.reshape(depth, 1, 4 * RG_WIDTH)


def _rglru(zr, prm, l, h0):
    seq, bsz, c = zr.shape
    tt = min(seq, RG_TILE)
    nt = seq // tt
    ng = tt // SUBLANES
    state_spec = pl.BlockSpec((2, bsz, c), lambda i: (0, 0, 0))
    out_sds = jax.ShapeDtypeStruct((seq // SUBLANES, bsz, SUBLANES, c), F32)
    return pl.pallas_call(
        functools.partial(_rg_kernel, seq=seq, tt=tt),
        grid=(nt,),
        in_specs=[_const_spec((seq, bsz, c)),
                  _layer_spec(l, (RG_CONV, c)), _layer_spec(l, (c, 4 * c)), _layer_spec(l, (1, 4 * c)),
                  _layer_spec(l, (2, c)), state_spec],
        out_specs=[pl.BlockSpec((ng, bsz, SUBLANES, c), lambda i: (i, 0, 0, 0)),
                   pl.BlockSpec((ng, bsz, SUBLANES, c), lambda i: (nt - 1 - i, 0, 0, 0)),
                   state_spec],
        out_shape=[out_sds, out_sds, jax.ShapeDtypeStruct((2, bsz, c), F32)],
        scratch_shapes=[pltpu.VMEM((tt + RG_CONV - 1, bsz, c), F32),
                        pltpu.VMEM((2, tt, bsz, c), F32), pltpu.VMEM((2, tt, bsz, c), F32),
                        pltpu.VMEM((2, tt, bsz, c), F32), pltpu.VMEM((2, bsz, c), F32)],
        compiler_params=_cparams(("arbitrary",)),
        name="rglru",
    )(zr, prm['rg_conv'], prm['rg_wg'], prm['rg_bg'], prm['rg_lambda'], h0)


def _group_mean_sq(y, group):
    assert 2 * group == LANES and y.shape[1] % LANES == 0
    sq = y * y
    outs = []
    for t in range(y.shape[1] // LANES):
        blk = sq[:, t * LANES:(t + 1) * LANES]
        low = lax.broadcasted_iota(jnp.int32, blk.shape, 1) < group
        s_low = jnp.sum(jnp.where(low, blk, 0.0), axis=-1, keepdims=True)
        s_high = jnp.sum(jnp.where(low, 0.0, blk), axis=-1, keepdims=True)
        outs.append(jnp.where(low, s_low, s_high) * (1.0 / group))
    return jnp.concatenate(outs, axis=-1)


def _mix_mlp_kernel(x_ref, yhy_ref, hm_ref, o_ref, hrf_ref, hrb_ref, gate_ref, g_ref, wo_ref,
                    gt_ref, sh_ref, sc_ref, gt2_ref, g2_ref, w1_ref, w2_ref, fg_ref, out_ref,
                    *, final, fc, hm_rows):
    g = g_ref[...]
    yhy = yhy_ref[...]
    yhy = yhy * lax.rsqrt(_group_mean_sq(yhy, HY_WIDTH // HY_GROUPS) + EPS) * g[:, :HY_WIDTH]
    parts = [yhy.astype(BF16)]
    if hm_rows:
        hm = jnp.concatenate([hm_ref[0, :, r, :] for r in range(hm_rows)], axis=0)
    else:
        hm = hm_ref[0]
    o_gate = _sigmoid(o_ref[0])
    for h in range(ML_HEADS):
        sl = slice(h * ML_HEAD_DIM, (h + 1) * ML_HEAD_DIM)
        hh = hm[:, sl]
        ms = jnp.mean(hh * hh, axis=-1, keepdims=True)
        yh = hh * lax.rsqrt(ms + EPS) * g[:, HY_WIDTH + h * ML_HEAD_DIM:HY_WIDTH + (h + 1) * ML_HEAD_DIM]
        parts.append((yh * o_gate[:, sl]).astype(BF16))
    gp = gate_ref[0]
    gelu = 0.5 * gp * (1.0 + jnp.tanh(math.sqrt(2.0 / math.pi) * (gp + 0.044715 * (gp * gp * gp))))
    hr = hrf_ref[...] + hrb_ref[...]
    yrg = hr.reshape(hr.shape[0] * SUBLANES, RG_WIDTH) * gelu
    yrg = yrg * lax.rsqrt(_group_mean_sq(yrg, RG_BLOCK) + EPS) * g[:, HY_WIDTH + ML_WIDTH:]
    parts.append(yrg.astype(BF16))
    y = jnp.concatenate(parts, axis=-1)
    x = x_ref[0] + gt_ref[...] * _dot(y, wo_ref[...])

    h = _modulated_norm(x, g2_ref[...], sh_ref[...], sc_ref[...]).astype(BF16)
    acc = None
    for j in range(D_FF // fc):
        a = jnp.maximum(_dot(h, w1_ref[:, j * fc:(j + 1) * fc]), 0.0)
        t = _dot((a * a).astype(BF16), w2_ref[j * fc:(j + 1) * fc, :])
        acc = t if acc is None else acc + t
    out = x + gt2_ref[...] * acc
    if final:
        ms = jnp.mean(out * out, axis=-1, keepdims=True)
        out = out * lax.rsqrt(ms + EPS) * fg_ref[...]
    out_ref[0] = out


def _mix_mlp(x, y_hy, hm, z_o, hr_f, hr_b, mods, prm, l, row, final_g, final, tm, hm_col_major):
    bsz, seq, d = x.shape
    tok = lambda w, j: pl.BlockSpec((1, tm, w), lambda i, b: (b, i, j))
    if hm_col_major:
        hm_rows = tm // GRID_W
        hm = hm.reshape(bsz, GRID_W, seq // GRID_W, ML_WIDTH)
        hm_spec = pl.BlockSpec((1, GRID_W, hm_rows, ML_WIDTH), lambda i, b: (b, 0, i, 0))
    else:
        hm_rows = 0
        hm_spec = tok(ML_WIDTH, 0)
    hr_spec = pl.BlockSpec((tm // SUBLANES, None, SUBLANES, RG_WIDTH), lambda i, b: (i, b, 0, 0))
    return pl.pallas_call(
        functools.partial(_mix_mlp_kernel, final=final, fc=1024, hm_rows=hm_rows),
        grid=(seq // tm, bsz),
        in_specs=[tok(d, 0),
                  pl.BlockSpec((tm, HY_WIDTH), lambda i, b: (i, b)),
                  hm_spec,
                  tok(ML_WIDTH, 0),
                  hr_spec, hr_spec,
                  tok(RG_WIDTH, 2),
                  _layer_spec(l, (1, d)),
                  _layer_spec(l, (d, d), single=True), _mod_spec(l, 2, row),
                  _mod_spec(l, 3, row), _mod_spec(l, 4, row), _mod_spec(l, 5, row),
                  _layer_spec(l, (1, d)),
                  _layer_spec(l, (d, D_FF), single=True), _layer_spec(l, (D_FF, d), single=True),
                  pl.BlockSpec((1, d), lambda i, b: (0, 0))],
        out_specs=tok(d, 0),
        out_shape=jax.ShapeDtypeStruct((bsz, seq, d), F32),
        compiler_params=_cparams(("parallel", "parallel")),
        name="mix_mlp",
    )(x, y_hy, hm, z_o, hr_f, hr_b, z_o, prm['mix_g'], prm['w_o'], mods, mods, mods, mods, prm['g2'],
      prm['w1'], prm['w2'], final_g)


def _split_w_in_kernel(w_ref, wa_ref, wm_ref):
    w = w_ref[0]
    rg = OFF_RG
    o0 = OFF_ML + 3 * ML_WIDTH
    g0 = OFF_ML + 4 * ML_WIDTH
    wa_ref[0] = jnp.concatenate([w[:, :HY_COLS], w[:, rg:rg + RG_WIDTH], w[:, o0:o0 + ML_WIDTH],
                                 w[:, rg + RG_WIDTH:rg + 2 * RG_WIDTH]], axis=1).astype(BF16)
    pad = jnp.zeros((w.shape[0], ZM_COLS - ZM_MAIN - ML_GATE_COLS), F32)
    wm_ref[0] = jnp.concatenate([w[:, OFF_ML:o0], w[:, g0:g0 + ML_GATE_COLS], pad], axis=1).astype(BF16)


def _split_w_in(w_in):
    depth, d, p_in = w_in.shape
    tr = 256
    return pl.pallas_call(
        _split_w_in_kernel,
        grid=(depth, d // tr),
        in_specs=[pl.BlockSpec((1, tr, p_in), lambda l, i: (l, i, 0))],
        out_specs=[pl.BlockSpec((1, tr, ZA_COLS), lambda l, i: (l, i, 0)),
                   pl.BlockSpec((1, tr, ZM_COLS), lambda l, i: (l, i, 0))],
        out_shape=[jax.ShapeDtypeStruct((depth, d, ZA_COLS), BF16),
                   jax.ShapeDtypeStruct((depth, d, ZM_COLS), BF16)],
        compiler_params=_cparams(("parallel", "parallel")),
        name="split_w_in",
    )(w_in)


def _prepare(norm1_g, norm2_g, w_in, hy_conv, hy_w1, hy_b1, hy_w2, hy_b2, hy_w3, hy_b3, hy_freq, hy_bias,
             ml_conv, ml_gate_b, rg_conv, rg_gate_w, rg_gate_b, rg_lambda, mix_norm_g, w_out, mlp_w1, mlp_w2):
    w_a, w_m = _split_w_in(w_in)
    rg_wg, rg_bg = _rg_gate_weights(rg_gate_w, rg_gate_b)
    return dict(
        g1=norm1_g[:, None, :], g2=norm2_g[:, None, :], mix_g=mix_norm_g[:, None, :],
        w_a=w_a, w_m=w_m, w_o=w_out.astype(BF16), w1=mlp_w1.astype(BF16), w2=mlp_w2.astype(BF16),
        hy_conv=hy_conv, hy_bias=hy_bias[:, None, :],
        hy_w1=jnp.swapaxes(jnp.pad(hy_w1, ((0, 0), (0, LANES - HY_POS_DIM), (0, 0))), 1, 2),
        hy_b1=hy_b1[:, :, None], hy_w2=jnp.swapaxes(hy_w2, 1, 2), hy_b2=hy_b2[:, :, None],
        hy_w3=jnp.swapaxes(hy_w3, 1, 2), hy_b3=hy_b3[:, None, :], hy_freq=hy_freq[:, :, :, None],
        ml_conv=ml_conv, ml_gate_b=ml_gate_b,
        rg_conv=rg_conv, rg_wg=rg_wg, rg_bg=rg_bg, rg_lambda=rg_lambda)


def kernel(x, c, ctx, c_ctx, ada_w, ada_b, norm1_g, norm2_g, w_in, hy_conv, hy_w1, hy_b1, hy_w2, hy_b2, hy_w3, hy_b3, hy_freq, hy_bias, ml_conv, ml_gate_b, rg_conv, rg_gate_w, rg_gate_b, rg_lambda, mix_norm_g, w_out, mlp_w1, mlp_w2, final_g):
    bsz, seq, d = x.shape
    clen = ctx.shape[1]
    prm = _prepare(norm1_g, norm2_g, w_in, hy_conv, hy_w1, hy_b1, hy_w2, hy_b2, hy_w3, hy_b3, hy_freq,
                   hy_bias, ml_conv, ml_gate_b, rg_conv, rg_gate_w, rg_gate_b, rg_lambda, mix_norm_g,
                   w_out, mlp_w1, mlp_w2)

    cc = jnp.concatenate([c, c_ctx[None], jnp.zeros((MOD_ROWS - bsz - 1, d), F32)], axis=0)
    mods = _ada(cc, ada_w, ada_b).reshape(DEPTH, MOD_ROWS, N_MOD, 1, d)
    ctx_row = bsz

    fwd_l, inv_l = _dft_tables(seq)
    fwd_c, inv_c = _dft_tables(clen)
    filt_l = _hy_filter(seq, prm, fwd_l)
    filt_c = _hy_filter(clen, prm, fwd_c)
    dh, nh = ML_HEAD_DIM, ML_HEADS
    ml_zero = (jnp.zeros((bsz, 2, nh, dh, 2 * dh), F32), jnp.zeros((bsz, 2, nh, 1, dh), F32))
    rg_zero = jnp.zeros((2, bsz, RG_WIDTH), F32)
    fg = final_g[None]

    ctx_s = ctx
    for l in range(DEPTH):
        need_ctx = l < DEPTH - 1
        zc_h, zc_o, zc_m, zc_g, zc_r = _inproj_seq(ctx_s, mods, prm, l, ctx_row, clen)
        hm_c, ml_c, ml_m = _mlstm(zc_m, zc_g, prm, l, ml_zero)
        hrf_c, hrb_c, rg_state = _rglru(zc_r, prm, l, rg_zero)

        z_h, z_o, z_m, z_g, z_r = _inproj_grid(x, mods, prm, l, 16)
        y_hy = _hy_conv(z_h, prm, l, filt_l, fwd_l, inv_l)
        hm, _, _ = _mlstm(z_m, z_g, prm, l, (ml_c, ml_m))
        hr_f, hr_b, _ = _rglru(z_r, prm, l, rg_state)
        x = _mix_mlp(x, y_hy, hm, z_o, hr_f, hr_b, mods, prm, l, None, fg, final=(l == DEPTH - 1), tm=512,
                     hm_col_major=True)

        if need_ctx:
            y_hy_c = _hy_conv(zc_h, prm, l, filt_c, fwd_c, inv_c)
            ctx_s = _mix_mlp(ctx_s, y_hy_c, hm_c, zc_o, hrf_c, hrb_c, mods, prm, l, ctx_row, fg, final=False,
                             tm=clen, hm_col_major=False)
    return x
```

```python
import functools
import math

import numpy as np
import jax
import jax.numpy as jnp
from jax import lax
from jax.experimental import pallas as pl
from jax.experimental.pallas import tpu as pltpu

F32 = jnp.float32
BF16 = jnp.bfloat16

D_MODEL = 1024
DEPTH = 2
GRID_W = 64
HY_WIDTH = 256
HY_GROUPS = 4
HY_SHORT = 3
HY_POS_DIM = 33
HY_FFN = 64
HY_FAST_DECAY = 0.3
HY_SLOW_DECAY = 1.5
HY_TARGET = 1e-2
ML_WIDTH = 512
ML_HEADS = 4
ML_HEAD_DIM = 128
ML_CONV = 4
ML_CHUNK = 128
RG_WIDTH = 256
RG_HEADS = 4
RG_BLOCK = 64
RG_CONV = 4
RG_C = 8.0
D_FF = 4096
N_MOD = 6
EPS = 1e-6
HY_COLS = 3 * HY_WIDTH
ML_GATE_COLS = 4 * ML_HEADS
ML_COLS = 4 * ML_WIDTH + ML_GATE_COLS
OFF_ML = HY_COLS
OFF_RG = HY_COLS + ML_COLS

LANES = 128
SUBLANES = 8
VMEM_LIMIT = 56 * 1024 * 1024

ZA_COLS = 1792
ZM_COLS = 1664
ZM_MAIN = 3 * ML_WIDTH
ZO_COLS = ML_WIDTH + RG_WIDTH
PAD = SUBLANES
MOD_ROWS = 2 * SUBLANES
ML_HEADS_PER_STEP = 4


def _cparams(sem):
    return pltpu.CompilerParams(dimension_semantics=sem, vmem_limit_bytes=VMEM_LIMIT)


def _const_spec(shape):
    nd = len(shape)
    return pl.BlockSpec(shape, lambda *_: (0,) * nd, pipeline_mode=pl.Buffered(1))


def _layer_spec(l, shape, single=False):
    nd = len(shape)
    mode = dict(pipeline_mode=pl.Buffered(1)) if single else {}
    return pl.BlockSpec((None,) + tuple(shape), lambda *_: (l,) + (0,) * nd, **mode)


def _mod_spec(l, idx, row):
    blk = (None, None, None, 1, D_MODEL)
    if row is None:
        return pl.BlockSpec(blk, lambda i, b: (l, b, idx, 0, 0))
    return pl.BlockSpec(blk, lambda i, b: (l, row, idx, 0, 0))


def _split3(v):
    hi = v.astype(BF16)
    r = v - hi.astype(F32)
    mid = r.astype(BF16)
    lo = (r - mid.astype(F32)).astype(BF16)
    return hi, mid, lo


def _dot(a, b):
    return jnp.dot(a, b, preferred_element_type=F32)


def _dot_f32(a, b):
    ah, am, al = _split3(a)
    bh, bm, bl = _split3(b)
    return (_dot(ah, bh) + (_dot(ah, bm) + _dot(am, bh))
            + (_dot(ah, bl) + _dot(am, bm) + _dot(al, bh)))


def _dot_2x(a, b):
    ah = a.astype(BF16)
    al = (a - ah.astype(F32)).astype(BF16)
    bh = b.astype(BF16)
    bl = (b - bh.astype(F32)).astype(BF16)
    return _dot(ah, bh) + (_dot(ah, bl) + _dot(al, bh))


def _sigmoid(v):
    return 0.5 + 0.5 * jnp.tanh(0.5 * v)


def _silu(v):
    h = 0.5 * v
    return h + h * jnp.tanh(h)


def _modulated_norm(x, g, shift, scale):
    ms = jnp.mean(x * x, axis=-1, keepdims=True)
    return (x * lax.rsqrt(ms + EPS) * g) * (1.0 + scale) + shift


def _dwconv(pad_ref, w_ref, start, nrows, ksize):
    left = (ksize - 1) // 2
    acc = None
    for j in range(ksize):
        lo = PAD + start + j - left
        term = w_ref[j:j + 1, :] * pad_ref[lo:lo + nrows, :]
        acc = term if acc is None else acc + term
    return acc


def _fill_padded(pad_ref, rows, seq):
    width = pad_ref.shape[1]
    pad_ref[0:PAD, :] = jnp.zeros((PAD, width), F32)
    pad_ref[PAD + seq:2 * PAD + seq, :] = jnp.zeros((PAD, width), F32)
    pad_ref[PAD:PAD + seq, :] = rows


def _ada_kernel(c_ref, w_ref, b_ref, o_ref):
    o_ref[0] = _dot_2x(_silu(c_ref[...]), w_ref[0]) + b_ref[0]


def _ada(cc, ada_w, ada_b):
    depth, d, n = ada_w.shape
    tn = 1536
    return pl.pallas_call(
        _ada_kernel,
        grid=(depth, n // tn),
        in_specs=[pl.BlockSpec(cc.shape, lambda l, j: (0, 0)),
                  pl.BlockSpec((1, d, tn), lambda l, j: (l, 0, j)),
                  pl.BlockSpec((1, 1, tn), lambda l, j: (l, 0, j))],
        out_specs=pl.BlockSpec((1, cc.shape[0], tn), lambda l, j: (l, 0, j)),
        out_shape=jax.ShapeDtypeStruct((depth, cc.shape[0], n), F32),
        compiler_params=_cparams(("parallel", "parallel")),
        name="ada",
    )(cc, ada_w, ada_b.reshape(depth, 1, n))


def _inproj_seq_kernel(x_ref, sh_ref, sc_ref, g_ref, wa_ref, wm_ref, zh_ref, zo_ref, zm_ref, zg_ref, zr_ref):
    b = pl.program_id(1)
    h = _modulated_norm(x_ref[0], g_ref[...], sh_ref[...], sc_ref[...]).astype(BF16)
    za = _dot(h, wa_ref[...])
    zh_ref[0] = za[:, :HY_COLS].astype(BF16)
    zo_ref[0] = za[:, HY_COLS + RG_WIDTH:]
    zm = _dot(h, wm_ref[...])
    zm_ref[0] = zm[:, :ZM_MAIN].astype(BF16)
    zg_ref[0] = zm[:, ZM_MAIN:]
    zr_ref[:, pl.ds(b, 1), :] = za[:, HY_COLS:HY_COLS + RG_WIDTH].reshape(za.shape[0], 1, RG_WIDTH)


def _inproj_seq(x, mods, prm, l, row, tm):
    bsz, seq, d = x.shape
    return pl.pallas_call(
        _inproj_seq_kernel,
        grid=(seq // tm, bsz),
        in_specs=[pl.BlockSpec((1, tm, d), lambda i, b: (b, i, 0)),
                  _mod_spec(l, 0, row), _mod_spec(l, 1, row),
                  _layer_spec(l, (1, d)),
                  _layer_spec(l, (d, ZA_COLS), single=True), _layer_spec(l, (d, ZM_COLS), single=True)],
        out_specs=[pl.BlockSpec((1, tm, HY_COLS), lambda i, b: (b, i, 0)),
                   pl.BlockSpec((1, tm, ZO_COLS), lambda i, b: (b, i, 0)),
                   pl.BlockSpec((1, tm, ZM_MAIN), lambda i, b: (b, i, 0)),
                   pl.BlockSpec((1, tm, ZM_COLS - ZM_MAIN), lambda i, b: (b, i, 0)),
                   pl.BlockSpec((tm, bsz, RG_WIDTH), lambda i, b: (i, 0, 0))],
        out_shape=[jax.ShapeDtypeStruct((bsz, seq, HY_COLS), BF16),
                   jax.ShapeDtypeStruct((bsz, seq, ZO_COLS), F32),
                   jax.ShapeDtypeStruct((bsz, seq, ZM_MAIN), BF16),
                   jax.ShapeDtypeStruct((bsz, seq, ZM_COLS - ZM_MAIN), F32),
                   jax.ShapeDtypeStruct((seq, bsz, RG_WIDTH), F32)],
        compiler_params=_cparams(("parallel", "arbitrary")),
        name="inproj_seq",
    )(x, mods, mods, prm['g1'], prm['w_a'], prm['w_m'])


def _inproj_grid_kernel(x_ref, sh_ref, sc_ref, g_ref, wa_ref, wm_ref, perm_ref, zh_ref, zo_ref, zm_ref,
                        zg_ref, zr_ref, *, rows, ncol):
    b = pl.program_id(1)
    d = g_ref.shape[1]
    x = x_ref[0].reshape(rows * ncol, d)
    h = _modulated_norm(x, g_ref[...], sh_ref[...], sc_ref[...]).astype(BF16)
    za = _dot(h, wa_ref[...])
    zh_ref[0] = za[:, :HY_COLS].astype(BF16).reshape(rows, ncol, HY_COLS)
    zo_ref[0] = za[:, HY_COLS + RG_WIDTH:].reshape(rows, ncol, ZO_COLS)
    zr_ref[:, :, pl.ds(b, 1), :] = za[:, HY_COLS:HY_COLS + RG_WIDTH].reshape(rows, ncol, 1, RG_WIDTH)
    hp = _dot(perm_ref[...], h).astype(BF16)
    zm = _dot(hp, wm_ref[...])
    zm_ref[0] = zm[:, :ZM_MAIN].astype(BF16)
    zg_ref[0] = zm[:, ZM_MAIN:]


def _inproj_grid(x, mods, prm, l, ncol):
    bsz, seq, d = x.shape
    rows = seq // GRID_W
    tm = rows * ncol
    src = np.arange(tm)
    r, c = src // ncol, src % ncol
    perm = np.zeros((tm, tm), np.float32)
    perm[c * rows + r, src] = 1.0
    zh, zo, zm, zg, zr = pl.pallas_call(
        functools.partial(_inproj_grid_kernel, rows=rows, ncol=ncol),
        grid=(GRID_W // ncol, bsz),
        in_specs=[pl.BlockSpec((1, rows, ncol, d), lambda i, b: (b, 0, i, 0)),
                  _mod_spec(l, 0, None), _mod_spec(l, 1, None),
                  _layer_spec(l, (1, d)),
                  _layer_spec(l, (d, ZA_COLS), single=True), _layer_spec(l, (d, ZM_COLS), single=True),
                  _const_spec((tm, tm))],
        out_specs=[pl.BlockSpec((1, rows, ncol, HY_COLS), lambda i, b: (b, 0, i, 0)),
                   pl.BlockSpec((1, rows, ncol, ZO_COLS), lambda i, b: (b, 0, i, 0)),
                   pl.BlockSpec((1, tm, ZM_MAIN), lambda i, b: (b, i, 0)),
                   pl.BlockSpec((1, tm, ZM_COLS - ZM_MAIN), lambda i, b: (b, i, 0)),
                   pl.BlockSpec((rows, ncol, bsz, RG_WIDTH), lambda i, b: (0, i, 0, 0))],
        out_shape=[jax.ShapeDtypeStruct((bsz, rows, GRID_W, HY_COLS), BF16),
                   jax.ShapeDtypeStruct((bsz, rows, GRID_W, ZO_COLS), F32),
                   jax.ShapeDtypeStruct((bsz, seq, ZM_MAIN), BF16),
                   jax.ShapeDtypeStruct((bsz, seq, ZM_COLS - ZM_MAIN), F32),
                   jax.ShapeDtypeStruct((rows, GRID_W, bsz, RG_WIDTH), F32)],
        compiler_params=_cparams(("parallel", "arbitrary")),
        name="inproj_grid",
    )(x.reshape(bsz, rows, GRID_W, d), mods, mods, prm['g1'], prm['w_a'], prm['w_m'],
      jnp.asarray(perm, dtype=BF16))
    return (zh.reshape(bsz, seq, HY_COLS), zo.reshape(bsz, seq, ZO_COLS), zm, zg,
            zr.reshape(seq, bsz, RG_WIDTH))


def _dft_tables(seq):
    n = 2 * seq
    k = np.arange(seq, dtype=np.int64)[:, None]
    s = np.arange(seq, dtype=np.int64)[None, :]
    ang = (2.0 * np.pi / n) * ((k * s) % n).astype(np.float64)
    top = np.cos(ang)
    bot = -np.sin(ang)
    bot[0, :] = np.where(np.arange(seq) % 2 == 0, 1.0, -1.0)
    fwd = np.concatenate([top, bot], axis=0)
    scale = np.full((n, 1), 2.0 / n)
    scale[0, 0] = 1.0 / n
    scale[seq, 0] = 1.0 / n
    inv = (fwd * scale).T
    return jnp.asarray(fwd.astype(np.float32)), jnp.asarray(inv.astype(np.float32))


def _hy_feature_tables(seq):
    pos = np.arange(seq, dtype=np.float64)[:, None]
    t = pos / max(seq - 1, 1)
    bands = (HY_POS_DIM - 1) // 2
    fr = np.linspace(1e-4, bands - 1, bands).astype(np.float32).astype(np.float64)[None]
    ang = (2.0 * math.pi / seq) * fr * pos
    feats = np.concatenate([t, np.cos(ang), -np.sin(ang)], axis=-1)
    feats = np.pad(feats, ((0, 0), (0, LANES - HY_POS_DIM))).T
    max_decay = math.log(HY_TARGET) / HY_FAST_DECAY
    min_decay = math.log(HY_TARGET) / HY_SLOW_DECAY
    deltas = np.linspace(min_decay, max_decay, HY_WIDTH).astype(np.float32).astype(np.float64)
    decay = np.exp(-t * np.abs(deltas)[None, :])
    return jnp.asarray(feats, dtype=F32), jnp.asarray(decay, dtype=F32)


def _hy_filter_kernel(feats_ref, w1_ref, b1_ref, w2_ref, b2_ref, w3_ref, b3_ref, fq_ref,
                      decay_ref, wf_ref, f_ref, h_scr, *, seq, tr):
    i = pl.program_id(0)
    depth = w1_ref.shape[0]
    fw = 2 * HY_WIDTH

    @pl.when(i == 0)
    def _():
        dec = decay_ref[...]
        for l in range(depth):
            h = jnp.sin(fq_ref[l, 0] * (_dot_f32(w1_ref[l], feats_ref[...]) + b1_ref[l]))
            h = jnp.sin(fq_ref[l, 1] * (_dot_f32(w2_ref[l], h) + b2_ref[l]))
            h = _dot_f32(w3_ref[l], h).T + b3_ref[l]
            hf = h[:, :HY_WIDTH] * dec
            hb = h[:, HY_WIDTH:] * dec
            row = lax.broadcasted_iota(jnp.int32, hb.shape, 0)
            hb = jnp.where(row == 0, 0.0, hb)
            h_scr[:, l * fw:l * fw + HY_WIDTH] = hf.astype(BF16)
            h_scr[:, l * fw + HY_WIDTH:(l + 1) * fw] = hb.astype(BF16)

    g = _dot(wf_ref[...].astype(BF16), h_scr[...])
    row = i * tr + lax.broadcasted_iota(jnp.int32, (tr, HY_WIDTH), 0)
    for l in range(depth):
        gf = g[:, l * fw:l * fw + HY_WIDTH]
        gb = g[:, l * fw + HY_WIDTH:(l + 1) * fw]
        f_ref[l] = jnp.where(row <= seq, gf + gb, gf - gb)


def _hy_filter(seq, prm, fwd):
    feats, decay = _hy_feature_tables(seq)
    depth = prm['hy_w1'].shape[0]
    tr = min(seq, 512)
    full = lambda shp: pl.BlockSpec(shp, lambda i: (0,) * len(shp))
    names = ('hy_w1', 'hy_b1', 'hy_w2', 'hy_b2', 'hy_w3', 'hy_b3', 'hy_freq')
    return pl.pallas_call(
        functools.partial(_hy_filter_kernel, seq=seq, tr=tr),
        grid=(2 * seq // tr,),
        in_specs=[full((LANES, seq))] + [full(prm[k].shape) for k in names]
                 + [full((seq, HY_WIDTH)), pl.BlockSpec((tr, seq), lambda i: (i, 0))],
        out_specs=pl.BlockSpec((depth, tr, HY_WIDTH), lambda i: (0, i, 0)),
        out_shape=jax.ShapeDtypeStruct((depth, 2 * seq, HY_WIDTH), F32),
        scratch_shapes=[pltpu.VMEM((seq, depth * 2 * HY_WIDTH), BF16)],
        compiler_params=_cparams(("arbitrary",)),
        name="hy_filter",
    )(feats, *[prm[k] for k in names], decay, fwd)


def _hy_pre_kernel(z_ref, cw_ref, bias_ref, x0_ref, e_ref, u_ref, pad_ref, *, seq):
    _fill_padded(pad_ref, z_ref[0].astype(F32), seq)
    zc = _dwconv(pad_ref, cw_ref, 0, seq, HY_SHORT)
    x0 = zc[:, :HY_WIDTH]
    u = zc[:, HY_WIDTH:2 * HY_WIDTH] * zc[:, 2 * HY_WIDTH:]
    x0_ref[...] = x0
    e_ref[...] = x0 * (u * bias_ref[...])
    u_ref[...] = u.astype(BF16)


def _hy_fwd_kernel(u_ref, wt_ref, wb_ref, ft_ref, fb_ref, yt_ref, yb_ref, *, nb):
    ft, fb = ft_ref[...], fb_ref[...]
    first = pl.program_id(0) == 0
    row0 = jnp.logical_and(lax.broadcasted_iota(jnp.int32, ft.shape, 0) == 0, first)
    f_im = jnp.where(row0, 0.0, fb)
    f_re2 = jnp.where(row0, fb, ft)
    wt = wt_ref[...].astype(BF16)
    wb = wb_ref[...].astype(BF16)
    for b in range(nb):
        cols = slice(b * HY_WIDTH, (b + 1) * HY_WIDTH)
        ub = u_ref[:, cols]
        pr = _dot(wt, ub)
        pi = _dot(wb, ub)
        yt_ref[:, cols] = (pr * ft - pi * f_im).astype(BF16)
        yb_ref[:, cols] = (pr * f_im + pi * f_re2).astype(BF16)


def _hy_inv_kernel(yt_ref, yb_ref, wt_ref, wb_ref, x0_ref, e_ref, o_ref, *, nb):
    wt = wt_ref[...].astype(BF16)
    wb = wb_ref[...].astype(BF16)
    for b in range(nb):
        cols = slice(b * HY_WIDTH, (b + 1) * HY_WIDTH)
        y = _dot(wt, yt_ref[:, cols]) + _dot(wb, yb_ref[:, cols])
        o_ref[:, cols] = x0_ref[:, cols] * y + e_ref[:, cols]


def _hy_conv(z_a, prm, l, filt, fwd, inv):
    bsz, seq, _ = z_a.shape
    wide = bsz * HY_WIDTH
    bcol = pl.BlockSpec((seq, HY_WIDTH), lambda b: (0, b))
    x0, e, u = pl.pallas_call(
        functools.partial(_hy_pre_kernel, seq=seq),
        grid=(bsz,),
        in_specs=[pl.BlockSpec((1, seq, HY_COLS), lambda b: (b, 0, 0)),
                  _layer_spec(l, (HY_SHORT, HY_COLS)), _layer_spec(l, (1, HY_WIDTH))],
        out_specs=[bcol, bcol, bcol],
        out_shape=[jax.ShapeDtypeStruct((seq, wide), F32), jax.ShapeDtypeStruct((seq, wide), F32),
                   jax.ShapeDtypeStruct((seq, wide), BF16)],
        scratch_shapes=[pltpu.VMEM((seq + 2 * PAD, HY_COLS), F32)],
        compiler_params=_cparams(("parallel",)),
        name="hy_pre",
    )(z_a, prm['hy_conv'], prm['hy_bias'])

    tk = min(seq, 512)
    kt = seq // tk
    yt, yb = pl.pallas_call(
        functools.partial(_hy_fwd_kernel, nb=bsz),
        grid=(kt,),
        in_specs=[_const_spec((seq, wide)),
                  pl.BlockSpec((tk, seq), lambda i: (i, 0)),
                  pl.BlockSpec((tk, seq), lambda i: (kt + i, 0)),
                  pl.BlockSpec((None, tk, HY_WIDTH), lambda i: (l, i, 0)),
                  pl.BlockSpec((None, tk, HY_WIDTH), lambda i: (l, kt + i, 0))],
        out_specs=[pl.BlockSpec((tk, wide), lambda i: (i, 0)), pl.BlockSpec((tk, wide), lambda i: (i, 0))],
        out_shape=[jax.ShapeDtypeStruct((seq, wide), BF16), jax.ShapeDtypeStruct((seq, wide), BF16)],
        compiler_params=_cparams(("parallel",)),
        name="hy_fwd",
    )(u, fwd, fwd, filt, filt)

    tt = min(seq, 256)
    return pl.pallas_call(
        functools.partial(_hy_inv_kernel, nb=bsz),
        grid=(seq // tt,),
        in_specs=[_const_spec((seq, wide)), _const_spec((seq, wide)),
                  pl.BlockSpec((tt, seq), lambda i: (i, 0)),
                  pl.BlockSpec((tt, seq), lambda i: (i, 1)),
                  pl.BlockSpec((tt, wide), lambda i: (i, 0)),
                  pl.BlockSpec((tt, wide), lambda i: (i, 0))],
        out_specs=pl.BlockSpec((tt, wide), lambda i: (i, 0)),
        out_shape=jax.ShapeDtypeStruct((seq, wide), F32),
        compiler_params=_cparams(("parallel",)),
        name="hy_inv",
    )(yt, yb, inv, inv, x0, e)


def _running_max(v, reverse):
    row = lax.broadcasted_iota(jnp.int32, v.shape, 0)
    k = 1
    while k < ML_CHUNK:
        if reverse:
            sh = jnp.where(row < ML_CHUNK - k, pltpu.roll(v, ML_CHUNK - k, 0), -jnp.inf)
        else:
            sh = jnp.where(row >= k, pltpu.roll(v, k, 0), -jnp.inf)
        v = jnp.maximum(v, sh)
        k *= 2
    return v


def _ml_gate_kernel(x_ref, b_ref, low_ref, up_ref, o_ref):
    for d, tri_ref in ((0, low_ref), (1, up_ref)):
        li = x_ref[2 * d] + b_ref[2 * d]
        pre = x_ref[2 * d + 1] + b_ref[2 * d + 1]
        lf = jnp.minimum(pre, 0.0) - jnp.log(1.0 + jnp.exp(-jnp.abs(pre)))
        hi, mid, lo = _split3(lf)
        tri = tri_ref[...]
        cum = _dot(tri, hi) + _dot(tri, mid) + _dot(tri, lo)
        r = li - cum
        o_ref[3 * d] = r
        o_ref[3 * d + 1] = _running_max(r, reverse=(d == 1))
        o_ref[3 * d + 2] = cum
    o_ref[6] = jnp.zeros(o_ref.shape[1:], F32)
    o_ref[7] = jnp.zeros(o_ref.shape[1:], F32)


def _ml_gates(gates, gate_b):
    bsz, seq, _ = gates.shape
    n = seq // ML_CHUNK
    m = bsz * ML_HEADS * n
    x = gates.reshape(bsz, n, ML_CHUNK, 4, ML_HEADS).transpose(3, 2, 0, 4, 1).reshape(4, ML_CHUNK, m)
    bias = jnp.broadcast_to(gate_b.reshape(4, 1, 1, ML_HEADS, 1), (4, 1, bsz, ML_HEADS, n)).reshape(4, 1, m)
    r = np.arange(ML_CHUNK)
    low = jnp.asarray(r[:, None] >= r[None, :], dtype=BF16)
    up = jnp.asarray(r[:, None] <= r[None, :], dtype=BF16)
    full = lambda shp: pl.BlockSpec(shp, lambda: (0,) * len(shp))
    out = pl.pallas_call(
        _ml_gate_kernel,
        in_specs=[full((4, ML_CHUNK, m)), full((4, 1, m)), full((ML_CHUNK, ML_CHUNK)),
                  full((ML_CHUNK, ML_CHUNK))],
        out_specs=full((SUBLANES, ML_CHUNK, m)),
        out_shape=jax.ShapeDtypeStruct((SUBLANES, ML_CHUNK, m), F32),
        compiler_params=pltpu.CompilerParams(vmem_limit_bytes=VMEM_LIMIT),
        name="ml_gates",
    )(x, bias, low, up)
    return out.reshape(SUBLANES, ML_CHUNK, bsz, ML_HEADS, n).transpose(2, 3, 4, 0, 1)


def _ml_column(rows, j):
    if j:
        rows = pltpu.roll(rows, SUBLANES - j, 0)
    return rows.T[:, 0:1]


def _ml_state_step(d, hh, n, kt_scr, va_scr, row_ref, c_ref, m_ref, cseq_scr, mseq_scr):
    sl = pl.ds(pl.multiple_of(n * ML_CHUNK, ML_CHUNK), ML_CHUNK)
    dh = ML_HEAD_DIM
    kt = kt_scr[hh, n]
    va = va_scr[sl, hh * 2 * dh:(hh + 1) * 2 * dh]
    rows = row_ref[0, hh, n]
    r_r = rows[3 * d:3 * d + 1, :]
    last = ML_CHUNK - 1 if d == 0 else 0
    mx_all = rows[3 * d + 1:3 * d + 2, last:last + 1]
    b_all = rows[3 * d + 2:3 * d + 3, last:last + 1]
    c0 = c_ref[0, d, hh]
    m0v = m_ref[0, d, hh]
    m0 = m0v[:, 0:1]
    cseq_scr[d, hh, n] = c0.astype(BF16)
    mseq_scr[d, hh, n] = m0v
    w_r = jnp.exp(r_r - mx_all)
    upd = _dot(kt * w_r.astype(BF16), va)
    m_loc = b_all + mx_all
    m_new = jnp.maximum(b_all + m0, m_loc)
    a = jnp.exp(b_all + m0 - m_new)
    sc = jnp.exp(m_loc - m_new)
    c_ref[0, d, hh] = a * c0 + sc * upd
    m_ref[0, d, hh] = jnp.broadcast_to(m_new, (1, dh))


def _ml_output(hh, n, q_scr, kt_scr, va_scr, row_ref, cseq_scr, mseq_scr, h_ref):
    dh = ML_HEAD_DIM
    sl = pl.ds(pl.multiple_of(n * ML_CHUNK, ML_CHUNK), ML_CHUNK)
    q = q_scr[sl, hh * dh:(hh + 1) * dh]
    va = va_scr[sl, hh * 2 * dh:(hh + 1) * 2 * dh]
    qk = _dot(q, kt_scr[hh, n])
    tt = lax.broadcasted_iota(jnp.int32, (ML_CHUNK, ML_CHUNK), 0)
    ss = lax.broadcasted_iota(jnp.int32, (ML_CHUNK, ML_CHUNK), 1)
    rows = row_ref[0, hh, n]
    h = None
    for d in range(2):
        r_r = rows[3 * d:3 * d + 1, :]
        mx_c = _ml_column(rows, 3 * d + 1)
        b_c = _ml_column(rows, 3 * d + 2)
        m0 = mseq_scr[d, hh, n][:, 0:1]
        seen = (ss <= tt) if d == 0 else (ss >= tt)
        mu_c = jnp.maximum(m0, mx_c)
        mu = jnp.broadcast_to(mu_c, (ML_CHUNK, ML_CHUNK))
        floor = jnp.broadcast_to(jnp.exp(-(b_c + mu_c)), (ML_CHUNK, ML_CHUNK))
        e = jnp.where(seen, jnp.exp(r_r - mu), 0.0)
        wi = jnp.exp(m0 - mu).astype(BF16)
        s = (qk * e).astype(BF16)
        res = _dot(jnp.concatenate([s, wi * q], axis=1), jnp.concatenate([va, cseq_scr[d, hh, n]], axis=0))
        hd = res[:, :dh] / jnp.maximum(jnp.abs(res[:, dh:]), floor)
        h = hd if h is None else h + hd
    h_ref[0, sl, hh * dh:(hh + 1) * dh] = h


def _ml_kernel(zq_ref, zk_ref, zv_ref, cwq_ref, cwk_ref, row_ref, c0_ref, m0_ref,
               h_ref, c_ref, m_ref, pad_ref, q_scr, kt_scr, va_scr, cseq_scr, mseq_scr, *, seq):
    nchunk = seq // ML_CHUNK
    dh = ML_HEAD_DIM
    hp = ML_HEADS_PER_STEP
    _fill_padded(pad_ref, zq_ref[0].astype(F32), seq)
    q_scr[...] = _silu(_dwconv(pad_ref, cwq_ref, 0, seq, ML_CONV)).astype(BF16)
    pad_ref[PAD:PAD + seq, :] = zk_ref[0].astype(F32)
    for n in range(nchunk):
        rows = slice(n * ML_CHUNK, (n + 1) * ML_CHUNK)
        kk = _silu(_dwconv(pad_ref, cwk_ref, n * ML_CHUNK, ML_CHUNK, ML_CONV)) * (dh ** -0.5)
        for hh in range(hp):
            kt_scr[hh, n] = kk[:, hh * dh:(hh + 1) * dh].T.astype(BF16)
            va_scr[rows, hh * 2 * dh:hh * 2 * dh + dh] = zv_ref[0, rows, hh * dh:(hh + 1) * dh].astype(BF16)
            va_scr[rows, hh * 2 * dh + dh:(hh + 1) * 2 * dh] = jnp.ones((ML_CHUNK, dh), BF16)
    c_ref[...] = c0_ref[...]
    m_ref[...] = m0_ref[...]

    def state_body(i, carry):
        for hh in range(hp):
            for d, n in ((0, i), (1, nchunk - 1 - i)):
                _ml_state_step(d, hh, n, kt_scr, va_scr, row_ref, c_ref, m_ref, cseq_scr, mseq_scr)
        return carry

    lax.fori_loop(0, nchunk, state_body, 0)

    def out_body(n, carry):
        for hh in range(hp):
            _ml_output(hh, n, q_scr, kt_scr, va_scr, row_ref, cseq_scr, mseq_scr, h_ref)
        return carry

    lax.fori_loop(0, nchunk, out_body, 0, unroll=min(8, nchunk))


def _mlstm(z_ml, z_gate, prm, l, state):
    bsz, seq, _ = z_ml.shape
    nchunk = seq // ML_CHUNK
    row = _ml_gates(z_gate[..., :ML_GATE_COLS], prm['ml_gate_b'][l])
    dh, nh, hp = ML_HEAD_DIM, ML_HEADS, ML_HEADS_PER_STEP
    ng = nh // hp
    w = hp * dh
    zspec = lambda off: pl.BlockSpec((1, seq, w), lambda b, g: (b, 0, off + g))
    cspec = pl.BlockSpec((1, 2, hp, dh, 2 * dh), lambda b, g: (b, 0, g, 0, 0))
    mspec = pl.BlockSpec((1, 2, hp, 1, dh), lambda b, g: (b, 0, g, 0, 0))
    return pl.pallas_call(
        functools.partial(_ml_kernel, seq=seq),
        grid=(bsz, ng),
        in_specs=[zspec(0), zspec(ng), zspec(2 * ng),
                  pl.BlockSpec((None, ML_CONV, w), lambda b, g: (l, 0, g)),
                  pl.BlockSpec((None, ML_CONV, w), lambda b, g: (l, 0, ng + g)),
                  pl.BlockSpec((1, hp, nchunk, SUBLANES, ML_CHUNK), lambda b, g: (b, g, 0, 0, 0)),
                  cspec, mspec],
        out_specs=[pl.BlockSpec((1, seq, w), lambda b, g: (b, 0, g)), cspec, mspec],
        out_shape=[jax.ShapeDtypeStruct((bsz, seq, ML_WIDTH), F32),
                   jax.ShapeDtypeStruct((bsz, 2, nh, dh, 2 * dh), F32),
                   jax.ShapeDtypeStruct((bsz, 2, nh, 1, dh), F32)],
        scratch_shapes=[pltpu.VMEM((seq + 2 * PAD, w), F32),
                        pltpu.VMEM((seq, w), BF16),
                        pltpu.VMEM((hp, nchunk, dh, ML_CHUNK), BF16),
                        pltpu.VMEM((seq, 2 * w), BF16),
                        pltpu.VMEM((2, hp, nchunk, dh, 2 * dh), BF16),
                        pltpu.VMEM((2, hp, nchunk, 1, dh), F32)],
        compiler_params=_cparams(("parallel", "parallel")),
        name="mlstm",
    )(z_ml, z_ml, z_ml, prm['ml_conv'], prm['ml_conv'], row, *state)


RG_TILE = 256


def _rg_kernel(z_ref, cw_ref, wg_ref, bg_ref, lam_ref, h0_ref, of_ref, ob_ref, st_ref,
               zp_scr, a_scr, b_scr, h_scr, carry_scr, *, seq, tt):
    i = pl.program_id(0)
    nt = seq // tt
    nb = z_ref.shape[1]
    c = RG_WIDTH

    @pl.when(i == 0)
    def _():
        carry_scr[...] = h0_ref[...]

    lam = lam_ref[...]
    sp = jnp.maximum(-lam, 0.0) + jnp.log(1.0 + jnp.exp(-jnp.abs(lam)))
    coef = (-0.5 * RG_C * math.log2(math.e)) * sp
    left = (RG_CONV - 1) // 2
    right = RG_CONV - 1 - left
    for d, tile in ((0, i), (1, nt - 1 - i)):
        t0 = tile * tt
        zp_scr[left:left + tt] = z_ref[pl.ds(t0, tt)]
        for k in range(left):
            src = t0 - left + k
            zp_scr[k:k + 1] = z_ref[pl.ds(jnp.maximum(src, 0), 1)] * (src >= 0).astype(F32)
        for k in range(right):
            src = t0 + tt + k
            zp_scr[left + tt + k:left + tt + k + 1] = (
                z_ref[pl.ds(jnp.minimum(src, seq - 1), 1)] * (src < seq).astype(F32))
        xr = None
        for j in range(RG_CONV):
            term = cw_ref[j:j + 1, :] * zp_scr[j:j + tt]
            xr = term if xr is None else xr + term
        half_pre = (_dot(xr.reshape(tt * nb, c).astype(BF16), wg_ref[:, 2 * d * c:2 * (d + 1) * c])
                    + bg_ref[:, 2 * d * c:2 * (d + 1) * c])
        t_r = jnp.tanh(half_pre[:, :c])
        t_i = jnp.tanh(half_pre[:, c:])
        cd = coef[d:d + 1, :]
        a = jnp.exp2(cd + cd * t_r)
        y = 1.0 - a * a
        root = jnp.where(y > 0.0, y * lax.rsqrt(y), 0.0)
        half_x = 0.5 * xr.reshape(tt * nb, c)
        a_scr[d] = a.reshape(tt, nb, c)
        b_scr[d] = (root * (half_x + half_x * t_i)).reshape(tt, nb, c)

    def body(s, carry):
        hf, hb = carry
        hf = a_scr[0, s] * hf + b_scr[0, s]
        h_scr[0, s] = hf
        sb = tt - 1 - s
        hb = a_scr[1, sb] * hb + b_scr[1, sb]
        h_scr[1, sb] = hb
        return hf, hb

    hf, hb = lax.fori_loop(0, tt, body, (carry_scr[0], carry_scr[1]), unroll=8)
    carry_scr[0] = hf
    carry_scr[1] = hb
    st_ref[0] = hf
    st_ref[1] = hb
    for d, o_ref in ((0, of_ref), (1, ob_ref)):
        o_ref[...] = jnp.swapaxes(h_scr[d].reshape(tt // SUBLANES, SUBLANES, nb, c), 1, 2)


def _rg_gate_weights(rg_gate_w, rg_gate_b):
    depth = rg_gate_w.shape[0]
    eye = jnp.eye(RG_HEADS, dtype=rg_gate_w.dtype)
    full = jnp.einsum('ldghij,hk->ldghikj', 0.5 * rg_gate_w, eye)
    full = full.reshape(depth, 2, 2, RG_WIDTH, RG_WIDTH)
    wg = full.transpose(0, 3, 1, 2, 4).reshape(depth, RG_WIDTH, 4 * RG_WIDTH).astype(BF16)
    return wg, 0.5 * rg_gate_b.reshape(depth, 1, 4 * RG_WIDTH)


def _rglru(zr, prm, l, h0):
    seq, bsz, c = zr.shape
    tt = min(seq, RG_TILE)
    nt = seq // tt
    ng = tt // SUBLANES
    state_spec = pl.BlockSpec((2, bsz, c), lambda i: (0, 0, 0))
    out_sds = jax.ShapeDtypeStruct((seq // SUBLANES, bsz, SUBLANES, c), F32)
    return pl.pallas_call(
        functools.partial(_rg_kernel, seq=seq, tt=tt),
        grid=(nt,),
        in_specs=[_const_spec((seq, bsz, c)),
                  _layer_spec(l, (RG_CONV, c)), _layer_spec(l, (c, 4 * c)), _layer_spec(l, (1, 4 * c)),
                  _layer_spec(l, (2, c)), state_spec],
        out_specs=[pl.BlockSpec((ng, bsz, SUBLANES, c), lambda i: (i, 0, 0, 0)),
                   pl.BlockSpec((ng, bsz, SUBLANES, c), lambda i: (nt - 1 - i, 0, 0, 0)),
                   state_spec],
        out_shape=[out_sds, out_sds, jax.ShapeDtypeStruct((2, bsz, c), F32)],
        scratch_shapes=[pltpu.VMEM((tt + RG_CONV - 1, bsz, c), F32),
                        pltpu.VMEM((2, tt, bsz, c), F32), pltpu.VMEM((2, tt, bsz, c), F32),
                        pltpu.VMEM((2, tt, bsz, c), F32), pltpu.VMEM((2, bsz, c), F32)],
        compiler_params=_cparams(("arbitrary",)),
        name="rglru",
    )(zr, prm['rg_conv'], prm['rg_wg'], prm['rg_bg'], prm['rg_lambda'], h0)


def _group_mean_sq(y, group):
    assert 2 * group == LANES and y.shape[1] % LANES == 0
    sq = y * y
    outs = []
    for t in range(y.shape[1] // LANES):
        blk = sq[:, t * LANES:(t + 1) * LANES]
        low = lax.broadcasted_iota(jnp.int32, blk.shape, 1) < group
        s_low = jnp.sum(jnp.where(low, blk, 0.0), axis=-1, keepdims=True)
        s_high = jnp.sum(jnp.where(low, 0.0, blk), axis=-1, keepdims=True)
        outs.append(jnp.where(low, s_low, s_high) * (1.0 / group))
    return jnp.concatenate(outs, axis=-1)


def _mix_mlp_kernel(x_ref, yhy_ref, hm_ref, o_ref, hrf_ref, hrb_ref, gate_ref, g_ref, wo_ref,
                    gt_ref, sh_ref, sc_ref, gt2_ref, g2_ref, w1_ref, w2_ref, fg_ref, out_ref,
                    *, final, fc, hm_rows):
    g = g_ref[...]
    yhy = yhy_ref[...]
    yhy = yhy * lax.rsqrt(_group_mean_sq(yhy, HY_WIDTH // HY_GROUPS) + EPS) * g[:, :HY_WIDTH]
    parts = [yhy.astype(BF16)]
    if hm_rows:
        hm = jnp.concatenate([hm_ref[0, :, r, :] for r in range(hm_rows)], axis=0)
    else:
        hm = hm_ref[0]
    o_gate = _sigmoid(o_ref[0])
    for h in range(ML_HEADS):
        sl = slice(h * ML_HEAD_DIM, (h + 1) * ML_HEAD_DIM)
        hh = hm[:, sl]
        ms = jnp.mean(hh * hh, axis=-1, keepdims=True)
        yh = hh * lax.rsqrt(ms + EPS) * g[:, HY_WIDTH + h * ML_HEAD_DIM:HY_WIDTH + (h + 1) * ML_HEAD_DIM]
        parts.append((yh * o_gate[:, sl]).astype(BF16))
    gp = gate_ref[0]
    gelu = 0.5 * gp * (1.0 + jnp.tanh(math.sqrt(2.0 / math.pi) * (gp + 0.044715 * (gp * gp * gp))))
    hr = hrf_ref[...] + hrb_ref[...]
    yrg = hr.reshape(hr.shape[0] * SUBLANES, RG_WIDTH) * gelu
    yrg = yrg * lax.rsqrt(_group_mean_sq(yrg, RG_BLOCK) + EPS) * g[:, HY_WIDTH + ML_WIDTH:]
    parts.append(yrg.astype(BF16))
    y = jnp.concatenate(parts, axis=-1)
    x = x_ref[0] + gt_ref[...] * _dot(y, wo_ref[...])

    h = _modulated_norm(x, g2_ref[...], sh_ref[...], sc_ref[...]).astype(BF16)
    acc = None
    for j in range(D_FF // fc):
        a = jnp.maximum(_dot(h, w1_ref[:, j * fc:(j + 1) * fc]), 0.0)
        t = _dot((a * a).astype(BF16), w2_ref[j * fc:(j + 1) * fc, :])
        acc = t if acc is None else acc + t
    out = x + gt2_ref[...] * acc
    if final:
        ms = jnp.mean(out * out, axis=-1, keepdims=True)
        out = out * lax.rsqrt(ms + EPS) * fg_ref[...]
    out_ref[0] = out


def _mix_mlp(x, y_hy, hm, z_a, hr_f, hr_b, mods, prm, l, row, final_g, final, tm, hm_col_major):
    bsz, seq, d = x.shape
    tok = lambda w, j: pl.BlockSpec((1, tm, w), lambda i, b: (b, i, j))
    if hm_col_major:
        hm_rows = tm // GRID_W
        hm = hm.reshape(bsz, GRID_W, seq // GRID_W, ML_WIDTH)
        hm_spec = pl.BlockSpec((1, GRID_W, hm_rows, ML_WIDTH), lambda i, b: (b, 0, i, 0))
    else:
        hm_rows = 0
        hm_spec = tok(ML_WIDTH, 0)
    hr_spec = pl.BlockSpec((tm // SUBLANES, None, SUBLANES, RG_WIDTH), lambda i, b: (i, b, 0, 0))
    return pl.pallas_call(
        functools.partial(_mix_mlp_kernel, final=final, fc=1024, hm_rows=hm_rows),
        grid=(seq // tm, bsz),
        in_specs=[tok(d, 0),
                  pl.BlockSpec((tm, HY_WIDTH), lambda i, b: (i, b)),
                  hm_spec,
                  tok(ML_WIDTH, 0),
                  hr_spec, hr_spec,
                  tok(RG_WIDTH, 2),
                  _layer_spec(l, (1, d)),
                  _layer_spec(l, (d, d), single=True), _mod_spec(l, 2, row),
                  _mod_spec(l, 3, row), _mod_spec(l, 4, row), _mod_spec(l, 5, row),
                  _layer_spec(l, (1, d)),
                  _layer_spec(l, (d, D_FF), single=True), _layer_spec(l, (D_FF, d), single=True),
                  pl.BlockSpec((1, d), lambda i, b: (0, 0))],
        out_specs=tok(d, 0),
        out_shape=jax.ShapeDtypeStruct((bsz, seq, d), F32),
        compiler_params=_cparams(("parallel", "parallel")),
        name="mix_mlp",
    )(x, y_hy, hm, z_a, hr_f, hr_b, z_a, prm['mix_g'], prm['w_o'], mods, mods, mods, mods, prm['g2'],
      prm['w1'], prm['w2'], final_g)


def _split_w_in_kernel(w_ref, wa_ref, wm_ref):
    w = w_ref[0]
    rg = OFF_RG
    o0 = OFF_ML + 3 * ML_WIDTH
    g0 = OFF_ML + 4 * ML_WIDTH
    wa_ref[0] = jnp.concatenate([w[:, :HY_COLS], w[:, rg:rg + RG_WIDTH], w[:, o0:o0 + ML_WIDTH],
                                 w[:, rg + RG_WIDTH:rg + 2 * RG_WIDTH]], axis=1).astype(BF16)
    pad = jnp.zeros((w.shape[0], ZM_COLS - ZM_MAIN - ML_GATE_COLS), F32)
    wm_ref[0] = jnp.concatenate([w[:, OFF_ML:o0], w[:, g0:g0 + ML_GATE_COLS], pad], axis=1).astype(BF16)


def _split_w_in(w_in):
    depth, d, p_in = w_in.shape
    tr = 256
    return pl.pallas_call(
        _split_w_in_kernel,
        grid=(depth, d // tr),
        in_specs=[pl.BlockSpec((1, tr, p_in), lambda l, i: (l, i, 0))],
        out_specs=[pl.BlockSpec((1, tr, ZA_COLS), lambda l, i: (l, i, 0)),
                   pl.BlockSpec((1, tr, ZM_COLS), lambda l, i: (l, i, 0))],
        out_shape=[jax.ShapeDtypeStruct((depth, d, ZA_COLS), BF16),
                   jax.ShapeDtypeStruct((depth, d, ZM_COLS), BF16)],
        compiler_params=_cparams(("parallel", "parallel")),
        name="split_w_in",
    )(w_in)


def _prepare(norm1_g, norm2_g, w_in, hy_conv, hy_w1, hy_b1, hy_w2, hy_b2, hy_w3, hy_b3, hy_freq, hy_bias,
             ml_conv, ml_gate_b, rg_conv, rg_gate_w, rg_gate_b, rg_lambda, mix_norm_g, w_out, mlp_w1, mlp_w2):
    w_a, w_m = _split_w_in(w_in)
    rg_wg, rg_bg = _rg_gate_weights(rg_gate_w, rg_gate_b)
    return dict(
        g1=norm1_g[:, None, :], g2=norm2_g[:, None, :], mix_g=mix_norm_g[:, None, :],
        w_a=w_a, w_m=w_m, w_o=w_out.astype(BF16), w1=mlp_w1.astype(BF16), w2=mlp_w2.astype(BF16),
        hy_conv=hy_conv, hy_bias=hy_bias[:, None, :],
        hy_w1=jnp.swapaxes(jnp.pad(hy_w1, ((0, 0), (0, LANES - HY_POS_DIM), (0, 0))), 1, 2),
        hy_b1=hy_b1[:, :, None], hy_w2=jnp.swapaxes(hy_w2, 1, 2), hy_b2=hy_b2[:, :, None],
        hy_w3=jnp.swapaxes(hy_w3, 1, 2), hy_b3=hy_b3[:, None, :], hy_freq=hy_freq[:, :, :, None],
        ml_conv=ml_conv, ml_gate_b=ml_gate_b,
        rg_conv=rg_conv, rg_wg=rg_wg, rg_bg=rg_bg, rg_lambda=rg_lambda)


def kernel(x, c, ctx, c_ctx, ada_w, ada_b, norm1_g, norm2_g, w_in, hy_conv, hy_w1, hy_b1, hy_w2, hy_b2, hy_w3, hy_b3, hy_freq, hy_bias, ml_conv, ml_gate_b, rg_conv, rg_gate_w, rg_gate_b, rg_lambda, mix_norm_g, w_out, mlp_w1, mlp_w2, final_g):
    bsz, seq, d = x.shape
    clen = ctx.shape[1]
    prm = _prepare(norm1_g, norm2_g, w_in, hy_conv, hy_w1, hy_b1, hy_w2, hy_b2, hy_w3, hy_b3, hy_freq,
                   hy_bias, ml_conv, ml_gate_b, rg_conv, rg_gate_w, rg_gate_b, rg_lambda, mix_norm_g,
                   w_out, mlp_w1, mlp_w2)

    cc = jnp.concatenate([c, c_ctx[None], jnp.zeros((MOD_ROWS - bsz - 1, d), F32)], axis=0)
    mods = _ada(cc, ada_w, ada_b).reshape(DEPTH, MOD_ROWS, N_MOD, 1, d)
    ctx_row = bsz

    fwd_l, inv_l = _dft_tables(seq)
    fwd_c, inv_c = _dft_tables(clen)
    filt_l = _hy_filter(seq, prm, fwd_l)
    filt_c = _hy_filter(clen, prm, fwd_c)
    dh, nh = ML_HEAD_DIM, ML_HEADS
    ml_zero = (jnp.zeros((bsz, 2, nh, dh, 2 * dh), F32), jnp.zeros((bsz, 2, nh, 1, dh), F32))
    rg_zero = jnp.zeros((2, bsz, RG_WIDTH), F32)
    fg = final_g[None]

    ctx_s = ctx
    for l in range(DEPTH):
        need_ctx = l < DEPTH - 1
        zc_h, zc_a, zc_m, zc_g, zc_r = _inproj_seq(ctx_s, mods, prm, l, ctx_row, clen)
        hm_c, ml_c, ml_m = _mlstm(zc_m, zc_g, prm, l, ml_zero)
        hrf_c, hrb_c, rg_state = _rglru(zc_r, prm, l, rg_zero)

        z_h, z_a, z_m, z_g, z_r = _inproj_grid(x, mods, prm, l, 16)
        y_hy = _hy_conv(z_h, prm, l, filt_l, fwd_l, inv_l)
        hm, _, _ = _mlstm(z_m, z_g, prm, l, (ml_c, ml_m))
        hr_f, hr_b, _ = _rglru(z_r, prm, l, rg_state)
        x = _mix_mlp(x, y_hy, hm, z_a, hr_f, hr_b, mods, prm, l, None, fg, final=(l == DEPTH - 1), tm=512,
                     hm_col_major=True)

        if need_ctx:
            y_hy_c = _hy_conv(zc_h, prm, l, filt_c, fwd_c, inv_c)
            ctx_s = _mix_mlp(ctx_s, y_hy_c, hm_c, zc_a, hrf_c, hrb_c, mods, prm, l, ctx_row, fg, final=False,
                             tm=clen, hm_col_major=False)
    return x
```
